```python
import math
import jax, jax.numpy as jnp
from jax import lax
import numpy as np

D_MODEL = 1024
BATCH = 32
SEQ = 2048
DEPTH = 4

D_RNN = 1024
RNN_BLOCKS = 16
RNN_BLOCK = D_RNN // RNN_BLOCKS
CONV_WIDTH = 4
LRU_C = 8.0
N_HEADS = 16
QK_NOPE = 64
QK_ROPE = 32
V_HEAD = 64
Q_LORA = 384
KV_LORA = 256
ROPE_THETA = 10000.0
Q_BLOCK = 128
D_MIX = N_HEADS * V_HEAD
IN_WIDTH = 2 * D_RNN + Q_LORA + KV_LORA + QK_ROPE + 2 * D_MIX
D_FF = 3 * D_MODEL
FFN_CONV_WIDTH = 3
ALPHA = (2 * DEPTH) ** 0.25
BETA = (8 * DEPTH) ** -0.25
EPS = 1e-6
NEG_INF = -1e30

kernel_name = "hybrid_rglru_mla_convffn_deepnorm"


def _split_points():
    sizes = (D_RNN, D_RNN, Q_LORA, KV_LORA, QK_ROPE, D_MIX, D_MIX)
    pts, acc = [], 0
    for s in sizes[:-1]:
        acc += s
        pts.append(acc)
    return pts


def layer_norm(x, g, b):
    xf = x.astype(jnp.float32)
    mu = xf.mean(-1, keepdims=True)
    var = jnp.square(xf - mu).mean(-1, keepdims=True)
    y = (xf - mu) * lax.rsqrt(var + EPS) * g.astype(jnp.float32) + b.astype(jnp.float32)
    return y.astype(x.dtype)


def rms_norm(x, g):
    xf = x.astype(jnp.float32)
    y = xf * lax.rsqrt(jnp.mean(xf * xf, -1, keepdims=True) + EPS) * g.astype(jnp.float32)
    return y.astype(x.dtype)


def causal_dwconv(x, w, b):
    width, c = w.shape
    y = lax.conv_general_dilated(
        x, w[:, None, :].astype(x.dtype), window_strides=(1,), padding=[(width - 1, 0)],
        dimension_numbers=("NWC", "WIO", "NWC"), feature_group_count=c)
    return y + b.astype(x.dtype)


def rope_tables(positions, dtype):
    inv_freq = ROPE_THETA ** (-jnp.arange(0, QK_ROPE, 2, dtype=jnp.float32) / QK_ROPE)
    ang = positions.astype(jnp.float32)[..., None] * inv_freq
    return jnp.cos(ang)[:, :, None, :].astype(dtype), jnp.sin(ang)[:, :, None, :].astype(dtype)


def apply_rope(x, cos, sin):
    x1, x2 = jnp.split(x, 2, axis=-1)
    return jnp.concatenate([x1 * cos - x2 * sin, x2 * cos + x1 * sin], axis=-1)


def rg_lru(x, gx_w, gx_b, ga_w, ga_b, lru_lambda):
    B, S, _ = x.shape
    xb = x.reshape(B, S, RNN_BLOCKS, RNN_BLOCK)
    gate_x = jax.nn.sigmoid(jnp.einsum("bshi,hij->bshj", xb, gx_w).reshape(B, S, D_RNN) + gx_b)
    gate_a = jax.nn.sigmoid(jnp.einsum("bshi,hij->bshj", xb, ga_w).reshape(B, S, D_RNN) + ga_b)
    log_a = -LRU_C * gate_a.astype(jnp.float32) * jax.nn.softplus(-lru_lambda.astype(jnp.float32))
    a = jnp.exp(log_a)
    mult = jnp.sqrt(-jnp.expm1(2.0 * log_a))
    u = mult * (gate_x * x).astype(jnp.float32)

    def step(h, au):
        a_t, u_t = au
        h = a_t * h + u_t
        return h, h

    _, hs = lax.scan(step, jnp.zeros((B, D_RNN), jnp.float32),
                     (jnp.swapaxes(a, 0, 1), jnp.swapaxes(u, 0, 1)))
    return jnp.swapaxes(hs, 0, 1).astype(x.dtype)


def mla_attention(q_lat, kv_lat, k_rope_raw, cos, sin, q_norm_g, w_uq, kv_norm_g, w_ukv):
    B, S, _ = q_lat.shape
    q = (rms_norm(q_lat, q_norm_g) @ w_uq).reshape(B, S, N_HEADS, QK_NOPE + QK_ROPE)
    q_nope, q_pe = q[..., :QK_NOPE], apply_rope(q[..., QK_NOPE:], cos, sin)
    kv = (rms_norm(kv_lat, kv_norm_g) @ w_ukv).reshape(B, S, N_HEADS, QK_NOPE + V_HEAD)
    k_nope, v = kv[..., :QK_NOPE], kv[..., QK_NOPE:]
    k_pe = apply_rope(k_rope_raw[:, :, None, :], cos, sin)[:, :, 0, :]
    scale = (QK_NOPE + QK_ROPE) ** -0.5
    outs = []
    for start in range(0, S, Q_BLOCK):
        end = start + Q_BLOCK
        s = (jnp.einsum("bqhd,bkhd->bhqk", q_nope[:, start:end], k_nope[:, :end])
             + jnp.einsum("bqhd,bkd->bhqk", q_pe[:, start:end], k_pe[:, :end]))
        s = s.astype(jnp.float32) * scale
        causal = (start + jnp.arange(Q_BLOCK))[:, None] >= jnp.arange(end)[None, :]
        p = jax.nn.softmax(jnp.where(causal, s, NEG_INF), axis=-1).astype(v.dtype)
        outs.append(jnp.einsum("bhqk,bkhd->bqhd", p, v[:, :end]))
    return jnp.concatenate(outs, axis=1).reshape(B, S, D_MIX)


def mixer_sublayer(x, cos, sin, w_in, conv_w, conv_b, gx_w, gx_b, ga_w, ga_b, lru_lambda,
                   q_norm_g, w_uq, kv_norm_g, w_ukv, w_out):
    proj = x @ w_in
    x_rnn, g_rnn, q_lat, kv_lat, k_rope, gate_a, gate_b = jnp.split(proj, _split_points(), axis=-1)
    y_rnn = jax.nn.gelu(g_rnn) * rg_lru(causal_dwconv(x_rnn, conv_w, conv_b),
                                        gx_w, gx_b, ga_w, ga_b, lru_lambda)
    y_mla = mla_attention(q_lat, kv_lat, k_rope, cos, sin, q_norm_g, w_uq, kv_norm_g, w_ukv)
    merged = jax.nn.sigmoid(gate_a) * y_rnn + jax.nn.sigmoid(gate_b) * y_mla
    return merged @ w_out


def conv_ffn(x, w_up, ffn_conv_w, ffn_conv_b, w_down):
    h = causal_dwconv(x @ w_up, ffn_conv_w, ffn_conv_b)
    h_gate, h_val = jnp.split(h, 2, axis=-1)
    return (jax.nn.gelu(h_gate) * h_val) @ w_down


def _fwd_setup_inputs(seed: int = 0) -> dict:
    key = jax.random.key(seed)
    ks = jax.random.split(key, 24)
    f32 = jnp.float32

    def nrm(k, shape, scale):
        return jax.random.normal(k, shape, f32) * scale

    x = jax.random.normal(ks[0], (BATCH, SEQ, D_MODEL), f32)
    offsets = jax.random.randint(ks[1], (BATCH, 1), 0, 4096, dtype=jnp.int32)
    positions = (jnp.arange(SEQ, dtype=jnp.int32)[None, :] + offsets).astype(jnp.int32)
    u = jax.random.uniform(ks[2], (DEPTH, D_RNN), f32, 0.9, 0.999)
    a0 = u ** (1.0 / LRU_C)
    lru_lambda = jnp.log(a0) - jnp.log1p(-a0)
    return {
        "x": x,
        "positions": positions,
        "w_in": nrm(ks[3], (DEPTH, D_MODEL, IN_WIDTH), D_MODEL ** -0.5),
        "conv_w": nrm(ks[4], (DEPTH, CONV_WIDTH, D_RNN), CONV_WIDTH ** -0.5),
        "conv_b": nrm(ks[5], (DEPTH, D_RNN), 0.02),
        "gx_w": nrm(ks[6], (DEPTH, RNN_BLOCKS, RNN_BLOCK, RNN_BLOCK), RNN_BLOCK ** -0.5),
        "gx_b": nrm(ks[7], (DEPTH, D_RNN), 0.02),
        "ga_w": nrm(ks[8], (DEPTH, RNN_BLOCKS, RNN_BLOCK, RNN_BLOCK), RNN_BLOCK ** -0.5),
        "ga_b": nrm(ks[9], (DEPTH, D_RNN), 0.02),
        "lru_lambda": lru_lambda,
        "q_norm_g": 1.0 + nrm(ks[10], (DEPTH, Q_LORA), 0.02),
        "w_uq": nrm(ks[11], (DEPTH, Q_LORA, N_HEADS * (QK_NOPE + QK_ROPE)), Q_LORA ** -0.5),
        "kv_norm_g": 1.0 + nrm(ks[12], (DEPTH, KV_LORA), 0.02),
        "w_ukv": nrm(ks[13], (DEPTH, KV_LORA, N_HEADS * (QK_NOPE + V_HEAD)), KV_LORA ** -0.5),
        "w_out": nrm(ks[14], (DEPTH, D_MIX, D_MODEL), BETA * D_MIX ** -0.5),
        "ln1_g": 1.0 + nrm(ks[15], (DEPTH, D_MODEL), 0.02),
        "ln1_b": nrm(ks[16], (DEPTH, D_MODEL), 0.02),
        "w_up": nrm(ks[17], (DEPTH, D_MODEL, 2 * D_FF), D_MODEL ** -0.5),
        "ffn_conv_w": nrm(ks[18], (DEPTH, FFN_CONV_WIDTH, 2 * D_FF), FFN_CONV_WIDTH ** -0.5),
        "ffn_conv_b": nrm(ks[19], (DEPTH, 2 * D_FF), 0.02),
        "w_down": nrm(ks[20], (DEPTH, D_FF, D_MODEL), BETA * D_FF ** -0.5),
        "ln2_g": 1.0 + nrm(ks[21], (DEPTH, D_MODEL), 0.02),
        "ln2_b": nrm(ks[22], (DEPTH, D_MODEL), 0.02),
    }


def _fwd_reference(x, positions, w_in, conv_w, conv_b, gx_w, gx_b, ga_w, ga_b, lru_lambda,
              q_norm_g, w_uq, kv_norm_g, w_ukv, w_out, ln1_g, ln1_b,
              w_up, ffn_conv_w, ffn_conv_b, w_down, ln2_g, ln2_b):
    cos, sin = rope_tables(positions, x.dtype)
    for l in range(DEPTH):
        mix = mixer_sublayer(x, cos, sin, w_in[l], conv_w[l], conv_b[l], gx_w[l], gx_b[l],
                             ga_w[l], ga_b[l], lru_lambda[l], q_norm_g[l], w_uq[l],
                             kv_norm_g[l], w_ukv[l], w_out[l])
        x = layer_norm(ALPHA * x + mix, ln1_g[l], ln1_b[l])
        ffn = conv_ffn(x, w_up[l], ffn_conv_w[l], ffn_conv_b[l], w_down[l])
        x = layer_norm(ALPHA * x + ffn, ln2_g[l], ln2_b[l])
    return x


import jax as _jax
import jax.numpy as _jnp

TWIN_FORMAT = 'train_step'
FWD_PARAMS = ['x', 'positions', 'w_in', 'conv_w', 'conv_b', 'gx_w', 'gx_b', 'ga_w', 'ga_b', 'lru_lambda', 'q_norm_g', 'w_uq', 'kv_norm_g', 'w_ukv', 'w_out', 'ln1_g', 'ln1_b', 'w_up', 'ffn_conv_w', 'ffn_conv_b', 'w_down', 'ln2_g', 'ln2_b']
TWIN_WEIGHTS = ['w_in', 'conv_w', 'conv_b', 'gx_w', 'gx_b', 'ga_w', 'ga_b', 'lru_lambda', 'q_norm_g', 'w_uq', 'kv_norm_g', 'w_ukv', 'w_out', 'ln1_g', 'ln1_b', 'w_up', 'ffn_conv_w', 'ffn_conv_b', 'w_down', 'ln2_g', 'ln2_b']
TWIN_DIFF_INPUT = 'x'
TWIN_INPUTS = ['x', 'positions', 'w_in', 'conv_w', 'conv_b', 'gx_w', 'gx_b', 'ga_w', 'ga_b', 'lru_lambda', 'q_norm_g', 'w_uq', 'kv_norm_g', 'w_ukv', 'w_out', 'ln1_g', 'ln1_b', 'w_up', 'ffn_conv_w', 'ffn_conv_b', 'w_down', 'ln2_g', 'ln2_b', 'loss_target', 'm_w_in', 'm_conv_w', 'm_conv_b', 'm_gx_w', 'm_gx_b', 'm_ga_w', 'm_ga_b', 'm_lru_lambda', 'm_q_norm_g', 'm_w_uq', 'm_kv_norm_g', 'm_w_ukv', 'm_w_out', 'm_ln1_g', 'm_ln1_b', 'm_w_up', 'm_ffn_conv_w', 'm_ffn_conv_b', 'm_w_down', 'm_ln2_g', 'm_ln2_b', 'v_w_in', 'v_conv_w', 'v_conv_b', 'v_gx_w', 'v_gx_b', 'v_ga_w', 'v_ga_b', 'v_lru_lambda', 'v_q_norm_g', 'v_w_uq', 'v_kv_norm_g', 'v_w_ukv', 'v_w_out', 'v_ln1_g', 'v_ln1_b', 'v_w_up', 'v_ffn_conv_w', 'v_ffn_conv_b', 'v_w_down', 'v_ln2_g', 'v_ln2_b']
TWIN_OUTPUTS = ['loss', 'grad_x', 'grad_w_in', 'grad_conv_w', 'grad_conv_b', 'grad_gx_w', 'grad_gx_b', 'grad_ga_w', 'grad_ga_b', 'grad_lru_lambda', 'grad_q_norm_g', 'grad_w_uq', 'grad_kv_norm_g', 'grad_w_ukv', 'grad_w_out', 'grad_ln1_g', 'grad_ln1_b', 'grad_w_up', 'grad_ffn_conv_w', 'grad_ffn_conv_b', 'grad_w_down', 'grad_ln2_g', 'grad_ln2_b', 'delta_w_in', 'delta_conv_w', 'delta_conv_b', 'delta_gx_w', 'delta_gx_b', 'delta_ga_w', 'delta_ga_b', 'delta_lru_lambda', 'delta_q_norm_g', 'delta_w_uq', 'delta_kv_norm_g', 'delta_w_ukv', 'delta_w_out', 'delta_ln1_g', 'delta_ln1_b', 'delta_w_up', 'delta_ffn_conv_w', 'delta_ffn_conv_b', 'delta_w_down', 'delta_ln2_g', 'delta_ln2_b', 'new_m_w_in', 'new_m_conv_w', 'new_m_conv_b', 'new_m_gx_w', 'new_m_gx_b', 'new_m_ga_w', 'new_m_ga_b', 'new_m_lru_lambda', 'new_m_q_norm_g', 'new_m_w_uq', 'new_m_kv_norm_g', 'new_m_w_ukv', 'new_m_w_out', 'new_m_ln1_g', 'new_m_ln1_b', 'new_m_w_up', 'new_m_ffn_conv_w', 'new_m_ffn_conv_b', 'new_m_w_down', 'new_m_ln2_g', 'new_m_ln2_b', 'new_v_w_in', 'new_v_conv_w', 'new_v_conv_b', 'new_v_gx_w', 'new_v_gx_b', 'new_v_ga_w', 'new_v_ga_b', 'new_v_lru_lambda', 'new_v_q_norm_g', 'new_v_w_uq', 'new_v_kv_norm_g', 'new_v_w_ukv', 'new_v_w_out', 'new_v_ln1_g', 'new_v_ln1_b', 'new_v_w_up', 'new_v_ffn_conv_w', 'new_v_ffn_conv_b', 'new_v_w_down', 'new_v_ln2_g', 'new_v_ln2_b']
TWIN_LEAF_KINDS = {'loss': 'loss', 'grad_x': 'grad_x', 'grad_w_in': 'grad_w', 'grad_conv_w': 'grad_w', 'grad_conv_b': 'grad_w', 'grad_gx_w': 'grad_w', 'grad_gx_b': 'grad_w', 'grad_ga_w': 'grad_w', 'grad_ga_b': 'grad_w', 'grad_lru_lambda': 'grad_w', 'grad_q_norm_g': 'grad_w', 'grad_w_uq': 'grad_w', 'grad_kv_norm_g': 'grad_w', 'grad_w_ukv': 'grad_w', 'grad_w_out': 'grad_w', 'grad_ln1_g': 'grad_w', 'grad_ln1_b': 'grad_w', 'grad_w_up': 'grad_w', 'grad_ffn_conv_w': 'grad_w', 'grad_ffn_conv_b': 'grad_w', 'grad_w_down': 'grad_w', 'grad_ln2_g': 'grad_w', 'grad_ln2_b': 'grad_w', 'delta_w_in': 'delta_w', 'delta_conv_w': 'delta_w', 'delta_conv_b': 'delta_w', 'delta_gx_w': 'delta_w', 'delta_gx_b': 'delta_w', 'delta_ga_w': 'delta_w', 'delta_ga_b': 'delta_w', 'delta_lru_lambda': 'delta_w', 'delta_q_norm_g': 'delta_w', 'delta_w_uq': 'delta_w', 'delta_kv_norm_g': 'delta_w', 'delta_w_ukv': 'delta_w', 'delta_w_out': 'delta_w', 'delta_ln1_g': 'delta_w', 'delta_ln1_b': 'delta_w', 'delta_w_up': 'delta_w', 'delta_ffn_conv_w': 'delta_w', 'delta_ffn_conv_b': 'delta_w', 'delta_w_down': 'delta_w', 'delta_ln2_g': 'delta_w', 'delta_ln2_b': 'delta_w', 'new_m_w_in': 'new_m', 'new_m_conv_w': 'new_m', 'new_m_conv_b': 'new_m', 'new_m_gx_w': 'new_m', 'new_m_gx_b': 'new_m', 'new_m_ga_w': 'new_m', 'new_m_ga_b': 'new_m', 'new_m_lru_lambda': 'new_m', 'new_m_q_norm_g': 'new_m', 'new_m_w_uq': 'new_m', 'new_m_kv_norm_g': 'new_m', 'new_m_w_ukv': 'new_m', 'new_m_w_out': 'new_m', 'new_m_ln1_g': 'new_m', 'new_m_ln1_b': 'new_m', 'new_m_w_up': 'new_m', 'new_m_ffn_conv_w': 'new_m', 'new_m_ffn_conv_b': 'new_m', 'new_m_w_down': 'new_m', 'new_m_ln2_g': 'new_m', 'new_m_ln2_b': 'new_m', 'new_v_w_in': 'new_v', 'new_v_conv_w': 'new_v', 'new_v_conv_b': 'new_v', 'new_v_gx_w': 'new_v', 'new_v_gx_b': 'new_v', 'new_v_ga_w': 'new_v', 'new_v_ga_b': 'new_v', 'new_v_lru_lambda': 'new_v', 'new_v_q_norm_g': 'new_v', 'new_v_w_uq': 'new_v', 'new_v_kv_norm_g': 'new_v', 'new_v_w_ukv': 'new_v', 'new_v_w_out': 'new_v', 'new_v_ln1_g': 'new_v', 'new_v_ln1_b': 'new_v', 'new_v_w_up': 'new_v', 'new_v_ffn_conv_w': 'new_v', 'new_v_ffn_conv_b': 'new_v', 'new_v_w_down': 'new_v', 'new_v_ln2_g': 'new_v', 'new_v_ln2_b': 'new_v'}


def _forward(args):
    return _fwd_reference(*[args[k] for k in FWD_PARAMS])


def _output_shape():
    out = _jax.eval_shape(lambda: _forward(_fwd_setup_inputs(0)))
    return out.shape, out.dtype

N_MICROBATCH = 1
ADAM_LR = 0.001
ADAM_B1 = 0.9
ADAM_B2 = 0.999
ADAM_EPS = 1e-08
ADAM_WD = 0.01
ADAM_STEP = 10
PER_EXAMPLE_BATCH_AXIS = {'x': 0, 'positions': 0, 'loss_target': 0}
SHARED_INPUTS = []
_WEIGHT_DTYPES = {'w_in': _jnp.float32, 'conv_w': _jnp.float32, 'conv_b': _jnp.float32, 'gx_w': _jnp.float32, 'gx_b': _jnp.float32, 'ga_w': _jnp.float32, 'ga_b': _jnp.float32, 'lru_lambda': _jnp.float32, 'q_norm_g': _jnp.float32, 'w_uq': _jnp.float32, 'kv_norm_g': _jnp.float32, 'w_ukv': _jnp.float32, 'w_out': _jnp.float32, 'ln1_g': _jnp.float32, 'ln1_b': _jnp.float32, 'w_up': _jnp.float32, 'ffn_conv_w': _jnp.float32, 'ffn_conv_b': _jnp.float32, 'w_down': _jnp.float32, 'ln2_g': _jnp.float32, 'ln2_b': _jnp.float32}
MOMENT_SCALE = {'w_in': 1.173055e-02, 'conv_w': 2.250074e-02, 'conv_b': 3.132157e-01, 'gx_w': 1.985168e-02, 'gx_b': 7.697473e-03, 'ga_w': 1.043056e-02, 'ga_b': 8.572888e-03, 'lru_lambda': 1.253024e-02, 'q_norm_g': 1.027744e-02, 'w_uq': 5.238393e-03, 'kv_norm_g': 2.031430e-02, 'w_ukv': 6.942657e-03, 'w_out': 4.938937e-02, 'ln1_g': 1.796933e+00, 'ln1_b': 8.526948e-01, 'w_up': 2.462104e-02, 'ffn_conv_w': 2.460208e-02, 'ffn_conv_b': 3.014891e-02, 'w_down': 1.000709e-01, 'ln2_g': 3.212978e+01, 'ln2_b': 1.753103e+00}


def _to_microbatches(a, axis):
    t = _jnp.moveaxis(a, axis, 0)
    t = t.reshape((N_MICROBATCH, t.shape[0] // N_MICROBATCH) + t.shape[1:])
    return _jnp.moveaxis(t, 1, axis + 1)


def setup_inputs(seed: int = 0) -> dict:
    inp = _fwd_setup_inputs(seed)
    key = _jax.random.fold_in(_jax.random.key(seed), 7919)
    shape, _ = _output_shape()
    out = dict(inp)
    out["loss_target"] = _jax.random.normal(_jax.random.fold_in(key, 0), shape, _jnp.float32)
    for i, name in enumerate(TWIN_WEIGHTS):
        w = inp[name].astype(_jnp.float32)
        if MOMENT_SCALE is None:
            s = _jnp.sqrt(_jnp.mean(_jnp.square(w)) + 1e-30)
        else:
            s = MOMENT_SCALE[name]
        km, kv = _jax.random.split(_jax.random.fold_in(key, i + 1))
        out[name] = w
        out["m_" + name] = s * _jax.random.normal(km, w.shape, _jnp.float32)
        out["v_" + name] = (s * s) * _jax.random.uniform(kv, w.shape, _jnp.float32, 0.5, 1.5)
    if N_MICROBATCH > 1:
        for name, axis in PER_EXAMPLE_BATCH_AXIS.items():
            out[name] = _to_microbatches(out[name], axis)
    return {'x': out['x'], 'positions': out['positions'], 'w_in': out['w_in'], 'conv_w': out['conv_w'], 'conv_b': out['conv_b'], 'gx_w': out['gx_w'], 'gx_b': out['gx_b'], 'ga_w': out['ga_w'], 'ga_b': out['ga_b'], 'lru_lambda': out['lru_lambda'], 'q_norm_g': out['q_norm_g'], 'w_uq': out['w_uq'], 'kv_norm_g': out['kv_norm_g'], 'w_ukv': out['w_ukv'], 'w_out': out['w_out'], 'ln1_g': out['ln1_g'], 'ln1_b': out['ln1_b'], 'w_up': out['w_up'], 'ffn_conv_w': out['ffn_conv_w'], 'ffn_conv_b': out['ffn_conv_b'], 'w_down': out['w_down'], 'ln2_g': out['ln2_g'], 'ln2_b': out['ln2_b'], 'loss_target': out['loss_target'], 'm_w_in': out['m_w_in'], 'm_conv_w': out['m_conv_w'], 'm_conv_b': out['m_conv_b'], 'm_gx_w': out['m_gx_w'], 'm_gx_b': out['m_gx_b'], 'm_ga_w': out['m_ga_w'], 'm_ga_b': out['m_ga_b'], 'm_lru_lambda': out['m_lru_lambda'], 'm_q_norm_g': out['m_q_norm_g'], 'm_w_uq': out['m_w_uq'], 'm_kv_norm_g': out['m_kv_norm_g'], 'm_w_ukv': out['m_w_ukv'], 'm_w_out': out['m_w_out'], 'm_ln1_g': out['m_ln1_g'], 'm_ln1_b': out['m_ln1_b'], 'm_w_up': out['m_w_up'], 'm_ffn_conv_w': out['m_ffn_conv_w'], 'm_ffn_conv_b': out['m_ffn_conv_b'], 'm_w_down': out['m_w_down'], 'm_ln2_g': out['m_ln2_g'], 'm_ln2_b': out['m_ln2_b'], 'v_w_in': out['v_w_in'], 'v_conv_w': out['v_conv_w'], 'v_conv_b': out['v_conv_b'], 'v_gx_w': out['v_gx_w'], 'v_gx_b': out['v_gx_b'], 'v_ga_w': out['v_ga_w'], 'v_ga_b': out['v_ga_b'], 'v_lru_lambda': out['v_lru_lambda'], 'v_q_norm_g': out['v_q_norm_g'], 'v_w_uq': out['v_w_uq'], 'v_kv_norm_g': out['v_kv_norm_g'], 'v_w_ukv': out['v_w_ukv'], 'v_w_out': out['v_w_out'], 'v_ln1_g': out['v_ln1_g'], 'v_ln1_b': out['v_ln1_b'], 'v_w_up': out['v_w_up'], 'v_ffn_conv_w': out['v_ffn_conv_w'], 'v_ffn_conv_b': out['v_ffn_conv_b'], 'v_w_down': out['v_w_down'], 'v_ln2_g': out['v_ln2_g'], 'v_ln2_b': out['v_ln2_b']}


def _loss(weights, diff, rest, loss_target):
    with _jax.named_scope("forward"):
        args = {**rest, TWIN_DIFF_INPUT: diff, **{k: w.astype(_WEIGHT_DTYPES[k]) for k, w in weights.items()}}
        y = _forward(args)
    with _jax.named_scope("loss_head"):
        err = _jnp.square(y.astype(_jnp.float32) - loss_target)
        return 0.5 * _jnp.sum(_jnp.mean(err, axis=-1)) if err.ndim else 0.5 * err


def _adamw(w, g, m, v):
    m = ADAM_B1 * m + (1.0 - ADAM_B1) * g
    v = ADAM_B2 * v + (1.0 - ADAM_B2) * _jnp.square(g)
    m_hat = m / (1.0 - ADAM_B1 ** ADAM_STEP)
    v_hat = v / (1.0 - ADAM_B2 ** ADAM_STEP)
    delta = -ADAM_LR * (m_hat / (_jnp.sqrt(v_hat) + ADAM_EPS) + ADAM_WD * w)
    return delta, m, v


def reference(x, positions, w_in, conv_w, conv_b, gx_w, gx_b, ga_w, ga_b, lru_lambda, q_norm_g, w_uq, kv_norm_g, w_ukv, w_out, ln1_g, ln1_b, w_up, ffn_conv_w, ffn_conv_b, w_down, ln2_g, ln2_b, loss_target, m_w_in, m_conv_w, m_conv_b, m_gx_w, m_gx_b, m_ga_w, m_ga_b, m_lru_lambda, m_q_norm_g, m_w_uq, m_kv_norm_g, m_w_ukv, m_w_out, m_ln1_g, m_ln1_b, m_w_up, m_ffn_conv_w, m_ffn_conv_b, m_w_down, m_ln2_g, m_ln2_b, v_w_in, v_conv_w, v_conv_b, v_gx_w, v_gx_b, v_ga_w, v_ga_b, v_lru_lambda, v_q_norm_g, v_w_uq, v_kv_norm_g, v_w_ukv, v_w_out, v_ln1_g, v_ln1_b, v_w_up, v_ffn_conv_w, v_ffn_conv_b, v_w_down, v_ln2_g, v_ln2_b):
    given = dict(x=x, positions=positions, w_in=w_in, conv_w=conv_w, conv_b=conv_b, gx_w=gx_w, gx_b=gx_b, ga_w=ga_w, ga_b=ga_b, lru_lambda=lru_lambda, q_norm_g=q_norm_g, w_uq=w_uq, kv_norm_g=kv_norm_g, w_ukv=w_ukv, w_out=w_out, ln1_g=ln1_g, ln1_b=ln1_b, w_up=w_up, ffn_conv_w=ffn_conv_w, ffn_conv_b=ffn_conv_b, w_down=w_down, ln2_g=ln2_g, ln2_b=ln2_b, loss_target=loss_target, m_w_in=m_w_in, m_conv_w=m_conv_w, m_conv_b=m_conv_b, m_gx_w=m_gx_w, m_gx_b=m_gx_b, m_ga_w=m_ga_w, m_ga_b=m_ga_b, m_lru_lambda=m_lru_lambda, m_q_norm_g=m_q_norm_g, m_w_uq=m_w_uq, m_kv_norm_g=m_kv_norm_g, m_w_ukv=m_w_ukv, m_w_out=m_w_out, m_ln1_g=m_ln1_g, m_ln1_b=m_ln1_b, m_w_up=m_w_up, m_ffn_conv_w=m_ffn_conv_w, m_ffn_conv_b=m_ffn_conv_b, m_w_down=m_w_down, m_ln2_g=m_ln2_g, m_ln2_b=m_ln2_b, v_w_in=v_w_in, v_conv_w=v_conv_w, v_conv_b=v_conv_b, v_gx_w=v_gx_w, v_gx_b=v_gx_b, v_ga_w=v_ga_w, v_ga_b=v_ga_b, v_lru_lambda=v_lru_lambda, v_q_norm_g=v_q_norm_g, v_w_uq=v_w_uq, v_kv_norm_g=v_kv_norm_g, v_w_ukv=v_w_ukv, v_w_out=v_w_out, v_ln1_g=v_ln1_g, v_ln1_b=v_ln1_b, v_w_up=v_w_up, v_ffn_conv_w=v_ffn_conv_w, v_ffn_conv_b=v_ffn_conv_b, v_w_down=v_w_down, v_ln2_g=v_ln2_g, v_ln2_b=v_ln2_b)
    weights = {n: given[n] for n in TWIN_WEIGHTS}
    shared = {n: given[n] for n in SHARED_INPUTS}
    per_example = {n: given[n] for n in ['x', 'positions']}
    grad_fn = _jax.value_and_grad(_loss, argnums=(0, 1))

    def one_microbatch(ex, loss_target):
        ex = dict(ex)
        diff = ex.pop(TWIN_DIFF_INPUT)
        return grad_fn(weights, diff, {**shared, **ex}, loss_target)

    if N_MICROBATCH == 1:
        loss, (grad_w, grad_x) = one_microbatch(per_example, given["loss_target"])
    else:
        def body(carry, xs):
            loss_sum, grad_sum = carry
            l_k, (gw_k, gx_k) = one_microbatch(xs[0], xs[1])
            with _jax.named_scope("update"):
                return (loss_sum + l_k, _jax.tree.map(_jnp.add, grad_sum, gw_k)), gx_k

        init = (_jnp.zeros((), _jnp.float32), _jax.tree.map(_jnp.zeros_like, weights))
        (loss, grad_w), grad_x = _jax.lax.scan(body, init, (per_example, given["loss_target"]))
    with _jax.named_scope("update"):
        delta_w, new_m, new_v = {}, {}, {}
        for n in TWIN_WEIGHTS:
            delta_w[n], new_m[n], new_v[n] = _adamw(weights[n], grad_w[n], given["m_" + n], given["v_" + n])
    return (loss, grad_x, *[grad_w[n] for n in TWIN_WEIGHTS], *[delta_w[n] for n in TWIN_WEIGHTS],
            *[new_m[n] for n in TWIN_WEIGHTS], *[new_v[n] for n in TWIN_WEIGHTS])
```

```python
import functools
import math

import jax
import jax.numpy as jnp
from jax import lax
from jax.experimental import pallas as pl
from jax.experimental.pallas import tpu as pltpu

F32 = jnp.float32
MXU_DTYPE = jnp.bfloat16

D = 1024
DEPTH = 4
N_RNN_BLOCKS = 16
RNN_BLOCK = 64
CONV_W = 4
LRU_C = 8.0
N_HEADS = 16
QK_NOPE = 64
QK_ROPE = 32
V_HEAD = 64
Q_LORA = 384
KV_LORA = 256
ROPE_THETA = 10000.0
D_FF = 3 * D
FFN_CONV_W = 3
IN_WIDTH = 2 * D + Q_LORA + KV_LORA + QK_ROPE + 2 * D
ALPHA = (2 * DEPTH) ** 0.25
EPS = 1e-6
NEG_INF = -1e30
ATT_SCALE = (QK_NOPE + QK_ROPE) ** -0.5
GELU_C = math.sqrt(2.0 / math.pi)

ADAM_LR = 0.001
ADAM_B1 = 0.9
ADAM_B2 = 0.999
ADAM_EPS = 1e-08
ADAM_WD = 0.01
ADAM_STEP = 10

N_DEV = 8
LANES = 128
SUBLANES = 8
MXU_GROUP = 256
N_GATE_GROUPS = D // MXU_GROUP
HEADS_PER_STEP = 2
N_HEAD_PAIRS = N_HEADS // HEADS_PER_STEP
HEAD_LANES = 128
PROJ_W = 4 * D + Q_LORA + KV_LORA + LANES
VMEM_BIG = 56 * 2 ** 20

WEIGHT_NAMES = ['w_in', 'conv_w', 'conv_b', 'gx_w', 'gx_b', 'ga_w', 'ga_b', 'lru_lambda', 'q_norm_g', 'w_uq',
                'kv_norm_g', 'w_ukv', 'w_out', 'ln1_g', 'ln1_b', 'w_up', 'ffn_conv_w', 'ffn_conv_b', 'w_down',
                'ln2_g', 'ln2_b']
SHARDED = {
    'w_in': (2, (DEPTH, D, IN_WIDTH)),
    'conv_w': (2, (DEPTH, CONV_W, D)),
    'w_uq': (2, (DEPTH, Q_LORA, N_HEADS * (QK_NOPE + QK_ROPE))),
    'w_ukv': (2, (DEPTH, KV_LORA, N_HEADS * (QK_NOPE + V_HEAD))),
    'w_out': (1, (DEPTH, D, D)),
    'w_up': (2, (DEPTH, D, 2 * D_FF)),
    'ffn_conv_w': (2, (DEPTH, FFN_CONV_W, 2 * D_FF)),
    'w_down': (1, (DEPTH, D_FF, D)),
}
MATMUL_WEIGHTS = ('w_in', 'w_uq', 'w_ukv', 'w_out', 'w_up', 'w_down')


def _mm(a, b):
    return jnp.dot(a.astype(MXU_DTYPE), b.astype(MXU_DTYPE), preferred_element_type=F32)


def _mm_tn(a, b):
    return lax.dot_general(a.astype(MXU_DTYPE), b.astype(MXU_DTYPE), (((0,), (0,)), ((), ())),
                           preferred_element_type=F32)


def _mm_nt(a, b):
    return lax.dot_general(a.astype(MXU_DTYPE), b.astype(MXU_DTYPE), (((1,), (1,)), ((), ())),
                           preferred_element_type=F32)


def _sigmoid(x):
    return 1.0 / (1.0 + jnp.exp(-x))


def _gelu(x):
    t = jnp.tanh(GELU_C * (x + 0.044715 * (x * x * x)))
    return 0.5 * x * (1.0 + t), t


def _gelu_grad(x, t):
    return 0.5 * (1.0 + t) + 0.5 * x * (1.0 - t * t) * (GELU_C * (1.0 + 3.0 * 0.044715 * (x * x)))


def _neg_expm1(y):
    series = -y * (1.0 + 0.5 * y * (1.0 + (y / 3.0) * (1.0 + 0.25 * y * (1.0 + 0.2 * y))))
    return jnp.where(y > -0.05, series, 1.0 - jnp.exp(y))


def _ln_stats(z):
    mu = jnp.mean(z, axis=-1, keepdims=True)
    zc = z - mu
    var = jnp.mean(zc * zc, axis=-1, keepdims=True)
    r = lax.rsqrt(var + EPS)
    return zc * r, r


def _ln_bwd(dy, z, g):
    xhat, r = _ln_stats(z)
    dxh = dy * g
    dz = r * (dxh - jnp.mean(dxh, axis=-1, keepdims=True)
              - xhat * jnp.mean(dxh * xhat, axis=-1, keepdims=True))
    return dz, jnp.sum(dy * xhat, axis=0, keepdims=True), jnp.sum(dy, axis=0, keepdims=True)


def _rms_stats(x):
    r = lax.rsqrt(jnp.mean(x * x, axis=-1, keepdims=True) + EPS)
    return x * r, r


def _rms_bwd(dy, x, g):
    xn, r = _rms_stats(x)
    dxn = dy * g
    dx = r * (dxn - xn * jnp.mean(dxn * xn, axis=-1, keepdims=True))
    return dx, jnp.sum(dy * xn, axis=0, keepdims=True)


def _shift_down(x, halo, s, axis):
    if s == 0:
        return x
    r = pltpu.roll(x, s, axis)
    hr = pltpu.roll(halo, s, axis)
    idx = lax.broadcasted_iota(jnp.int32, hr.shape, axis)
    head = lax.slice_in_dim(r, 0, SUBLANES, axis=axis)
    rest = lax.slice_in_dim(r, SUBLANES, x.shape[axis], axis=axis)
    return jnp.concatenate([jnp.where(idx < s, hr, head), rest], axis=axis)


def _shift_up(x, halo, s, axis):
    if s == 0:
        return x
    n = x.shape[axis]
    r = pltpu.roll(x, n - s, axis)
    hr = pltpu.roll(halo, SUBLANES - s, axis)
    idx = lax.broadcasted_iota(jnp.int32, hr.shape, axis)
    body = lax.slice_in_dim(r, 0, n - SUBLANES, axis=axis)
    tail = lax.slice_in_dim(r, n - SUBLANES, n, axis=axis)
    return jnp.concatenate([body, jnp.where(idx >= SUBLANES - s, hr, tail)], axis=axis)


def _const_spec(shape):
    nd = len(shape)
    return pl.BlockSpec(shape, lambda *_: (0,) * nd)


def _resident(shape):
    nd = len(shape)
    return pl.BlockSpec(shape, lambda *_: (0,) * nd, pipeline_mode=pl.Buffered(1))


def _params(vmem=None):
    return pltpu.CompilerParams(vmem_limit_bytes=vmem)


def inproj_fwd(x, w_in_p, tm):
    n = x.shape[0]

    def body(x_ref, w_ref, rnn4_ref, ql_ref, kvl_ref, kr_ref):
        xb = x_ref[...].astype(MXU_DTYPE)
        for j in range(4):
            rnn4_ref[:, j * D:(j + 1) * D] = _mm(xb, w_ref[:, j * D:(j + 1) * D])
        o = 4 * D
        ql_ref[...] = _mm(xb, w_ref[:, o:o + Q_LORA])
        kvl_ref[...] = _mm(xb, w_ref[:, o + Q_LORA:o + Q_LORA + KV_LORA])
        kr_ref[...] = _mm(xb, w_ref[:, o + Q_LORA + KV_LORA:PROJ_W])

    row = lambda i: (i, 0)
    return pl.pallas_call(
        body, name="inproj_fwd", grid=(n // tm,),
        in_specs=[pl.BlockSpec((tm, D), row), _resident((D, PROJ_W))],
        out_specs=[pl.BlockSpec((tm, 4 * D), row), pl.BlockSpec((tm, Q_LORA), row),
                   pl.BlockSpec((tm, KV_LORA), row), pl.BlockSpec((tm, LANES), row)],
        out_shape=[jax.ShapeDtypeStruct((n, 4 * D), F32), jax.ShapeDtypeStruct((n, Q_LORA), F32),
                   jax.ShapeDtypeStruct((n, KV_LORA), F32), jax.ShapeDtypeStruct((n, LANES), F32)],
        compiler_params=_params(VMEM_BIG),
    )(x, w_in_p)


def inproj_bwd(dxr, dg3, dql, dkvl, dkr, dz1, w_in_p, tm):
    n = dz1.shape[0]

    def body(dxr_ref, dg3_ref, dql_ref, dkvl_ref, dkr_ref, dz_ref, w_ref, dx_ref, dp_ref):
        dp = jnp.concatenate([dxr_ref[...], dg3_ref[...], dql_ref[...], dkvl_ref[...], dkr_ref[...]],
                             axis=1).astype(MXU_DTYPE)
        dp_ref[...] = dp
        dx_ref[...] = ALPHA * dz_ref[...] + _mm_nt(dp, w_ref[...])

    row = lambda i: (i, 0)
    return pl.pallas_call(
        body, name="inproj_bwd", grid=(n // tm,),
        in_specs=[pl.BlockSpec((tm, D), row), pl.BlockSpec((tm, 3 * D), row), pl.BlockSpec((tm, Q_LORA), row),
                  pl.BlockSpec((tm, KV_LORA), row), pl.BlockSpec((tm, LANES), row), pl.BlockSpec((tm, D), row),
                  _resident((D, PROJ_W))],
        out_specs=[pl.BlockSpec((tm, D), row), pl.BlockSpec((tm, PROJ_W), row)],
        out_shape=[jax.ShapeDtypeStruct((n, D), F32), jax.ShapeDtypeStruct((n, PROJ_W), MXU_DTYPE)],
        compiler_params=_params(VMEM_BIG),
    )(dxr, dg3, dql, dkvl, dkr, dz1, w_in_p)


def matmul_dw(x, dy, tn, tmc, name):
    n, k = x.shape
    m = dy.shape[1]

    def body(x_ref, dy_ref, dw_ref):
        @pl.when(pl.program_id(1) == 0)
        def _():
            dw_ref[...] = jnp.zeros_like(dw_ref)
        dw_ref[...] += _mm_tn(x_ref[...], dy_ref[...])

    return pl.pallas_call(
        body, name=name, grid=(m // tmc, n // tn),
        in_specs=[pl.BlockSpec((tn, k), lambda j, i: (i, 0)), pl.BlockSpec((tn, tmc), lambda j, i: (i, j))],
        out_specs=pl.BlockSpec((k, tmc), lambda j, i: (0, j)),
        out_shape=jax.ShapeDtypeStruct((k, m), F32),
        compiler_params=_params(VMEM_BIG),
    )(x, dy)


def matmul_dx(dy, w, add, add_scale, tm, name):
    n, m = dy.shape
    k = w.shape[0]

    def body(dy_ref, w_ref, add_ref, dx_ref):
        dx_ref[...] = add_scale * add_ref[...] + _mm_nt(dy_ref[...], w_ref[...])

    row = lambda i: (i, 0)
    return pl.pallas_call(
        body, name=name, grid=(n // tm,),
        in_specs=[pl.BlockSpec((tm, m), row), _resident((k, m)), pl.BlockSpec((tm, k), row)],
        out_specs=pl.BlockSpec((tm, k), row),
        out_shape=jax.ShapeDtypeStruct((n, k), F32),
        compiler_params=_params(VMEM_BIG),
    )(dy, w, add)


def _group(g):
    return slice(g * MXU_GROUP, (g + 1) * MXU_GROUP)


def _rnn_gates(x, halo, g, cw_ref, cb_ref, wgx_ref, bgx_ref, wga_ref, bga_ref, lam_ref):
    b, ts, gw = x.shape
    sl = _group(g)
    xc = cb_ref[:, sl][None]
    for k in range(CONV_W):
        xc = xc + cw_ref[k:k + 1, sl][None] * _shift_down(x, halo, CONV_W - 1 - k, 1)
    xc2 = xc.reshape(b * ts, gw)
    xcb = xc2.astype(MXU_DTYPE)
    gx = _sigmoid(_mm(xcb, wgx_ref[g]) + bgx_ref[:, sl])
    ga = _sigmoid(_mm(xcb, wga_ref[g]) + bga_ref[:, sl])
    nl = -lam_ref[:, sl]
    sp = jnp.maximum(nl, 0.0) + jnp.log1p(jnp.exp(-jnp.abs(nl)))
    log_a = (-LRU_C) * ga * sp
    a = jnp.exp(log_a)
    mult = jnp.sqrt(_neg_expm1(2.0 * log_a))
    return xc2, xcb, gx, ga, sp, a, mult


def rnn_fwd(rnn4, lw, b, s, ts):
    ns = s // ts

    def body(x_ref, cw_ref, cb_ref, wgx_ref, bgx_ref, wga_ref, bga_ref, lam_ref, h_ref,
             halo_sc, hstate_sc, a_sc, u_sc):
        @pl.when(pl.program_id(0) == 0)
        def _():
            halo_sc[...] = jnp.zeros_like(halo_sc)
            hstate_sc[...] = jnp.zeros_like(hstate_sc)

        for g in range(N_GATE_GROUPS):
            sl = _group(g)
            x = x_ref[:, :, sl]
            xc2, _, gx, _, _, a, mult = _rnn_gates(x, halo_sc[:, :, sl], g, cw_ref, cb_ref, wgx_ref, bgx_ref,
                                                   wga_ref, bga_ref, lam_ref)
            halo_sc[:, :, sl] = x[:, ts - SUBLANES:, :]
            a_sc[...] = a.reshape(b, ts, MXU_GROUP)
            u_sc[...] = (mult * gx * xc2).reshape(b, ts, MXU_GROUP)

            def step(t, h, sl=sl):
                h = a_sc[:, pl.ds(t, 1), :] * h + u_sc[:, pl.ds(t, 1), :]
                h_ref[:, pl.ds(t, 1), sl] = h
                return h

            hstate_sc[:, :, sl] = lax.fori_loop(0, ts, step, hstate_sc[:, :, sl], unroll=8)

    tile = lambda i: (0, i, 0)
    gshape = (N_GATE_GROUPS, MXU_GROUP, MXU_GROUP)
    return pl.pallas_call(
        body, name="rnn_fwd", grid=(ns,),
        in_specs=[pl.BlockSpec((b, ts, D), tile), _const_spec((CONV_W, D)), _const_spec((1, D)),
                  _const_spec(gshape), _const_spec((1, D)), _const_spec(gshape), _const_spec((1, D)),
                  _const_spec((1, D))],
        out_specs=pl.BlockSpec((b, ts, D), tile),
        out_shape=jax.ShapeDtypeStruct((b, s, D), F32),
        scratch_shapes=[pltpu.VMEM((b, SUBLANES, D), F32), pltpu.VMEM((b, 1, D), F32),
                        pltpu.VMEM((b, ts, MXU_GROUP), F32), pltpu.VMEM((b, ts, MXU_GROUP), F32)],
        compiler_params=_params(VMEM_BIG),
    )(rnn4, lw['conv_w'], lw['conv_b'], lw['wgx'], lw['gx_b'], lw['wga'], lw['ga_b'], lw['lru_lambda'])


def rnn_bwd(dh, rnn4, h, lw, b, s, ts):
    ns = s // ts
    hb = ts // SUBLANES

    def body(dh_ref, x_ref, xh_ref, h_ref, hh_ref, cw_ref, cb_ref, wgx_ref, bgx_ref, wga_ref, bga_ref, lam_ref,
             dx_ref, dcw_ref, dcb_ref, dwgx_ref, dbgx_ref, dwga_ref, dbga_ref, dlam_ref,
             carry_sc, dxc_halo_sc, a_sc, delta_sc):
        i = pl.program_id(0)

        @pl.when(i == 0)
        def _():
            carry_sc[...] = jnp.zeros_like(carry_sc)
            dxc_halo_sc[...] = jnp.zeros_like(dxc_halo_sc)
            for r in (dcw_ref, dcb_ref, dwgx_ref, dbgx_ref, dwga_ref, dbga_ref, dlam_ref):
                r[...] = jnp.zeros_like(r)

        keep = jnp.where(i == ns - 1, 0.0, 1.0)
        for g in range(N_GATE_GROUPS):
            sl = _group(g)
            x = x_ref[:, :, sl]
            xhalo = xh_ref[:, :, sl] * keep
            xc2, xcb, gx, ga, sp, a, mult = _rnn_gates(x, xhalo, g, cw_ref, cb_ref, wgx_ref, bgx_ref, wga_ref,
                                                       bga_ref, lam_ref)
            a_sc[...] = a.reshape(b, ts, MXU_GROUP)

            def step(j, c, sl=sl):
                t = ts - 1 - j
                d = dh_ref[:, pl.ds(t, 1), sl] + c
                delta_sc[:, pl.ds(t, 1), :] = d
                return a_sc[:, pl.ds(t, 1), :] * d

            carry_sc[:, :, sl] = lax.fori_loop(0, ts, step, carry_sc[:, :, sl], unroll=8)

            delta = delta_sc[...].reshape(b * ts, MXU_GROUP)
            hprev = _shift_down(h_ref[:, :, sl], hh_ref[:, :, sl] * keep, 1, 1).reshape(b * ts, MXU_GROUP)
            dmult = delta * gx * xc2
            dl = delta * hprev * a - dmult * (a * a) / mult
            dga = dl * ((-LRU_C) * sp)
            dlam_ref[:, sl] += (jnp.sum(dl * ((-LRU_C) * ga), axis=0, keepdims=True)
                                * (-_sigmoid(-lam_ref[:, sl])))
            dpa = dga * ga * (1.0 - ga)
            dpx = (delta * mult * xc2) * gx * (1.0 - gx)
            dbga_ref[:, sl] += jnp.sum(dpa, axis=0, keepdims=True)
            dbgx_ref[:, sl] += jnp.sum(dpx, axis=0, keepdims=True)
            dpab = dpa.astype(MXU_DTYPE)
            dpxb = dpx.astype(MXU_DTYPE)
            dwga_ref[g] += _mm_tn(xcb, dpab)
            dwgx_ref[g] += _mm_tn(xcb, dpxb)
            dxc2 = delta * mult * gx + _mm_nt(dpab, wga_ref[g]) + _mm_nt(dpxb, wgx_ref[g])
            dcb_ref[:, sl] += jnp.sum(dxc2, axis=0, keepdims=True)
            dxc = dxc2.reshape(b, ts, MXU_GROUP)
            nhalo = dxc_halo_sc[:, :, sl]
            dx = jnp.zeros_like(dxc)
            for k in range(CONV_W):
                sft = CONV_W - 1 - k
                xs = _shift_down(x, xhalo, sft, 1)
                dcw_ref[k:k + 1, sl] += jnp.sum((dxc * xs).reshape(b * ts, MXU_GROUP), axis=0, keepdims=True)
                dx = dx + cw_ref[k:k + 1, sl][None] * _shift_up(dxc, nhalo, sft, 1)
            dx_ref[:, :, sl] = dx
            dxc_halo_sc[:, :, sl] = dxc[:, :SUBLANES, :]

    tile = lambda i: (0, ns - 1 - i, 0)
    halo = lambda i: (0, jnp.maximum((ns - 1 - i) * hb - 1, 0), 0)
    gshape = (N_GATE_GROUPS, MXU_GROUP, MXU_GROUP)
    vec = jax.ShapeDtypeStruct((1, D), F32)
    return pl.pallas_call(
        body, name="rnn_bwd", grid=(ns,),
        in_specs=[pl.BlockSpec((b, ts, D), tile), pl.BlockSpec((b, ts, D), tile),
                  pl.BlockSpec((b, SUBLANES, D), halo), pl.BlockSpec((b, ts, D), tile),
                  pl.BlockSpec((b, SUBLANES, D), halo),
                  _const_spec((CONV_W, D)), _const_spec((1, D)), _const_spec(gshape), _const_spec((1, D)),
                  _const_spec(gshape), _const_spec((1, D)), _const_spec((1, D))],
        out_specs=[pl.BlockSpec((b, ts, D), tile), _const_spec((CONV_W, D)), _const_spec((1, D)),
                   _const_spec(gshape), _const_spec((1, D)), _const_spec(gshape), _const_spec((1, D)),
                   _const_spec((1, D))],
        out_shape=[jax.ShapeDtypeStruct((b, s, D), F32), jax.ShapeDtypeStruct((CONV_W, D), F32), vec,
                   jax.ShapeDtypeStruct(gshape, F32), vec, jax.ShapeDtypeStruct(gshape, F32), vec, vec],
        scratch_shapes=[pltpu.VMEM((b, 1, D), F32), pltpu.VMEM((b, SUBLANES, D), F32),
                        pltpu.VMEM((b, ts, MXU_GROUP), F32), pltpu.VMEM((b, ts, MXU_GROUP), F32)],
        compiler_params=_params(VMEM_BIG),
    )(dh, rnn4, rnn4, h, h, lw['conv_w'], lw['conv_b'], lw['wgx'], lw['gx_b'], lw['wga'], lw['ga_b'],
      lw['lru_lambda'])


def _rope(x, cos, sa, sb):
    return x * cos + pltpu.roll(x, HEAD_LANES - QK_ROPE // 2, 1) * sa + pltpu.roll(x, QK_ROPE // 2, 1) * sb


def _unrope(d, cos, sa, sb):
    return d * cos + pltpu.roll(d * sa, QK_ROPE // 2, 1) + pltpu.roll(d * sb, HEAD_LANES - QK_ROPE // 2, 1)


def _softmax_block(q_blk, k_keys, qb, tq):
    kl = k_keys.shape[0]
    sc = _mm_nt(q_blk, k_keys) * ATT_SCALE
    row = lax.broadcasted_iota(jnp.int32, (tq, kl), 0) + qb * tq
    col = lax.broadcasted_iota(jnp.int32, (tq, kl), 1)
    sc = jnp.where(row >= col, sc, NEG_INF)
    e = jnp.exp(sc - jnp.max(sc, axis=-1, keepdims=True))
    return e / jnp.sum(e, axis=-1, keepdims=True)


def _mla_project(ql_ref, kvl_ref, kr_ref, gq_ref, gkv_ref, wq_ref, wkv_ref):
    qn, _ = _rms_stats(ql_ref[0])
    qn = (qn * gq_ref[...]).astype(MXU_DTYPE)
    kvn, _ = _rms_stats(kvl_ref[0])
    kvn = (kvn * gkv_ref[...]).astype(MXU_DTYPE)
    return qn, kvn, _mm(qn, wq_ref[0]), _mm(kvn, wkv_ref[0])


def mla_fwd(ql, kvl, kr, tabs, lw, b, s, tq):
    nq = s // tq
    cos_t, sa_t, sb_t = tabs

    def body(ql_ref, kvl_ref, kr_ref, cos_ref, sa_ref, sb_ref, gq_ref, gkv_ref, wq_ref, wkv_ref, o_ref):
        _, _, qp, kvp = _mla_project(ql_ref, kvl_ref, kr_ref, gq_ref, gkv_ref, wq_ref, wkv_ref)
        cos, sa, sb = cos_ref[0], sa_ref[0], sb_ref[0]
        for hh in range(HEADS_PER_STEP):
            hs = slice(hh * HEAD_LANES, (hh + 1) * HEAD_LANES)
            q = _rope(qp[:, hs], cos, sa, sb).astype(MXU_DTYPE)
            k = _rope(kvp[:, hs] + kr_ref[0], cos, sa, sb).astype(MXU_DTYPE)
            v = kvp[:, HEADS_PER_STEP * HEAD_LANES + hh * HEAD_LANES:
                    HEADS_PER_STEP * HEAD_LANES + (hh + 1) * HEAD_LANES].astype(MXU_DTYPE)
            for qb in range(nq):
                kl = (qb + 1) * tq
                p = _softmax_block(q[qb * tq:kl], k[:kl], qb, tq)
                o = _mm(p, v[:kl])
                if hh == 0:
                    o_ref[0, qb * tq:kl, :] = o
                else:
                    o_ref[0, qb * tq:kl, :] += o

    seq = lambda bi, p: (bi, 0, 0)
    pair = lambda bi, p: (p, 0, 0)

    def per_seq(w):
        return pl.BlockSpec((1, s, w), seq, pipeline_mode=pl.Buffered(1))

    return pl.pallas_call(
        body, name="mla_fwd", grid=(b, N_HEAD_PAIRS),
        in_specs=[per_seq(Q_LORA), per_seq(KV_LORA), per_seq(LANES), per_seq(LANES), per_seq(LANES), per_seq(LANES),
                  _const_spec((1, Q_LORA)), _const_spec((1, KV_LORA)),
                  pl.BlockSpec((1, Q_LORA, HEADS_PER_STEP * HEAD_LANES), pair),
                  pl.BlockSpec((1, KV_LORA, 2 * HEADS_PER_STEP * HEAD_LANES), pair)],
        out_specs=pl.BlockSpec((1, s, LANES), lambda bi, p: (bi, 0, p)),
        out_shape=jax.ShapeDtypeStruct((b, s, D), F32),
        compiler_params=_params(VMEM_BIG),
    )(ql, kvl, kr, cos_t, sa_t, sb_t, lw['q_norm_g'], lw['kv_norm_g'], lw['wq_pairs'], lw['wkv_pairs'])


def mla_bwd(dy, ql, kvl, kr, tabs, lw, b, s, tq):
    nq = s // tq
    cos_t, sa_t, sb_t = tabs
    qw = HEADS_PER_STEP * HEAD_LANES
    kvw = 2 * HEADS_PER_STEP * HEAD_LANES

    def body(dy_ref, ql_ref, kvl_ref, kr_ref, cos_ref, sa_ref, sb_ref, gq_ref, gkv_ref, wq_ref, wkv_ref,
             dql_ref, dkvl_ref, dkr_ref, dwq_ref, dwkv_ref, dgq_ref, dgkv_ref, dk_sc, dv_sc):
        bi, p = pl.program_id(0), pl.program_id(1)

        @pl.when((bi == 0) & (p == 0))
        def _():
            for r in (dwq_ref, dwkv_ref, dgq_ref, dgkv_ref):
                r[...] = jnp.zeros_like(r)

        @pl.when(p == 0)
        def _():
            for r in (dql_ref, dkvl_ref, dkr_ref):
                r[...] = jnp.zeros_like(r)

        qn, kvn, qp, kvp = _mla_project(ql_ref, kvl_ref, kr_ref, gq_ref, gkv_ref, wq_ref, wkv_ref)
        cos, sa, sb = cos_ref[0], sa_ref[0], sb_ref[0]
        do = dy_ref[0].astype(MXU_DTYPE)
        dq_heads, dk_heads, dv_heads = [], [], []
        for hh in range(HEADS_PER_STEP):
            hs = slice(hh * HEAD_LANES, (hh + 1) * HEAD_LANES)
            q = _rope(qp[:, hs], cos, sa, sb).astype(MXU_DTYPE)
            k = _rope(kvp[:, hs] + kr_ref[0], cos, sa, sb).astype(MXU_DTYPE)
            v = kvp[:, qw + hh * HEAD_LANES:qw + (hh + 1) * HEAD_LANES].astype(MXU_DTYPE)
            dk_sc[...] = jnp.zeros_like(dk_sc)
            dv_sc[...] = jnp.zeros_like(dv_sc)
            dq_blocks = []
            for qb in range(nq):
                kl = (qb + 1) * tq
                qs = slice(qb * tq, kl)
                pr = _softmax_block(q[qs], k[:kl], qb, tq)
                dp = _mm_nt(do[qs], v[:kl])
                dv_sc[0:kl, :] += _mm_tn(pr, do[qs])
                ds = (pr * (dp - jnp.sum(pr * dp, axis=-1, keepdims=True)) * ATT_SCALE).astype(MXU_DTYPE)
                dq_blocks.append(_mm(ds, k[:kl]))
                dk_sc[0:kl, :] += _mm_tn(ds, q[qs])
            dq_heads.append(_unrope(jnp.concatenate(dq_blocks, axis=0), cos, sa, sb))
            dk_full = _unrope(dk_sc[...], cos, sa, sb)
            dkr_ref[0] += dk_full
            dk_heads.append(dk_full)
            dv_heads.append(dv_sc[...])
        dqp = jnp.concatenate(dq_heads, axis=1).astype(MXU_DTYPE)
        dkvp = jnp.concatenate(dk_heads + dv_heads, axis=1).astype(MXU_DTYPE)
        dwq_ref[p] += _mm_tn(qn, dqp)
        dwkv_ref[p] += _mm_tn(kvn, dkvp)
        dql_ref[0] += _mm_nt(dqp, wq_ref[0])
        dkvl_ref[0] += _mm_nt(dkvp, wkv_ref[0])

        @pl.when(p == N_HEAD_PAIRS - 1)
        def _():
            dx, dg = _rms_bwd(dql_ref[0], ql_ref[0], gq_ref[...])
            dql_ref[0] = dx
            dgq_ref[...] += dg
            dx, dg = _rms_bwd(dkvl_ref[0], kvl_ref[0], gkv_ref[...])
            dkvl_ref[0] = dx
            dgkv_ref[...] += dg

    seq = lambda bi, p: (bi, 0, 0)
    pair = lambda bi, p: (p, 0, 0)

    def per_seq(w):
        return pl.BlockSpec((1, s, w), seq, pipeline_mode=pl.Buffered(1))

    return pl.pallas_call(
        body, name="mla_bwd", grid=(b, N_HEAD_PAIRS),
        in_specs=[pl.BlockSpec((1, s, LANES), lambda bi, p: (bi, 0, p)),
                  per_seq(Q_LORA), per_seq(KV_LORA), per_seq(LANES), per_seq(LANES), per_seq(LANES), per_seq(LANES),
                  _const_spec((1, Q_LORA)), _const_spec((1, KV_LORA)),
                  pl.BlockSpec((1, Q_LORA, qw), pair), pl.BlockSpec((1, KV_LORA, kvw), pair)],
        out_specs=[pl.BlockSpec((1, s, Q_LORA), seq), pl.BlockSpec((1, s, KV_LORA), seq),
                   pl.BlockSpec((1, s, LANES), seq),
                   _const_spec((N_HEAD_PAIRS, Q_LORA, qw)), _const_spec((N_HEAD_PAIRS, KV_LORA, kvw)),
                   _const_spec((1, Q_LORA)), _const_spec((1, KV_LORA))],
        out_shape=[jax.ShapeDtypeStruct((b, s, Q_LORA), F32), jax.ShapeDtypeStruct((b, s, KV_LORA), F32),
                   jax.ShapeDtypeStruct((b, s, LANES), F32),
                   jax.ShapeDtypeStruct((N_HEAD_PAIRS, Q_LORA, qw), F32),
                   jax.ShapeDtypeStruct((N_HEAD_PAIRS, KV_LORA, kvw), F32),
                   jax.ShapeDtypeStruct((1, Q_LORA), F32), jax.ShapeDtypeStruct((1, KV_LORA), F32)],
        scratch_shapes=[pltpu.VMEM((s, HEAD_LANES), F32), pltpu.VMEM((s, HEAD_LANES), F32)],
        compiler_params=_params(VMEM_BIG),
    )(dy, ql, kvl, kr, cos_t, sa_t, sb_t, lw['q_norm_g'], lw['kv_norm_g'], lw['wq_pairs'], lw['wkv_pairs'])


COL_CHUNK = 256


def _merge(g_rnn, gate_a, gate_b, h, y_mla):
    ge, t = _gelu(g_rnn)
    sa, sb = _sigmoid(gate_a), _sigmoid(gate_b)
    y_rnn = ge * h
    return ge, t, sa, sb, y_rnn, sa * y_rnn + sb * y_mla


def mixout_fwd(x, rnn4, h, y_mla, lw, tm):
    n = x.shape[0]

    def body(x_ref, gr_ref, gta_ref, gtb_ref, h_ref, y_ref, w_ref, g_ref, b_ref, z_ref, o_ref):
        z = ALPHA * x_ref[...]
        for c in range(0, D, COL_CHUNK):
            cs = slice(c, c + COL_CHUNK)
            merged = _merge(gr_ref[:, cs], gta_ref[:, cs], gtb_ref[:, cs], h_ref[:, cs], y_ref[:, cs])[-1]
            z = z + _mm(merged, w_ref[cs, :])
        z_ref[...] = z
        o_ref[...] = _ln_stats(z)[0] * g_ref[...] + b_ref[...]

    row = lambda i: (i, 0)
    col = lambda j: (lambda i: (i, j))
    blk = pl.BlockSpec((tm, D), row)
    return pl.pallas_call(
        body, name="mixout_fwd", grid=(n // tm,),
        in_specs=[blk, pl.BlockSpec((tm, D), col(1)), pl.BlockSpec((tm, D), col(2)), pl.BlockSpec((tm, D), col(3)),
                  blk, blk, _resident((D, D)), _const_spec((1, D)), _const_spec((1, D))],
        out_specs=[blk, blk],
        out_shape=[jax.ShapeDtypeStruct((n, D), F32), jax.ShapeDtypeStruct((n, D), F32)],
        compiler_params=_params(VMEM_BIG),
    )(x, rnn4, rnn4, rnn4, h, y_mla, lw['w_out'], lw['ln1_g'], lw['ln1_b'])


def mixout_bwd(dx1, z1, rnn4, h, y_mla, lw, tm):
    n = dx1.shape[0]

    def body(d_ref, z_ref, gr_ref, gta_ref, gtb_ref, h_ref, y_ref, w_ref, g_ref,
             dz_ref, dh_ref, dy_ref, dg3_ref, dw_ref, dg_ref, db_ref):
        @pl.when(pl.program_id(0) == 0)
        def _():
            for r in (dw_ref, dg_ref, db_ref):
                r[...] = jnp.zeros_like(r)

        dz, dg, db = _ln_bwd(d_ref[...], z_ref[...], g_ref[...])
        dz_ref[...] = dz
        dg_ref[...] += dg
        db_ref[...] += db
        dzb = dz.astype(MXU_DTYPE)
        for c in range(0, D, COL_CHUNK):
            cs = slice(c, c + COL_CHUNK)
            g_rnn, h, y_mla = gr_ref[:, cs], h_ref[:, cs], y_ref[:, cs]
            ge, t, sa, sb, y_rnn, merged = _merge(g_rnn, gta_ref[:, cs], gtb_ref[:, cs], h, y_mla)
            dw_ref[cs, :] += _mm_tn(merged, dzb)
            dm = _mm_nt(dzb, w_ref[cs, :])
            dy_rnn = dm * sa
            dy_ref[:, cs] = dm * sb
            dh_ref[:, cs] = dy_rnn * ge
            dg3_ref[:, c:c + COL_CHUNK] = dy_rnn * h * _gelu_grad(g_rnn, t)
            dg3_ref[:, D + c:D + c + COL_CHUNK] = dm * y_rnn * sa * (1.0 - sa)
            dg3_ref[:, 2 * D + c:2 * D + c + COL_CHUNK] = dm * y_mla * sb * (1.0 - sb)

    row = lambda i: (i, 0)
    col = lambda j: (lambda i: (i, j))
    blk = pl.BlockSpec((tm, D), row)
    vec = jax.ShapeDtypeStruct((1, D), F32)
    act = jax.ShapeDtypeStruct((n, D), F32)
    return pl.pallas_call(
        body, name="mixout_bwd", grid=(n // tm,),
        in_specs=[blk, blk, pl.BlockSpec((tm, D), col(1)), pl.BlockSpec((tm, D), col(2)),
                  pl.BlockSpec((tm, D), col(3)), blk, blk, _resident((D, D)), _const_spec((1, D))],
        out_specs=[blk, blk, blk, pl.BlockSpec((tm, 3 * D), row), _const_spec((D, D)), _const_spec((1, D)),
                   _const_spec((1, D))],
        out_shape=[act, act, act, jax.ShapeDtypeStruct((n, 3 * D), F32), jax.ShapeDtypeStruct((D, D), F32), vec, vec],
        compiler_params=_params(VMEM_BIG),
    )(dx1, z1, rnn4, rnn4, rnn4, h, y_mla, lw['w_out'], lw['ln1_g'])


FFN_CHUNK = 512


def _conv3(u, halo, cs, fcw_ref, fcb_ref):
    hc = fcb_ref[:, cs]
    for k in range(FFN_CONV_W):
        hc = hc + fcw_ref[k:k + 1, cs] * _shift_down(u, halo, FFN_CONV_W - 1 - k, 0)
    return hc


def ffn_fwd(x1, lw, b, s, ts):
    ns = s // ts
    n = b * s

    def body(x_ref, wu_ref, fcw_ref, fcb_ref, wd_ref, g_ref, b_ref, up_ref, z_ref, o_ref, halo_sc):
        @pl.when(pl.program_id(1) == 0)
        def _():
            halo_sc[...] = jnp.zeros_like(halo_sc)

        x = x_ref[...]
        xb = x.astype(MXU_DTYPE)
        z = ALPHA * x
        for c in range(0, D_FF, FFN_CHUNK):
            gs, vs = slice(c, c + FFN_CHUNK), slice(D_FF + c, D_FF + c + FFN_CHUNK)
            ug, uv = _mm(xb, wu_ref[:, gs]), _mm(xb, wu_ref[:, vs])
            up_ref[:, gs] = ug
            up_ref[:, vs] = uv
            hg = _conv3(ug, halo_sc[:, gs], gs, fcw_ref, fcb_ref)
            hv = _conv3(uv, halo_sc[:, vs], vs, fcw_ref, fcb_ref)
            halo_sc[:, gs] = ug[ts - SUBLANES:, :]
            halo_sc[:, vs] = uv[ts - SUBLANES:, :]
            z = z + _mm(_gelu(hg)[0] * hv, wd_ref[c:c + FFN_CHUNK, :])
        z_ref[...] = z
        o_ref[...] = _ln_stats(z)[0] * g_ref[...] + b_ref[...]

    row = lambda bi, i: (bi * ns + i, 0)
    blk = pl.BlockSpec((ts, D), row)
    return pl.pallas_call(
        body, name="ffn_fwd", grid=(b, ns),
        in_specs=[blk, _resident((D, 2 * D_FF)), _const_spec((FFN_CONV_W, 2 * D_FF)), _const_spec((1, 2 * D_FF)),
                  _resident((D_FF, D)), _const_spec((1, D)), _const_spec((1, D))],
        out_specs=[pl.BlockSpec((ts, 2 * D_FF), row), blk, blk],
        out_shape=[jax.ShapeDtypeStruct((n, 2 * D_FF), F32), jax.ShapeDtypeStruct((n, D), F32),
                   jax.ShapeDtypeStruct((n, D), F32)],
        scratch_shapes=[pltpu.VMEM((SUBLANES, 2 * D_FF), F32)],
        compiler_params=_params(VMEM_BIG),
    )(x1, lw['w_up'], lw['ffn_conv_w'], lw['ffn_conv_b'], lw['w_down'], lw['ln2_g'], lw['ln2_b'])


def ffn_bwd(dx2, z2, up, lw, b, s, ts):
    ns = s // ts
    n = b * s
    hb = ts // SUBLANES

    def body(d_ref, z_ref, up_ref, uph_ref, fcw_ref, fcb_ref, wd_ref, g_ref,
             dz_ref, dup_ref, act_ref, dfcw_ref, dfcb_ref, dg_ref, db_ref, nhalo_sc):
        bi, i = pl.program_id(0), pl.program_id(1)

        @pl.when((bi == 0) & (i == 0))
        def _():
            for r in (dfcw_ref, dfcb_ref, dg_ref, db_ref):
                r[...] = jnp.zeros_like(r)

        @pl.when(i == 0)
        def _():
            nhalo_sc[...] = jnp.zeros_like(nhalo_sc)

        dz, dg, db = _ln_bwd(d_ref[...], z_ref[...], g_ref[...])
        dz_ref[...] = dz
        dg_ref[...] += dg
        db_ref[...] += db
        dzb = dz.astype(MXU_DTYPE)
        keep = jnp.where(i == ns - 1, 0.0, 1.0)
        for c in range(0, D_FF, FFN_CHUNK):
            gs, vs = slice(c, c + FFN_CHUNK), slice(D_FF + c, D_FF + c + FFN_CHUNK)
            ug, uv = up_ref[:, gs], up_ref[:, vs]
            hg_halo, hv_halo = uph_ref[:, gs] * keep, uph_ref[:, vs] * keep
            hg = _conv3(ug, hg_halo, gs, fcw_ref, fcb_ref)
            hv = _conv3(uv, hv_halo, vs, fcw_ref, fcb_ref)
            ge, t = _gelu(hg)
            act_ref[:, gs] = (ge * hv).astype(MXU_DTYPE)
            dact = _mm_nt(dzb, wd_ref[c:c + FFN_CHUNK, :])
            for cs, u, halo, dhc in ((gs, ug, hg_halo, dact * hv * _gelu_grad(hg, t)), (vs, uv, hv_halo, dact * ge)):
                dfcb_ref[:, cs] += jnp.sum(dhc, axis=0, keepdims=True)
                nhalo = nhalo_sc[:, cs]
                dup = jnp.zeros_like(dhc)
                for k in range(FFN_CONV_W):
                    sft = FFN_CONV_W - 1 - k
                    dfcw_ref[k:k + 1, cs] += jnp.sum(dhc * _shift_down(u, halo, sft, 0), axis=0, keepdims=True)
                    dup = dup + fcw_ref[k:k + 1, cs] * _shift_up(dhc, nhalo, sft, 0)
                dup_ref[:, cs] = dup.astype(MXU_DTYPE)
                nhalo_sc[:, cs] = dhc[:SUBLANES, :]

    row = lambda bi, i: (bi * ns + (ns - 1 - i), 0)
    halo = lambda bi, i: (jnp.maximum((bi * ns + (ns - 1 - i)) * hb - 1, 0), 0)
    blk = pl.BlockSpec((ts, D), row)
    wide = pl.BlockSpec((ts, 2 * D_FF), row)
    return pl.pallas_call(
        body, name="ffn_bwd", grid=(b, ns),
        in_specs=[blk, blk, wide, pl.BlockSpec((SUBLANES, 2 * D_FF), halo),
                  _const_spec((FFN_CONV_W, 2 * D_FF)), _const_spec((1, 2 * D_FF)), _resident((D_FF, D)),
                  _const_spec((1, D))],
        out_specs=[blk, wide, pl.BlockSpec((ts, D_FF), row), _const_spec((FFN_CONV_W, 2 * D_FF)),
                   _const_spec((1, 2 * D_FF)), _const_spec((1, D)), _const_spec((1, D))],
        out_shape=[jax.ShapeDtypeStruct((n, D), F32), jax.ShapeDtypeStruct((n, 2 * D_FF), MXU_DTYPE),
                   jax.ShapeDtypeStruct((n, D_FF), MXU_DTYPE), jax.ShapeDtypeStruct((FFN_CONV_W, 2 * D_FF), F32),
                   jax.ShapeDtypeStruct((1, 2 * D_FF), F32), jax.ShapeDtypeStruct((1, D), F32),
                   jax.ShapeDtypeStruct((1, D), F32)],
        scratch_shapes=[pltpu.VMEM((SUBLANES, 2 * D_FF), F32)],
        compiler_params=_params(VMEM_BIG),
    )(dx2, z2, up, up, lw['ffn_conv_w'], lw['ffn_conv_b'], lw['w_down'], lw['ln2_g'])


def loss_head(y, target, tm):
    n = y.shape[0]

    def body(y_ref, t_ref, l_ref, d_ref):
        @pl.when(pl.program_id(0) == 0)
        def _():
            l_ref[...] = jnp.zeros_like(l_ref)

        err = y_ref[...] - t_ref[...]
        d_ref[...] = err * (1.0 / D)
        part = jnp.sum(jnp.sum(err * err, axis=-1, keepdims=True), axis=0, keepdims=True)
        l_ref[...] += jnp.broadcast_to(part * (0.5 / D), l_ref.shape)

    row = lambda i: (i, 0)
    return pl.pallas_call(
        body, name="loss_head", grid=(n // tm,),
        in_specs=[pl.BlockSpec((tm, D), row), pl.BlockSpec((tm, D), row)],
        out_specs=[_const_spec((1, LANES)), pl.BlockSpec((tm, D), row)],
        out_shape=[jax.ShapeDtypeStruct((1, LANES), F32), jax.ShapeDtypeStruct((n, D), F32)],
    )(y, target)


def adamw(parts, w, m, v, tr):
    rows, cols = w.shape
    c1 = 1.0 - ADAM_B1 ** ADAM_STEP
    c2 = 1.0 - ADAM_B2 ** ADAM_STEP

    def body(p_ref, w_ref, m_ref, v_ref, g_ref, d_ref, mo_ref, vo_ref):
        g = p_ref[0]
        for i in range(1, N_DEV):
            g = g + p_ref[i]
        mn = ADAM_B1 * m_ref[...] + (1.0 - ADAM_B1) * g
        vn = ADAM_B2 * v_ref[...] + (1.0 - ADAM_B2) * (g * g)
        g_ref[...] = g
        mo_ref[...] = mn
        vo_ref[...] = vn
        d_ref[...] = -ADAM_LR * ((mn / c1) / (jnp.sqrt(vn / c2) + ADAM_EPS) + ADAM_WD * w_ref[...])

    row = lambda i: (i, 0)
    blk = pl.BlockSpec((tr, cols), row)
    out = jax.ShapeDtypeStruct((rows, cols), F32)
    return pl.pallas_call(
        body, name="adamw", grid=(rows // tr,),
        in_specs=[pl.BlockSpec((N_DEV, tr, cols), lambda i: (0, i, 0)), blk, blk, blk],
        out_specs=[blk, blk, blk, blk],
        out_shape=[out, out, out, out],
        compiler_params=_params(VMEM_BIG),
    )(parts, w, m, v)


def exchange(src, per_peer, name):
    blk = src.shape[1:] if per_peer else src.shape

    def body(src_ref, out_ref, send_sems, recv_sems, local_sem):
        x, y, c = lax.axis_index("x"), lax.axis_index("y"), lax.axis_index("c")
        me = 4 * x + 2 * y + c
        local = pltpu.make_async_copy(src_ref.at[me] if per_peer else src_ref, out_ref.at[me], local_sem)
        local.start()
        copies = []
        for k in range(1, N_DEV):
            px = 1 - x if k & 4 else x
            py = 1 - y if k & 2 else y
            pc = 1 - c if k & 1 else c
            pid = 4 * px + 2 * py + pc
            cp = pltpu.make_async_remote_copy(
                src_ref=src_ref.at[pid] if per_peer else src_ref, dst_ref=out_ref.at[me],
                send_sem=send_sems.at[k - 1], recv_sem=recv_sems.at[k - 1],
                device_id=(px, py, pc), device_id_type=pl.DeviceIdType.MESH)
            cp.start()
            copies.append(cp)
        for cp in copies:
            cp.wait()
        local.wait()

    return pl.pallas_call(
        body, name=name,
        in_specs=[pl.BlockSpec(memory_space=pl.ANY)],
        out_specs=pl.BlockSpec(memory_space=pl.ANY),
        out_shape=jax.ShapeDtypeStruct((N_DEV,) + tuple(blk), src.dtype),
        scratch_shapes=[pltpu.SemaphoreType.DMA((N_DEV - 1,)), pltpu.SemaphoreType.DMA((N_DEV - 1,)),
                        pltpu.SemaphoreType.DMA],
    )(src)


def _permute_w_in(w):
    o = [0, D, 2 * D, 2 * D + Q_LORA, 2 * D + Q_LORA + KV_LORA, 2 * D + Q_LORA + KV_LORA + QK_ROPE,
         3 * D + Q_LORA + KV_LORA + QK_ROPE, IN_WIDTH]
    xr, gr, qlat, kvl, kr, ga, gb = [w[:, o[i]:o[i + 1]] for i in range(7)]
    z = lambda c: jnp.zeros((w.shape[0], c), w.dtype)
    return jnp.concatenate([xr, gr, ga, gb, qlat, kvl, z(QK_NOPE), kr, z(HEAD_LANES - QK_NOPE - QK_ROPE)], axis=1)


def _unpermute_dw_in(dw):
    o = 4 * D
    k0 = o + Q_LORA + KV_LORA + QK_NOPE
    return jnp.concatenate([dw[:, 0:2 * D], dw[:, o:o + Q_LORA + KV_LORA], dw[:, k0:k0 + QK_ROPE],
                            dw[:, 2 * D:4 * D]], axis=1)


def _pair_wq(w):
    w = w.reshape(Q_LORA, N_HEADS, QK_NOPE + QK_ROPE)
    w = jnp.pad(w, ((0, 0), (0, 0), (0, HEAD_LANES - QK_NOPE - QK_ROPE)))
    return w.reshape(Q_LORA, N_HEAD_PAIRS, HEADS_PER_STEP * HEAD_LANES).transpose(1, 0, 2)


def _unpair_dwq(dw):
    dw = dw.transpose(1, 0, 2).reshape(Q_LORA, N_HEADS, HEAD_LANES)
    return dw[:, :, :QK_NOPE + QK_ROPE].reshape(Q_LORA, N_HEADS * (QK_NOPE + QK_ROPE))


def _pair_wkv(w):
    w = w.reshape(KV_LORA, N_HEAD_PAIRS, HEADS_PER_STEP, QK_NOPE + V_HEAD)
    kn, vv = w[..., :QK_NOPE], w[..., QK_NOPE:]
    z = jnp.zeros_like(kn[:, :, 0])
    out = jnp.concatenate([kn[:, :, 0], z, kn[:, :, 1], z, vv[:, :, 0], z, z, vv[:, :, 1]], axis=-1)
    return out.transpose(1, 0, 2)


def _unpair_dwkv(dw):
    dw = dw.transpose(1, 0, 2)
    h0 = jnp.concatenate([dw[..., 0:64], dw[..., 256:320]], axis=-1)
    h1 = jnp.concatenate([dw[..., 128:192], dw[..., 448:512]], axis=-1)
    return jnp.stack([h0, h1], axis=2).reshape(KV_LORA, N_HEADS * (QK_NOPE + V_HEAD))


def _group_gate(w):
    per = MXU_GROUP // RNN_BLOCK
    w = w.reshape(N_GATE_GROUPS, per, RNN_BLOCK, RNN_BLOCK)
    eye = jnp.eye(per, dtype=w.dtype)
    return jnp.einsum('gpij,pq->gpiqj', w, eye).reshape(N_GATE_GROUPS, MXU_GROUP, MXU_GROUP)


def _ungroup_dgate(dw):
    per = MXU_GROUP // RNN_BLOCK
    dw = dw.reshape(N_GATE_GROUPS, per, RNN_BLOCK, per, RNN_BLOCK)
    return jnp.stack([dw[:, p, :, p, :] for p in range(per)], axis=1).reshape(N_RNN_BLOCKS, RNN_BLOCK, RNN_BLOCK)


def _rope_tables(positions):
    inv_freq = ROPE_THETA ** (-jnp.arange(0, QK_ROPE, 2, dtype=F32) / QK_ROPE)
    ang = positions.astype(F32)[..., None] * inv_freq
    cos, sin = jnp.cos(ang), jnp.sin(ang)
    one, zero = jnp.ones_like(cos), jnp.zeros_like(cos)
    nope = lambda v: jnp.concatenate([v] * (QK_NOPE // (QK_ROPE // 2)), axis=-1)
    tail = jnp.concatenate([zero, zero], axis=-1)
    cos_t = jnp.concatenate([nope(one), cos, cos, tail], axis=-1)
    sa_t = jnp.concatenate([nope(zero), -sin, zero, tail], axis=-1)
    sb_t = jnp.concatenate([nope(zero), zero, sin, tail], axis=-1)
    return cos_t, sa_t, sb_t


def _layer_weights(full, l):
    row = lambda name: full[name][l][None, :]
    return {
        'w_in_p': _permute_w_in(full['w_in'][l]),
        'conv_w': full['conv_w'][l], 'conv_b': row('conv_b'),
        'wgx': _group_gate(full['gx_w'][l]).astype(MXU_DTYPE), 'gx_b': row('gx_b'),
        'wga': _group_gate(full['ga_w'][l]).astype(MXU_DTYPE), 'ga_b': row('ga_b'),
        'lru_lambda': row('lru_lambda'),
        'q_norm_g': row('q_norm_g'), 'kv_norm_g': row('kv_norm_g'),
        'wq_pairs': _pair_wq(full['w_uq'][l]), 'wkv_pairs': _pair_wkv(full['w_ukv'][l]),
        'w_out': full['w_out'][l], 'ln1_g': row('ln1_g'), 'ln1_b': row('ln1_b'),
        'w_up': full['w_up'][l], 'ffn_conv_w': full['ffn_conv_w'][l], 'ffn_conv_b': row('ffn_conv_b'),
        'w_down': full['w_down'][l], 'ln2_g': row('ln2_g'), 'ln2_b': row('ln2_b'),
    }


TM = 256
TS_RNN = 128
TS_FFN = 256
TQ = 256
TN_DW = 512


def local_step(x, positions, loss_target, full):
    b, s, _ = x.shape
    n = b * s
    tabs = _rope_tables(positions)
    lws = [_layer_weights(full, l) for l in range(DEPTH)]
    saved = []
    xc = x.reshape(n, D)
    for l in range(DEPTH):
        lw = lws[l]
        rnn4, ql, kvl, kr = inproj_fwd(xc, lw['w_in_p'], TM)
        rnn4_3 = rnn4.reshape(b, s, 4 * D)
        h = rnn_fwd(rnn4_3, lw, b, s, TS_RNN)
        lat = (ql.reshape(b, s, Q_LORA), kvl.reshape(b, s, KV_LORA), kr.reshape(b, s, LANES))
        y_mla = mla_fwd(*lat, tabs, lw, b, s, TQ)
        z1, x1 = mixout_fwd(xc, rnn4, h.reshape(n, D), y_mla.reshape(n, D), lw, TM)
        up, z2, x2 = ffn_fwd(x1, lw, b, s, TS_FFN)
        saved.append((xc, rnn4, lat, h, y_mla, z1, x1, up, z2))
        xc = x2
    loss_part, dx = loss_head(xc, loss_target.reshape(n, D), TM)

    grads = {name: [None] * DEPTH for name in WEIGHT_NAMES}
    for l in reversed(range(DEPTH)):
        lw = lws[l]
        x0, rnn4, lat, h, y_mla, z1, x1, up, z2 = saved[l]
        dz2, dup, act, dfcw, dfcb, dg2, db2 = ffn_bwd(dx, z2, up, lw, b, s, TS_FFN)
        dx1 = matmul_dx(dup, lw['w_up'], dz2, ALPHA, TM, "ffn_up_dx")
        dw_up = matmul_dw(x1, dup, TN_DW, 2 * D_FF // 3, "ffn_up_dw")
        dw_down = matmul_dw(act, dz2, TN_DW, D // 2, "ffn_down_dw")
        dz1, dh, dy_mla, dg3, dw_out, dg1, db1 = mixout_bwd(dx1, z1, rnn4, h.reshape(n, D), y_mla.reshape(n, D),
                                                           lw, TM)
        dql, dkvl, dkr, dwq, dwkv, dgq, dgkv = mla_bwd(dy_mla.reshape(b, s, D), *lat, tabs, lw, b, s, TQ)
        dxr, dcw, dcb, dwgx, dbgx, dwga, dbga, dlam = rnn_bwd(dh.reshape(b, s, D), rnn4.reshape(b, s, 4 * D), h,
                                                              lw, b, s, TS_RNN)
        dx, dproj = inproj_bwd(dxr.reshape(n, D), dg3, dql.reshape(n, Q_LORA), dkvl.reshape(n, KV_LORA),
                               dkr.reshape(n, LANES), dz1, lw['w_in_p'], TM)
        dw_in_p = matmul_dw(x0, dproj, TN_DW, PROJ_W // 2, "inproj_dw")
        vec = lambda a: a.reshape(-1)
        for name, val in (
                ('w_in', _unpermute_dw_in(dw_in_p)), ('conv_w', dcw), ('conv_b', vec(dcb)),
                ('gx_w', _ungroup_dgate(dwgx)), ('gx_b', vec(dbgx)), ('ga_w', _ungroup_dgate(dwga)),
                ('ga_b', vec(dbga)), ('lru_lambda', vec(dlam)), ('q_norm_g', vec(dgq)), ('w_uq', _unpair_dwq(dwq)),
                ('kv_norm_g', vec(dgkv)), ('w_ukv', _unpair_dwkv(dwkv)), ('w_out', dw_out), ('ln1_g', vec(dg1)),
                ('ln1_b', vec(db1)), ('w_up', dw_up), ('ffn_conv_w', dfcw), ('ffn_conv_b', vec(dfcb)),
                ('w_down', dw_down), ('ln2_g', vec(dg2)), ('ln2_b', vec(db2))):
            grads[name][l] = val
    return loss_part, dx.reshape(b, s, D), {k: jnp.stack(v) for k, v in grads.items()}


FLAT_COLS = 1024
FLAT_ROW_TILE = 256


def _pad_rows(flat, lead):
    size = flat.shape[-1]
    chunk = FLAT_COLS * FLAT_ROW_TILE
    padded = -(-size // chunk) * chunk
    flat = jnp.pad(flat, [(0, 0)] * lead + [(0, padded - size)])
    return flat.reshape(flat.shape[:lead] + (padded // FLAT_COLS, FLAT_COLS))


def _as_payload(a, is_matmul):
    if is_matmul or MXU_DTYPE == F32:
        return a.astype(MXU_DTYPE).reshape(-1)
    return lax.bitcast_convert_type(a, MXU_DTYPE).reshape(-1)


def _from_payload(flat, shape, is_matmul):
    if is_matmul or MXU_DTYPE == F32:
        return flat.reshape(shape)
    return lax.bitcast_convert_type(flat.reshape(shape + (2,)), F32)


def _unshard(stacked, name):
    axis, _ = SHARDED[name]
    return jnp.concatenate([stacked[i] for i in range(N_DEV)], axis=axis)


def _shard_pieces(g, name):
    axis, _ = SHARDED[name]
    return jnp.stack([p.reshape(-1) for p in jnp.split(g, N_DEV, axis=axis)])


def gather_weights(w):
    flats, meta = [], []
    for name in SHARDED:
        is_mm = name in MATMUL_WEIGHTS
        f = _as_payload(w[name], is_mm)
        flats.append(f)
        meta.append((name, f.shape[0], w[name].shape, is_mm))
    payload = _pad_rows(jnp.concatenate(flats), 0)
    got = exchange(payload, False, "gather_weights").reshape(N_DEV, -1)
    full, off = {}, 0
    for name, size, shape, is_mm in meta:
        stacked = _from_payload(got[:, off:off + size], (N_DEV,) + shape, is_mm)
        full[name] = _unshard(stacked, name)
        off += size
    for name in WEIGHT_NAMES:
        if name not in SHARDED:
            full[name] = w[name]
    return full


def _flat_local(blocks):
    return _pad_rows(jnp.concatenate([blocks[name].reshape(-1) for name in WEIGHT_NAMES]), 0)


def _unflat_local(flat, like):
    flat = flat.reshape(-1)
    out, off = {}, 0
    for name in WEIGHT_NAMES:
        size = like[name].size
        out[name] = flat[off:off + size].reshape(like[name].shape)
        off += size
    return out


def scatter_grads(grads):
    cols = []
    for name in WEIGHT_NAMES:
        g = grads[name]
        if name in SHARDED:
            cols.append(_shard_pieces(g, name))
        else:
            cols.append(jnp.broadcast_to(g.reshape(1, -1), (N_DEV, g.size)))
    send = _pad_rows(jnp.concatenate(cols, axis=1), 1)
    return exchange(send, True, "scatter_grads")


def kernel(x, positions, w_in, conv_w, conv_b, gx_w, gx_b, ga_w, ga_b, lru_lambda, q_norm_g, w_uq, kv_norm_g, w_ukv, w_out, ln1_g, ln1_b, w_up, ffn_conv_w, ffn_conv_b, w_down, ln2_g, ln2_b, loss_target, m_w_in, m_conv_w, m_conv_b, m_gx_w, m_gx_b, m_ga_w, m_ga_b, m_lru_lambda, m_q_norm_g, m_w_uq, m_kv_norm_g, m_w_ukv, m_w_out, m_ln1_g, m_ln1_b, m_w_up, m_ffn_conv_w, m_ffn_conv_b, m_w_down, m_ln2_g, m_ln2_b, v_w_in, v_conv_w, v_conv_b, v_gx_w, v_gx_b, v_ga_w, v_ga_b, v_lru_lambda, v_q_norm_g, v_w_uq, v_kv_norm_g, v_w_ukv, v_w_out, v_ln1_g, v_ln1_b, v_w_up, v_ffn_conv_w, v_ffn_conv_b, v_w_down, v_ln2_g, v_ln2_b):
    w = dict(zip(WEIGHT_NAMES, (w_in, conv_w, conv_b, gx_w, gx_b, ga_w, ga_b, lru_lambda, q_norm_g, w_uq, kv_norm_g,
                                w_ukv, w_out, ln1_g, ln1_b, w_up, ffn_conv_w, ffn_conv_b, w_down, ln2_g, ln2_b)))
    m = dict(zip(WEIGHT_NAMES, (m_w_in, m_conv_w, m_conv_b, m_gx_w, m_gx_b, m_ga_w, m_ga_b, m_lru_lambda,
                                m_q_norm_g, m_w_uq, m_kv_norm_g, m_w_ukv, m_w_out, m_ln1_g, m_ln1_b, m_w_up,
                                m_ffn_conv_w, m_ffn_conv_b, m_w_down, m_ln2_g, m_ln2_b)))
    v = dict(zip(WEIGHT_NAMES, (v_w_in, v_conv_w, v_conv_b, v_gx_w, v_gx_b, v_ga_w, v_ga_b, v_lru_lambda,
                                v_q_norm_g, v_w_uq, v_kv_norm_g, v_w_ukv, v_w_out, v_ln1_g, v_ln1_b, v_w_up,
                                v_ffn_conv_w, v_ffn_conv_b, v_w_down, v_ln2_g, v_ln2_b)))
    full = gather_weights(w)
    loss_part, grad_x, grads = local_step(x, positions, loss_target, full)
    parts = scatter_grads(grads)
    g, delta, new_m, new_v = adamw(parts, _flat_local(w), _flat_local(m), _flat_local(v), FLAT_ROW_TILE)
    g, delta, new_m, new_v = (_unflat_local(a, w) for a in (g, delta, new_m, new_v))
    loss = lax.psum(loss_part[0, 0], ("x", "y", "c"))
    return (loss, grad_x, *[g[k] for k in WEIGHT_NAMES], *[delta[k] for k in WEIGHT_NAMES],
            *[new_m[k] for k in WEIGHT_NAMES], *[new_v[k] for k in WEIGHT_NAMES])
```

```python
import functools
import math

import jax
import jax.numpy as jnp
from jax import lax
from jax.experimental import pallas as pl
from jax.experimental.pallas import tpu as pltpu

F32 = jnp.float32
MXU_DTYPE = jnp.bfloat16

D = 1024
DEPTH = 4
N_RNN_BLOCKS = 16
RNN_BLOCK = 64
CONV_W = 4
LRU_C = 8.0
N_HEADS = 16
QK_NOPE = 64
QK_ROPE = 32
V_HEAD = 64
Q_LORA = 384
KV_LORA = 256
ROPE_THETA = 10000.0
D_FF = 3 * D
FFN_CONV_W = 3
IN_WIDTH = 2 * D + Q_LORA + KV_LORA + QK_ROPE + 2 * D
ALPHA = (2 * DEPTH) ** 0.25
EPS = 1e-6
NEG_INF = -1e30
ATT_SCALE = (QK_NOPE + QK_ROPE) ** -0.5
GELU_C = math.sqrt(2.0 / math.pi)

ADAM_LR = 0.001
ADAM_B1 = 0.9
ADAM_B2 = 0.999
ADAM_EPS = 1e-08
ADAM_WD = 0.01
ADAM_STEP = 10

N_DEV = 8
LANES = 128
SUBLANES = 8
MXU_GROUP = 256
N_GATE_GROUPS = D // MXU_GROUP
HEADS_PER_STEP = 2
N_HEAD_PAIRS = N_HEADS // HEADS_PER_STEP
HEAD_LANES = 128
PROJ_W = 4 * D + Q_LORA + KV_LORA + LANES
VMEM_BIG = 56 * 2 ** 20

WEIGHT_NAMES = ['w_in', 'conv_w', 'conv_b', 'gx_w', 'gx_b', 'ga_w', 'ga_b', 'lru_lambda', 'q_norm_g', 'w_uq',
                'kv_norm_g', 'w_ukv', 'w_out', 'ln1_g', 'ln1_b', 'w_up', 'ffn_conv_w', 'ffn_conv_b', 'w_down',
                'ln2_g', 'ln2_b']
SHARDED = {
    'w_in': (2, (DEPTH, D, IN_WIDTH)),
    'conv_w': (2, (DEPTH, CONV_W, D)),
    'w_uq': (2, (DEPTH, Q_LORA, N_HEADS * (QK_NOPE + QK_ROPE))),
    'w_ukv': (2, (DEPTH, KV_LORA, N_HEADS * (QK_NOPE + V_HEAD))),
    'w_out': (1, (DEPTH, D, D)),
    'w_up': (2, (DEPTH, D, 2 * D_FF)),
    'ffn_conv_w': (2, (DEPTH, FFN_CONV_W, 2 * D_FF)),
    'w_down': (1, (DEPTH, D_FF, D)),
}
MATMUL_WEIGHTS = ('w_in', 'w_uq', 'w_ukv', 'w_out', 'w_up', 'w_down')


def _mm(a, b):
    return jnp.dot(a.astype(MXU_DTYPE), b.astype(MXU_DTYPE), preferred_element_type=F32)


def _mm_tn(a, b):
    return lax.dot_general(a.astype(MXU_DTYPE), b.astype(MXU_DTYPE), (((0,), (0,)), ((), ())),
                           preferred_element_type=F32)


def _mm_nt(a, b):
    return lax.dot_general(a.astype(MXU_DTYPE), b.astype(MXU_DTYPE), (((1,), (1,)), ((), ())),
                           preferred_element_type=F32)


def _sigmoid(x):
    return 1.0 / (1.0 + jnp.exp(-x))


def _gelu(x):
    t = jnp.tanh(GELU_C * (x + 0.044715 * (x * x * x)))
    return 0.5 * x * (1.0 + t), t


def _gelu_grad(x, t):
    return 0.5 * (1.0 + t) + 0.5 * x * (1.0 - t * t) * (GELU_C * (1.0 + 3.0 * 0.044715 * (x * x)))


def _neg_expm1(y):
    series = -y * (1.0 + 0.5 * y * (1.0 + (y / 3.0) * (1.0 + 0.25 * y * (1.0 + 0.2 * y))))
    return jnp.where(y > -0.05, series, 1.0 - jnp.exp(y))


def _ln_stats(z):
    mu = jnp.mean(z, axis=-1, keepdims=True)
    zc = z - mu
    var = jnp.mean(zc * zc, axis=-1, keepdims=True)
    r = lax.rsqrt(var + EPS)
    return zc * r, r


def _ln_bwd(dy, z, g):
    xhat, r = _ln_stats(z)
    dxh = dy * g
    dz = r * (dxh - jnp.mean(dxh, axis=-1, keepdims=True)
              - xhat * jnp.mean(dxh * xhat, axis=-1, keepdims=True))
    return dz, jnp.sum(dy * xhat, axis=0, keepdims=True), jnp.sum(dy, axis=0, keepdims=True)


def _rms_stats(x):
    r = lax.rsqrt(jnp.mean(x * x, axis=-1, keepdims=True) + EPS)
    return x * r, r


def _rms_bwd(dy, x, g):
    xn, r = _rms_stats(x)
    dxn = dy * g
    dx = r * (dxn - xn * jnp.mean(dxn * xn, axis=-1, keepdims=True))
    return dx, jnp.sum(dy * xn, axis=0, keepdims=True)


def _shift_down(x, halo, s, axis):
    if s == 0:
        return x
    r = pltpu.roll(x, s, axis)
    hr = pltpu.roll(halo, s, axis)
    idx = lax.broadcasted_iota(jnp.int32, hr.shape, axis)
    head = lax.slice_in_dim(r, 0, SUBLANES, axis=axis)
    rest = lax.slice_in_dim(r, SUBLANES, x.shape[axis], axis=axis)
    return jnp.concatenate([jnp.where(idx < s, hr, head), rest], axis=axis)


def _shift_up(x, halo, s, axis):
    if s == 0:
        return x
    n = x.shape[axis]
    r = pltpu.roll(x, n - s, axis)
    hr = pltpu.roll(halo, SUBLANES - s, axis)
    idx = lax.broadcasted_iota(jnp.int32, hr.shape, axis)
    body = lax.slice_in_dim(r, 0, n - SUBLANES, axis=axis)
    tail = lax.slice_in_dim(r, n - SUBLANES, n, axis=axis)
    return jnp.concatenate([body, jnp.where(idx >= SUBLANES - s, hr, tail)], axis=axis)


def _const_spec(shape):
    nd = len(shape)
    return pl.BlockSpec(shape, lambda *_: (0,) * nd)


def _layer_spec(shape, l):
    nd = len(shape)
    return pl.BlockSpec((None,) + tuple(shape), lambda *_: (l,) + (0,) * nd)


def _resident(shape):
    nd = len(shape)
    return pl.BlockSpec(shape, lambda *_: (0,) * nd, pipeline_mode=pl.Buffered(1))


def _params(vmem=None):
    return pltpu.CompilerParams(vmem_limit_bytes=vmem)


def inproj_fwd(x, w_in_p, tm):
    n = x.shape[0]

    def body(x_ref, w_ref, rnn4_ref, ql_ref, kvl_ref, kr_ref):
        xb = x_ref[...].astype(MXU_DTYPE)
        for j in range(4):
            rnn4_ref[:, j * D:(j + 1) * D] = _mm(xb, w_ref[:, j * D:(j + 1) * D])
        o = 4 * D
        ql_ref[...] = _mm(xb, w_ref[:, o:o + Q_LORA])
        kvl_ref[...] = _mm(xb, w_ref[:, o + Q_LORA:o + Q_LORA + KV_LORA])
        kr_ref[...] = _mm(xb, w_ref[:, o + Q_LORA + KV_LORA:PROJ_W])

    row = lambda i: (i, 0)
    return pl.pallas_call(
        body, name="inproj_fwd", grid=(n // tm,),
        in_specs=[pl.BlockSpec((tm, D), row), _resident((D, PROJ_W))],
        out_specs=[pl.BlockSpec((tm, 4 * D), row), pl.BlockSpec((tm, Q_LORA), row),
                   pl.BlockSpec((tm, KV_LORA), row), pl.BlockSpec((tm, LANES), row)],
        out_shape=[jax.ShapeDtypeStruct((n, 4 * D), F32), jax.ShapeDtypeStruct((n, Q_LORA), F32),
                   jax.ShapeDtypeStruct((n, KV_LORA), F32), jax.ShapeDtypeStruct((n, LANES), F32)],
        compiler_params=_params(VMEM_BIG),
    )(x, w_in_p)


def inproj_bwd(dxr, dg3, dql, dkvl, dkr, dz1, w_in_p, tm):
    n = dz1.shape[0]

    def body(dxr_ref, dg3_ref, dql_ref, dkvl_ref, dkr_ref, dz_ref, w_ref, dx_ref, dp_ref):
        dp = jnp.concatenate([dxr_ref[...], dg3_ref[...], dql_ref[...], dkvl_ref[...], dkr_ref[...]],
                             axis=1).astype(MXU_DTYPE)
        dp_ref[...] = dp
        dx_ref[...] = ALPHA * dz_ref[...] + _mm_nt(dp, w_ref[...])

    row = lambda i: (i, 0)
    return pl.pallas_call(
        body, name="inproj_bwd", grid=(n // tm,),
        in_specs=[pl.BlockSpec((tm, D), row), pl.BlockSpec((tm, 3 * D), row), pl.BlockSpec((tm, Q_LORA), row),
                  pl.BlockSpec((tm, KV_LORA), row), pl.BlockSpec((tm, LANES), row), pl.BlockSpec((tm, D), row),
                  _resident((D, PROJ_W))],
        out_specs=[pl.BlockSpec((tm, D), row), pl.BlockSpec((tm, PROJ_W), row)],
        out_shape=[jax.ShapeDtypeStruct((n, D), F32), jax.ShapeDtypeStruct((n, PROJ_W), MXU_DTYPE)],
        compiler_params=_params(VMEM_BIG),
    )(dxr, dg3, dql, dkvl, dkr, dz1, w_in_p)


def matmul_dw(x, dy, tn, tmc, name):
    n, k = x.shape
    m = dy.shape[1]

    def body(x_ref, dy_ref, dw_ref):
        @pl.when(pl.program_id(1) == 0)
        def _():
            dw_ref[...] = jnp.zeros_like(dw_ref)
        dw_ref[...] += _mm_tn(x_ref[...], dy_ref[...])

    return pl.pallas_call(
        body, name=name, grid=(m // tmc, n // tn),
        in_specs=[pl.BlockSpec((tn, k), lambda j, i: (i, 0)), pl.BlockSpec((tn, tmc), lambda j, i: (i, j))],
        out_specs=pl.BlockSpec((k, tmc), lambda j, i: (0, j)),
        out_shape=jax.ShapeDtypeStruct((k, m), F32),
        compiler_params=_params(VMEM_BIG),
    )(x, dy)


def matmul_dx(dy, w, add, add_scale, tm, name):
    n, m = dy.shape
    k = w.shape[0]

    def body(dy_ref, w_ref, add_ref, dx_ref):
        dx_ref[...] = add_scale * add_ref[...] + _mm_nt(dy_ref[...], w_ref[...])

    row = lambda i: (i, 0)
    return pl.pallas_call(
        body, name=name, grid=(n // tm,),
        in_specs=[pl.BlockSpec((tm, m), row), _resident((k, m)), pl.BlockSpec((tm, k), row)],
        out_specs=pl.BlockSpec((tm, k), row),
        out_shape=jax.ShapeDtypeStruct((n, k), F32),
        compiler_params=_params(VMEM_BIG),
    )(dy, w, add)


def _group(g):
    return slice(g * MXU_GROUP, (g + 1) * MXU_GROUP)


def _rnn_gates(x, halo, g, l, cw_ref, cb_ref, wgx_ref, bgx_ref, wga_ref, bga_ref, lam_ref):
    b, ts, gw = x.shape
    sl = _group(g)
    lr = slice(l, l + 1)
    xc = cb_ref[lr, sl][None]
    for k in range(CONV_W):
        xc = xc + cw_ref[k:k + 1, sl][None] * _shift_down(x, halo, CONV_W - 1 - k, 1)
    xc2 = xc.reshape(b * ts, gw)
    xcb = xc2.astype(MXU_DTYPE)
    gx = _sigmoid(_mm(xcb, wgx_ref[g]) + bgx_ref[lr, sl])
    ga = _sigmoid(_mm(xcb, wga_ref[g]) + bga_ref[lr, sl])
    nl = -lam_ref[lr, sl]
    sp = jnp.maximum(nl, 0.0) + jnp.log1p(jnp.exp(-jnp.abs(nl)))
    log_a = (-LRU_C) * ga * sp
    a = jnp.exp(log_a)
    mult = jnp.sqrt(_neg_expm1(2.0 * log_a))
    return xc2, xcb, gx, ga, sp, a, mult


def rnn_fwd(rnn4, lw, l, b, s, ts):
    ns = s // ts

    def body(x_ref, cw_ref, cb_ref, wgx_ref, bgx_ref, wga_ref, bga_ref, lam_ref, h_ref,
             halo_sc, hstate_sc, a_sc, u_sc):
        @pl.when(pl.program_id(0) == 0)
        def _():
            halo_sc[...] = jnp.zeros_like(halo_sc)
            hstate_sc[...] = jnp.zeros_like(hstate_sc)

        for g in range(N_GATE_GROUPS):
            sl = _group(g)
            x = x_ref[:, :, sl]
            xc2, _, gx, _, _, a, mult = _rnn_gates(x, halo_sc[:, :, sl], g, l, cw_ref, cb_ref, wgx_ref, bgx_ref,
                                                   wga_ref, bga_ref, lam_ref)
            halo_sc[:, :, sl] = x[:, ts - SUBLANES:, :]
            a_sc[...] = a.reshape(b, ts, MXU_GROUP)
            u_sc[...] = (mult * gx * xc2).reshape(b, ts, MXU_GROUP)

            def step(t, h, sl=sl):
                h = a_sc[:, pl.ds(t, 1), :] * h + u_sc[:, pl.ds(t, 1), :]
                h_ref[:, pl.ds(t, 1), sl] = h
                return h

            hstate_sc[:, :, sl] = lax.fori_loop(0, ts, step, hstate_sc[:, :, sl], unroll=8)

    tile = lambda i: (0, i, 0)
    vecs = _const_spec((DEPTH, D))
    gates = _layer_spec((N_GATE_GROUPS, MXU_GROUP, MXU_GROUP), l)
    return pl.pallas_call(
        body, name="rnn_fwd", grid=(ns,),
        in_specs=[pl.BlockSpec((b, ts, D), tile), _const_spec((CONV_W, D)), vecs, gates, vecs, gates, vecs, vecs],
        out_specs=pl.BlockSpec((b, ts, D), tile),
        out_shape=jax.ShapeDtypeStruct((b, s, D), F32),
        scratch_shapes=[pltpu.VMEM((b, SUBLANES, D), F32), pltpu.VMEM((b, 1, D), F32),
                        pltpu.VMEM((b, ts, MXU_GROUP), F32), pltpu.VMEM((b, ts, MXU_GROUP), F32)],
        compiler_params=_params(VMEM_BIG),
    )(rnn4, lw['conv_w'], lw['conv_b'], lw['wgx'], lw['gx_b'], lw['wga'], lw['ga_b'], lw['lru_lambda'])


def rnn_bwd(dh, rnn4, h, lw, l, b, s, ts):
    ns = s // ts
    hb = ts // SUBLANES

    def body(dh_ref, x_ref, xh_ref, h_ref, hh_ref, cw_ref, cb_ref, wgx_ref, bgx_ref, wga_ref, bga_ref, lam_ref,
             dx_ref, dcw_ref, dcb_ref, dwgx_ref, dbgx_ref, dwga_ref, dbga_ref, dlam_ref,
             carry_sc, dxc_halo_sc, a_sc, delta_sc):
        i = pl.program_id(0)

        @pl.when(i == 0)
        def _():
            carry_sc[...] = jnp.zeros_like(carry_sc)
            dxc_halo_sc[...] = jnp.zeros_like(dxc_halo_sc)
            for r in (dcw_ref, dcb_ref, dwgx_ref, dbgx_ref, dwga_ref, dbga_ref, dlam_ref):
                r[...] = jnp.zeros_like(r)

        keep = jnp.where(i == ns - 1, 0.0, 1.0)
        for g in range(N_GATE_GROUPS):
            sl = _group(g)
            x = x_ref[:, :, sl]
            xhalo = xh_ref[:, :, sl] * keep
            xc2, xcb, gx, ga, sp, a, mult = _rnn_gates(x, xhalo, g, l, cw_ref, cb_ref, wgx_ref, bgx_ref, wga_ref,
                                                       bga_ref, lam_ref)
            a_sc[...] = a.reshape(b, ts, MXU_GROUP)

            def step(j, c, sl=sl):
                t = ts - 1 - j
                d = dh_ref[:, pl.ds(t, 1), sl] + c
                delta_sc[:, pl.ds(t, 1), :] = d
                return a_sc[:, pl.ds(t, 1), :] * d

            carry_sc[:, :, sl] = lax.fori_loop(0, ts, step, carry_sc[:, :, sl], unroll=8)

            delta = delta_sc[...].reshape(b * ts, MXU_GROUP)
            hprev = _shift_down(h_ref[:, :, sl], hh_ref[:, :, sl] * keep, 1, 1).reshape(b * ts, MXU_GROUP)
            dmult = delta * gx * xc2
            dl = delta * hprev * a - dmult * (a * a) / mult
            dga = dl * ((-LRU_C) * sp)
            dlam_ref[:, sl] += (jnp.sum(dl * ((-LRU_C) * ga), axis=0, keepdims=True)
                                * (-_sigmoid(-lam_ref[l:l + 1, sl])))
            dpa = dga * ga * (1.0 - ga)
            dpx = (delta * mult * xc2) * gx * (1.0 - gx)
            dbga_ref[:, sl] += jnp.sum(dpa, axis=0, keepdims=True)
            dbgx_ref[:, sl] += jnp.sum(dpx, axis=0, keepdims=True)
            dpab = dpa.astype(MXU_DTYPE)
            dpxb = dpx.astype(MXU_DTYPE)
            dwga_ref[g] += _mm_tn(xcb, dpab)
            dwgx_ref[g] += _mm_tn(xcb, dpxb)
            dxc2 = delta * mult * gx + _mm_nt(dpab, wga_ref[g]) + _mm_nt(dpxb, wgx_ref[g])
            dcb_ref[:, sl] += jnp.sum(dxc2, axis=0, keepdims=True)
            dxc = dxc2.reshape(b, ts, MXU_GROUP)
            nhalo = dxc_halo_sc[:, :, sl]
            dx = jnp.zeros_like(dxc)
            for k in range(CONV_W):
                sft = CONV_W - 1 - k
                xs = _shift_down(x, xhalo, sft, 1)
                dcw_ref[k:k + 1, sl] += jnp.sum((dxc * xs).reshape(b * ts, MXU_GROUP), axis=0, keepdims=True)
                dx = dx + cw_ref[k:k + 1, sl][None] * _shift_up(dxc, nhalo, sft, 1)
            dx_ref[:, :, sl] = dx
            dxc_halo_sc[:, :, sl] = dxc[:, :SUBLANES, :]

    tile = lambda i: (0, ns - 1 - i, 0)
    halo = lambda i: (0, jnp.maximum((ns - 1 - i) * hb - 1, 0), 0)
    gshape = (N_GATE_GROUPS, MXU_GROUP, MXU_GROUP)
    vecs = _const_spec((DEPTH, D))
    gates = _layer_spec(gshape, l)
    vec = jax.ShapeDtypeStruct((1, D), F32)
    return pl.pallas_call(
        body, name="rnn_bwd", grid=(ns,),
        in_specs=[pl.BlockSpec((b, ts, D), tile), pl.BlockSpec((b, ts, D), tile),
                  pl.BlockSpec((b, SUBLANES, D), halo), pl.BlockSpec((b, ts, D), tile),
                  pl.BlockSpec((b, SUBLANES, D), halo),
                  _const_spec((CONV_W, D)), vecs, gates, vecs, gates, vecs, vecs],
        out_specs=[pl.BlockSpec((b, ts, D), tile), _const_spec((CONV_W, D)), _const_spec((1, D)),
                   _const_spec(gshape), _const_spec((1, D)), _const_spec(gshape), _const_spec((1, D)),
                   _const_spec((1, D))],
        out_shape=[jax.ShapeDtypeStruct((b, s, D), F32), jax.ShapeDtypeStruct((CONV_W, D), F32), vec,
                   jax.ShapeDtypeStruct(gshape, F32), vec, jax.ShapeDtypeStruct(gshape, F32), vec, vec],
        scratch_shapes=[pltpu.VMEM((b, 1, D), F32), pltpu.VMEM((b, SUBLANES, D), F32),
                        pltpu.VMEM((b, ts, MXU_GROUP), F32), pltpu.VMEM((b, ts, MXU_GROUP), F32)],
        compiler_params=_params(VMEM_BIG),
    )(dh, rnn4, rnn4, h, h, lw['conv_w'], lw['conv_b'], lw['wgx'], lw['gx_b'], lw['wga'], lw['ga_b'],
      lw['lru_lambda'])


def _rope(x, cos, sa, sb):
    return x * cos + pltpu.roll(x, HEAD_LANES - QK_ROPE // 2, 1) * sa + pltpu.roll(x, QK_ROPE // 2, 1) * sb


def _unrope(d, cos, sa, sb):
    return d * cos + pltpu.roll(d * sa, QK_ROPE // 2, 1) + pltpu.roll(d * sb, HEAD_LANES - QK_ROPE // 2, 1)


def _scores(q_blk, keys):
    return _mm_nt(q_blk, keys) * ATT_SCALE


def _diag_scores(q_blk, keys):
    tq = q_blk.shape[0]
    keep = lax.broadcasted_iota(jnp.int32, (tq, tq), 0) >= lax.broadcasted_iota(jnp.int32, (tq, tq), 1)
    return jnp.where(keep, _scores(q_blk, keys), NEG_INF)


def _mla_project(ql_ref, kvl_ref, l, gq_ref, gkv_ref, wq_ref, wkv_ref):
    qn, _ = _rms_stats(ql_ref[0])
    qn = (qn * gq_ref[l:l + 1, :]).astype(MXU_DTYPE)
    kvn, _ = _rms_stats(kvl_ref[0])
    kvn = (kvn * gkv_ref[l:l + 1, :]).astype(MXU_DTYPE)
    return qn, kvn, _mm(qn, wq_ref[0]), _mm(kvn, wkv_ref[0])


def mla_fwd(ql, kvl, kr, tabs, lw, l, b, s, tq):
    nq = s // tq
    cos_t, sa_t, sb_t = tabs

    def body(ql_ref, kvl_ref, kr_ref, cos_ref, sa_ref, sb_ref, gq_ref, gkv_ref, wq_ref, wkv_ref, o_ref, lse_ref):
        _, _, qp, kvp = _mla_project(ql_ref, kvl_ref, l, gq_ref, gkv_ref, wq_ref, wkv_ref)
        cos, sa, sb = cos_ref[0], sa_ref[0], sb_ref[0]
        for hh in range(HEADS_PER_STEP):
            hs = slice(hh * HEAD_LANES, (hh + 1) * HEAD_LANES)
            q = _rope(qp[:, hs], cos, sa, sb).astype(MXU_DTYPE)
            k = _rope(kvp[:, hs] + kr_ref[0], cos, sa, sb).astype(MXU_DTYPE)
            v = kvp[:, HEADS_PER_STEP * HEAD_LANES + hh * HEAD_LANES:
                    HEADS_PER_STEP * HEAD_LANES + (hh + 1) * HEAD_LANES].astype(MXU_DTYPE)
            for qb in range(nq):
                lo, hi = qb * tq, (qb + 1) * tq
                sd = _diag_scores(q[lo:hi], k[lo:hi])
                m = jnp.max(sd, axis=-1, keepdims=True)
                if qb:
                    sf = _scores(q[lo:hi], k[:lo])
                    m = jnp.maximum(m, jnp.max(sf, axis=-1, keepdims=True))
                ed = jnp.exp(sd - m)
                den = jnp.sum(ed, axis=-1, keepdims=True)
                o = _mm(ed, v[lo:hi])
                if qb:
                    ef = jnp.exp(sf - m)
                    den = den + jnp.sum(ef, axis=-1, keepdims=True)
                    o = o + _mm(ef, v[:lo])
                o = o * (1.0 / den)
                lse_ref[0, hh, lo:hi, :] = jnp.broadcast_to(m + jnp.log(den), (tq, LANES))
                if hh == 0:
                    o_ref[0, lo:hi, :] = o
                else:
                    o_ref[0, lo:hi, :] += o

    seq = lambda bi, p: (bi, 0, 0)
    pair = lambda bi, p: (p, 0, 0)

    def per_seq(w):
        return pl.BlockSpec((1, s, w), seq, pipeline_mode=pl.Buffered(1))

    return pl.pallas_call(
        body, name="mla_fwd", grid=(b, N_HEAD_PAIRS),
        in_specs=[per_seq(Q_LORA), per_seq(KV_LORA), per_seq(LANES), per_seq(LANES), per_seq(LANES), per_seq(LANES),
                  _const_spec((DEPTH, Q_LORA)), _const_spec((DEPTH, KV_LORA)),
                  pl.BlockSpec((1, Q_LORA, HEADS_PER_STEP * HEAD_LANES), pair),
                  pl.BlockSpec((1, KV_LORA, 2 * HEADS_PER_STEP * HEAD_LANES), pair)],
        out_specs=[pl.BlockSpec((1, s, LANES), lambda bi, p: (bi, 0, p)),
                   pl.BlockSpec((1, HEADS_PER_STEP, s, LANES), lambda bi, p: (bi, p, 0, 0))],
        out_shape=[jax.ShapeDtypeStruct((b, s, D), F32), jax.ShapeDtypeStruct((b, N_HEADS, s, LANES), F32)],
        compiler_params=_params(VMEM_BIG),
    )(ql, kvl, kr, cos_t, sa_t, sb_t, lw['q_norm_g'], lw['kv_norm_g'], lw['wq_pairs'], lw['wkv_pairs'])


def mla_bwd(dy, y, lse, ql, kvl, kr, tabs, lw, l, b, s, tq):
    nq = s // tq
    cos_t, sa_t, sb_t = tabs
    qw = HEADS_PER_STEP * HEAD_LANES
    kvw = 2 * HEADS_PER_STEP * HEAD_LANES

    def body(dy_ref, y_ref, lse_ref, ql_ref, kvl_ref, kr_ref, cos_ref, sa_ref, sb_ref, gq_ref, gkv_ref, wq_ref,
             wkv_ref, dql_ref, dkvl_ref, dkr_ref, dwq_ref, dwkv_ref, dgq_ref, dgkv_ref, dk_sc, dv_sc):
        bi, p = pl.program_id(0), pl.program_id(1)

        @pl.when((bi == 0) & (p == 0))
        def _():
            for r in (dwq_ref, dwkv_ref, dgq_ref, dgkv_ref):
                r[...] = jnp.zeros_like(r)

        @pl.when(p == 0)
        def _():
            for r in (dql_ref, dkvl_ref, dkr_ref):
                r[...] = jnp.zeros_like(r)

        qn, kvn, qp, kvp = _mla_project(ql_ref, kvl_ref, l, gq_ref, gkv_ref, wq_ref, wkv_ref)
        cos, sa, sb = cos_ref[0], sa_ref[0], sb_ref[0]
        dof = dy_ref[0]
        do = dof.astype(MXU_DTYPE)
        prod = dof * y_ref[0]
        lane = lax.broadcasted_iota(jnp.int32, prod.shape, 1)
        dq_heads, dk_heads, dv_heads = [], [], []
        for hh in range(HEADS_PER_STEP):
            hs = slice(hh * HEAD_LANES, (hh + 1) * HEAD_LANES)
            q = _rope(qp[:, hs], cos, sa, sb).astype(MXU_DTYPE)
            k = _rope(kvp[:, hs] + kr_ref[0], cos, sa, sb).astype(MXU_DTYPE)
            v = kvp[:, qw + hh * HEAD_LANES:qw + (hh + 1) * HEAD_LANES].astype(MXU_DTYPE)
            mine = (lane >= hh * V_HEAD) & (lane < (hh + 1) * V_HEAD)
            delta = jnp.sum(jnp.where(mine, prod, 0.0), axis=-1, keepdims=True)
            dk_sc[...] = jnp.zeros_like(dk_sc)
            dv_sc[...] = jnp.zeros_like(dv_sc)
            dq_blocks = []
            for qb in range(nq):
                lo, hi = qb * tq, (qb + 1) * tq
                lse = lse_ref[0, hh, lo:hi, 0:1]
                dl = delta[lo:hi]
                dq = None
                for ks, diag in (((slice(lo, hi), True),) + (((slice(0, lo), False),) if qb else ())):
                    sc = _diag_scores(q[lo:hi], k[ks]) if diag else _scores(q[lo:hi], k[ks])
                    pr = jnp.exp(sc - lse)
                    dp = _mm_nt(do[lo:hi], v[ks])
                    dv_sc[ks, :] += _mm_tn(pr, do[lo:hi])
                    ds = (pr * (dp - dl)).astype(MXU_DTYPE)
                    part = _mm(ds, k[ks])
                    dq = part if dq is None else dq + part
                    dk_sc[ks, :] += _mm_tn(ds, q[lo:hi])
                dq_blocks.append(dq)
            dq_heads.append(_unrope(jnp.concatenate(dq_blocks, axis=0) * ATT_SCALE, cos, sa, sb))
            dk_full = _unrope(dk_sc[...] * ATT_SCALE, cos, sa, sb)
            dkr_ref[0] += dk_full
            dk_heads.append(dk_full)
            dv_heads.append(dv_sc[...])
        dqp = jnp.concatenate(dq_heads, axis=1).astype(MXU_DTYPE)
        dkvp = jnp.concatenate(dk_heads + dv_heads, axis=1).astype(MXU_DTYPE)
        dwq_ref[p] += _mm_tn(qn, dqp)
        dwkv_ref[p] += _mm_tn(kvn, dkvp)
        dql_ref[0] += _mm_nt(dqp, wq_ref[0])
        dkvl_ref[0] += _mm_nt(dkvp, wkv_ref[0])

        @pl.when(p == N_HEAD_PAIRS - 1)
        def _():
            dx, dg = _rms_bwd(dql_ref[0], ql_ref[0], gq_ref[l:l + 1, :])
            dql_ref[0] = dx
            dgq_ref[...] += dg
            dx, dg = _rms_bwd(dkvl_ref[0], kvl_ref[0], gkv_ref[l:l + 1, :])
            dkvl_ref[0] = dx
            dgkv_ref[...] += dg

    seq = lambda bi, p: (bi, 0, 0)
    pair = lambda bi, p: (p, 0, 0)

    def per_seq(w):
        return pl.BlockSpec((1, s, w), seq, pipeline_mode=pl.Buffered(1))

    return pl.pallas_call(
        body, name="mla_bwd", grid=(b, N_HEAD_PAIRS),
        in_specs=[pl.BlockSpec((1, s, LANES), lambda bi, p: (bi, 0, p)),
                  pl.BlockSpec((1, s, LANES), lambda bi, p: (bi, 0, p)),
                  pl.BlockSpec((1, HEADS_PER_STEP, s, LANES), lambda bi, p: (bi, p, 0, 0)),
                  per_seq(Q_LORA), per_seq(KV_LORA), per_seq(LANES), per_seq(LANES), per_seq(LANES), per_seq(LANES),
                  _const_spec((DEPTH, Q_LORA)), _const_spec((DEPTH, KV_LORA)),
                  pl.BlockSpec((1, Q_LORA, qw), pair), pl.BlockSpec((1, KV_LORA, kvw), pair)],
        out_specs=[pl.BlockSpec((1, s, Q_LORA), seq), pl.BlockSpec((1, s, KV_LORA), seq),
                   pl.BlockSpec((1, s, LANES), seq),
                   _const_spec((N_HEAD_PAIRS, Q_LORA, qw)), _const_spec((N_HEAD_PAIRS, KV_LORA, kvw)),
                   _const_spec((1, Q_LORA)), _const_spec((1, KV_LORA))],
        out_shape=[jax.ShapeDtypeStruct((b, s, Q_LORA), F32), jax.ShapeDtypeStruct((b, s, KV_LORA), F32),
                   jax.ShapeDtypeStruct((b, s, LANES), F32),
                   jax.ShapeDtypeStruct((N_HEAD_PAIRS, Q_LORA, qw), F32),
                   jax.ShapeDtypeStruct((N_HEAD_PAIRS, KV_LORA, kvw), F32),
                   jax.ShapeDtypeStruct((1, Q_LORA), F32), jax.ShapeDtypeStruct((1, KV_LORA), F32)],
        scratch_shapes=[pltpu.VMEM((s, HEAD_LANES), F32), pltpu.VMEM((s, HEAD_LANES), F32)],
        compiler_params=_params(VMEM_BIG),
    )(dy, y, lse, ql, kvl, kr, cos_t, sa_t, sb_t, lw['q_norm_g'], lw['kv_norm_g'], lw['wq_pairs'], lw['wkv_pairs'])


COL_CHUNK = 256


def _merge(g_rnn, gate_a, gate_b, h, y_mla):
    ge, t = _gelu(g_rnn)
    sa, sb = _sigmoid(gate_a), _sigmoid(gate_b)
    y_rnn = ge * h
    return ge, t, sa, sb, y_rnn, sa * y_rnn + sb * y_mla


def mixout_fwd(x, rnn4, h, y_mla, lw, l, tm):
    n = x.shape[0]

    def body(x_ref, gr_ref, gta_ref, gtb_ref, h_ref, y_ref, w_ref, g_ref, b_ref, z_ref, o_ref):
        z = ALPHA * x_ref[...]
        for c in range(0, D, COL_CHUNK):
            cs = slice(c, c + COL_CHUNK)
            merged = _merge(gr_ref[:, cs], gta_ref[:, cs], gtb_ref[:, cs], h_ref[:, cs], y_ref[:, cs])[-1]
            z = z + _mm(merged, w_ref[cs, :])
        z_ref[...] = z
        o_ref[...] = _ln_stats(z)[0] * g_ref[l:l + 1, :] + b_ref[l:l + 1, :]

    row = lambda i: (i, 0)
    col = lambda j: (lambda i: (i, j))
    blk = pl.BlockSpec((tm, D), row)
    return pl.pallas_call(
        body, name="mixout_fwd", grid=(n // tm,),
        in_specs=[blk, pl.BlockSpec((tm, D), col(1)), pl.BlockSpec((tm, D), col(2)), pl.BlockSpec((tm, D), col(3)),
                  blk, blk, _resident((D, D)), _const_spec((DEPTH, D)), _const_spec((DEPTH, D))],
        out_specs=[blk, blk],
        out_shape=[jax.ShapeDtypeStruct((n, D), F32), jax.ShapeDtypeStruct((n, D), F32)],
        compiler_params=_params(VMEM_BIG),
    )(x, rnn4, rnn4, rnn4, h, y_mla, lw['w_out'], lw['ln1_g'], lw['ln1_b'])


def mixout_bwd(dx1, z1, rnn4, h, y_mla, lw, l, tm):
    n = dx1.shape[0]

    def body(d_ref, z_ref, gr_ref, gta_ref, gtb_ref, h_ref, y_ref, w_ref, g_ref,
             dz_ref, dh_ref, dy_ref, dg3_ref, dw_ref, dg_ref, db_ref):
        @pl.when(pl.program_id(0) == 0)
        def _():
            for r in (dw_ref, dg_ref, db_ref):
                r[...] = jnp.zeros_like(r)

        dz, dg, db = _ln_bwd(d_ref[...], z_ref[...], g_ref[l:l + 1, :])
        dz_ref[...] = dz
        dg_ref[...] += dg
        db_ref[...] += db
        dzb = dz.astype(MXU_DTYPE)
        for c in range(0, D, COL_CHUNK):
            cs = slice(c, c + COL_CHUNK)
            g_rnn, h, y_mla = gr_ref[:, cs], h_ref[:, cs], y_ref[:, cs]
            ge, t, sa, sb, y_rnn, merged = _merge(g_rnn, gta_ref[:, cs], gtb_ref[:, cs], h, y_mla)
            dw_ref[cs, :] += _mm_tn(merged, dzb)
            dm = _mm_nt(dzb, w_ref[cs, :])
            dy_rnn = dm * sa
            dy_ref[:, cs] = dm * sb
            dh_ref[:, cs] = dy_rnn * ge
            dg3_ref[:, c:c + COL_CHUNK] = dy_rnn * h * _gelu_grad(g_rnn, t)
            dg3_ref[:, D + c:D + c + COL_CHUNK] = dm * y_rnn * sa * (1.0 - sa)
            dg3_ref[:, 2 * D + c:2 * D + c + COL_CHUNK] = dm * y_mla * sb * (1.0 - sb)

    row = lambda i: (i, 0)
    col = lambda j: (lambda i: (i, j))
    blk = pl.BlockSpec((tm, D), row)
    vec = jax.ShapeDtypeStruct((1, D), F32)
    act = jax.ShapeDtypeStruct((n, D), F32)
    return pl.pallas_call(
        body, name="mixout_bwd", grid=(n // tm,),
        in_specs=[blk, blk, pl.BlockSpec((tm, D), col(1)), pl.BlockSpec((tm, D), col(2)),
                  pl.BlockSpec((tm, D), col(3)), blk, blk, _resident((D, D)), _const_spec((DEPTH, D))],
        out_specs=[blk, blk, blk, pl.BlockSpec((tm, 3 * D), row), _const_spec((D, D)), _const_spec((1, D)),
                   _const_spec((1, D))],
        out_shape=[act, act, act, jax.ShapeDtypeStruct((n, 3 * D), F32), jax.ShapeDtypeStruct((D, D), F32), vec, vec],
        compiler_params=_params(VMEM_BIG),
    )(dx1, z1, rnn4, rnn4, rnn4, h, y_mla, lw['w_out'], lw['ln1_g'])


FFN_CHUNK = 512


def _conv3(u, halo, cs, l, fcw_ref, fcb_ref):
    hc = fcb_ref[l:l + 1, cs]
    for k in range(FFN_CONV_W):
        hc = hc + fcw_ref[k:k + 1, cs] * _shift_down(u, halo, FFN_CONV_W - 1 - k, 0)
    return hc


def ffn_fwd(x1, lw, l, b, s, ts):
    ns = s // ts
    n = b * s

    def body(x_ref, wu_ref, fcw_ref, fcb_ref, wd_ref, g_ref, b_ref, up_ref, z_ref, o_ref, halo_sc):
        @pl.when(pl.program_id(1) == 0)
        def _():
            halo_sc[...] = jnp.zeros_like(halo_sc)

        x = x_ref[...]
        xb = x.astype(MXU_DTYPE)
        z = ALPHA * x
        for c in range(0, D_FF, FFN_CHUNK):
            gs, vs = slice(c, c + FFN_CHUNK), slice(D_FF + c, D_FF + c + FFN_CHUNK)
            ug, uv = _mm(xb, wu_ref[:, gs]), _mm(xb, wu_ref[:, vs])
            up_ref[:, gs] = ug
            up_ref[:, vs] = uv
            hg = _conv3(ug, halo_sc[:, gs], gs, l, fcw_ref, fcb_ref)
            hv = _conv3(uv, halo_sc[:, vs], vs, l, fcw_ref, fcb_ref)
            halo_sc[:, gs] = ug[ts - SUBLANES:, :]
            halo_sc[:, vs] = uv[ts - SUBLANES:, :]
            z = z + _mm(_gelu(hg)[0] * hv, wd_ref[c:c + FFN_CHUNK, :])
        z_ref[...] = z
        o_ref[...] = _ln_stats(z)[0] * g_ref[l:l + 1, :] + b_ref[l:l + 1, :]

    row = lambda bi, i: (bi * ns + i, 0)
    blk = pl.BlockSpec((ts, D), row)
    return pl.pallas_call(
        body, name="ffn_fwd", grid=(b, ns),
        in_specs=[blk, _resident((D, 2 * D_FF)), _const_spec((FFN_CONV_W, 2 * D_FF)), _const_spec((DEPTH, 2 * D_FF)),
                  _resident((D_FF, D)), _const_spec((DEPTH, D)), _const_spec((DEPTH, D))],
        out_specs=[pl.BlockSpec((ts, 2 * D_FF), row), blk, blk],
        out_shape=[jax.ShapeDtypeStruct((n, 2 * D_FF), F32), jax.ShapeDtypeStruct((n, D), F32),
                   jax.ShapeDtypeStruct((n, D), F32)],
        scratch_shapes=[pltpu.VMEM((SUBLANES, 2 * D_FF), F32)],
        compiler_params=_params(VMEM_BIG),
    )(x1, lw['w_up'], lw['ffn_conv_w'], lw['ffn_conv_b'], lw['w_down'], lw['ln2_g'], lw['ln2_b'])


def ffn_bwd(dx2, z2, up, lw, l, b, s, ts):
    ns = s // ts
    n = b * s
    hb = ts // SUBLANES

    def body(d_ref, z_ref, up_ref, uph_ref, fcw_ref, fcb_ref, wd_ref, g_ref,
             dz_ref, dup_ref, act_ref, dfcw_ref, dfcb_ref, dg_ref, db_ref, nhalo_sc):
        bi, i = pl.program_id(0), pl.program_id(1)

        @pl.when((bi == 0) & (i == 0))
        def _():
            for r in (dfcw_ref, dfcb_ref, dg_ref, db_ref):
                r[...] = jnp.zeros_like(r)

        @pl.when(i == 0)
        def _():
            nhalo_sc[...] = jnp.zeros_like(nhalo_sc)

        dz, dg, db = _ln_bwd(d_ref[...], z_ref[...], g_ref[l:l + 1, :])
        dz_ref[...] = dz
        dg_ref[...] += dg
        db_ref[...] += db
        dzb = dz.astype(MXU_DTYPE)
        keep = jnp.where(i == ns - 1, 0.0, 1.0)
        for c in range(0, D_FF, FFN_CHUNK):
            gs, vs = slice(c, c + FFN_CHUNK), slice(D_FF + c, D_FF + c + FFN_CHUNK)
            ug, uv = up_ref[:, gs], up_ref[:, vs]
            hg_halo, hv_halo = uph_ref[:, gs] * keep, uph_ref[:, vs] * keep
            hg = _conv3(ug, hg_halo, gs, l, fcw_ref, fcb_ref)
            hv = _conv3(uv, hv_halo, vs, l, fcw_ref, fcb_ref)
            ge, t = _gelu(hg)
            act_ref[:, gs] = (ge * hv).astype(MXU_DTYPE)
            dact = _mm_nt(dzb, wd_ref[c:c + FFN_CHUNK, :])
            for cs, u, halo, dhc in ((gs, ug, hg_halo, dact * hv * _gelu_grad(hg, t)), (vs, uv, hv_halo, dact * ge)):
                dfcb_ref[:, cs] += jnp.sum(dhc, axis=0, keepdims=True)
                nhalo = nhalo_sc[:, cs]
                dup = jnp.zeros_like(dhc)
                for k in range(FFN_CONV_W):
                    sft = FFN_CONV_W - 1 - k
                    dfcw_ref[k:k + 1, cs] += jnp.sum(dhc * _shift_down(u, halo, sft, 0), axis=0, keepdims=True)
                    dup = dup + fcw_ref[k:k + 1, cs] * _shift_up(dhc, nhalo, sft, 0)
                dup_ref[:, cs] = dup.astype(MXU_DTYPE)
                nhalo_sc[:, cs] = dhc[:SUBLANES, :]

    row = lambda bi, i: (bi * ns + (ns - 1 - i), 0)
    halo = lambda bi, i: (jnp.maximum((bi * ns + (ns - 1 - i)) * hb - 1, 0), 0)
    blk = pl.BlockSpec((ts, D), row)
    wide = pl.BlockSpec((ts, 2 * D_FF), row)
    return pl.pallas_call(
        body, name="ffn_bwd", grid=(b, ns),
        in_specs=[blk, blk, wide, pl.BlockSpec((SUBLANES, 2 * D_FF), halo),
                  _const_spec((FFN_CONV_W, 2 * D_FF)), _const_spec((DEPTH, 2 * D_FF)), _resident((D_FF, D)),
                  _const_spec((DEPTH, D))],
        out_specs=[blk, wide, pl.BlockSpec((ts, D_FF), row), _const_spec((FFN_CONV_W, 2 * D_FF)),
                   _const_spec((1, 2 * D_FF)), _const_spec((1, D)), _const_spec((1, D))],
        out_shape=[jax.ShapeDtypeStruct((n, D), F32), jax.ShapeDtypeStruct((n, 2 * D_FF), MXU_DTYPE),
                   jax.ShapeDtypeStruct((n, D_FF), MXU_DTYPE), jax.ShapeDtypeStruct((FFN_CONV_W, 2 * D_FF), F32),
                   jax.ShapeDtypeStruct((1, 2 * D_FF), F32), jax.ShapeDtypeStruct((1, D), F32),
                   jax.ShapeDtypeStruct((1, D), F32)],
        scratch_shapes=[pltpu.VMEM((SUBLANES, 2 * D_FF), F32)],
        compiler_params=_params(VMEM_BIG),
    )(dx2, z2, up, up, lw['ffn_conv_w'], lw['ffn_conv_b'], lw['w_down'], lw['ln2_g'])


def loss_head(y, target, tm):
    n = y.shape[0]

    def body(y_ref, t_ref, l_ref, d_ref):
        @pl.when(pl.program_id(0) == 0)
        def _():
            l_ref[...] = jnp.zeros_like(l_ref)

        err = y_ref[...] - t_ref[...]
        d_ref[...] = err * (1.0 / D)
        part = jnp.sum(jnp.sum(err * err, axis=-1, keepdims=True), axis=0, keepdims=True)
        l_ref[...] += jnp.broadcast_to(part * (0.5 / D), l_ref.shape)

    row = lambda i: (i, 0)
    return pl.pallas_call(
        body, name="loss_head", grid=(n // tm,),
        in_specs=[pl.BlockSpec((tm, D), row), pl.BlockSpec((tm, D), row)],
        out_specs=[_const_spec((1, LANES)), pl.BlockSpec((tm, D), row)],
        out_shape=[jax.ShapeDtypeStruct((1, LANES), F32), jax.ShapeDtypeStruct((n, D), F32)],
    )(y, target)


def _adam_update(g, w, m, v):
    c1 = 1.0 - ADAM_B1 ** ADAM_STEP
    c2 = 1.0 - ADAM_B2 ** ADAM_STEP
    mn = ADAM_B1 * m + (1.0 - ADAM_B1) * g
    vn = ADAM_B2 * v + (1.0 - ADAM_B2) * (g * g)
    return -ADAM_LR * ((mn / c1) / (jnp.sqrt(vn / c2) + ADAM_EPS) + ADAM_WD * w), mn, vn


def adamw_tiled(parts, w, m, v, name):
    _, r, c = w.shape
    tr = next(t for t in (256, 128, 64, 32, 16, 8) if r % t == 0)

    def body(p_ref, w_ref, m_ref, v_ref, g_ref, d_ref, mo_ref, vo_ref):
        g = p_ref[0]
        for i in range(1, N_DEV):
            g = g + p_ref[i]
        g_ref[...] = g
        d_ref[...], mo_ref[...], vo_ref[...] = _adam_update(g, w_ref[...], m_ref[...], v_ref[...])

    blk = pl.BlockSpec((None, tr, c), lambda l, i: (l, i, 0))
    out = jax.ShapeDtypeStruct(w.shape, F32)
    return pl.pallas_call(
        body, name="adamw_" + name, grid=(DEPTH, r // tr),
        in_specs=[pl.BlockSpec((N_DEV, None, tr, c), lambda l, i: (0, l, i, 0)), blk, blk, blk],
        out_specs=[blk, blk, blk, blk],
        out_shape=[out, out, out, out],
    )(parts, w, m, v)


def adamw_small(items):
    k = len(items)

    def body(*refs):
        ins, outs = refs[:4 * k], refs[4 * k:]
        for j in range(k):
            p_ref, w_ref, m_ref, v_ref = ins[4 * j:4 * j + 4]
            g_ref, d_ref, mo_ref, vo_ref = outs[j], outs[k + j], outs[2 * k + j], outs[3 * k + j]
            if len(p_ref.shape) == len(w_ref.shape) + 1:
                g = p_ref[0]
                for i in range(1, N_DEV):
                    g = g + p_ref[i]
                g_ref[...] = g
                d_ref[...], mo_ref[...], vo_ref[...] = _adam_update(g, w_ref[...], m_ref[...], v_ref[...])
            else:
                for l in range(DEPTH):
                    lr = slice(l, l + 1)
                    g = p_ref[0, l]
                    for i in range(1, N_DEV):
                        g = g + p_ref[i, l]
                    g_ref[lr, :] = g
                    d_ref[lr, :], mo_ref[lr, :], vo_ref[lr, :] = _adam_update(g, w_ref[lr, :], m_ref[lr, :],
                                                                              v_ref[lr, :])

    flat = [a for item in items for a in item]
    outs = [jax.ShapeDtypeStruct(item[1].shape, F32) for item in items] * 4
    return pl.pallas_call(
        body, name="adamw_small",
        in_specs=[pl.BlockSpec(memory_space=pltpu.VMEM)] * len(flat),
        out_specs=[pl.BlockSpec(memory_space=pltpu.VMEM)] * len(outs),
        out_shape=outs,
        compiler_params=_params(VMEM_BIG),
    )(*flat)


def exchange_layer(srcs, per_peer, src_layer, bufs, l, name):
    na = len(srcs)

    def body(*refs):
        src_refs, buf_in, buf_refs = refs[:na], refs[na:2 * na], refs[2 * na:3 * na]
        send_sems, recv_sems, local_sems = refs[3 * na:]
        del buf_in
        x, y, c = lax.axis_index("x"), lax.axis_index("y"), lax.axis_index("c")
        me = 4 * x + 2 * y + c

        def view(a, pid):
            r = src_refs[a]
            if src_layer is not None:
                r = r.at[src_layer]
            return r.at[pid] if per_peer[a] else r

        copies = []
        for a in range(na):
            cp = pltpu.make_async_copy(view(a, me), buf_refs[a].at[me, l], local_sems.at[a])
            cp.start()
            copies.append(cp)
        for k in range(1, N_DEV):
            px = 1 - x if k & 4 else x
            py = 1 - y if k & 2 else y
            pc = 1 - c if k & 1 else c
            pid = 4 * px + 2 * py + pc
            for a in range(na):
                cp = pltpu.make_async_remote_copy(
                    src_ref=view(a, pid), dst_ref=buf_refs[a].at[me, l],
                    send_sem=send_sems.at[a, k - 1], recv_sem=recv_sems.at[a, k - 1],
                    device_id=(px, py, pc), device_id_type=pl.DeviceIdType.MESH)
                cp.start()
                copies.append(cp)
        for cp in copies:
            cp.wait()

    anyspec = pl.BlockSpec(memory_space=pl.ANY)
    return pl.pallas_call(
        body, name=name,
        in_specs=[anyspec] * (2 * na), out_specs=[anyspec] * na,
        out_shape=[jax.ShapeDtypeStruct(bf.shape, bf.dtype) for bf in bufs],
        input_output_aliases={na + a: a for a in range(na)},
        scratch_shapes=[pltpu.SemaphoreType.DMA((na, N_DEV - 1)), pltpu.SemaphoreType.DMA((na, N_DEV - 1)),
                        pltpu.SemaphoreType.DMA((na,))],
    )(*srcs, *bufs)


def _permute_w_in(w):
    o = [0, D, 2 * D, 2 * D + Q_LORA, 2 * D + Q_LORA + KV_LORA, 2 * D + Q_LORA + KV_LORA + QK_ROPE,
         3 * D + Q_LORA + KV_LORA + QK_ROPE, IN_WIDTH]
    xr, gr, qlat, kvl, kr, ga, gb = [w[:, o[i]:o[i + 1]] for i in range(7)]
    z = lambda c: jnp.zeros((w.shape[0], c), w.dtype)
    return jnp.concatenate([xr, gr, ga, gb, qlat, kvl, z(QK_NOPE), kr, z(HEAD_LANES - QK_NOPE - QK_ROPE)], axis=1)


def _unpermute_dw_in(dw):
    o = 4 * D
    k0 = o + Q_LORA + KV_LORA + QK_NOPE
    return jnp.concatenate([dw[:, 0:2 * D], dw[:, o:o + Q_LORA + KV_LORA], dw[:, k0:k0 + QK_ROPE],
                            dw[:, 2 * D:4 * D]], axis=1)


def _pair_wq(w):
    w = w.reshape(Q_LORA, N_HEADS, QK_NOPE + QK_ROPE)
    w = jnp.pad(w, ((0, 0), (0, 0), (0, HEAD_LANES - QK_NOPE - QK_ROPE)))
    return w.reshape(Q_LORA, N_HEAD_PAIRS, HEADS_PER_STEP * HEAD_LANES).transpose(1, 0, 2)


def _unpair_dwq(dw):
    dw = dw.transpose(1, 0, 2).reshape(Q_LORA, N_HEADS, HEAD_LANES)
    return dw[:, :, :QK_NOPE + QK_ROPE].reshape(Q_LORA, N_HEADS * (QK_NOPE + QK_ROPE))


def _pair_wkv(w):
    w = w.reshape(KV_LORA, N_HEAD_PAIRS, HEADS_PER_STEP, QK_NOPE + V_HEAD)
    kn, vv = w[..., :QK_NOPE], w[..., QK_NOPE:]
    z = jnp.zeros_like(kn[:, :, 0])
    out = jnp.concatenate([kn[:, :, 0], z, kn[:, :, 1], z, vv[:, :, 0], z, z, vv[:, :, 1]], axis=-1)
    return out.transpose(1, 0, 2)


def _unpair_dwkv(dw):
    dw = dw.transpose(1, 0, 2)
    h0 = jnp.concatenate([dw[..., 0:64], dw[..., 256:320]], axis=-1)
    h1 = jnp.concatenate([dw[..., 128:192], dw[..., 448:512]], axis=-1)
    return jnp.stack([h0, h1], axis=2).reshape(KV_LORA, N_HEADS * (QK_NOPE + V_HEAD))


def _group_gates(w):
    per = MXU_GROUP // RNN_BLOCK
    w = w.reshape(DEPTH, N_GATE_GROUPS, per, RNN_BLOCK, RNN_BLOCK)
    eye = jnp.eye(per, dtype=w.dtype)
    return jnp.einsum('lgpij,pq->lgpiqj', w, eye).reshape(DEPTH, N_GATE_GROUPS, MXU_GROUP, MXU_GROUP)


def _ungroup_dgate(dw):
    per = MXU_GROUP // RNN_BLOCK
    dw = dw.reshape(N_GATE_GROUPS, per, RNN_BLOCK, per, RNN_BLOCK)
    return jnp.stack([dw[:, p, :, p, :] for p in range(per)], axis=1).reshape(N_RNN_BLOCKS, RNN_BLOCK, RNN_BLOCK)


def _rope_tables(positions):
    inv_freq = ROPE_THETA ** (-jnp.arange(0, QK_ROPE, 2, dtype=F32) / QK_ROPE)
    ang = positions.astype(F32)[..., None] * inv_freq
    cos, sin = jnp.cos(ang), jnp.sin(ang)
    one, zero = jnp.ones_like(cos), jnp.zeros_like(cos)
    nope = lambda v: jnp.concatenate([v] * (QK_NOPE // (QK_ROPE // 2)), axis=-1)
    tail = jnp.concatenate([zero, zero], axis=-1)
    cos_t = jnp.concatenate([nope(one), cos, cos, tail], axis=-1)
    sa_t = jnp.concatenate([nope(zero), -sin, zero, tail], axis=-1)
    sb_t = jnp.concatenate([nope(zero), zero, sin, tail], axis=-1)
    return cos_t, sa_t, sb_t


def _unshard(pieces, name):
    axis = SHARDED[name][0] - 1
    return jnp.concatenate([pieces[i] for i in range(N_DEV)], axis=axis)


def _shard_pieces(g, name):
    axis = SHARDED[name][0] - 1
    return jnp.stack(jnp.split(g, N_DEV, axis=axis))


def _layer_weights(gathered, shared, l):
    full = {name: _unshard(gathered[name][:, l], name) for name in SHARDED}
    lw = dict(shared)
    lw.update({
        'w_in_p': _permute_w_in(full['w_in']), 'conv_w': full['conv_w'],
        'wq_pairs': _pair_wq(full['w_uq']), 'wkv_pairs': _pair_wkv(full['w_ukv']),
        'w_out': full['w_out'], 'w_up': full['w_up'], 'ffn_conv_w': full['ffn_conv_w'], 'w_down': full['w_down'],
    })
    return lw


TM = 256
TS_RNN = 128
TS_FFN = 256
TQ = 256
TN_DW = 512


def layer_fwd(xc, tabs, lw, l, b, s):
    n = b * s
    rnn4, ql, kvl, kr = inproj_fwd(xc, lw['w_in_p'], TM)
    h = rnn_fwd(rnn4.reshape(b, s, 4 * D), lw, l, b, s, TS_RNN)
    lat = (ql.reshape(b, s, Q_LORA), kvl.reshape(b, s, KV_LORA), kr.reshape(b, s, LANES))
    y_mla, lse = mla_fwd(*lat, tabs, lw, l, b, s, TQ)
    z1, x1 = mixout_fwd(xc, rnn4, h.reshape(n, D), y_mla.reshape(n, D), lw, l, TM)
    up, z2, x2 = ffn_fwd(x1, lw, l, b, s, TS_FFN)
    return x2, (xc, rnn4, lat, h, y_mla, lse, z1, x1, up, z2)


def layer_bwd(dx, saved, tabs, lw, l, b, s):
    n = b * s
    x0, rnn4, lat, h, y_mla, lse, z1, x1, up, z2 = saved
    dz2, dup, act, dfcw, dfcb, dg2, db2 = ffn_bwd(dx, z2, up, lw, l, b, s, TS_FFN)
    dx1 = matmul_dx(dup, lw['w_up'], dz2, ALPHA, TM, "ffn_up_dx")
    dw_up = matmul_dw(x1, dup, TN_DW, 2 * D_FF // 3, "ffn_up_dw")
    dw_down = matmul_dw(act, dz2, TN_DW, D // 2, "ffn_down_dw")
    dz1, dh, dy_mla, dg3, dw_out, dg1, db1 = mixout_bwd(dx1, z1, rnn4, h.reshape(n, D), y_mla.reshape(n, D),
                                                       lw, l, TM)
    dql, dkvl, dkr, dwq, dwkv, dgq, dgkv = mla_bwd(dy_mla.reshape(b, s, D), y_mla, lse, *lat, tabs, lw, l, b, s, TQ)
    dxr, dcw, dcb, dwgx, dbgx, dwga, dbga, dlam = rnn_bwd(dh.reshape(b, s, D), rnn4.reshape(b, s, 4 * D), h,
                                                          lw, l, b, s, TS_RNN)
    dx, dproj = inproj_bwd(dxr.reshape(n, D), dg3, dql.reshape(n, Q_LORA), dkvl.reshape(n, KV_LORA),
                           dkr.reshape(n, LANES), dz1, lw['w_in_p'], TM)
    dw_in_p = matmul_dw(x0, dproj, TN_DW, PROJ_W // 2, "inproj_dw")
    grads = {
        'w_in': _unpermute_dw_in(dw_in_p), 'conv_w': dcw, 'conv_b': dcb, 'gx_w': _ungroup_dgate(dwgx), 'gx_b': dbgx,
        'ga_w': _ungroup_dgate(dwga), 'ga_b': dbga, 'lru_lambda': dlam, 'q_norm_g': dgq, 'w_uq': _unpair_dwq(dwq),
        'kv_norm_g': dgkv, 'w_ukv': _unpair_dwkv(dwkv), 'w_out': dw_out, 'ln1_g': dg1, 'ln1_b': db1, 'w_up': dw_up,
        'ffn_conv_w': dfcw, 'ffn_conv_b': dfcb, 'w_down': dw_down, 'ln2_g': dg2, 'ln2_b': db2,
    }
    return dx, {k: (_shard_pieces(g, k) if k in SHARDED else g) for k, g in grads.items()}


SMALL_WEIGHT_ELEMS = 1 << 16


def kernel(x, positions, w_in, conv_w, conv_b, gx_w, gx_b, ga_w, ga_b, lru_lambda, q_norm_g, w_uq, kv_norm_g, w_ukv, w_out, ln1_g, ln1_b, w_up, ffn_conv_w, ffn_conv_b, w_down, ln2_g, ln2_b, loss_target, m_w_in, m_conv_w, m_conv_b, m_gx_w, m_gx_b, m_ga_w, m_ga_b, m_lru_lambda, m_q_norm_g, m_w_uq, m_kv_norm_g, m_w_ukv, m_w_out, m_ln1_g, m_ln1_b, m_w_up, m_ffn_conv_w, m_ffn_conv_b, m_w_down, m_ln2_g, m_ln2_b, v_w_in, v_conv_w, v_conv_b, v_gx_w, v_gx_b, v_ga_w, v_ga_b, v_lru_lambda, v_q_norm_g, v_w_uq, v_kv_norm_g, v_w_ukv, v_w_out, v_ln1_g, v_ln1_b, v_w_up, v_ffn_conv_w, v_ffn_conv_b, v_w_down, v_ln2_g, v_ln2_b):
    w = dict(zip(WEIGHT_NAMES, (w_in, conv_w, conv_b, gx_w, gx_b, ga_w, ga_b, lru_lambda, q_norm_g, w_uq, kv_norm_g,
                                w_ukv, w_out, ln1_g, ln1_b, w_up, ffn_conv_w, ffn_conv_b, w_down, ln2_g, ln2_b)))
    m = dict(zip(WEIGHT_NAMES, (m_w_in, m_conv_w, m_conv_b, m_gx_w, m_gx_b, m_ga_w, m_ga_b, m_lru_lambda,
                                m_q_norm_g, m_w_uq, m_kv_norm_g, m_w_ukv, m_w_out, m_ln1_g, m_ln1_b, m_w_up,
                                m_ffn_conv_w, m_ffn_conv_b, m_w_down, m_ln2_g, m_ln2_b)))
    v = dict(zip(WEIGHT_NAMES, (v_w_in, v_conv_w, v_conv_b, v_gx_w, v_gx_b, v_ga_w, v_ga_b, v_lru_lambda,
                                v_q_norm_g, v_w_uq, v_kv_norm_g, v_w_ukv, v_w_out, v_ln1_g, v_ln1_b, v_w_up,
                                v_ffn_conv_w, v_ffn_conv_b, v_w_down, v_ln2_g, v_ln2_b)))
    b, s, _ = x.shape
    n = b * s
    tabs = _rope_tables(positions)
    shared = {name: w[name] for name in WEIGHT_NAMES if name not in SHARDED and w[name].ndim == 2}
    shared['wgx'] = _group_gates(w['gx_w']).astype(MXU_DTYPE)
    shared['wga'] = _group_gates(w['ga_w']).astype(MXU_DTYPE)

    names = list(SHARDED)
    send = [w[k].astype(MXU_DTYPE) if k in MATMUL_WEIGHTS else w[k] for k in names]
    gbufs = [lax.empty((N_DEV,) + a.shape, a.dtype) for a in send]
    for l in range(DEPTH):
        gbufs = exchange_layer(send, [False] * len(names), l, gbufs, l, "gather_weights")
    gathered = dict(zip(names, gbufs))

    lws, saved = [], []
    xc = x.reshape(n, D)
    for l in range(DEPTH):
        lws.append(_layer_weights(gathered, shared, l))
        xc, sv = layer_fwd(xc, tabs, lws[l], l, b, s)
        saved.append(sv)
    loss_part, dx = loss_head(xc, loss_target.reshape(n, D), TM)

    pbufs = None
    for l in reversed(range(DEPTH)):
        dx, grads = layer_bwd(dx, saved[l], tabs, lws[l], l, b, s)
        srcs = [grads[k] for k in WEIGHT_NAMES]
        if pbufs is None:
            pbufs = [lax.empty((N_DEV, DEPTH) + (a.shape[1:] if k in SHARDED else a.shape), F32)
                     for k, a in zip(WEIGHT_NAMES, srcs)]
        pbufs = exchange_layer(srcs, [k in SHARDED for k in WEIGHT_NAMES], None, pbufs, l, "scatter_grads")
    parts = dict(zip(WEIGHT_NAMES, pbufs))

    out = {}
    small = [k for k in WEIGHT_NAMES if w[k].size <= SMALL_WEIGHT_ELEMS]
    res = adamw_small([(parts[k], w[k], m[k], v[k]) for k in small])
    for j, k in enumerate(small):
        out[k] = tuple(res[i * len(small) + j] for i in range(4))
    for k in WEIGHT_NAMES:
        if k in out:
            continue
        shape = w[k].shape
        view = (DEPTH, -1, shape[-1])
        r = adamw_tiled(parts[k].reshape((N_DEV,) + view), w[k].reshape(view), m[k].reshape(view),
                        v[k].reshape(view), k)
        out[k] = tuple(a.reshape(shape) for a in r)
    loss = lax.psum(loss_part[0, 0], ("x", "y", "c"))
    return (loss, dx.reshape(b, s, D), *[out[k][0] for k in WEIGHT_NAMES], *[out[k][1] for k in WEIGHT_NAMES],
            *[out[k][2] for k in WEIGHT_NAMES], *[out[k][3] for k in WEIGHT_NAMES])
```

```python
import functools
import math

import jax
import jax.numpy as jnp
from jax import lax
from jax.experimental import pallas as pl
from jax.experimental.pallas import tpu as pltpu

F32 = jnp.float32
MXU_DTYPE = jnp.bfloat16

D = 1024
DEPTH = 4
N_RNN_BLOCKS = 16
RNN_BLOCK = 64
CONV_W = 4
LRU_C = 8.0
N_HEADS = 16
QK_NOPE = 64
QK_ROPE = 32
V_HEAD = 64
Q_LORA = 384
KV_LORA = 256
ROPE_THETA = 10000.0
D_FF = 3 * D
FFN_CONV_W = 3
IN_WIDTH = 2 * D + Q_LORA + KV_LORA + QK_ROPE + 2 * D
ALPHA = (2 * DEPTH) ** 0.25
EPS = 1e-6
NEG_INF = -1e30
ATT_SCALE = (QK_NOPE + QK_ROPE) ** -0.5
GELU_C = math.sqrt(2.0 / math.pi)

ADAM_LR = 0.001
ADAM_B1 = 0.9
ADAM_B2 = 0.999
ADAM_EPS = 1e-08
ADAM_WD = 0.01
ADAM_STEP = 10

N_DEV = 8
LANES = 128
SUBLANES = 8
MXU_GROUP = 256
N_GATE_GROUPS = D // MXU_GROUP
HEADS_PER_STEP = 2
N_HEAD_PAIRS = N_HEADS // HEADS_PER_STEP
HEAD_LANES = 128
PROJ_W = 4 * D + Q_LORA + KV_LORA + LANES
VMEM_BIG = 56 * 2 ** 20

WEIGHT_NAMES = ['w_in', 'conv_w', 'conv_b', 'gx_w', 'gx_b', 'ga_w', 'ga_b', 'lru_lambda', 'q_norm_g', 'w_uq',
                'kv_norm_g', 'w_ukv', 'w_out', 'ln1_g', 'ln1_b', 'w_up', 'ffn_conv_w', 'ffn_conv_b', 'w_down',
                'ln2_g', 'ln2_b']
SHARDED = {
    'w_in': (2, (DEPTH, D, IN_WIDTH)),
    'conv_w': (2, (DEPTH, CONV_W, D)),
    'w_uq': (2, (DEPTH, Q_LORA, N_HEADS * (QK_NOPE + QK_ROPE))),
    'w_ukv': (2, (DEPTH, KV_LORA, N_HEADS * (QK_NOPE + V_HEAD))),
    'w_out': (1, (DEPTH, D, D)),
    'w_up': (2, (DEPTH, D, 2 * D_FF)),
    'ffn_conv_w': (2, (DEPTH, FFN_CONV_W, 2 * D_FF)),
    'w_down': (1, (DEPTH, D_FF, D)),
}
MATMUL_WEIGHTS = ('w_in', 'w_uq', 'w_ukv', 'w_out', 'w_up', 'w_down')


def _mm(a, b):
    return jnp.dot(a.astype(MXU_DTYPE), b.astype(MXU_DTYPE), preferred_element_type=F32)


def _mm_tn(a, b):
    return lax.dot_general(a.astype(MXU_DTYPE), b.astype(MXU_DTYPE), (((0,), (0,)), ((), ())),
                           preferred_element_type=F32)


def _mm_nt(a, b):
    return lax.dot_general(a.astype(MXU_DTYPE), b.astype(MXU_DTYPE), (((1,), (1,)), ((), ())),
                           preferred_element_type=F32)


def _sigmoid(x):
    return 1.0 / (1.0 + jnp.exp(-x))


def _gelu(x):
    t = jnp.tanh(GELU_C * (x + 0.044715 * (x * x * x)))
    return 0.5 * x * (1.0 + t), t


def _gelu_grad(x, t):
    return 0.5 * (1.0 + t) + 0.5 * x * (1.0 - t * t) * (GELU_C * (1.0 + 3.0 * 0.044715 * (x * x)))


def _neg_expm1(y):
    series = -y * (1.0 + 0.5 * y * (1.0 + (y / 3.0) * (1.0 + 0.25 * y * (1.0 + 0.2 * y))))
    return jnp.where(y > -0.05, series, 1.0 - jnp.exp(y))


def _ln_stats(z):
    mu = jnp.mean(z, axis=-1, keepdims=True)
    zc = z - mu
    var = jnp.mean(zc * zc, axis=-1, keepdims=True)
    r = lax.rsqrt(var + EPS)
    return zc * r, r


def _ln_bwd(dy, z, g):
    xhat, r = _ln_stats(z)
    dxh = dy * g
    dz = r * (dxh - jnp.mean(dxh, axis=-1, keepdims=True)
              - xhat * jnp.mean(dxh * xhat, axis=-1, keepdims=True))
    return dz, jnp.sum(dy * xhat, axis=0, keepdims=True), jnp.sum(dy, axis=0, keepdims=True)


def _rms_stats(x):
    r = lax.rsqrt(jnp.mean(x * x, axis=-1, keepdims=True) + EPS)
    return x * r, r


def _rms_bwd(dy, x, g):
    xn, r = _rms_stats(x)
    dxn = dy * g
    dx = r * (dxn - xn * jnp.mean(dxn * xn, axis=-1, keepdims=True))
    return dx, jnp.sum(dy * xn, axis=0, keepdims=True)


def _shift_down(x, halo, s, axis):
    if s == 0:
        return x
    r = pltpu.roll(x, s, axis)
    hr = pltpu.roll(halo, s, axis)
    idx = lax.broadcasted_iota(jnp.int32, hr.shape, axis)
    head = lax.slice_in_dim(r, 0, SUBLANES, axis=axis)
    rest = lax.slice_in_dim(r, SUBLANES, x.shape[axis], axis=axis)
    return jnp.concatenate([jnp.where(idx < s, hr, head), rest], axis=axis)


def _shift_up(x, halo, s, axis):
    if s == 0:
        return x
    n = x.shape[axis]
    r = pltpu.roll(x, n - s, axis)
    hr = pltpu.roll(halo, SUBLANES - s, axis)
    idx = lax.broadcasted_iota(jnp.int32, hr.shape, axis)
    body = lax.slice_in_dim(r, 0, n - SUBLANES, axis=axis)
    tail = lax.slice_in_dim(r, n - SUBLANES, n, axis=axis)
    return jnp.concatenate([body, jnp.where(idx >= SUBLANES - s, hr, tail)], axis=axis)


def _const_spec(shape):
    nd = len(shape)
    return pl.BlockSpec(shape, lambda *_: (0,) * nd)


def _layer_spec(shape, l):
    nd = len(shape)
    return pl.BlockSpec((None,) + tuple(shape), lambda *_: (l,) + (0,) * nd)


def _resident(shape):
    nd = len(shape)
    return pl.BlockSpec(shape, lambda *_: (0,) * nd, pipeline_mode=pl.Buffered(1))


def _params(vmem=None):
    return pltpu.CompilerParams(vmem_limit_bytes=vmem)


def inproj_fwd(x, w_in_p, tm):
    n = x.shape[0]

    def body(x_ref, w_ref, rnn4_ref, ql_ref, kvl_ref, kr_ref):
        xb = x_ref[...].astype(MXU_DTYPE)
        for j in range(4):
            rnn4_ref[:, j * D:(j + 1) * D] = _mm(xb, w_ref[:, j * D:(j + 1) * D])
        o = 4 * D
        ql_ref[...] = _mm(xb, w_ref[:, o:o + Q_LORA])
        kvl_ref[...] = _mm(xb, w_ref[:, o + Q_LORA:o + Q_LORA + KV_LORA])
        kr_ref[...] = _mm(xb, w_ref[:, o + Q_LORA + KV_LORA:PROJ_W])

    row = lambda i: (i, 0)
    return pl.pallas_call(
        body, name="inproj_fwd", grid=(n // tm,),
        in_specs=[pl.BlockSpec((tm, D), row), _resident((D, PROJ_W))],
        out_specs=[pl.BlockSpec((tm, 4 * D), row), pl.BlockSpec((tm, Q_LORA), row),
                   pl.BlockSpec((tm, KV_LORA), row), pl.BlockSpec((tm, LANES), row)],
        out_shape=[jax.ShapeDtypeStruct((n, 4 * D), F32), jax.ShapeDtypeStruct((n, Q_LORA), F32),
                   jax.ShapeDtypeStruct((n, KV_LORA), F32), jax.ShapeDtypeStruct((n, LANES), F32)],
        compiler_params=_params(VMEM_BIG),
    )(x, w_in_p)


def inproj_bwd(dxr, dg3, dql, dkvl, dkr, dz1, w_in_p, tm):
    n = dz1.shape[0]

    def body(dxr_ref, dg3_ref, dql_ref, dkvl_ref, dkr_ref, dz_ref, w_ref, dx_ref, dp_ref):
        dp = jnp.concatenate([dxr_ref[...], dg3_ref[...], dql_ref[...], dkvl_ref[...], dkr_ref[...]],
                             axis=1).astype(MXU_DTYPE)
        dp_ref[...] = dp
        dx_ref[...] = ALPHA * dz_ref[...] + _mm_nt(dp, w_ref[...])

    row = lambda i: (i, 0)
    return pl.pallas_call(
        body, name="inproj_bwd", grid=(n // tm,),
        in_specs=[pl.BlockSpec((tm, D), row), pl.BlockSpec((tm, 3 * D), row), pl.BlockSpec((tm, Q_LORA), row),
                  pl.BlockSpec((tm, KV_LORA), row), pl.BlockSpec((tm, LANES), row), pl.BlockSpec((tm, D), row),
                  _resident((D, PROJ_W))],
        out_specs=[pl.BlockSpec((tm, D), row), pl.BlockSpec((tm, PROJ_W), row)],
        out_shape=[jax.ShapeDtypeStruct((n, D), F32), jax.ShapeDtypeStruct((n, PROJ_W), MXU_DTYPE)],
        compiler_params=_params(VMEM_BIG),
    )(dxr, dg3, dql, dkvl, dkr, dz1, w_in_p)


def matmul_dw(x, dy, tn, tmc, name):
    n, k = x.shape
    m = dy.shape[1]

    def body(x_ref, dy_ref, dw_ref):
        @pl.when(pl.program_id(1) == 0)
        def _():
            dw_ref[...] = jnp.zeros_like(dw_ref)
        dw_ref[...] += _mm_tn(x_ref[...], dy_ref[...])

    return pl.pallas_call(
        body, name=name, grid=(m // tmc, n // tn),
        in_specs=[pl.BlockSpec((tn, k), lambda j, i: (i, 0)), pl.BlockSpec((tn, tmc), lambda j, i: (i, j))],
        out_specs=pl.BlockSpec((k, tmc), lambda j, i: (0, j)),
        out_shape=jax.ShapeDtypeStruct((k, m), F32),
        compiler_params=_params(VMEM_BIG),
    )(x, dy)


def matmul_dx(dy, w, add, add_scale, tm, name):
    n, m = dy.shape
    k = w.shape[0]

    def body(dy_ref, w_ref, add_ref, dx_ref):
        dx_ref[...] = add_scale * add_ref[...] + _mm_nt(dy_ref[...], w_ref[...])

    row = lambda i: (i, 0)
    return pl.pallas_call(
        body, name=name, grid=(n // tm,),
        in_specs=[pl.BlockSpec((tm, m), row), _resident((k, m)), pl.BlockSpec((tm, k), row)],
        out_specs=pl.BlockSpec((tm, k), row),
        out_shape=jax.ShapeDtypeStruct((n, k), F32),
        compiler_params=_params(VMEM_BIG),
    )(dy, w, add)


def _group(g):
    return slice(g * MXU_GROUP, (g + 1) * MXU_GROUP)


def _rnn_gates(x, halo, g, l, cw_ref, cb_ref, wgx_ref, bgx_ref, wga_ref, bga_ref, lam_ref):
    b, ts, gw = x.shape
    sl = _group(g)
    lr = slice(l, l + 1)
    xc = cb_ref[lr, sl][None]
    for k in range(CONV_W):
        xc = xc + cw_ref[k:k + 1, sl][None] * _shift_down(x, halo, CONV_W - 1 - k, 1)
    xc2 = xc.reshape(b * ts, gw)
    xcb = xc2.astype(MXU_DTYPE)
    gx = _sigmoid(_mm(xcb, wgx_ref[g]) + bgx_ref[lr, sl])
    ga = _sigmoid(_mm(xcb, wga_ref[g]) + bga_ref[lr, sl])
    nl = -lam_ref[lr, sl]
    sp = jnp.maximum(nl, 0.0) + jnp.log1p(jnp.exp(-jnp.abs(nl)))
    log_a = (-LRU_C) * ga * sp
    a = jnp.exp(log_a)
    mult = jnp.sqrt(_neg_expm1(2.0 * log_a))
    return xc2, xcb, gx, ga, sp, a, mult


def rnn_fwd(rnn4, lw, l, b, s, ts):
    ns = s // ts

    def body(x_ref, cw_ref, cb_ref, wgx_ref, bgx_ref, wga_ref, bga_ref, lam_ref, h_ref,
             halo_sc, hstate_sc, a_sc, u_sc):
        @pl.when(pl.program_id(0) == 0)
        def _():
            halo_sc[...] = jnp.zeros_like(halo_sc)
            hstate_sc[...] = jnp.zeros_like(hstate_sc)

        for g in range(N_GATE_GROUPS):
            sl = _group(g)
            x = x_ref[:, :, sl]
            xc2, _, gx, _, _, a, mult = _rnn_gates(x, halo_sc[:, :, sl], g, l, cw_ref, cb_ref, wgx_ref, bgx_ref,
                                                   wga_ref, bga_ref, lam_ref)
            halo_sc[:, :, sl] = x[:, ts - SUBLANES:, :]
            a_sc[...] = a.reshape(b, ts, MXU_GROUP)
            u_sc[...] = (mult * gx * xc2).reshape(b, ts, MXU_GROUP)

            def step(t, h, sl=sl):
                h = a_sc[:, pl.ds(t, 1), :] * h + u_sc[:, pl.ds(t, 1), :]
                h_ref[:, pl.ds(t, 1), sl] = h
                return h

            hstate_sc[:, :, sl] = lax.fori_loop(0, ts, step, hstate_sc[:, :, sl], unroll=8)

    tile = lambda i: (0, i, 0)
    vecs = _const_spec((DEPTH, D))
    gates = _layer_spec((N_GATE_GROUPS, MXU_GROUP, MXU_GROUP), l)
    return pl.pallas_call(
        body, name="rnn_fwd", grid=(ns,),
        in_specs=[pl.BlockSpec((b, ts, D), tile), _const_spec((CONV_W, D)), vecs, gates, vecs, gates, vecs, vecs],
        out_specs=pl.BlockSpec((b, ts, D), tile),
        out_shape=jax.ShapeDtypeStruct((b, s, D), F32),
        scratch_shapes=[pltpu.VMEM((b, SUBLANES, D), F32), pltpu.VMEM((b, 1, D), F32),
                        pltpu.VMEM((b, ts, MXU_GROUP), F32), pltpu.VMEM((b, ts, MXU_GROUP), F32)],
        compiler_params=_params(VMEM_BIG),
    )(rnn4, lw['conv_w'], lw['conv_b'], lw['wgx'], lw['gx_b'], lw['wga'], lw['ga_b'], lw['lru_lambda'])


def rnn_bwd(dh, rnn4, h, lw, l, b, s, ts):
    ns = s // ts
    hb = ts // SUBLANES

    def body(dh_ref, x_ref, xh_ref, h_ref, hh_ref, cw_ref, cb_ref, wgx_ref, bgx_ref, wga_ref, bga_ref, lam_ref,
             dx_ref, dcw_ref, dcb_ref, dwgx_ref, dbgx_ref, dwga_ref, dbga_ref, dlam_ref,
             carry_sc, dxc_halo_sc, a_sc, delta_sc):
        i = pl.program_id(0)

        @pl.when(i == 0)
        def _():
            carry_sc[...] = jnp.zeros_like(carry_sc)
            dxc_halo_sc[...] = jnp.zeros_like(dxc_halo_sc)
            for r in (dcw_ref, dcb_ref, dwgx_ref, dbgx_ref, dwga_ref, dbga_ref, dlam_ref):
                r[...] = jnp.zeros_like(r)

        keep = jnp.where(i == ns - 1, 0.0, 1.0)
        for g in range(N_GATE_GROUPS):
            sl = _group(g)
            x = x_ref[:, :, sl]
            xhalo = xh_ref[:, :, sl] * keep
            xc2, xcb, gx, ga, sp, a, mult = _rnn_gates(x, xhalo, g, l, cw_ref, cb_ref, wgx_ref, bgx_ref, wga_ref,
                                                       bga_ref, lam_ref)
            a_sc[...] = a.reshape(b, ts, MXU_GROUP)

            def step(j, c, sl=sl):
                t = ts - 1 - j
                d = dh_ref[:, pl.ds(t, 1), sl] + c
                delta_sc[:, pl.ds(t, 1), :] = d
                return a_sc[:, pl.ds(t, 1), :] * d

            carry_sc[:, :, sl] = lax.fori_loop(0, ts, step, carry_sc[:, :, sl], unroll=8)

            delta = delta_sc[...].reshape(b * ts, MXU_GROUP)
            hprev = _shift_down(h_ref[:, :, sl], hh_ref[:, :, sl] * keep, 1, 1).reshape(b * ts, MXU_GROUP)
            dmult = delta * gx * xc2
            dl = delta * hprev * a - dmult * (a * a) / mult
            dga = dl * ((-LRU_C) * sp)
            dlam_ref[:, sl] += (jnp.sum(dl * ((-LRU_C) * ga), axis=0, keepdims=True)
                                * (-_sigmoid(-lam_ref[l:l + 1, sl])))
            dpa = dga * ga * (1.0 - ga)
            dpx = (delta * mult * xc2) * gx * (1.0 - gx)
            dbga_ref[:, sl] += jnp.sum(dpa, axis=0, keepdims=True)
            dbgx_ref[:, sl] += jnp.sum(dpx, axis=0, keepdims=True)
            dpab = dpa.astype(MXU_DTYPE)
            dpxb = dpx.astype(MXU_DTYPE)
            dwga_ref[g] += _mm_tn(xcb, dpab)
            dwgx_ref[g] += _mm_tn(xcb, dpxb)
            dxc2 = delta * mult * gx + _mm_nt(dpab, wga_ref[g]) + _mm_nt(dpxb, wgx_ref[g])
            dcb_ref[:, sl] += jnp.sum(dxc2, axis=0, keepdims=True)
            dxc = dxc2.reshape(b, ts, MXU_GROUP)
            nhalo = dxc_halo_sc[:, :, sl]
            dx = jnp.zeros_like(dxc)
            for k in range(CONV_W):
                sft = CONV_W - 1 - k
                xs = _shift_down(x, xhalo, sft, 1)
                dcw_ref[k:k + 1, sl] += jnp.sum((dxc * xs).reshape(b * ts, MXU_GROUP), axis=0, keepdims=True)
                dx = dx + cw_ref[k:k + 1, sl][None] * _shift_up(dxc, nhalo, sft, 1)
            dx_ref[:, :, sl] = dx
            dxc_halo_sc[:, :, sl] = dxc[:, :SUBLANES, :]

    tile = lambda i: (0, ns - 1 - i, 0)
    halo = lambda i: (0, jnp.maximum((ns - 1 - i) * hb - 1, 0), 0)
    gshape = (N_GATE_GROUPS, MXU_GROUP, MXU_GROUP)
    vecs = _const_spec((DEPTH, D))
    gates = _layer_spec(gshape, l)
    vec = jax.ShapeDtypeStruct((1, D), F32)
    return pl.pallas_call(
        body, name="rnn_bwd", grid=(ns,),
        in_specs=[pl.BlockSpec((b, ts, D), tile), pl.BlockSpec((b, ts, D), tile),
                  pl.BlockSpec((b, SUBLANES, D), halo), pl.BlockSpec((b, ts, D), tile),
                  pl.BlockSpec((b, SUBLANES, D), halo),
                  _const_spec((CONV_W, D)), vecs, gates, vecs, gates, vecs, vecs],
        out_specs=[pl.BlockSpec((b, ts, D), tile), _const_spec((CONV_W, D)), _const_spec((1, D)),
                   _const_spec(gshape), _const_spec((1, D)), _const_spec(gshape), _const_spec((1, D)),
                   _const_spec((1, D))],
        out_shape=[jax.ShapeDtypeStruct((b, s, D), F32), jax.ShapeDtypeStruct((CONV_W, D), F32), vec,
                   jax.ShapeDtypeStruct(gshape, F32), vec, jax.ShapeDtypeStruct(gshape, F32), vec, vec],
        scratch_shapes=[pltpu.VMEM((b, 1, D), F32), pltpu.VMEM((b, SUBLANES, D), F32),
                        pltpu.VMEM((b, ts, MXU_GROUP), F32), pltpu.VMEM((b, ts, MXU_GROUP), F32)],
        compiler_params=_params(VMEM_BIG),
    )(dh, rnn4, rnn4, h, h, lw['conv_w'], lw['conv_b'], lw['wgx'], lw['gx_b'], lw['wga'], lw['ga_b'],
      lw['lru_lambda'])


def _rope(x, cos, sa, sb):
    return x * cos + pltpu.roll(x, HEAD_LANES - QK_ROPE // 2, 1) * sa + pltpu.roll(x, QK_ROPE // 2, 1) * sb


def _unrope(d, cos, sa, sb):
    return d * cos + pltpu.roll(d * sa, QK_ROPE // 2, 1) + pltpu.roll(d * sb, HEAD_LANES - QK_ROPE // 2, 1)


def _scores(q_blk, keys):
    return _mm_nt(q_blk, keys) * ATT_SCALE


def _diag_scores(q_blk, keys):
    tq = q_blk.shape[0]
    keep = lax.broadcasted_iota(jnp.int32, (tq, tq), 0) >= lax.broadcasted_iota(jnp.int32, (tq, tq), 1)
    return jnp.where(keep, _scores(q_blk, keys), NEG_INF)


def _mla_project(ql_ref, kvl_ref, l, gq_ref, gkv_ref, wq_ref, wkv_ref):
    qn, _ = _rms_stats(ql_ref[0])
    qn = (qn * gq_ref[l:l + 1, :]).astype(MXU_DTYPE)
    kvn, _ = _rms_stats(kvl_ref[0])
    kvn = (kvn * gkv_ref[l:l + 1, :]).astype(MXU_DTYPE)
    return qn, kvn, _mm(qn, wq_ref[0]), _mm(kvn, wkv_ref[0])


def mla_fwd(ql, kvl, kr, tabs, lw, l, b, s, tq, ex=None):
    nq = s // tq
    cos_t, sa_t, sb_t = tabs
    n_in, n_out = 10, 2
    x_args, x_in, x_out, x_shapes, x_alias, x_scratch = _fuse_exchange(ex, n_in, n_out)

    def body(*refs):
        bi, p = pl.program_id(0), pl.program_id(1)
        first = (bi == 0) & (p == 0)
        last = (bi == b - 1) & (p == N_HEAD_PAIRS - 1)

        @_run_exchange(ex, refs, n_in, n_out, 0, first, last)
        def _():
            compute(*refs[:n_in], *refs[n_in + len(x_in):n_in + len(x_in) + n_out])

    def compute(ql_ref, kvl_ref, kr_ref, cos_ref, sa_ref, sb_ref, gq_ref, gkv_ref, wq_ref, wkv_ref, o_ref, lse_ref):
        _, _, qp, kvp = _mla_project(ql_ref, kvl_ref, l, gq_ref, gkv_ref, wq_ref, wkv_ref)
        cos, sa, sb = cos_ref[0], sa_ref[0], sb_ref[0]
        for hh in range(HEADS_PER_STEP):
            hs = slice(hh * HEAD_LANES, (hh + 1) * HEAD_LANES)
            q = _rope(qp[:, hs], cos, sa, sb).astype(MXU_DTYPE)
            k = _rope(kvp[:, hs] + kr_ref[0], cos, sa, sb).astype(MXU_DTYPE)
            v = kvp[:, HEADS_PER_STEP * HEAD_LANES + hh * HEAD_LANES:
                    HEADS_PER_STEP * HEAD_LANES + (hh + 1) * HEAD_LANES].astype(MXU_DTYPE)
            for qb in range(nq):
                lo, hi = qb * tq, (qb + 1) * tq
                sd = _diag_scores(q[lo:hi], k[lo:hi])
                m = jnp.max(sd, axis=-1, keepdims=True)
                if qb:
                    sf = _scores(q[lo:hi], k[:lo])
                    m = jnp.maximum(m, jnp.max(sf, axis=-1, keepdims=True))
                ed = jnp.exp(sd - m)
                den = jnp.sum(ed, axis=-1, keepdims=True)
                o = _mm(ed, v[lo:hi])
                if qb:
                    ef = jnp.exp(sf - m)
                    den = den + jnp.sum(ef, axis=-1, keepdims=True)
                    o = o + _mm(ef, v[:lo])
                o = o * (1.0 / den)
                lse_ref[0, hh, lo:hi, :] = jnp.broadcast_to(m + jnp.log(den), (tq, LANES))
                if hh == 0:
                    o_ref[0, lo:hi, :] = o
                else:
                    o_ref[0, lo:hi, :] += o

    seq = lambda bi, p: (bi, 0, 0)
    pair = lambda bi, p: (p, 0, 0)

    def per_seq(w):
        return pl.BlockSpec((1, s, w), seq, pipeline_mode=pl.Buffered(1))

    return pl.pallas_call(
        body, name="mla_fwd", grid=(b, N_HEAD_PAIRS),
        in_specs=[per_seq(Q_LORA), per_seq(KV_LORA), per_seq(LANES), per_seq(LANES), per_seq(LANES), per_seq(LANES),
                  _const_spec((DEPTH, Q_LORA)), _const_spec((DEPTH, KV_LORA)),
                  pl.BlockSpec((1, Q_LORA, HEADS_PER_STEP * HEAD_LANES), pair),
                  pl.BlockSpec((1, KV_LORA, 2 * HEADS_PER_STEP * HEAD_LANES), pair)] + x_in,
        out_specs=[pl.BlockSpec((1, s, LANES), lambda bi, p: (bi, 0, p)),
                   pl.BlockSpec((1, HEADS_PER_STEP, s, LANES), lambda bi, p: (bi, p, 0, 0))] + x_out,
        out_shape=[jax.ShapeDtypeStruct((b, s, D), F32), jax.ShapeDtypeStruct((b, N_HEADS, s, LANES), F32)] + x_shapes,
        input_output_aliases=x_alias, scratch_shapes=x_scratch,
        compiler_params=_params(VMEM_BIG),
    )(ql, kvl, kr, cos_t, sa_t, sb_t, lw['q_norm_g'], lw['kv_norm_g'], lw['wq_pairs'], lw['wkv_pairs'], *x_args)


def mla_bwd(dy, y, lse, ql, kvl, kr, tabs, lw, l, b, s, tq, ex=None):
    nq = s // tq
    cos_t, sa_t, sb_t = tabs
    qw = HEADS_PER_STEP * HEAD_LANES
    kvw = 2 * HEADS_PER_STEP * HEAD_LANES
    n_in, n_out, n_scratch = 13, 7, 2
    x_args, x_in, x_out, x_shapes, x_alias, x_scratch = _fuse_exchange(ex, n_in, n_out)

    def body(*refs):
        bi, p = pl.program_id(0), pl.program_id(1)
        first = (bi == 0) & (p == 0)
        last = (bi == b - 1) & (p == N_HEAD_PAIRS - 1)
        o0 = n_in + len(x_in)
        s0 = o0 + n_out + len(x_out)

        @_run_exchange(ex, refs, n_in, n_out, n_scratch, first, last)
        def _():
            compute(*refs[:n_in], *refs[o0:o0 + n_out], *refs[s0:s0 + n_scratch])

    def compute(dy_ref, y_ref, lse_ref, ql_ref, kvl_ref, kr_ref, cos_ref, sa_ref, sb_ref, gq_ref, gkv_ref, wq_ref,
                wkv_ref, dql_ref, dkvl_ref, dkr_ref, dwq_ref, dwkv_ref, dgq_ref, dgkv_ref, dk_sc, dv_sc):
        bi, p = pl.program_id(0), pl.program_id(1)

        @pl.when((bi == 0) & (p == 0))
        def _():
            for r in (dwq_ref, dwkv_ref, dgq_ref, dgkv_ref):
                r[...] = jnp.zeros_like(r)

        @pl.when(p == 0)
        def _():
            for r in (dql_ref, dkvl_ref, dkr_ref):
                r[...] = jnp.zeros_like(r)

        qn, kvn, qp, kvp = _mla_project(ql_ref, kvl_ref, l, gq_ref, gkv_ref, wq_ref, wkv_ref)
        cos, sa, sb = cos_ref[0], sa_ref[0], sb_ref[0]
        dof = dy_ref[0]
        do = dof.astype(MXU_DTYPE)
        prod = dof * y_ref[0]
        lane = lax.broadcasted_iota(jnp.int32, prod.shape, 1)
        dq_heads, dk_heads, dv_heads = [], [], []
        for hh in range(HEADS_PER_STEP):
            hs = slice(hh * HEAD_LANES, (hh + 1) * HEAD_LANES)
            q = _rope(qp[:, hs], cos, sa, sb).astype(MXU_DTYPE)
            k = _rope(kvp[:, hs] + kr_ref[0], cos, sa, sb).astype(MXU_DTYPE)
            v = kvp[:, qw + hh * HEAD_LANES:qw + (hh + 1) * HEAD_LANES].astype(MXU_DTYPE)
            mine = (lane >= hh * V_HEAD) & (lane < (hh + 1) * V_HEAD)
            delta = jnp.sum(jnp.where(mine, prod, 0.0), axis=-1, keepdims=True)
            dk_sc[...] = jnp.zeros_like(dk_sc)
            dv_sc[...] = jnp.zeros_like(dv_sc)
            dq_blocks = []
            for qb in range(nq):
                lo, hi = qb * tq, (qb + 1) * tq
                lse = lse_ref[0, hh, lo:hi, 0:1]
                dl = delta[lo:hi]
                dq = None
                for ks, diag in (((slice(lo, hi), True),) + (((slice(0, lo), False),) if qb else ())):
                    sc = _diag_scores(q[lo:hi], k[ks]) if diag else _scores(q[lo:hi], k[ks])
                    pr = jnp.exp(sc - lse)
                    dp = _mm_nt(do[lo:hi], v[ks])
                    dv_sc[ks, :] += _mm_tn(pr, do[lo:hi])
                    ds = (pr * (dp - dl)).astype(MXU_DTYPE)
                    part = _mm(ds, k[ks])
                    dq = part if dq is None else dq + part
                    dk_sc[ks, :] += _mm_tn(ds, q[lo:hi])
                dq_blocks.append(dq)
            dq_heads.append(_unrope(jnp.concatenate(dq_blocks, axis=0) * ATT_SCALE, cos, sa, sb))
            dk_full = _unrope(dk_sc[...] * ATT_SCALE, cos, sa, sb)
            dkr_ref[0] += dk_full
            dk_heads.append(dk_full)
            dv_heads.append(dv_sc[...])
        dqp = jnp.concatenate(dq_heads, axis=1).astype(MXU_DTYPE)
        dkvp = jnp.concatenate(dk_heads + dv_heads, axis=1).astype(MXU_DTYPE)
        dwq_ref[p] += _mm_tn(qn, dqp)
        dwkv_ref[p] += _mm_tn(kvn, dkvp)
        dql_ref[0] += _mm_nt(dqp, wq_ref[0])
        dkvl_ref[0] += _mm_nt(dkvp, wkv_ref[0])

        @pl.when(p == N_HEAD_PAIRS - 1)
        def _():
            dx, dg = _rms_bwd(dql_ref[0], ql_ref[0], gq_ref[l:l + 1, :])
            dql_ref[0] = dx
            dgq_ref[...] += dg
            dx, dg = _rms_bwd(dkvl_ref[0], kvl_ref[0], gkv_ref[l:l + 1, :])
            dkvl_ref[0] = dx
            dgkv_ref[...] += dg

    seq = lambda bi, p: (bi, 0, 0)
    pair = lambda bi, p: (p, 0, 0)

    def per_seq(w):
        return pl.BlockSpec((1, s, w), seq, pipeline_mode=pl.Buffered(1))

    return pl.pallas_call(
        body, name="mla_bwd", grid=(b, N_HEAD_PAIRS),
        in_specs=[pl.BlockSpec((1, s, LANES), lambda bi, p: (bi, 0, p)),
                  pl.BlockSpec((1, s, LANES), lambda bi, p: (bi, 0, p)),
                  pl.BlockSpec((1, HEADS_PER_STEP, s, LANES), lambda bi, p: (bi, p, 0, 0)),
                  per_seq(Q_LORA), per_seq(KV_LORA), per_seq(LANES), per_seq(LANES), per_seq(LANES), per_seq(LANES),
                  _const_spec((DEPTH, Q_LORA)), _const_spec((DEPTH, KV_LORA)),
                  pl.BlockSpec((1, Q_LORA, qw), pair), pl.BlockSpec((1, KV_LORA, kvw), pair)] + x_in,
        out_specs=[pl.BlockSpec((1, s, Q_LORA), seq), pl.BlockSpec((1, s, KV_LORA), seq),
                   pl.BlockSpec((1, s, LANES), seq),
                   _const_spec((N_HEAD_PAIRS, Q_LORA, qw)), _const_spec((N_HEAD_PAIRS, KV_LORA, kvw)),
                   _const_spec((1, Q_LORA)), _const_spec((1, KV_LORA))] + x_out,
        out_shape=[jax.ShapeDtypeStruct((b, s, Q_LORA), F32), jax.ShapeDtypeStruct((b, s, KV_LORA), F32),
                   jax.ShapeDtypeStruct((b, s, LANES), F32),
                   jax.ShapeDtypeStruct((N_HEAD_PAIRS, Q_LORA, qw), F32),
                   jax.ShapeDtypeStruct((N_HEAD_PAIRS, KV_LORA, kvw), F32),
                   jax.ShapeDtypeStruct((1, Q_LORA), F32), jax.ShapeDtypeStruct((1, KV_LORA), F32)] + x_shapes,
        input_output_aliases=x_alias,
        scratch_shapes=[pltpu.VMEM((s, HEAD_LANES), F32), pltpu.VMEM((s, HEAD_LANES), F32)] + x_scratch,
        compiler_params=_params(VMEM_BIG),
    )(dy, y, lse, ql, kvl, kr, cos_t, sa_t, sb_t, lw['q_norm_g'], lw['kv_norm_g'], lw['wq_pairs'], lw['wkv_pairs'],
      *x_args)


COL_CHUNK = 256


def _merge(g_rnn, gate_a, gate_b, h, y_mla):
    ge, t = _gelu(g_rnn)
    sa, sb = _sigmoid(gate_a), _sigmoid(gate_b)
    y_rnn = ge * h
    return ge, t, sa, sb, y_rnn, sa * y_rnn + sb * y_mla


def mixout_fwd(x, rnn4, h, y_mla, lw, l, tm):
    n = x.shape[0]

    def body(x_ref, gr_ref, gta_ref, gtb_ref, h_ref, y_ref, w_ref, g_ref, b_ref, z_ref, o_ref):
        z = ALPHA * x_ref[...]
        for c in range(0, D, COL_CHUNK):
            cs = slice(c, c + COL_CHUNK)
            merged = _merge(gr_ref[:, cs], gta_ref[:, cs], gtb_ref[:, cs], h_ref[:, cs], y_ref[:, cs])[-1]
            z = z + _mm(merged, w_ref[cs, :])
        z_ref[...] = z
        o_ref[...] = _ln_stats(z)[0] * g_ref[l:l + 1, :] + b_ref[l:l + 1, :]

    row = lambda i: (i, 0)
    col = lambda j: (lambda i: (i, j))
    blk = pl.BlockSpec((tm, D), row)
    return pl.pallas_call(
        body, name="mixout_fwd", grid=(n // tm,),
        in_specs=[blk, pl.BlockSpec((tm, D), col(1)), pl.BlockSpec((tm, D), col(2)), pl.BlockSpec((tm, D), col(3)),
                  blk, blk, _resident((D, D)), _const_spec((DEPTH, D)), _const_spec((DEPTH, D))],
        out_specs=[blk, blk],
        out_shape=[jax.ShapeDtypeStruct((n, D), F32), jax.ShapeDtypeStruct((n, D), F32)],
        compiler_params=_params(VMEM_BIG),
    )(x, rnn4, rnn4, rnn4, h, y_mla, lw['w_out'], lw['ln1_g'], lw['ln1_b'])


def mixout_bwd(dx1, z1, rnn4, h, y_mla, lw, l, tm):
    n = dx1.shape[0]

    def body(d_ref, z_ref, gr_ref, gta_ref, gtb_ref, h_ref, y_ref, w_ref, g_ref,
             dz_ref, dh_ref, dy_ref, dg3_ref, dw_ref, dg_ref, db_ref):
        @pl.when(pl.program_id(0) == 0)
        def _():
            for r in (dw_ref, dg_ref, db_ref):
                r[...] = jnp.zeros_like(r)

        dz, dg, db = _ln_bwd(d_ref[...], z_ref[...], g_ref[l:l + 1, :])
        dz_ref[...] = dz
        dg_ref[...] += dg
        db_ref[...] += db
        dzb = dz.astype(MXU_DTYPE)
        for c in range(0, D, COL_CHUNK):
            cs = slice(c, c + COL_CHUNK)
            g_rnn, h, y_mla = gr_ref[:, cs], h_ref[:, cs], y_ref[:, cs]
            ge, t, sa, sb, y_rnn, merged = _merge(g_rnn, gta_ref[:, cs], gtb_ref[:, cs], h, y_mla)
            dw_ref[cs, :] += _mm_tn(merged, dzb)
            dm = _mm_nt(dzb, w_ref[cs, :])
            dy_rnn = dm * sa
            dy_ref[:, cs] = dm * sb
            dh_ref[:, cs] = dy_rnn * ge
            dg3_ref[:, c:c + COL_CHUNK] = dy_rnn * h * _gelu_grad(g_rnn, t)
            dg3_ref[:, D + c:D + c + COL_CHUNK] = dm * y_rnn * sa * (1.0 - sa)
            dg3_ref[:, 2 * D + c:2 * D + c + COL_CHUNK] = dm * y_mla * sb * (1.0 - sb)

    row = lambda i: (i, 0)
    col = lambda j: (lambda i: (i, j))
    blk = pl.BlockSpec((tm, D), row)
    vec = jax.ShapeDtypeStruct((1, D), F32)
    act = jax.ShapeDtypeStruct((n, D), F32)
    return pl.pallas_call(
        body, name="mixout_bwd", grid=(n // tm,),
        in_specs=[blk, blk, pl.BlockSpec((tm, D), col(1)), pl.BlockSpec((tm, D), col(2)),
                  pl.BlockSpec((tm, D), col(3)), blk, blk, _resident((D, D)), _const_spec((DEPTH, D))],
        out_specs=[blk, blk, blk, pl.BlockSpec((tm, 3 * D), row), _const_spec((D, D)), _const_spec((1, D)),
                   _const_spec((1, D))],
        out_shape=[act, act, act, jax.ShapeDtypeStruct((n, 3 * D), F32), jax.ShapeDtypeStruct((D, D), F32), vec, vec],
        compiler_params=_params(VMEM_BIG),
    )(dx1, z1, rnn4, rnn4, rnn4, h, y_mla, lw['w_out'], lw['ln1_g'])


FFN_CHUNK = 512


def _conv3(u, halo, cs, l, fcw_ref, fcb_ref):
    hc = fcb_ref[l:l + 1, cs]
    for k in range(FFN_CONV_W):
        hc = hc + fcw_ref[k:k + 1, cs] * _shift_down(u, halo, FFN_CONV_W - 1 - k, 0)
    return hc


def ffn_fwd(x1, lw, l, b, s, ts):
    ns = s // ts
    n = b * s

    def body(x_ref, wu_ref, fcw_ref, fcb_ref, wd_ref, g_ref, b_ref, up_ref, z_ref, o_ref, halo_sc):
        @pl.when(pl.program_id(1) == 0)
        def _():
            halo_sc[...] = jnp.zeros_like(halo_sc)

        x = x_ref[...]
        xb = x.astype(MXU_DTYPE)
        z = ALPHA * x
        for c in range(0, D_FF, FFN_CHUNK):
            gs, vs = slice(c, c + FFN_CHUNK), slice(D_FF + c, D_FF + c + FFN_CHUNK)
            ug, uv = _mm(xb, wu_ref[:, gs]), _mm(xb, wu_ref[:, vs])
            up_ref[:, gs] = ug
            up_ref[:, vs] = uv
            hg = _conv3(ug, halo_sc[:, gs], gs, l, fcw_ref, fcb_ref)
            hv = _conv3(uv, halo_sc[:, vs], vs, l, fcw_ref, fcb_ref)
            halo_sc[:, gs] = ug[ts - SUBLANES:, :]
            halo_sc[:, vs] = uv[ts - SUBLANES:, :]
            z = z + _mm(_gelu(hg)[0] * hv, wd_ref[c:c + FFN_CHUNK, :])
        z_ref[...] = z
        o_ref[...] = _ln_stats(z)[0] * g_ref[l:l + 1, :] + b_ref[l:l + 1, :]

    row = lambda bi, i: (bi * ns + i, 0)
    blk = pl.BlockSpec((ts, D), row)
    return pl.pallas_call(
        body, name="ffn_fwd", grid=(b, ns),
        in_specs=[blk, _resident((D, 2 * D_FF)), _const_spec((FFN_CONV_W, 2 * D_FF)), _const_spec((DEPTH, 2 * D_FF)),
                  _resident((D_FF, D)), _const_spec((DEPTH, D)), _const_spec((DEPTH, D))],
        out_specs=[pl.BlockSpec((ts, 2 * D_FF), row), blk, blk],
        out_shape=[jax.ShapeDtypeStruct((n, 2 * D_FF), F32), jax.ShapeDtypeStruct((n, D), F32),
                   jax.ShapeDtypeStruct((n, D), F32)],
        scratch_shapes=[pltpu.VMEM((SUBLANES, 2 * D_FF), F32)],
        compiler_params=_params(VMEM_BIG),
    )(x1, lw['w_up'], lw['ffn_conv_w'], lw['ffn_conv_b'], lw['w_down'], lw['ln2_g'], lw['ln2_b'])


def ffn_bwd(dx2, z2, up, lw, l, b, s, ts):
    ns = s // ts
    n = b * s
    hb = ts // SUBLANES

    def body(d_ref, z_ref, up_ref, uph_ref, fcw_ref, fcb_ref, wd_ref, g_ref,
             dz_ref, dup_ref, act_ref, dfcw_ref, dfcb_ref, dg_ref, db_ref, nhalo_sc):
        bi, i = pl.program_id(0), pl.program_id(1)

        @pl.when((bi == 0) & (i == 0))
        def _():
            for r in (dfcw_ref, dfcb_ref, dg_ref, db_ref):
                r[...] = jnp.zeros_like(r)

        @pl.when(i == 0)
        def _():
            nhalo_sc[...] = jnp.zeros_like(nhalo_sc)

        dz, dg, db = _ln_bwd(d_ref[...], z_ref[...], g_ref[l:l + 1, :])
        dz_ref[...] = dz
        dg_ref[...] += dg
        db_ref[...] += db
        dzb = dz.astype(MXU_DTYPE)
        keep = jnp.where(i == ns - 1, 0.0, 1.0)
        for c in range(0, D_FF, FFN_CHUNK):
            gs, vs = slice(c, c + FFN_CHUNK), slice(D_FF + c, D_FF + c + FFN_CHUNK)
            ug, uv = up_ref[:, gs], up_ref[:, vs]
            hg_halo, hv_halo = uph_ref[:, gs] * keep, uph_ref[:, vs] * keep
            hg = _conv3(ug, hg_halo, gs, l, fcw_ref, fcb_ref)
            hv = _conv3(uv, hv_halo, vs, l, fcw_ref, fcb_ref)
            ge, t = _gelu(hg)
            act_ref[:, gs] = (ge * hv).astype(MXU_DTYPE)
            dact = _mm_nt(dzb, wd_ref[c:c + FFN_CHUNK, :])
            for cs, u, halo, dhc in ((gs, ug, hg_halo, dact * hv * _gelu_grad(hg, t)), (vs, uv, hv_halo, dact * ge)):
                dfcb_ref[:, cs] += jnp.sum(dhc, axis=0, keepdims=True)
                nhalo = nhalo_sc[:, cs]
                dup = jnp.zeros_like(dhc)
                for k in range(FFN_CONV_W):
                    sft = FFN_CONV_W - 1 - k
                    dfcw_ref[k:k + 1, cs] += jnp.sum(dhc * _shift_down(u, halo, sft, 0), axis=0, keepdims=True)
                    dup = dup + fcw_ref[k:k + 1, cs] * _shift_up(dhc, nhalo, sft, 0)
                dup_ref[:, cs] = dup.astype(MXU_DTYPE)
                nhalo_sc[:, cs] = dhc[:SUBLANES, :]

    row = lambda bi, i: (bi * ns + (ns - 1 - i), 0)
    halo = lambda bi, i: (jnp.maximum((bi * ns + (ns - 1 - i)) * hb - 1, 0), 0)
    blk = pl.BlockSpec((ts, D), row)
    wide = pl.BlockSpec((ts, 2 * D_FF), row)
    return pl.pallas_call(
        body, name="ffn_bwd", grid=(b, ns),
        in_specs=[blk, blk, wide, pl.BlockSpec((SUBLANES, 2 * D_FF), halo),
                  _const_spec((FFN_CONV_W, 2 * D_FF)), _const_spec((DEPTH, 2 * D_FF)), _resident((D_FF, D)),
                  _const_spec((DEPTH, D))],
        out_specs=[blk, wide, pl.BlockSpec((ts, D_FF), row), _const_spec((FFN_CONV_W, 2 * D_FF)),
                   _const_spec((1, 2 * D_FF)), _const_spec((1, D)), _const_spec((1, D))],
        out_shape=[jax.ShapeDtypeStruct((n, D), F32), jax.ShapeDtypeStruct((n, 2 * D_FF), MXU_DTYPE),
                   jax.ShapeDtypeStruct((n, D_FF), MXU_DTYPE), jax.ShapeDtypeStruct((FFN_CONV_W, 2 * D_FF), F32),
                   jax.ShapeDtypeStruct((1, 2 * D_FF), F32), jax.ShapeDtypeStruct((1, D), F32),
                   jax.ShapeDtypeStruct((1, D), F32)],
        scratch_shapes=[pltpu.VMEM((SUBLANES, 2 * D_FF), F32)],
        compiler_params=_params(VMEM_BIG),
    )(dx2, z2, up, up, lw['ffn_conv_w'], lw['ffn_conv_b'], lw['w_down'], lw['ln2_g'])


def loss_head(y, target, tm):
    n = y.shape[0]

    def body(y_ref, t_ref, l_ref, d_ref):
        @pl.when(pl.program_id(0) == 0)
        def _():
            l_ref[...] = jnp.zeros_like(l_ref)

        err = y_ref[...] - t_ref[...]
        d_ref[...] = err * (1.0 / D)
        part = jnp.sum(jnp.sum(err * err, axis=-1, keepdims=True), axis=0, keepdims=True)
        l_ref[...] += jnp.broadcast_to(part * (0.5 / D), l_ref.shape)

    row = lambda i: (i, 0)
    return pl.pallas_call(
        body, name="loss_head", grid=(n // tm,),
        in_specs=[pl.BlockSpec((tm, D), row), pl.BlockSpec((tm, D), row)],
        out_specs=[_const_spec((1, LANES)), pl.BlockSpec((tm, D), row)],
        out_shape=[jax.ShapeDtypeStruct((1, LANES), F32), jax.ShapeDtypeStruct((n, D), F32)],
    )(y, target)


def _adam_update(g, w, m, v):
    c1 = 1.0 - ADAM_B1 ** ADAM_STEP
    c2 = 1.0 - ADAM_B2 ** ADAM_STEP
    mn = ADAM_B1 * m + (1.0 - ADAM_B1) * g
    vn = ADAM_B2 * v + (1.0 - ADAM_B2) * (g * g)
    return -ADAM_LR * ((mn / c1) / (jnp.sqrt(vn / c2) + ADAM_EPS) + ADAM_WD * w), mn, vn


def adamw_tiled(parts, w, m, v, name):
    _, r, c = w.shape
    tr = next(t for t in (256, 128, 64, 32, 16, 8) if r % t == 0)

    def body(p_ref, w_ref, m_ref, v_ref, g_ref, d_ref, mo_ref, vo_ref):
        g = p_ref[0].astype(F32)
        for i in range(1, N_DEV):
            g = g + p_ref[i].astype(F32)
        g_ref[...] = g
        d_ref[...], mo_ref[...], vo_ref[...] = _adam_update(g, w_ref[...], m_ref[...], v_ref[...])

    blk = pl.BlockSpec((None, tr, c), lambda l, i: (l, i, 0))
    out = jax.ShapeDtypeStruct(w.shape, F32)
    return pl.pallas_call(
        body, name="adamw_" + name, grid=(DEPTH, r // tr),
        in_specs=[pl.BlockSpec((N_DEV, None, tr, c), lambda l, i: (0, l, i, 0)), blk, blk, blk],
        out_specs=[blk, blk, blk, blk],
        out_shape=[out, out, out, out],
    )(parts, w, m, v)


def adamw_small(items):
    k = len(items)

    def body(*refs):
        ins, outs = refs[:4 * k], refs[4 * k:]
        for j in range(k):
            p_ref, w_ref, m_ref, v_ref = ins[4 * j:4 * j + 4]
            g_ref, d_ref, mo_ref, vo_ref = outs[j], outs[k + j], outs[2 * k + j], outs[3 * k + j]
            if len(p_ref.shape) == len(w_ref.shape) + 1:
                g = p_ref[0]
                for i in range(1, N_DEV):
                    g = g + p_ref[i]
                g_ref[...] = g
                d_ref[...], mo_ref[...], vo_ref[...] = _adam_update(g, w_ref[...], m_ref[...], v_ref[...])
            else:
                for l in range(DEPTH):
                    lr = slice(l, l + 1)
                    g = p_ref[0, l]
                    for i in range(1, N_DEV):
                        g = g + p_ref[i, l]
                    g_ref[lr, :] = g
                    d_ref[lr, :], mo_ref[lr, :], vo_ref[lr, :] = _adam_update(g, w_ref[lr, :], m_ref[lr, :],
                                                                              v_ref[lr, :])

    flat = [a for item in items for a in item]
    outs = [jax.ShapeDtypeStruct(item[1].shape, F32) for item in items] * 4
    return pl.pallas_call(
        body, name="adamw_small",
        in_specs=[pl.BlockSpec(memory_space=pltpu.VMEM)] * len(flat),
        out_specs=[pl.BlockSpec(memory_space=pltpu.VMEM)] * len(outs),
        out_shape=outs,
        compiler_params=_params(VMEM_BIG),
    )(*flat)


class Exchange:
    def __init__(self, srcs, per_peer, src_layer, bufs, l):
        self.srcs, self.per_peer, self.src_layer, self.bufs, self.l = list(srcs), list(per_peer), src_layer, list(bufs), l
        self.na = len(self.srcs)

    def scratch(self):
        return [pltpu.SemaphoreType.DMA((self.na, N_DEV - 1)), pltpu.SemaphoreType.DMA((self.na, N_DEV - 1)),
                pltpu.SemaphoreType.DMA((self.na,))]

    def out_shapes(self):
        return [jax.ShapeDtypeStruct(bf.shape, bf.dtype) for bf in self.bufs]

    def copies(self, src_refs, buf_refs, send_sems, recv_sems, local_sems):
        x, y, c = lax.axis_index("x"), lax.axis_index("y"), lax.axis_index("c")
        me = 4 * x + 2 * y + c

        def view(a, pid):
            r = src_refs[a]
            if self.src_layer is not None:
                r = r.at[self.src_layer]
            return r.at[pid] if self.per_peer[a] else r

        out = [pltpu.make_async_copy(view(a, me), buf_refs[a].at[me, self.l], local_sems.at[a])
               for a in range(self.na)]
        for k in range(1, N_DEV):
            px = 1 - x if k & 4 else x
            py = 1 - y if k & 2 else y
            pc = 1 - c if k & 1 else c
            pid = 4 * px + 2 * py + pc
            for a in range(self.na):
                out.append(pltpu.make_async_remote_copy(
                    src_ref=view(a, pid), dst_ref=buf_refs[a].at[me, self.l],
                    send_sem=send_sems.at[a, k - 1], recv_sem=recv_sems.at[a, k - 1],
                    device_id=(px, py, pc), device_id_type=pl.DeviceIdType.MESH))
        return out


ANY_SPEC = pl.BlockSpec(memory_space=pl.ANY)


def exchange_layer(ex, name):
    na = ex.na

    def body(*refs):
        src_refs, buf_refs = refs[:na], refs[2 * na:3 * na]
        cps = ex.copies(src_refs, buf_refs, *refs[3 * na:])
        for cp in cps:
            cp.start()
        for cp in cps:
            cp.wait()

    return pl.pallas_call(
        body, name=name,
        in_specs=[ANY_SPEC] * (2 * na), out_specs=[ANY_SPEC] * na,
        out_shape=ex.out_shapes(),
        input_output_aliases={na + a: a for a in range(na)},
        scratch_shapes=ex.scratch(),
    )(*ex.srcs, *ex.bufs)


def _fuse_exchange(ex, n_in, n_out):
    if ex is None:
        return [], [], [], [], {}, []
    na = ex.na
    aliases = {n_in + na + a: n_out + a for a in range(na)}
    return ex.srcs + ex.bufs, [ANY_SPEC] * (2 * na), [ANY_SPEC] * na, ex.out_shapes(), aliases, ex.scratch()


def _run_exchange(ex, refs, n_in, n_out, n_scratch, first, last):
    def deco(compute):
        if ex is None:
            compute()
            return
        na = ex.na
        n_all_in = n_in + 2 * na
        src_refs = refs[n_in:n_in + na]
        buf_refs = refs[n_all_in + n_out:n_all_in + n_out + na]
        sems = refs[n_all_in + n_out + na + n_scratch:]

        @pl.when(first)
        def _():
            for cp in ex.copies(src_refs, buf_refs, *sems):
                cp.start()

        compute()

        @pl.when(last)
        def _():
            for cp in ex.copies(src_refs, buf_refs, *sems):
                cp.wait()
    return deco


def _permute_w_in(w):
    o = [0, D, 2 * D, 2 * D + Q_LORA, 2 * D + Q_LORA + KV_LORA, 2 * D + Q_LORA + KV_LORA + QK_ROPE,
         3 * D + Q_LORA + KV_LORA + QK_ROPE, IN_WIDTH]
    xr, gr, qlat, kvl, kr, ga, gb = [w[:, o[i]:o[i + 1]] for i in range(7)]
    z = lambda c: jnp.zeros((w.shape[0], c), w.dtype)
    return jnp.concatenate([xr, gr, ga, gb, qlat, kvl, z(QK_NOPE), kr, z(HEAD_LANES - QK_NOPE - QK_ROPE)], axis=1)


def _unpermute_dw_in(dw):
    o = 4 * D
    k0 = o + Q_LORA + KV_LORA + QK_NOPE
    return jnp.concatenate([dw[:, 0:2 * D], dw[:, o:o + Q_LORA + KV_LORA], dw[:, k0:k0 + QK_ROPE],
                            dw[:, 2 * D:4 * D]], axis=1)


def _pair_wq(w):
    w = w.reshape(Q_LORA, N_HEADS, QK_NOPE + QK_ROPE)
    w = jnp.pad(w, ((0, 0), (0, 0), (0, HEAD_LANES - QK_NOPE - QK_ROPE)))
    return w.reshape(Q_LORA, N_HEAD_PAIRS, HEADS_PER_STEP * HEAD_LANES).transpose(1, 0, 2)


def _unpair_dwq(dw):
    dw = dw.transpose(1, 0, 2).reshape(Q_LORA, N_HEADS, HEAD_LANES)
    return dw[:, :, :QK_NOPE + QK_ROPE].reshape(Q_LORA, N_HEADS * (QK_NOPE + QK_ROPE))


def _pair_wkv(w):
    w = w.reshape(KV_LORA, N_HEAD_PAIRS, HEADS_PER_STEP, QK_NOPE + V_HEAD)
    kn, vv = w[..., :QK_NOPE], w[..., QK_NOPE:]
    z = jnp.zeros_like(kn[:, :, 0])
    out = jnp.concatenate([kn[:, :, 0], z, kn[:, :, 1], z, vv[:, :, 0], z, z, vv[:, :, 1]], axis=-1)
    return out.transpose(1, 0, 2)


def _unpair_dwkv(dw):
    dw = dw.transpose(1, 0, 2)
    h0 = jnp.concatenate([dw[..., 0:64], dw[..., 256:320]], axis=-1)
    h1 = jnp.concatenate([dw[..., 128:192], dw[..., 448:512]], axis=-1)
    return jnp.stack([h0, h1], axis=2).reshape(KV_LORA, N_HEADS * (QK_NOPE + V_HEAD))


def _group_gates(w):
    per = MXU_GROUP // RNN_BLOCK
    w = w.reshape(DEPTH, N_GATE_GROUPS, per, RNN_BLOCK, RNN_BLOCK)
    eye = jnp.eye(per, dtype=w.dtype)
    return jnp.einsum('lgpij,pq->lgpiqj', w, eye).reshape(DEPTH, N_GATE_GROUPS, MXU_GROUP, MXU_GROUP)


def _ungroup_dgate(dw):
    per = MXU_GROUP // RNN_BLOCK
    dw = dw.reshape(N_GATE_GROUPS, per, RNN_BLOCK, per, RNN_BLOCK)
    return jnp.stack([dw[:, p, :, p, :] for p in range(per)], axis=1).reshape(N_RNN_BLOCKS, RNN_BLOCK, RNN_BLOCK)


def _rope_tables(positions):
    inv_freq = ROPE_THETA ** (-jnp.arange(0, QK_ROPE, 2, dtype=F32) / QK_ROPE)
    ang = positions.astype(F32)[..., None] * inv_freq
    cos, sin = jnp.cos(ang), jnp.sin(ang)
    one, zero = jnp.ones_like(cos), jnp.zeros_like(cos)
    nope = lambda v: jnp.concatenate([v] * (QK_NOPE // (QK_ROPE // 2)), axis=-1)
    tail = jnp.concatenate([zero, zero], axis=-1)
    cos_t = jnp.concatenate([nope(one), cos, cos, tail], axis=-1)
    sa_t = jnp.concatenate([nope(zero), -sin, zero, tail], axis=-1)
    sb_t = jnp.concatenate([nope(zero), zero, sin, tail], axis=-1)
    return cos_t, sa_t, sb_t


def _unshard(pieces, name):
    axis = SHARDED[name][0] - 1
    return jnp.concatenate([pieces[i] for i in range(N_DEV)], axis=axis)


def _shard_pieces(g, name):
    axis = SHARDED[name][0] - 1
    return jnp.stack(jnp.split(g, N_DEV, axis=axis))


def _layer_weights(gathered, shared, l):
    full = {name: _unshard(gathered[name][:, l], name) for name in SHARDED}
    lw = dict(shared)
    lw.update({
        'w_in_p': _permute_w_in(full['w_in']), 'conv_w': full['conv_w'],
        'wq_pairs': _pair_wq(full['w_uq']), 'wkv_pairs': _pair_wkv(full['w_ukv']),
        'w_out': full['w_out'], 'w_up': full['w_up'], 'ffn_conv_w': full['ffn_conv_w'], 'w_down': full['w_down'],
    })
    return lw


TM = 256
TS_RNN = 128
TS_FFN = 256
TQ = 256
TN_DW = 512


def layer_fwd(xc, tabs, lw, l, b, s, ex=None):
    n = b * s
    rnn4, ql, kvl, kr = inproj_fwd(xc, lw['w_in_p'], TM)
    h = rnn_fwd(rnn4.reshape(b, s, 4 * D), lw, l, b, s, TS_RNN)
    lat = (ql.reshape(b, s, Q_LORA), kvl.reshape(b, s, KV_LORA), kr.reshape(b, s, LANES))
    y_mla, lse, *xbufs = mla_fwd(*lat, tabs, lw, l, b, s, TQ, ex)
    z1, x1 = mixout_fwd(xc, rnn4, h.reshape(n, D), y_mla.reshape(n, D), lw, l, TM)
    up, z2, x2 = ffn_fwd(x1, lw, l, b, s, TS_FFN)
    return x2, (xc, rnn4, lat, h, y_mla, lse, z1, x1, up, z2), xbufs


def layer_bwd(dx, saved, tabs, lw, l, b, s, ex=None):
    n = b * s
    x0, rnn4, lat, h, y_mla, lse, z1, x1, up, z2 = saved
    dz2, dup, act, dfcw, dfcb, dg2, db2 = ffn_bwd(dx, z2, up, lw, l, b, s, TS_FFN)
    dx1 = matmul_dx(dup, lw['w_up'], dz2, ALPHA, TM, "ffn_up_dx")
    dw_up = matmul_dw(x1, dup, TN_DW, 2 * D_FF // 3, "ffn_up_dw")
    dw_down = matmul_dw(act, dz2, TN_DW, D // 2, "ffn_down_dw")
    dz1, dh, dy_mla, dg3, dw_out, dg1, db1 = mixout_bwd(dx1, z1, rnn4, h.reshape(n, D), y_mla.reshape(n, D),
                                                       lw, l, TM)
    dql, dkvl, dkr, dwq, dwkv, dgq, dgkv, *xbufs = mla_bwd(dy_mla.reshape(b, s, D), y_mla, lse, *lat, tabs, lw, l,
                                                           b, s, TQ, ex)
    dxr, dcw, dcb, dwgx, dbgx, dwga, dbga, dlam = rnn_bwd(dh.reshape(b, s, D), rnn4.reshape(b, s, 4 * D), h,
                                                          lw, l, b, s, TS_RNN)
    dx, dproj = inproj_bwd(dxr.reshape(n, D), dg3, dql.reshape(n, Q_LORA), dkvl.reshape(n, KV_LORA),
                           dkr.reshape(n, LANES), dz1, lw['w_in_p'], TM)
    dw_in_p = matmul_dw(x0, dproj, TN_DW, PROJ_W // 2, "inproj_dw")
    grads = {
        'w_in': _unpermute_dw_in(dw_in_p), 'conv_w': dcw, 'conv_b': dcb, 'gx_w': _ungroup_dgate(dwgx), 'gx_b': dbgx,
        'ga_w': _ungroup_dgate(dwga), 'ga_b': dbga, 'lru_lambda': dlam, 'q_norm_g': dgq, 'w_uq': _unpair_dwq(dwq),
        'kv_norm_g': dgkv, 'w_ukv': _unpair_dwkv(dwkv), 'w_out': dw_out, 'ln1_g': dg1, 'ln1_b': db1, 'w_up': dw_up,
        'ffn_conv_w': dfcw, 'ffn_conv_b': dfcb, 'w_down': dw_down, 'ln2_g': dg2, 'ln2_b': db2,
    }
    wire = {k: (_shard_pieces(g, k) if k in SHARDED else g) for k, g in grads.items()}
    wire.update({k: wire[k].astype(MXU_DTYPE) for k in MATMUL_WEIGHTS})
    return dx, wire, xbufs


SMALL_WEIGHT_ELEMS = 1 << 16


def kernel(x, positions, w_in, conv_w, conv_b, gx_w, gx_b, ga_w, ga_b, lru_lambda, q_norm_g, w_uq, kv_norm_g, w_ukv, w_out, ln1_g, ln1_b, w_up, ffn_conv_w, ffn_conv_b, w_down, ln2_g, ln2_b, loss_target, m_w_in, m_conv_w, m_conv_b, m_gx_w, m_gx_b, m_ga_w, m_ga_b, m_lru_lambda, m_q_norm_g, m_w_uq, m_kv_norm_g, m_w_ukv, m_w_out, m_ln1_g, m_ln1_b, m_w_up, m_ffn_conv_w, m_ffn_conv_b, m_w_down, m_ln2_g, m_ln2_b, v_w_in, v_conv_w, v_conv_b, v_gx_w, v_gx_b, v_ga_w, v_ga_b, v_lru_lambda, v_q_norm_g, v_w_uq, v_kv_norm_g, v_w_ukv, v_w_out, v_ln1_g, v_ln1_b, v_w_up, v_ffn_conv_w, v_ffn_conv_b, v_w_down, v_ln2_g, v_ln2_b):
    w = dict(zip(WEIGHT_NAMES, (w_in, conv_w, conv_b, gx_w, gx_b, ga_w, ga_b, lru_lambda, q_norm_g, w_uq, kv_norm_g,
                                w_ukv, w_out, ln1_g, ln1_b, w_up, ffn_conv_w, ffn_conv_b, w_down, ln2_g, ln2_b)))
    m = dict(zip(WEIGHT_NAMES, (m_w_in, m_conv_w, m_conv_b, m_gx_w, m_gx_b, m_ga_w, m_ga_b, m_lru_lambda,
                                m_q_norm_g, m_w_uq, m_kv_norm_g, m_w_ukv, m_w_out, m_ln1_g, m_ln1_b, m_w_up,
                                m_ffn_conv_w, m_ffn_conv_b, m_w_down, m_ln2_g, m_ln2_b)))
    v = dict(zip(WEIGHT_NAMES, (v_w_in, v_conv_w, v_conv_b, v_gx_w, v_gx_b, v_ga_w, v_ga_b, v_lru_lambda,
                                v_q_norm_g, v_w_uq, v_kv_norm_g, v_w_ukv, v_w_out, v_ln1_g, v_ln1_b, v_w_up,
                                v_ffn_conv_w, v_ffn_conv_b, v_w_down, v_ln2_g, v_ln2_b)))
    b, s, _ = x.shape
    n = b * s
    tabs = _rope_tables(positions)
    shared = {name: w[name] for name in WEIGHT_NAMES if name not in SHARDED and w[name].ndim == 2}
    shared['wgx'] = _group_gates(w['gx_w']).astype(MXU_DTYPE)
    shared['wga'] = _group_gates(w['ga_w']).astype(MXU_DTYPE)

    names = list(SHARDED)
    send = [w[k].astype(MXU_DTYPE) if k in MATMUL_WEIGHTS else w[k] for k in names]
    gbufs = [lax.empty((N_DEV,) + a.shape, a.dtype) for a in send]
    gather = lambda l, bufs: Exchange(send, [False] * len(names), l, bufs, l)
    gbufs = exchange_layer(gather(0, gbufs), "gather_weights")

    lws, saved = [], []
    xc = x.reshape(n, D)
    for l in range(DEPTH):
        lws.append(_layer_weights(dict(zip(names, gbufs)), shared, l))
        xc, sv, nxt = layer_fwd(xc, tabs, lws[l], l, b, s, gather(l + 1, gbufs) if l + 1 < DEPTH else None)
        gbufs = nxt or gbufs
        saved.append(sv)
    loss_part, dx = loss_head(xc, loss_target.reshape(n, D), TM)

    pbufs, pending = None, None
    scatter = lambda grads, l, bufs: Exchange([grads[k] for k in WEIGHT_NAMES], [k in SHARDED for k in WEIGHT_NAMES],
                                              None, bufs, l)
    for l in reversed(range(DEPTH)):
        dx, grads, nxt = layer_bwd(dx, saved[l], tabs, lws[l], l, b, s, pending)
        pbufs = nxt or pbufs
        if pbufs is None:
            pbufs = [lax.empty((N_DEV, DEPTH) + (grads[k].shape[1:] if k in SHARDED else grads[k].shape),
                               grads[k].dtype) for k in WEIGHT_NAMES]
        pending = scatter(grads, l, pbufs)
    pbufs = exchange_layer(pending, "scatter_grads")
    parts = dict(zip(WEIGHT_NAMES, pbufs))

    out = {}
    small = [k for k in WEIGHT_NAMES if w[k].size <= SMALL_WEIGHT_ELEMS]
    res = adamw_small([(parts[k], w[k], m[k], v[k]) for k in small])
    for j, k in enumerate(small):
        out[k] = tuple(res[i * len(small) + j] for i in range(4))
    for k in WEIGHT_NAMES:
        if k in out:
            continue
        shape = w[k].shape
        view = (DEPTH, -1, shape[-1])
        r = adamw_tiled(parts[k].reshape((N_DEV,) + view), w[k].reshape(view), m[k].reshape(view),
                        v[k].reshape(view), k)
        out[k] = tuple(a.reshape(shape) for a in r)
    loss = lax.psum(loss_part[0, 0], ("x", "y", "c"))
    return (loss, dx.reshape(b, s, D), *[out[k][0] for k in WEIGHT_NAMES], *[out[k][1] for k in WEIGHT_NAMES],
            *[out[k][2] for k in WEIGHT_NAMES], *[out[k][3] for k in WEIGHT_NAMES])
```

```python
import functools
import math

import jax
import jax.numpy as jnp
from jax import lax
from jax.experimental import pallas as pl
from jax.experimental.pallas import tpu as pltpu

F32 = jnp.float32
MXU_DTYPE = jnp.bfloat16

D = 1024
DEPTH = 4
N_RNN_BLOCKS = 16
RNN_BLOCK = 64
CONV_W = 4
LRU_C = 8.0
N_HEADS = 16
QK_NOPE = 64
QK_ROPE = 32
V_HEAD = 64
Q_LORA = 384
KV_LORA = 256
ROPE_THETA = 10000.0
D_FF = 3 * D
FFN_CONV_W = 3
IN_WIDTH = 2 * D + Q_LORA + KV_LORA + QK_ROPE + 2 * D
ALPHA = (2 * DEPTH) ** 0.25
EPS = 1e-6
NEG_INF = -1e30
ATT_SCALE = (QK_NOPE + QK_ROPE) ** -0.5
GELU_C = math.sqrt(2.0 / math.pi)

ADAM_LR = 0.001
ADAM_B1 = 0.9
ADAM_B2 = 0.999
ADAM_EPS = 1e-08
ADAM_WD = 0.01
ADAM_STEP = 10

N_DEV = 8
LANES = 128
SUBLANES = 8
MXU_GROUP = 256
N_GATE_GROUPS = D // MXU_GROUP
HEADS_PER_STEP = 2
N_HEAD_PAIRS = N_HEADS // HEADS_PER_STEP
HEAD_LANES = 128
PROJ_W = 4 * D + Q_LORA + KV_LORA + LANES
VMEM_BIG = 56 * 2 ** 20

WEIGHT_NAMES = ['w_in', 'conv_w', 'conv_b', 'gx_w', 'gx_b', 'ga_w', 'ga_b', 'lru_lambda', 'q_norm_g', 'w_uq',
                'kv_norm_g', 'w_ukv', 'w_out', 'ln1_g', 'ln1_b', 'w_up', 'ffn_conv_w', 'ffn_conv_b', 'w_down',
                'ln2_g', 'ln2_b']
SHARDED = {
    'w_in': (2, (DEPTH, D, IN_WIDTH)),
    'conv_w': (2, (DEPTH, CONV_W, D)),
    'w_uq': (2, (DEPTH, Q_LORA, N_HEADS * (QK_NOPE + QK_ROPE))),
    'w_ukv': (2, (DEPTH, KV_LORA, N_HEADS * (QK_NOPE + V_HEAD))),
    'w_out': (1, (DEPTH, D, D)),
    'w_up': (2, (DEPTH, D, 2 * D_FF)),
    'ffn_conv_w': (2, (DEPTH, FFN_CONV_W, 2 * D_FF)),
    'w_down': (1, (DEPTH, D_FF, D)),
}
MATMUL_WEIGHTS = ('w_in', 'w_uq', 'w_ukv', 'w_out', 'w_up', 'w_down')


def _mm(a, b):
    return jnp.dot(a.astype(MXU_DTYPE), b.astype(MXU_DTYPE), preferred_element_type=F32)


def _mm_tn(a, b):
    return lax.dot_general(a.astype(MXU_DTYPE), b.astype(MXU_DTYPE), (((0,), (0,)), ((), ())),
                           preferred_element_type=F32)


def _mm_nt(a, b):
    return lax.dot_general(a.astype(MXU_DTYPE), b.astype(MXU_DTYPE), (((1,), (1,)), ((), ())),
                           preferred_element_type=F32)


def _sigmoid(x):
    return 1.0 / (1.0 + jnp.exp(-x))


def _gelu(x):
    t = jnp.tanh(GELU_C * (x + 0.044715 * (x * x * x)))
    return 0.5 * x * (1.0 + t), t


def _gelu_grad(x, t):
    return 0.5 * (1.0 + t) + 0.5 * x * (1.0 - t * t) * (GELU_C * (1.0 + 3.0 * 0.044715 * (x * x)))


def _neg_expm1(y):
    series = -y * (1.0 + 0.5 * y * (1.0 + (y / 3.0) * (1.0 + 0.25 * y * (1.0 + 0.2 * y))))
    return jnp.where(y > -0.05, series, 1.0 - jnp.exp(y))


def _ln_stats(z):
    mu = jnp.mean(z, axis=-1, keepdims=True)
    zc = z - mu
    var = jnp.mean(zc * zc, axis=-1, keepdims=True)
    r = lax.rsqrt(var + EPS)
    return zc * r, r


def _ln_bwd(dy, z, g):
    xhat, r = _ln_stats(z)
    dxh = dy * g
    dz = r * (dxh - jnp.mean(dxh, axis=-1, keepdims=True)
              - xhat * jnp.mean(dxh * xhat, axis=-1, keepdims=True))
    return dz, jnp.sum(dy * xhat, axis=0, keepdims=True), jnp.sum(dy, axis=0, keepdims=True)


def _rms_stats(x):
    r = lax.rsqrt(jnp.mean(x * x, axis=-1, keepdims=True) + EPS)
    return x * r, r


def _rms_bwd(dy, x, g):
    xn, r = _rms_stats(x)
    dxn = dy * g
    dx = r * (dxn - xn * jnp.mean(dxn * xn, axis=-1, keepdims=True))
    return dx, jnp.sum(dy * xn, axis=0, keepdims=True)


def _shift_down(x, halo, s, axis):
    if s == 0:
        return x
    r = pltpu.roll(x, s, axis)
    hr = pltpu.roll(halo, s, axis)
    idx = lax.broadcasted_iota(jnp.int32, hr.shape, axis)
    head = lax.slice_in_dim(r, 0, SUBLANES, axis=axis)
    rest = lax.slice_in_dim(r, SUBLANES, x.shape[axis], axis=axis)
    return jnp.concatenate([jnp.where(idx < s, hr, head), rest], axis=axis)


def _shift_up(x, halo, s, axis):
    if s == 0:
        return x
    n = x.shape[axis]
    r = pltpu.roll(x, n - s, axis)
    hr = pltpu.roll(halo, SUBLANES - s, axis)
    idx = lax.broadcasted_iota(jnp.int32, hr.shape, axis)
    body = lax.slice_in_dim(r, 0, n - SUBLANES, axis=axis)
    tail = lax.slice_in_dim(r, n - SUBLANES, n, axis=axis)
    return jnp.concatenate([body, jnp.where(idx >= SUBLANES - s, hr, tail)], axis=axis)


def _const_spec(shape):
    nd = len(shape)
    return pl.BlockSpec(shape, lambda *_: (0,) * nd)


def _layer_spec(shape, l):
    nd = len(shape)
    return pl.BlockSpec((None,) + tuple(shape), lambda *_: (l,) + (0,) * nd)


def _resident(shape):
    nd = len(shape)
    return pl.BlockSpec(shape, lambda *_: (0,) * nd, pipeline_mode=pl.Buffered(1))


def _params(vmem=None):
    return pltpu.CompilerParams(vmem_limit_bytes=vmem)


def inproj_fwd(x, w_in_p, tm):
    n = x.shape[0]

    def body(x_ref, w_ref, rnn4_ref, ql_ref, kvl_ref, kr_ref):
        xb = x_ref[...].astype(MXU_DTYPE)
        for j in range(4):
            rnn4_ref[:, j * D:(j + 1) * D] = _mm(xb, w_ref[:, j * D:(j + 1) * D])
        o = 4 * D
        ql_ref[...] = _mm(xb, w_ref[:, o:o + Q_LORA])
        kvl_ref[...] = _mm(xb, w_ref[:, o + Q_LORA:o + Q_LORA + KV_LORA])
        kr_ref[...] = _mm(xb, w_ref[:, o + Q_LORA + KV_LORA:PROJ_W])

    row = lambda i: (i, 0)
    return pl.pallas_call(
        body, name="inproj_fwd", grid=(n // tm,),
        in_specs=[pl.BlockSpec((tm, D), row), _resident((D, PROJ_W))],
        out_specs=[pl.BlockSpec((tm, 4 * D), row), pl.BlockSpec((tm, Q_LORA), row),
                   pl.BlockSpec((tm, KV_LORA), row), pl.BlockSpec((tm, LANES), row)],
        out_shape=[jax.ShapeDtypeStruct((n, 4 * D), F32), jax.ShapeDtypeStruct((n, Q_LORA), F32),
                   jax.ShapeDtypeStruct((n, KV_LORA), F32), jax.ShapeDtypeStruct((n, LANES), F32)],
        compiler_params=_params(VMEM_BIG),
    )(x, w_in_p)


def inproj_bwd(dxr, dg3, dql, dkvl, dkr, dz1, w_in_p, tm):
    n = dz1.shape[0]

    def body(dxr_ref, dg3_ref, dql_ref, dkvl_ref, dkr_ref, dz_ref, w_ref, dx_ref, dp_ref):
        dp = jnp.concatenate([dxr_ref[...], dg3_ref[...], dql_ref[...], dkvl_ref[...], dkr_ref[...]],
                             axis=1).astype(MXU_DTYPE)
        dp_ref[...] = dp
        dx_ref[...] = ALPHA * dz_ref[...] + _mm_nt(dp, w_ref[...])

    row = lambda i: (i, 0)
    return pl.pallas_call(
        body, name="inproj_bwd", grid=(n // tm,),
        in_specs=[pl.BlockSpec((tm, D), row), pl.BlockSpec((tm, 3 * D), row), pl.BlockSpec((tm, Q_LORA), row),
                  pl.BlockSpec((tm, KV_LORA), row), pl.BlockSpec((tm, LANES), row), pl.BlockSpec((tm, D), row),
                  _resident((D, PROJ_W))],
        out_specs=[pl.BlockSpec((tm, D), row), pl.BlockSpec((tm, PROJ_W), row)],
        out_shape=[jax.ShapeDtypeStruct((n, D), F32), jax.ShapeDtypeStruct((n, PROJ_W), MXU_DTYPE)],
        compiler_params=_params(VMEM_BIG),
    )(dxr, dg3, dql, dkvl, dkr, dz1, w_in_p)


def matmul_dw(x, dy, tn, tmc, name):
    n, k = x.shape
    m = dy.shape[1]

    def body(x_ref, dy_ref, dw_ref):
        @pl.when(pl.program_id(1) == 0)
        def _():
            dw_ref[...] = jnp.zeros_like(dw_ref)
        dw_ref[...] += _mm_tn(x_ref[...], dy_ref[...])

    return pl.pallas_call(
        body, name=name, grid=(m // tmc, n // tn),
        in_specs=[pl.BlockSpec((tn, k), lambda j, i: (i, 0)), pl.BlockSpec((tn, tmc), lambda j, i: (i, j))],
        out_specs=pl.BlockSpec((k, tmc), lambda j, i: (0, j)),
        out_shape=jax.ShapeDtypeStruct((k, m), F32),
        compiler_params=_params(VMEM_BIG),
    )(x, dy)


def matmul_dx(dy, w, add, add_scale, tm, name):
    n, m = dy.shape
    k = w.shape[0]

    def body(dy_ref, w_ref, add_ref, dx_ref):
        dx_ref[...] = add_scale * add_ref[...] + _mm_nt(dy_ref[...], w_ref[...])

    row = lambda i: (i, 0)
    return pl.pallas_call(
        body, name=name, grid=(n // tm,),
        in_specs=[pl.BlockSpec((tm, m), row), _resident((k, m)), pl.BlockSpec((tm, k), row)],
        out_specs=pl.BlockSpec((tm, k), row),
        out_shape=jax.ShapeDtypeStruct((n, k), F32),
        compiler_params=_params(VMEM_BIG),
    )(dy, w, add)


def _group(g):
    return slice(g * MXU_GROUP, (g + 1) * MXU_GROUP)


def _rnn_gates(x, halo, g, l, cw_ref, cb_ref, wgx_ref, bgx_ref, wga_ref, bga_ref, lam_ref):
    b, ts, gw = x.shape
    sl = _group(g)
    lr = slice(l, l + 1)
    xc = cb_ref[lr, sl][None]
    for k in range(CONV_W):
        xc = xc + cw_ref[k:k + 1, sl][None] * _shift_down(x, halo, CONV_W - 1 - k, 1)
    xc2 = xc.reshape(b * ts, gw)
    xcb = xc2.astype(MXU_DTYPE)
    gx = _sigmoid(_mm(xcb, wgx_ref[g]) + bgx_ref[lr, sl])
    ga = _sigmoid(_mm(xcb, wga_ref[g]) + bga_ref[lr, sl])
    nl = -lam_ref[lr, sl]
    sp = jnp.maximum(nl, 0.0) + jnp.log1p(jnp.exp(-jnp.abs(nl)))
    log_a = (-LRU_C) * ga * sp
    a = jnp.exp(log_a)
    mult = jnp.sqrt(_neg_expm1(2.0 * log_a))
    return xc2, xcb, gx, ga, sp, a, mult


def rnn_fwd(rnn4, lw, l, b, s, ts):
    ns = s // ts

    def body(x_ref, cw_ref, cb_ref, wgx_ref, bgx_ref, wga_ref, bga_ref, lam_ref, h_ref,
             halo_sc, hstate_sc, a_sc, u_sc):
        @pl.when(pl.program_id(0) == 0)
        def _():
            halo_sc[...] = jnp.zeros_like(halo_sc)
            hstate_sc[...] = jnp.zeros_like(hstate_sc)

        for g in range(N_GATE_GROUPS):
            sl = _group(g)
            x = x_ref[:, :, sl]
            xc2, _, gx, _, _, a, mult = _rnn_gates(x, halo_sc[:, :, sl], g, l, cw_ref, cb_ref, wgx_ref, bgx_ref,
                                                   wga_ref, bga_ref, lam_ref)
            halo_sc[:, :, sl] = x[:, ts - SUBLANES:, :]
            a_sc[...] = a.reshape(b, ts, MXU_GROUP)
            u_sc[...] = (mult * gx * xc2).reshape(b, ts, MXU_GROUP)

            def step(t, h, sl=sl):
                h = a_sc[:, pl.ds(t, 1), :] * h + u_sc[:, pl.ds(t, 1), :]
                h_ref[:, pl.ds(t, 1), sl] = h
                return h

            hstate_sc[:, :, sl] = lax.fori_loop(0, ts, step, hstate_sc[:, :, sl], unroll=8)

    tile = lambda i: (0, i, 0)
    vecs = _const_spec((DEPTH, D))
    gates = _layer_spec((N_GATE_GROUPS, MXU_GROUP, MXU_GROUP), l)
    return pl.pallas_call(
        body, name="rnn_fwd", grid=(ns,),
        in_specs=[pl.BlockSpec((b, ts, D), tile), _const_spec((CONV_W, D)), vecs, gates, vecs, gates, vecs, vecs],
        out_specs=pl.BlockSpec((b, ts, D), tile),
        out_shape=jax.ShapeDtypeStruct((b, s, D), F32),
        scratch_shapes=[pltpu.VMEM((b, SUBLANES, D), F32), pltpu.VMEM((b, 1, D), F32),
                        pltpu.VMEM((b, ts, MXU_GROUP), F32), pltpu.VMEM((b, ts, MXU_GROUP), F32)],
        compiler_params=_params(VMEM_BIG),
    )(rnn4, lw['conv_w'], lw['conv_b'], lw['wgx'], lw['gx_b'], lw['wga'], lw['ga_b'], lw['lru_lambda'])


def rnn_bwd(dh, rnn4, h, lw, l, b, s, ts):
    ns = s // ts
    hb = ts // SUBLANES

    def body(dh_ref, x_ref, xh_ref, h_ref, hh_ref, cw_ref, cb_ref, wgx_ref, bgx_ref, wga_ref, bga_ref, lam_ref,
             dx_ref, dcw_ref, dcb_ref, dwgx_ref, dbgx_ref, dwga_ref, dbga_ref, dlam_ref,
             carry_sc, dxc_halo_sc, a_sc, delta_sc):
        i = pl.program_id(0)

        @pl.when(i == 0)
        def _():
            carry_sc[...] = jnp.zeros_like(carry_sc)
            dxc_halo_sc[...] = jnp.zeros_like(dxc_halo_sc)
            for r in (dcw_ref, dcb_ref, dwgx_ref, dbgx_ref, dwga_ref, dbga_ref, dlam_ref):
                r[...] = jnp.zeros_like(r)

        keep = jnp.where(i == ns - 1, 0.0, 1.0)
        for g in range(N_GATE_GROUPS):
            sl = _group(g)
            x = x_ref[:, :, sl]
            xhalo = xh_ref[:, :, sl] * keep
            xc2, xcb, gx, ga, sp, a, mult = _rnn_gates(x, xhalo, g, l, cw_ref, cb_ref, wgx_ref, bgx_ref, wga_ref,
                                                       bga_ref, lam_ref)
            a_sc[...] = a.reshape(b, ts, MXU_GROUP)

            def step(j, c, sl=sl):
                t = ts - 1 - j
                d = dh_ref[:, pl.ds(t, 1), sl] + c
                delta_sc[:, pl.ds(t, 1), :] = d
                return a_sc[:, pl.ds(t, 1), :] * d

            carry_sc[:, :, sl] = lax.fori_loop(0, ts, step, carry_sc[:, :, sl], unroll=8)

            delta = delta_sc[...].reshape(b * ts, MXU_GROUP)
            hprev = _shift_down(h_ref[:, :, sl], hh_ref[:, :, sl] * keep, 1, 1).reshape(b * ts, MXU_GROUP)
            dmult = delta * gx * xc2
            dl = delta * hprev * a - dmult * (a * a) / mult
            dga = dl * ((-LRU_C) * sp)
            dlam_ref[:, sl] += (jnp.sum(dl * ((-LRU_C) * ga), axis=0, keepdims=True)
                                * (-_sigmoid(-lam_ref[l:l + 1, sl])))
            dpa = dga * ga * (1.0 - ga)
            dpx = (delta * mult * xc2) * gx * (1.0 - gx)
            dbga_ref[:, sl] += jnp.sum(dpa, axis=0, keepdims=True)
            dbgx_ref[:, sl] += jnp.sum(dpx, axis=0, keepdims=True)
            dpab = dpa.astype(MXU_DTYPE)
            dpxb = dpx.astype(MXU_DTYPE)
            dwga_ref[g] += _mm_tn(xcb, dpab)
            dwgx_ref[g] += _mm_tn(xcb, dpxb)
            dxc2 = delta * mult * gx + _mm_nt(dpab, wga_ref[g]) + _mm_nt(dpxb, wgx_ref[g])
            dcb_ref[:, sl] += jnp.sum(dxc2, axis=0, keepdims=True)
            dxc = dxc2.reshape(b, ts, MXU_GROUP)
            nhalo = dxc_halo_sc[:, :, sl]
            dx = jnp.zeros_like(dxc)
            for k in range(CONV_W):
                sft = CONV_W - 1 - k
                xs = _shift_down(x, xhalo, sft, 1)
                dcw_ref[k:k + 1, sl] += jnp.sum((dxc * xs).reshape(b * ts, MXU_GROUP), axis=0, keepdims=True)
                dx = dx + cw_ref[k:k + 1, sl][None] * _shift_up(dxc, nhalo, sft, 1)
            dx_ref[:, :, sl] = dx
            dxc_halo_sc[:, :, sl] = dxc[:, :SUBLANES, :]

    tile = lambda i: (0, ns - 1 - i, 0)
    halo = lambda i: (0, jnp.maximum((ns - 1 - i) * hb - 1, 0), 0)
    gshape = (N_GATE_GROUPS, MXU_GROUP, MXU_GROUP)
    vecs = _const_spec((DEPTH, D))
    gates = _layer_spec(gshape, l)
    vec = jax.ShapeDtypeStruct((1, D), F32)
    return pl.pallas_call(
        body, name="rnn_bwd", grid=(ns,),
        in_specs=[pl.BlockSpec((b, ts, D), tile), pl.BlockSpec((b, ts, D), tile),
                  pl.BlockSpec((b, SUBLANES, D), halo), pl.BlockSpec((b, ts, D), tile),
                  pl.BlockSpec((b, SUBLANES, D), halo),
                  _const_spec((CONV_W, D)), vecs, gates, vecs, gates, vecs, vecs],
        out_specs=[pl.BlockSpec((b, ts, D), tile), _const_spec((CONV_W, D)), _const_spec((1, D)),
                   _const_spec(gshape), _const_spec((1, D)), _const_spec(gshape), _const_spec((1, D)),
                   _const_spec((1, D))],
        out_shape=[jax.ShapeDtypeStruct((b, s, D), F32), jax.ShapeDtypeStruct((CONV_W, D), F32), vec,
                   jax.ShapeDtypeStruct(gshape, F32), vec, jax.ShapeDtypeStruct(gshape, F32), vec, vec],
        scratch_shapes=[pltpu.VMEM((b, 1, D), F32), pltpu.VMEM((b, SUBLANES, D), F32),
                        pltpu.VMEM((b, ts, MXU_GROUP), F32), pltpu.VMEM((b, ts, MXU_GROUP), F32)],
        compiler_params=_params(VMEM_BIG),
    )(dh, rnn4, rnn4, h, h, lw['conv_w'], lw['conv_b'], lw['wgx'], lw['gx_b'], lw['wga'], lw['ga_b'],
      lw['lru_lambda'])


def _rope(x, cos, sa, sb):
    return x * cos + pltpu.roll(x, HEAD_LANES - QK_ROPE // 2, 1) * sa + pltpu.roll(x, QK_ROPE // 2, 1) * sb


def _unrope(d, cos, sa, sb):
    return d * cos + pltpu.roll(d * sa, QK_ROPE // 2, 1) + pltpu.roll(d * sb, HEAD_LANES - QK_ROPE // 2, 1)


def _scores(q_blk, keys):
    return _mm_nt(q_blk, keys) * ATT_SCALE


def _diag_scores(q_blk, keys):
    tq = q_blk.shape[0]
    keep = lax.broadcasted_iota(jnp.int32, (tq, tq), 0) >= lax.broadcasted_iota(jnp.int32, (tq, tq), 1)
    return jnp.where(keep, _scores(q_blk, keys), NEG_INF)


def _mla_project(ql_ref, kvl_ref, l, gq_ref, gkv_ref, wq_ref, wkv_ref):
    qn, _ = _rms_stats(ql_ref[0])
    qn = (qn * gq_ref[l:l + 1, :]).astype(MXU_DTYPE)
    kvn, _ = _rms_stats(kvl_ref[0])
    kvn = (kvn * gkv_ref[l:l + 1, :]).astype(MXU_DTYPE)
    return qn, kvn, _mm(qn, wq_ref[0]), _mm(kvn, wkv_ref[0])


def mla_fwd(ql, kvl, kr, tabs, lw, l, b, s, tq, ex=None):
    nq = s // tq
    cos_t, sa_t, sb_t = tabs
    n_in, n_out = 10, 2
    x_args, x_in, x_out, x_shapes, x_alias, x_scratch = _fuse_exchange(ex, n_in, n_out)

    def body(*refs):
        bi, p = pl.program_id(0), pl.program_id(1)
        first = (bi == 0) & (p == 0)
        last = (bi == b - 1) & (p == N_HEAD_PAIRS - 1)

        @_run_exchange(ex, refs, n_in, n_out, 0, first, last)
        def _():
            compute(*refs[:n_in], *refs[n_in + len(x_in):n_in + len(x_in) + n_out])

    def compute(ql_ref, kvl_ref, kr_ref, cos_ref, sa_ref, sb_ref, gq_ref, gkv_ref, wq_ref, wkv_ref, o_ref, lse_ref):
        _, _, qp, kvp = _mla_project(ql_ref, kvl_ref, l, gq_ref, gkv_ref, wq_ref, wkv_ref)
        cos, sa, sb = cos_ref[0], sa_ref[0], sb_ref[0]
        for hh in range(HEADS_PER_STEP):
            hs = slice(hh * HEAD_LANES, (hh + 1) * HEAD_LANES)
            q = _rope(qp[:, hs], cos, sa, sb).astype(MXU_DTYPE)
            k = _rope(kvp[:, hs] + kr_ref[0], cos, sa, sb).astype(MXU_DTYPE)
            v = kvp[:, HEADS_PER_STEP * HEAD_LANES + hh * HEAD_LANES:
                    HEADS_PER_STEP * HEAD_LANES + (hh + 1) * HEAD_LANES].astype(MXU_DTYPE)
            def block_scores(qb):
                lo, hi = qb * tq, (qb + 1) * tq
                return _diag_scores(q[lo:hi], k[lo:hi]), (_scores(q[lo:hi], k[:lo]) if qb else None)

            def block_softmax(qb, sd, sf):
                m = jnp.max(sd, axis=-1, keepdims=True)
                if qb:
                    m = jnp.maximum(m, jnp.max(sf, axis=-1, keepdims=True))
                ed = jnp.exp(sd - m)
                den = jnp.sum(ed, axis=-1, keepdims=True)
                ef = None
                if qb:
                    ef = jnp.exp(sf - m)
                    den = den + jnp.sum(ef, axis=-1, keepdims=True)
                return ed.astype(MXU_DTYPE), (ef.astype(MXU_DTYPE) if qb else None), den, m + jnp.log(den)

            def block_out(qb, ed, ef, den, lse, hh=hh, v=v):
                lo, hi = qb * tq, (qb + 1) * tq
                o = _mm(ed, v[lo:hi])
                if qb:
                    o = o + _mm(ef, v[:lo])
                o = o * (1.0 / den)
                lse_ref[0, hh, lo:hi, :] = jnp.broadcast_to(lse, (tq, LANES))
                if hh == 0:
                    o_ref[0, lo:hi, :] = o
                else:
                    o_ref[0, lo:hi, :] += o

            nxt, prev = block_scores(0), None
            for qb in range(nq):
                sd, sf = nxt
                if qb + 1 < nq:
                    nxt = block_scores(qb + 1)
                if prev is not None:
                    block_out(qb - 1, *prev)
                prev = block_softmax(qb, sd, sf)
            block_out(nq - 1, *prev)

    seq = lambda bi, p: (bi, 0, 0)
    pair = lambda bi, p: (p, 0, 0)

    def per_seq(w):
        return pl.BlockSpec((1, s, w), seq, pipeline_mode=pl.Buffered(1))

    return pl.pallas_call(
        body, name="mla_fwd", grid=(b, N_HEAD_PAIRS),
        in_specs=[per_seq(Q_LORA), per_seq(KV_LORA), per_seq(LANES), per_seq(LANES), per_seq(LANES), per_seq(LANES),
                  _const_spec((DEPTH, Q_LORA)), _const_spec((DEPTH, KV_LORA)),
                  pl.BlockSpec((1, Q_LORA, HEADS_PER_STEP * HEAD_LANES), pair),
                  pl.BlockSpec((1, KV_LORA, 2 * HEADS_PER_STEP * HEAD_LANES), pair)] + x_in,
        out_specs=[pl.BlockSpec((1, s, LANES), lambda bi, p: (bi, 0, p)),
                   pl.BlockSpec((1, HEADS_PER_STEP, s, LANES), lambda bi, p: (bi, p, 0, 0))] + x_out,
        out_shape=[jax.ShapeDtypeStruct((b, s, D), F32), jax.ShapeDtypeStruct((b, N_HEADS, s, LANES), F32)] + x_shapes,
        input_output_aliases=x_alias, scratch_shapes=x_scratch,
        compiler_params=_params(VMEM_BIG),
    )(ql, kvl, kr, cos_t, sa_t, sb_t, lw['q_norm_g'], lw['kv_norm_g'], lw['wq_pairs'], lw['wkv_pairs'], *x_args)


def mla_bwd(dy, y, lse, ql, kvl, kr, tabs, lw, l, b, s, tq, ex=None):
    nq = s // tq
    cos_t, sa_t, sb_t = tabs
    qw = HEADS_PER_STEP * HEAD_LANES
    kvw = 2 * HEADS_PER_STEP * HEAD_LANES
    n_in, n_out, n_scratch = 13, 7, 2
    x_args, x_in, x_out, x_shapes, x_alias, x_scratch = _fuse_exchange(ex, n_in, n_out)

    def body(*refs):
        bi, p = pl.program_id(0), pl.program_id(1)
        first = (bi == 0) & (p == 0)
        last = (bi == b - 1) & (p == N_HEAD_PAIRS - 1)
        o0 = n_in + len(x_in)
        s0 = o0 + n_out + len(x_out)

        @_run_exchange(ex, refs, n_in, n_out, n_scratch, first, last)
        def _():
            compute(*refs[:n_in], *refs[o0:o0 + n_out], *refs[s0:s0 + n_scratch])

    def compute(dy_ref, y_ref, lse_ref, ql_ref, kvl_ref, kr_ref, cos_ref, sa_ref, sb_ref, gq_ref, gkv_ref, wq_ref,
                wkv_ref, dql_ref, dkvl_ref, dkr_ref, dwq_ref, dwkv_ref, dgq_ref, dgkv_ref, dk_sc, dv_sc):
        bi, p = pl.program_id(0), pl.program_id(1)

        @pl.when((bi == 0) & (p == 0))
        def _():
            for r in (dwq_ref, dwkv_ref, dgq_ref, dgkv_ref):
                r[...] = jnp.zeros_like(r)

        @pl.when(p == 0)
        def _():
            for r in (dql_ref, dkvl_ref, dkr_ref):
                r[...] = jnp.zeros_like(r)

        qn, kvn, qp, kvp = _mla_project(ql_ref, kvl_ref, l, gq_ref, gkv_ref, wq_ref, wkv_ref)
        cos, sa, sb = cos_ref[0], sa_ref[0], sb_ref[0]
        dof = dy_ref[0]
        do = dof.astype(MXU_DTYPE)
        prod = dof * y_ref[0]
        lane = lax.broadcasted_iota(jnp.int32, prod.shape, 1)
        dq_heads, dk_heads, dv_heads = [], [], []
        for hh in range(HEADS_PER_STEP):
            hs = slice(hh * HEAD_LANES, (hh + 1) * HEAD_LANES)
            q = _rope(qp[:, hs], cos, sa, sb).astype(MXU_DTYPE)
            k = _rope(kvp[:, hs] + kr_ref[0], cos, sa, sb).astype(MXU_DTYPE)
            v = kvp[:, qw + hh * HEAD_LANES:qw + (hh + 1) * HEAD_LANES].astype(MXU_DTYPE)
            mine = (lane >= hh * V_HEAD) & (lane < (hh + 1) * V_HEAD)
            delta = jnp.sum(jnp.where(mine, prod, 0.0), axis=-1, keepdims=True)
            dk_sc[...] = jnp.zeros_like(dk_sc)
            dv_sc[...] = jnp.zeros_like(dv_sc)
            units = []
            for qb in range(nq):
                units.append((qb, slice(qb * tq, (qb + 1) * tq), True))
                if qb:
                    units.append((qb, slice(0, qb * tq), False))

            def unit_matmuls_in(u, q=q, k=k, v=v):
                qb, ks, diag = u
                qs = slice(qb * tq, (qb + 1) * tq)
                sc = _diag_scores(q[qs], k[ks]) if diag else _scores(q[qs], k[ks])
                return sc, _mm_nt(do[qs], v[ks])

            def unit_elementwise(u, sc, dp, hh=hh, delta=delta):
                qs = slice(u[0] * tq, (u[0] + 1) * tq)
                pr = jnp.exp(sc - lse_ref[0, hh, qs, 0:1])
                return pr.astype(MXU_DTYPE), (pr * (dp - delta[qs])).astype(MXU_DTYPE)

            dq_blocks = [None] * nq

            def unit_matmuls_out(u, prb, ds, q=q, k=k):
                qb, ks, _ = u
                qs = slice(qb * tq, (qb + 1) * tq)
                dv_sc[ks, :] += _mm_tn(prb, do[qs])
                part = _mm(ds, k[ks])
                dq_blocks[qb] = part if dq_blocks[qb] is None else dq_blocks[qb] + part
                dk_sc[ks, :] += _mm_tn(ds, q[qs])

            nxt, prev = unit_matmuls_in(units[0]), None
            for i, u in enumerate(units):
                sc, dp = nxt
                if i + 1 < len(units):
                    nxt = unit_matmuls_in(units[i + 1])
                if prev is not None:
                    unit_matmuls_out(units[i - 1], *prev)
                prev = unit_elementwise(u, sc, dp)
            unit_matmuls_out(units[-1], *prev)
            dq_heads.append(_unrope(jnp.concatenate(dq_blocks, axis=0) * ATT_SCALE, cos, sa, sb))
            dk_full = _unrope(dk_sc[...] * ATT_SCALE, cos, sa, sb)
            dkr_ref[0] += dk_full
            dk_heads.append(dk_full)
            dv_heads.append(dv_sc[...])
        dqp = jnp.concatenate(dq_heads, axis=1).astype(MXU_DTYPE)
        dkvp = jnp.concatenate(dk_heads + dv_heads, axis=1).astype(MXU_DTYPE)
        dwq_ref[p] += _mm_tn(qn, dqp)
        dwkv_ref[p] += _mm_tn(kvn, dkvp)
        dql_ref[0] += _mm_nt(dqp, wq_ref[0])
        dkvl_ref[0] += _mm_nt(dkvp, wkv_ref[0])

        @pl.when(p == N_HEAD_PAIRS - 1)
        def _():
            dx, dg = _rms_bwd(dql_ref[0], ql_ref[0], gq_ref[l:l + 1, :])
            dql_ref[0] = dx
            dgq_ref[...] += dg
            dx, dg = _rms_bwd(dkvl_ref[0], kvl_ref[0], gkv_ref[l:l + 1, :])
            dkvl_ref[0] = dx
            dgkv_ref[...] += dg

    seq = lambda bi, p: (bi, 0, 0)
    pair = lambda bi, p: (p, 0, 0)

    def per_seq(w):
        return pl.BlockSpec((1, s, w), seq, pipeline_mode=pl.Buffered(1))

    return pl.pallas_call(
        body, name="mla_bwd", grid=(b, N_HEAD_PAIRS),
        in_specs=[pl.BlockSpec((1, s, LANES), lambda bi, p: (bi, 0, p)),
                  pl.BlockSpec((1, s, LANES), lambda bi, p: (bi, 0, p)),
                  pl.BlockSpec((1, HEADS_PER_STEP, s, LANES), lambda bi, p: (bi, p, 0, 0)),
                  per_seq(Q_LORA), per_seq(KV_LORA), per_seq(LANES), per_seq(LANES), per_seq(LANES), per_seq(LANES),
                  _const_spec((DEPTH, Q_LORA)), _const_spec((DEPTH, KV_LORA)),
                  pl.BlockSpec((1, Q_LORA, qw), pair), pl.BlockSpec((1, KV_LORA, kvw), pair)] + x_in,
        out_specs=[pl.BlockSpec((1, s, Q_LORA), seq), pl.BlockSpec((1, s, KV_LORA), seq),
                   pl.BlockSpec((1, s, LANES), seq),
                   _const_spec((N_HEAD_PAIRS, Q_LORA, qw)), _const_spec((N_HEAD_PAIRS, KV_LORA, kvw)),
                   _const_spec((1, Q_LORA)), _const_spec((1, KV_LORA))] + x_out,
        out_shape=[jax.ShapeDtypeStruct((b, s, Q_LORA), F32), jax.ShapeDtypeStruct((b, s, KV_LORA), F32),
                   jax.ShapeDtypeStruct((b, s, LANES), F32),
                   jax.ShapeDtypeStruct((N_HEAD_PAIRS, Q_LORA, qw), F32),
                   jax.ShapeDtypeStruct((N_HEAD_PAIRS, KV_LORA, kvw), F32),
                   jax.ShapeDtypeStruct((1, Q_LORA), F32), jax.ShapeDtypeStruct((1, KV_LORA), F32)] + x_shapes,
        input_output_aliases=x_alias,
        scratch_shapes=[pltpu.VMEM((s, HEAD_LANES), F32), pltpu.VMEM((s, HEAD_LANES), F32)] + x_scratch,
        compiler_params=_params(VMEM_BIG),
    )(dy, y, lse, ql, kvl, kr, cos_t, sa_t, sb_t, lw['q_norm_g'], lw['kv_norm_g'], lw['wq_pairs'], lw['wkv_pairs'],
      *x_args)


COL_CHUNK = 256


def _merge(g_rnn, gate_a, gate_b, h, y_mla):
    ge, t = _gelu(g_rnn)
    sa, sb = _sigmoid(gate_a), _sigmoid(gate_b)
    y_rnn = ge * h
    return ge, t, sa, sb, y_rnn, sa * y_rnn + sb * y_mla


def mixout_fwd(x, rnn4, h, y_mla, lw, l, tm):
    n = x.shape[0]

    def body(x_ref, gr_ref, gta_ref, gtb_ref, h_ref, y_ref, w_ref, g_ref, b_ref, z_ref, o_ref):
        z = ALPHA * x_ref[...]
        for c in range(0, D, COL_CHUNK):
            cs = slice(c, c + COL_CHUNK)
            merged = _merge(gr_ref[:, cs], gta_ref[:, cs], gtb_ref[:, cs], h_ref[:, cs], y_ref[:, cs])[-1]
            z = z + _mm(merged, w_ref[cs, :])
        z_ref[...] = z
        o_ref[...] = _ln_stats(z)[0] * g_ref[l:l + 1, :] + b_ref[l:l + 1, :]

    row = lambda i: (i, 0)
    col = lambda j: (lambda i: (i, j))
    blk = pl.BlockSpec((tm, D), row)
    return pl.pallas_call(
        body, name="mixout_fwd", grid=(n // tm,),
        in_specs=[blk, pl.BlockSpec((tm, D), col(1)), pl.BlockSpec((tm, D), col(2)), pl.BlockSpec((tm, D), col(3)),
                  blk, blk, _resident((D, D)), _const_spec((DEPTH, D)), _const_spec((DEPTH, D))],
        out_specs=[blk, blk],
        out_shape=[jax.ShapeDtypeStruct((n, D), F32), jax.ShapeDtypeStruct((n, D), F32)],
        compiler_params=_params(VMEM_BIG),
    )(x, rnn4, rnn4, rnn4, h, y_mla, lw['w_out'], lw['ln1_g'], lw['ln1_b'])


def mixout_bwd(dx1, z1, rnn4, h, y_mla, lw, l, tm):
    n = dx1.shape[0]

    def body(d_ref, z_ref, gr_ref, gta_ref, gtb_ref, h_ref, y_ref, w_ref, g_ref,
             dz_ref, dh_ref, dy_ref, dg3_ref, dw_ref, dg_ref, db_ref):
        @pl.when(pl.program_id(0) == 0)
        def _():
            for r in (dw_ref, dg_ref, db_ref):
                r[...] = jnp.zeros_like(r)

        dz, dg, db = _ln_bwd(d_ref[...], z_ref[...], g_ref[l:l + 1, :])
        dz_ref[...] = dz
        dg_ref[...] += dg
        db_ref[...] += db
        dzb = dz.astype(MXU_DTYPE)
        for c in range(0, D, COL_CHUNK):
            cs = slice(c, c + COL_CHUNK)
            g_rnn, h, y_mla = gr_ref[:, cs], h_ref[:, cs], y_ref[:, cs]
            ge, t, sa, sb, y_rnn, merged = _merge(g_rnn, gta_ref[:, cs], gtb_ref[:, cs], h, y_mla)
            dw_ref[cs, :] += _mm_tn(merged, dzb)
            dm = _mm_nt(dzb, w_ref[cs, :])
            dy_rnn = dm * sa
            dy_ref[:, cs] = dm * sb
            dh_ref[:, cs] = dy_rnn * ge
            dg3_ref[:, c:c + COL_CHUNK] = dy_rnn * h * _gelu_grad(g_rnn, t)
            dg3_ref[:, D + c:D + c + COL_CHUNK] = dm * y_rnn * sa * (1.0 - sa)
            dg3_ref[:, 2 * D + c:2 * D + c + COL_CHUNK] = dm * y_mla * sb * (1.0 - sb)

    row = lambda i: (i, 0)
    col = lambda j: (lambda i: (i, j))
    blk = pl.BlockSpec((tm, D), row)
    vec = jax.ShapeDtypeStruct((1, D), F32)
    act = jax.ShapeDtypeStruct((n, D), F32)
    return pl.pallas_call(
        body, name="mixout_bwd", grid=(n // tm,),
        in_specs=[blk, blk, pl.BlockSpec((tm, D), col(1)), pl.BlockSpec((tm, D), col(2)),
                  pl.BlockSpec((tm, D), col(3)), blk, blk, _resident((D, D)), _const_spec((DEPTH, D))],
        out_specs=[blk, blk, blk, pl.BlockSpec((tm, 3 * D), row), _const_spec((D, D)), _const_spec((1, D)),
                   _const_spec((1, D))],
        out_shape=[act, act, act, jax.ShapeDtypeStruct((n, 3 * D), F32), jax.ShapeDtypeStruct((D, D), F32), vec, vec],
        compiler_params=_params(VMEM_BIG),
    )(dx1, z1, rnn4, rnn4, rnn4, h, y_mla, lw['w_out'], lw['ln1_g'])


FFN_CHUNK = 512


def _conv3(u, halo, cs, l, fcw_ref, fcb_ref):
    hc = fcb_ref[l:l + 1, cs]
    for k in range(FFN_CONV_W):
        hc = hc + fcw_ref[k:k + 1, cs] * _shift_down(u, halo, FFN_CONV_W - 1 - k, 0)
    return hc


def ffn_fwd(x1, lw, l, b, s, ts):
    ns = s // ts
    n = b * s

    def body(x_ref, wu_ref, fcw_ref, fcb_ref, wd_ref, g_ref, b_ref, up_ref, z_ref, o_ref, halo_sc):
        @pl.when(pl.program_id(1) == 0)
        def _():
            halo_sc[...] = jnp.zeros_like(halo_sc)

        x = x_ref[...]
        xb = x.astype(MXU_DTYPE)
        z = ALPHA * x

        def up_chunk(c):
            return (_mm(xb, wu_ref[:, c:c + FFN_CHUNK]), _mm(xb, wu_ref[:, D_FF + c:D_FF + c + FFN_CHUNK]))

        nxt, act_prev = up_chunk(0), None
        for c in range(0, D_FF, FFN_CHUNK):
            gs, vs = slice(c, c + FFN_CHUNK), slice(D_FF + c, D_FF + c + FFN_CHUNK)
            ug, uv = nxt
            if c + FFN_CHUNK < D_FF:
                nxt = up_chunk(c + FFN_CHUNK)
            if act_prev is not None:
                z = z + _mm(act_prev, wd_ref[c - FFN_CHUNK:c, :])
            up_ref[:, gs] = ug
            up_ref[:, vs] = uv
            hg = _conv3(ug, halo_sc[:, gs], gs, l, fcw_ref, fcb_ref)
            hv = _conv3(uv, halo_sc[:, vs], vs, l, fcw_ref, fcb_ref)
            halo_sc[:, gs] = ug[ts - SUBLANES:, :]
            halo_sc[:, vs] = uv[ts - SUBLANES:, :]
            act_prev = (_gelu(hg)[0] * hv).astype(MXU_DTYPE)
        z = z + _mm(act_prev, wd_ref[D_FF - FFN_CHUNK:D_FF, :])
        z_ref[...] = z
        o_ref[...] = _ln_stats(z)[0] * g_ref[l:l + 1, :] + b_ref[l:l + 1, :]

    row = lambda bi, i: (bi * ns + i, 0)
    blk = pl.BlockSpec((ts, D), row)
    return pl.pallas_call(
        body, name="ffn_fwd", grid=(b, ns),
        in_specs=[blk, _resident((D, 2 * D_FF)), _const_spec((FFN_CONV_W, 2 * D_FF)), _const_spec((DEPTH, 2 * D_FF)),
                  _resident((D_FF, D)), _const_spec((DEPTH, D)), _const_spec((DEPTH, D))],
        out_specs=[pl.BlockSpec((ts, 2 * D_FF), row), blk, blk],
        out_shape=[jax.ShapeDtypeStruct((n, 2 * D_FF), F32), jax.ShapeDtypeStruct((n, D), F32),
                   jax.ShapeDtypeStruct((n, D), F32)],
        scratch_shapes=[pltpu.VMEM((SUBLANES, 2 * D_FF), F32)],
        compiler_params=_params(VMEM_BIG),
    )(x1, lw['w_up'], lw['ffn_conv_w'], lw['ffn_conv_b'], lw['w_down'], lw['ln2_g'], lw['ln2_b'])


def ffn_bwd(dx2, z2, up, lw, l, b, s, ts):
    ns = s // ts
    n = b * s
    hb = ts // SUBLANES

    def body(d_ref, z_ref, up_ref, uph_ref, fcw_ref, fcb_ref, wd_ref, g_ref,
             dz_ref, dup_ref, act_ref, dfcw_ref, dfcb_ref, dg_ref, db_ref, nhalo_sc):
        bi, i = pl.program_id(0), pl.program_id(1)

        @pl.when((bi == 0) & (i == 0))
        def _():
            for r in (dfcw_ref, dfcb_ref, dg_ref, db_ref):
                r[...] = jnp.zeros_like(r)

        @pl.when(i == 0)
        def _():
            nhalo_sc[...] = jnp.zeros_like(nhalo_sc)

        dz, dg, db = _ln_bwd(d_ref[...], z_ref[...], g_ref[l:l + 1, :])
        dz_ref[...] = dz
        dg_ref[...] += dg
        db_ref[...] += db
        dzb = dz.astype(MXU_DTYPE)
        keep = jnp.where(i == ns - 1, 0.0, 1.0)
        for c in range(0, D_FF, FFN_CHUNK):
            gs, vs = slice(c, c + FFN_CHUNK), slice(D_FF + c, D_FF + c + FFN_CHUNK)
            ug, uv = up_ref[:, gs], up_ref[:, vs]
            hg_halo, hv_halo = uph_ref[:, gs] * keep, uph_ref[:, vs] * keep
            hg = _conv3(ug, hg_halo, gs, l, fcw_ref, fcb_ref)
            hv = _conv3(uv, hv_halo, vs, l, fcw_ref, fcb_ref)
            ge, t = _gelu(hg)
            act_ref[:, gs] = (ge * hv).astype(MXU_DTYPE)
            dact = _mm_nt(dzb, wd_ref[c:c + FFN_CHUNK, :])
            for cs, u, halo, dhc in ((gs, ug, hg_halo, dact * hv * _gelu_grad(hg, t)), (vs, uv, hv_halo, dact * ge)):
                dfcb_ref[:, cs] += jnp.sum(dhc, axis=0, keepdims=True)
                nhalo = nhalo_sc[:, cs]
                dup = jnp.zeros_like(dhc)
                for k in range(FFN_CONV_W):
                    sft = FFN_CONV_W - 1 - k
                    dfcw_ref[k:k + 1, cs] += jnp.sum(dhc * _shift_down(u, halo, sft, 0), axis=0, keepdims=True)
                    dup = dup + fcw_ref[k:k + 1, cs] * _shift_up(dhc, nhalo, sft, 0)
                dup_ref[:, cs] = dup.astype(MXU_DTYPE)
                nhalo_sc[:, cs] = dhc[:SUBLANES, :]

    row = lambda bi, i: (bi * ns + (ns - 1 - i), 0)
    halo = lambda bi, i: (jnp.maximum((bi * ns + (ns - 1 - i)) * hb - 1, 0), 0)
    blk = pl.BlockSpec((ts, D), row)
    wide = pl.BlockSpec((ts, 2 * D_FF), row)
    return pl.pallas_call(
        body, name="ffn_bwd", grid=(b, ns),
        in_specs=[blk, blk, wide, pl.BlockSpec((SUBLANES, 2 * D_FF), halo),
                  _const_spec((FFN_CONV_W, 2 * D_FF)), _const_spec((DEPTH, 2 * D_FF)), _resident((D_FF, D)),
                  _const_spec((DEPTH, D))],
        out_specs=[blk, wide, pl.BlockSpec((ts, D_FF), row), _const_spec((FFN_CONV_W, 2 * D_FF)),
                   _const_spec((1, 2 * D_FF)), _const_spec((1, D)), _const_spec((1, D))],
        out_shape=[jax.ShapeDtypeStruct((n, D), F32), jax.ShapeDtypeStruct((n, 2 * D_FF), MXU_DTYPE),
                   jax.ShapeDtypeStruct((n, D_FF), MXU_DTYPE), jax.ShapeDtypeStruct((FFN_CONV_W, 2 * D_FF), F32),
                   jax.ShapeDtypeStruct((1, 2 * D_FF), F32), jax.ShapeDtypeStruct((1, D), F32),
                   jax.ShapeDtypeStruct((1, D), F32)],
        scratch_shapes=[pltpu.VMEM((SUBLANES, 2 * D_FF), F32)],
        compiler_params=_params(VMEM_BIG),
    )(dx2, z2, up, up, lw['ffn_conv_w'], lw['ffn_conv_b'], lw['w_down'], lw['ln2_g'])


def loss_head(y, target, tm):
    n = y.shape[0]

    def body(y_ref, t_ref, l_ref, d_ref):
        @pl.when(pl.program_id(0) == 0)
        def _():
            l_ref[...] = jnp.zeros_like(l_ref)

        err = y_ref[...] - t_ref[...]
        d_ref[...] = err * (1.0 / D)
        part = jnp.sum(jnp.sum(err * err, axis=-1, keepdims=True), axis=0, keepdims=True)
        l_ref[...] += jnp.broadcast_to(part * (0.5 / D), l_ref.shape)

    row = lambda i: (i, 0)
    return pl.pallas_call(
        body, name="loss_head", grid=(n // tm,),
        in_specs=[pl.BlockSpec((tm, D), row), pl.BlockSpec((tm, D), row)],
        out_specs=[_const_spec((1, LANES)), pl.BlockSpec((tm, D), row)],
        out_shape=[jax.ShapeDtypeStruct((1, LANES), F32), jax.ShapeDtypeStruct((n, D), F32)],
    )(y, target)


def _adam_update(g, w, m, v):
    c1 = 1.0 - ADAM_B1 ** ADAM_STEP
    c2 = 1.0 - ADAM_B2 ** ADAM_STEP
    mn = ADAM_B1 * m + (1.0 - ADAM_B1) * g
    vn = ADAM_B2 * v + (1.0 - ADAM_B2) * (g * g)
    return -ADAM_LR * ((mn / c1) / (jnp.sqrt(vn / c2) + ADAM_EPS) + ADAM_WD * w), mn, vn


def adamw_tiled(parts, w, m, v, name):
    _, r, c = w.shape
    tr = next(t for t in (256, 128, 64, 32, 16, 8) if r % t == 0)

    def body(p_ref, w_ref, m_ref, v_ref, g_ref, d_ref, mo_ref, vo_ref):
        g = p_ref[0].astype(F32)
        for i in range(1, N_DEV):
            g = g + p_ref[i].astype(F32)
        g_ref[...] = g
        d_ref[...], mo_ref[...], vo_ref[...] = _adam_update(g, w_ref[...], m_ref[...], v_ref[...])

    blk = pl.BlockSpec((None, tr, c), lambda l, i: (l, i, 0))
    out = jax.ShapeDtypeStruct(w.shape, F32)
    return pl.pallas_call(
        body, name="adamw_" + name, grid=(DEPTH, r // tr),
        in_specs=[pl.BlockSpec((N_DEV, None, tr, c), lambda l, i: (0, l, i, 0)), blk, blk, blk],
        out_specs=[blk, blk, blk, blk],
        out_shape=[out, out, out, out],
    )(parts, w, m, v)


def adamw_small(items):
    k = len(items)

    def body(*refs):
        ins, outs = refs[:4 * k], refs[4 * k:]
        for j in range(k):
            p_ref, w_ref, m_ref, v_ref = ins[4 * j:4 * j + 4]
            g_ref, d_ref, mo_ref, vo_ref = outs[j], outs[k + j], outs[2 * k + j], outs[3 * k + j]
            if len(p_ref.shape) == len(w_ref.shape) + 1:
                g = p_ref[0]
                for i in range(1, N_DEV):
                    g = g + p_ref[i]
                g_ref[...] = g
                d_ref[...], mo_ref[...], vo_ref[...] = _adam_update(g, w_ref[...], m_ref[...], v_ref[...])
            else:
                for l in range(DEPTH):
                    lr = slice(l, l + 1)
                    g = p_ref[0, l]
                    for i in range(1, N_DEV):
                        g = g + p_ref[i, l]
                    g_ref[lr, :] = g
                    d_ref[lr, :], mo_ref[lr, :], vo_ref[lr, :] = _adam_update(g, w_ref[lr, :], m_ref[lr, :],
                                                                              v_ref[lr, :])

    flat = [a for item in items for a in item]
    outs = [jax.ShapeDtypeStruct(item[1].shape, F32) for item in items] * 4
    return pl.pallas_call(
        body, name="adamw_small",
        in_specs=[pl.BlockSpec(memory_space=pltpu.VMEM)] * len(flat),
        out_specs=[pl.BlockSpec(memory_space=pltpu.VMEM)] * len(outs),
        out_shape=outs,
        compiler_params=_params(VMEM_BIG),
    )(*flat)


class Exchange:
    def __init__(self, entries):
        self.names = [e[0] for e in entries]
        self.srcs = [e[1] for e in entries]
        self.per_peer = [e[2] for e in entries]
        self.src_layer = [e[3] for e in entries]
        self.dst_layer = [e[4] for e in entries]
        self.bufs = [e[5] for e in entries]
        self.na = len(entries)

    def updated(self, bufdict, outs):
        new = dict(bufdict)
        new.update(zip(self.names, outs))
        return new

    def scratch(self):
        return [pltpu.SemaphoreType.DMA((self.na, N_DEV - 1)), pltpu.SemaphoreType.DMA((self.na, N_DEV - 1)),
                pltpu.SemaphoreType.DMA((self.na,))]

    def out_shapes(self):
        return [jax.ShapeDtypeStruct(bf.shape, bf.dtype) for bf in self.bufs]

    def copies(self, src_refs, buf_refs, send_sems, recv_sems, local_sems):
        x, y, c = lax.axis_index("x"), lax.axis_index("y"), lax.axis_index("c")
        me = 4 * x + 2 * y + c

        def view(a, pid):
            r = src_refs[a]
            if self.src_layer[a] is not None:
                r = r.at[self.src_layer[a]]
            return r.at[pid] if self.per_peer[a] else r

        out = [pltpu.make_async_copy(view(a, me), buf_refs[a].at[me, self.dst_layer[a]], local_sems.at[a])
               for a in range(self.na)]
        for k in range(1, N_DEV):
            px = 1 - x if k & 4 else x
            py = 1 - y if k & 2 else y
            pc = 1 - c if k & 1 else c
            pid = 4 * px + 2 * py + pc
            for a in range(self.na):
                out.append(pltpu.make_async_remote_copy(
                    src_ref=view(a, pid), dst_ref=buf_refs[a].at[me, self.dst_layer[a]],
                    send_sem=send_sems.at[a, k - 1], recv_sem=recv_sems.at[a, k - 1],
                    device_id=(px, py, pc), device_id_type=pl.DeviceIdType.MESH))
        return out


ANY_SPEC = pl.BlockSpec(memory_space=pl.ANY)


def exchange_layer(ex, name):
    na = ex.na

    def body(*refs):
        src_refs, buf_refs = refs[:na], refs[2 * na:3 * na]
        cps = ex.copies(src_refs, buf_refs, *refs[3 * na:])
        for cp in cps:
            cp.start()
        for cp in cps:
            cp.wait()

    return pl.pallas_call(
        body, name=name,
        in_specs=[ANY_SPEC] * (2 * na), out_specs=[ANY_SPEC] * na,
        out_shape=ex.out_shapes(),
        input_output_aliases={na + a: a for a in range(na)},
        scratch_shapes=ex.scratch(),
    )(*ex.srcs, *ex.bufs)


def _fuse_exchange(ex, n_in, n_out):
    if ex is None:
        return [], [], [], [], {}, []
    na = ex.na
    aliases = {n_in + na + a: n_out + a for a in range(na)}
    return ex.srcs + ex.bufs, [ANY_SPEC] * (2 * na), [ANY_SPEC] * na, ex.out_shapes(), aliases, ex.scratch()


def _run_exchange(ex, refs, n_in, n_out, n_scratch, first, last):
    def deco(compute):
        if ex is None:
            compute()
            return
        na = ex.na
        n_all_in = n_in + 2 * na
        src_refs = refs[n_in:n_in + na]
        buf_refs = refs[n_all_in + n_out:n_all_in + n_out + na]
        sems = refs[n_all_in + n_out + na + n_scratch:]

        @pl.when(first)
        def _():
            for cp in ex.copies(src_refs, buf_refs, *sems):
                cp.start()

        compute()

        @pl.when(last)
        def _():
            for cp in ex.copies(src_refs, buf_refs, *sems):
                cp.wait()
    return deco


def _permute_w_in(w):
    o = [0, D, 2 * D, 2 * D + Q_LORA, 2 * D + Q_LORA + KV_LORA, 2 * D + Q_LORA + KV_LORA + QK_ROPE,
         3 * D + Q_LORA + KV_LORA + QK_ROPE, IN_WIDTH]
    xr, gr, qlat, kvl, kr, ga, gb = [w[:, o[i]:o[i + 1]] for i in range(7)]
    z = lambda c: jnp.zeros((w.shape[0], c), w.dtype)
    return jnp.concatenate([xr, gr, ga, gb, qlat, kvl, z(QK_NOPE), kr, z(HEAD_LANES - QK_NOPE - QK_ROPE)], axis=1)


def _unpermute_dw_in(dw):
    o = 4 * D
    k0 = o + Q_LORA + KV_LORA + QK_NOPE
    return jnp.concatenate([dw[:, 0:2 * D], dw[:, o:o + Q_LORA + KV_LORA], dw[:, k0:k0 + QK_ROPE],
                            dw[:, 2 * D:4 * D]], axis=1)


def _pair_wq(w):
    w = w.reshape(Q_LORA, N_HEADS, QK_NOPE + QK_ROPE)
    w = jnp.pad(w, ((0, 0), (0, 0), (0, HEAD_LANES - QK_NOPE - QK_ROPE)))
    return w.reshape(Q_LORA, N_HEAD_PAIRS, HEADS_PER_STEP * HEAD_LANES).transpose(1, 0, 2)


def _unpair_dwq(dw):
    dw = dw.transpose(1, 0, 2).reshape(Q_LORA, N_HEADS, HEAD_LANES)
    return dw[:, :, :QK_NOPE + QK_ROPE].reshape(Q_LORA, N_HEADS * (QK_NOPE + QK_ROPE))


def _pair_wkv(w):
    w = w.reshape(KV_LORA, N_HEAD_PAIRS, HEADS_PER_STEP, QK_NOPE + V_HEAD)
    kn, vv = w[..., :QK_NOPE], w[..., QK_NOPE:]
    z = jnp.zeros_like(kn[:, :, 0])
    out = jnp.concatenate([kn[:, :, 0], z, kn[:, :, 1], z, vv[:, :, 0], z, z, vv[:, :, 1]], axis=-1)
    return out.transpose(1, 0, 2)


def _unpair_dwkv(dw):
    dw = dw.transpose(1, 0, 2)
    h0 = jnp.concatenate([dw[..., 0:64], dw[..., 256:320]], axis=-1)
    h1 = jnp.concatenate([dw[..., 128:192], dw[..., 448:512]], axis=-1)
    return jnp.stack([h0, h1], axis=2).reshape(KV_LORA, N_HEADS * (QK_NOPE + V_HEAD))


def _group_gates(w):
    per = MXU_GROUP // RNN_BLOCK
    w = w.reshape(DEPTH, N_GATE_GROUPS, per, RNN_BLOCK, RNN_BLOCK)
    eye = jnp.eye(per, dtype=w.dtype)
    return jnp.einsum('lgpij,pq->lgpiqj', w, eye).reshape(DEPTH, N_GATE_GROUPS, MXU_GROUP, MXU_GROUP)


def _ungroup_dgate(dw):
    per = MXU_GROUP // RNN_BLOCK
    dw = dw.reshape(N_GATE_GROUPS, per, RNN_BLOCK, per, RNN_BLOCK)
    return jnp.stack([dw[:, p, :, p, :] for p in range(per)], axis=1).reshape(N_RNN_BLOCKS, RNN_BLOCK, RNN_BLOCK)


def _rope_tables(positions):
    inv_freq = ROPE_THETA ** (-jnp.arange(0, QK_ROPE, 2, dtype=F32) / QK_ROPE)
    ang = positions.astype(F32)[..., None] * inv_freq
    cos, sin = jnp.cos(ang), jnp.sin(ang)
    one, zero = jnp.ones_like(cos), jnp.zeros_like(cos)
    nope = lambda v: jnp.concatenate([v] * (QK_NOPE // (QK_ROPE // 2)), axis=-1)
    tail = jnp.concatenate([zero, zero], axis=-1)
    cos_t = jnp.concatenate([nope(one), cos, cos, tail], axis=-1)
    sa_t = jnp.concatenate([nope(zero), -sin, zero, tail], axis=-1)
    sb_t = jnp.concatenate([nope(zero), zero, sin, tail], axis=-1)
    return cos_t, sa_t, sb_t


def _unshard(pieces, name):
    axis = SHARDED[name][0] - 1
    return jnp.concatenate([pieces[i] for i in range(N_DEV)], axis=axis)


def _shard_pieces(g, name):
    axis = SHARDED[name][0] - 1
    return jnp.stack(jnp.split(g, N_DEV, axis=axis))


MIXER_SHARDED = ('w_in', 'conv_w', 'w_uq', 'w_ukv', 'w_out')
FFN_SHARDED = ('w_up', 'ffn_conv_w', 'w_down')
FFN_WEIGHTS = ('w_up', 'ffn_conv_w', 'ffn_conv_b', 'w_down', 'ln2_g', 'ln2_b')


def _mixer_weights(gathered, l):
    full = {name: _unshard(gathered[name][:, l], name) for name in MIXER_SHARDED}
    return {'w_in_p': _permute_w_in(full['w_in']), 'conv_w': full['conv_w'], 'wq_pairs': _pair_wq(full['w_uq']),
            'wkv_pairs': _pair_wkv(full['w_ukv']), 'w_out': full['w_out']}


def _ffn_weights(gathered, l):
    return {name: _unshard(gathered[name][:, l], name) for name in FFN_SHARDED}


TM = 256
TS_RNN = 128
TS_FFN = 256
TQ_FWD = 256
TQ_BWD = 256
TN_DW = 512


def layer_fwd(xc, tabs, shared, gbufs, send, l, b, s):
    n = b * s
    lw = dict(shared, **_mixer_weights(gbufs, l))
    rnn4, ql, kvl, kr = inproj_fwd(xc, lw['w_in_p'], TM)
    h = rnn_fwd(rnn4.reshape(b, s, 4 * D), lw, l, b, s, TS_RNN)
    lat = (ql.reshape(b, s, Q_LORA), kvl.reshape(b, s, KV_LORA), kr.reshape(b, s, LANES))
    ex = Exchange(_gather_entries(FFN_SHARDED, l, send, gbufs)
                  + (_gather_entries(MIXER_SHARDED, l + 1, send, gbufs) if l + 1 < DEPTH else []))
    y_mla, lse, *xbufs = mla_fwd(*lat, tabs, lw, l, b, s, TQ_FWD, ex)
    gbufs = ex.updated(gbufs, xbufs)
    lw.update(_ffn_weights(gbufs, l))
    z1, x1 = mixout_fwd(xc, rnn4, h.reshape(n, D), y_mla.reshape(n, D), lw, l, TM)
    up, z2, x2 = ffn_fwd(x1, lw, l, b, s, TS_FFN)
    return x2, (xc, rnn4, lat, h, y_mla, lse, z1, x1, up, z2), lw, gbufs


def _gather_entries(group, l, send, gbufs):
    return [(k, send[k], False, l, l, gbufs[k]) for k in group]


def _scatter_entries(grads, l, pbufs):
    out = []
    for k, g in grads.items():
        g = _shard_pieces(g, k) if k in SHARDED else g
        out.append((k, g.astype(pbufs[k].dtype), k in SHARDED, None, l, pbufs[k]))
    return out


def layer_bwd(dx, saved, tabs, lw, l, b, s, pending, pbufs):
    n = b * s
    x0, rnn4, lat, h, y_mla, lse, z1, x1, up, z2 = saved
    dz2, dup, act, dfcw, dfcb, dg2, db2 = ffn_bwd(dx, z2, up, lw, l, b, s, TS_FFN)
    dx1 = matmul_dx(dup, lw['w_up'], dz2, ALPHA, TM, "ffn_up_dx")
    dw_up = matmul_dw(x1, dup, TN_DW, 2 * D_FF // 3, "ffn_up_dw")
    dw_down = matmul_dw(act, dz2, TN_DW, D // 2, "ffn_down_dw")
    ffn_grads = {'w_up': dw_up, 'ffn_conv_w': dfcw, 'ffn_conv_b': dfcb, 'w_down': dw_down, 'ln2_g': dg2, 'ln2_b': db2}
    dz1, dh, dy_mla, dg3, dw_out, dg1, db1 = mixout_bwd(dx1, z1, rnn4, h.reshape(n, D), y_mla.reshape(n, D),
                                                       lw, l, TM)
    ex = Exchange(pending + _scatter_entries(ffn_grads, l, pbufs))
    dql, dkvl, dkr, dwq, dwkv, dgq, dgkv, *xbufs = mla_bwd(dy_mla.reshape(b, s, D), y_mla, lse, *lat, tabs, lw, l,
                                                           b, s, TQ_BWD, ex)
    pbufs = ex.updated(pbufs, xbufs)
    dxr, dcw, dcb, dwgx, dbgx, dwga, dbga, dlam = rnn_bwd(dh.reshape(b, s, D), rnn4.reshape(b, s, 4 * D), h,
                                                          lw, l, b, s, TS_RNN)
    dx, dproj = inproj_bwd(dxr.reshape(n, D), dg3, dql.reshape(n, Q_LORA), dkvl.reshape(n, KV_LORA),
                           dkr.reshape(n, LANES), dz1, lw['w_in_p'], TM)
    dw_in_p = matmul_dw(x0, dproj, TN_DW, PROJ_W // 2, "inproj_dw")
    mixer_grads = {
        'w_in': _unpermute_dw_in(dw_in_p), 'conv_w': dcw, 'conv_b': dcb, 'gx_w': _ungroup_dgate(dwgx), 'gx_b': dbgx,
        'ga_w': _ungroup_dgate(dwga), 'ga_b': dbga, 'lru_lambda': dlam, 'q_norm_g': dgq, 'w_uq': _unpair_dwq(dwq),
        'kv_norm_g': dgkv, 'w_ukv': _unpair_dwkv(dwkv), 'w_out': dw_out, 'ln1_g': dg1, 'ln1_b': db1,
    }
    return dx, _scatter_entries(mixer_grads, l, pbufs), pbufs


SMALL_WEIGHT_ELEMS = 1 << 16


def kernel(x, positions, w_in, conv_w, conv_b, gx_w, gx_b, ga_w, ga_b, lru_lambda, q_norm_g, w_uq, kv_norm_g, w_ukv, w_out, ln1_g, ln1_b, w_up, ffn_conv_w, ffn_conv_b, w_down, ln2_g, ln2_b, loss_target, m_w_in, m_conv_w, m_conv_b, m_gx_w, m_gx_b, m_ga_w, m_ga_b, m_lru_lambda, m_q_norm_g, m_w_uq, m_kv_norm_g, m_w_ukv, m_w_out, m_ln1_g, m_ln1_b, m_w_up, m_ffn_conv_w, m_ffn_conv_b, m_w_down, m_ln2_g, m_ln2_b, v_w_in, v_conv_w, v_conv_b, v_gx_w, v_gx_b, v_ga_w, v_ga_b, v_lru_lambda, v_q_norm_g, v_w_uq, v_kv_norm_g, v_w_ukv, v_w_out, v_ln1_g, v_ln1_b, v_w_up, v_ffn_conv_w, v_ffn_conv_b, v_w_down, v_ln2_g, v_ln2_b):
    w = dict(zip(WEIGHT_NAMES, (w_in, conv_w, conv_b, gx_w, gx_b, ga_w, ga_b, lru_lambda, q_norm_g, w_uq, kv_norm_g,
                                w_ukv, w_out, ln1_g, ln1_b, w_up, ffn_conv_w, ffn_conv_b, w_down, ln2_g, ln2_b)))
    m = dict(zip(WEIGHT_NAMES, (m_w_in, m_conv_w, m_conv_b, m_gx_w, m_gx_b, m_ga_w, m_ga_b, m_lru_lambda,
                                m_q_norm_g, m_w_uq, m_kv_norm_g, m_w_ukv, m_w_out, m_ln1_g, m_ln1_b, m_w_up,
                                m_ffn_conv_w, m_ffn_conv_b, m_w_down, m_ln2_g, m_ln2_b)))
    v = dict(zip(WEIGHT_NAMES, (v_w_in, v_conv_w, v_conv_b, v_gx_w, v_gx_b, v_ga_w, v_ga_b, v_lru_lambda,
                                v_q_norm_g, v_w_uq, v_kv_norm_g, v_w_ukv, v_w_out, v_ln1_g, v_ln1_b, v_w_up,
                                v_ffn_conv_w, v_ffn_conv_b, v_w_down, v_ln2_g, v_ln2_b)))
    b, s, _ = x.shape
    n = b * s
    tabs = _rope_tables(positions)
    shared = {name: w[name] for name in WEIGHT_NAMES if name not in SHARDED and w[name].ndim == 2}
    shared['wgx'] = _group_gates(w['gx_w']).astype(MXU_DTYPE)
    shared['wga'] = _group_gates(w['ga_w']).astype(MXU_DTYPE)

    send = {k: w[k].astype(MXU_DTYPE) if k in MATMUL_WEIGHTS else w[k] for k in SHARDED}
    gbufs = {k: lax.empty((N_DEV,) + a.shape, a.dtype) for k, a in send.items()}
    head = Exchange(_gather_entries(MIXER_SHARDED, 0, send, gbufs))
    gbufs = head.updated(gbufs, exchange_layer(head, "gather_weights"))

    lws, saved = [], []
    xc = x.reshape(n, D)
    for l in range(DEPTH):
        xc, sv, lw, gbufs = layer_fwd(xc, tabs, shared, gbufs, send, l, b, s)
        lws.append(lw)
        saved.append(sv)
    loss_part, dx = loss_head(xc, loss_target.reshape(n, D), TM)

    def part_shape(k):
        piece = w[k].shape[1:]
        return (N_DEV, DEPTH) + ((1,) + piece if len(piece) == 1 else piece)

    pbufs = {k: lax.empty(part_shape(k), MXU_DTYPE if k in MATMUL_WEIGHTS else F32) for k in WEIGHT_NAMES}
    pending = []
    for l in reversed(range(DEPTH)):
        dx, pending, pbufs = layer_bwd(dx, saved[l], tabs, lws[l], l, b, s, pending, pbufs)
    tail = Exchange(pending)
    parts = tail.updated(pbufs, exchange_layer(tail, "scatter_grads"))

    out = {}
    small = [k for k in WEIGHT_NAMES if w[k].size <= SMALL_WEIGHT_ELEMS]
    res = adamw_small([(parts[k], w[k], m[k], v[k]) for k in small])
    for j, k in enumerate(small):
        out[k] = tuple(res[i * len(small) + j] for i in range(4))
    for k in WEIGHT_NAMES:
        if k in out:
            continue
        shape = w[k].shape
        view = (DEPTH, -1, shape[-1])
        r = adamw_tiled(parts[k].reshape((N_DEV,) + view), w[k].reshape(view), m[k].reshape(view),
                        v[k].reshape(view), k)
        out[k] = tuple(a.reshape(shape) for a in r)
    loss = lax.psum(loss_part[0, 0], ("x", "y", "c"))
    return (loss, dx.reshape(b, s, D), *[out[k][0] for k in WEIGHT_NAMES], *[out[k][1] for k in WEIGHT_NAMES],
            *[out[k][2] for k in WEIGHT_NAMES], *[out[k][3] for k in WEIGHT_NAMES])
```

```python
import functools
import math

import jax
import jax.numpy as jnp
from jax import lax
from jax.experimental import pallas as pl
from jax.experimental.pallas import tpu as pltpu

F32 = jnp.float32
MXU_DTYPE = jnp.bfloat16

D = 1024
DEPTH = 4
N_RNN_BLOCKS = 16
RNN_BLOCK = 64
CONV_W = 4
LRU_C = 8.0
N_HEADS = 16
QK_NOPE = 64
QK_ROPE = 32
V_HEAD = 64
Q_LORA = 384
KV_LORA = 256
ROPE_THETA = 10000.0
D_FF = 3 * D
FFN_CONV_W = 3
IN_WIDTH = 2 * D + Q_LORA + KV_LORA + QK_ROPE + 2 * D
ALPHA = (2 * DEPTH) ** 0.25
EPS = 1e-6
NEG_INF = -1e30
ATT_SCALE = (QK_NOPE + QK_ROPE) ** -0.5
GELU_C = math.sqrt(2.0 / math.pi)

ADAM_LR = 0.001
ADAM_B1 = 0.9
ADAM_B2 = 0.999
ADAM_EPS = 1e-08
ADAM_WD = 0.01
ADAM_STEP = 10

N_DEV = 8
LANES = 128
SUBLANES = 8
MXU_GROUP = 256
N_GATE_GROUPS = D // MXU_GROUP
HEADS_PER_STEP = 2
N_HEAD_PAIRS = N_HEADS // HEADS_PER_STEP
HEAD_LANES = 128
PROJ_W = 4 * D + Q_LORA + KV_LORA + LANES
VMEM_BIG = 56 * 2 ** 20

WEIGHT_NAMES = ['w_in', 'conv_w', 'conv_b', 'gx_w', 'gx_b', 'ga_w', 'ga_b', 'lru_lambda', 'q_norm_g', 'w_uq',
                'kv_norm_g', 'w_ukv', 'w_out', 'ln1_g', 'ln1_b', 'w_up', 'ffn_conv_w', 'ffn_conv_b', 'w_down',
                'ln2_g', 'ln2_b']
SHARDED = {
    'w_in': (2, (DEPTH, D, IN_WIDTH)),
    'conv_w': (2, (DEPTH, CONV_W, D)),
    'w_uq': (2, (DEPTH, Q_LORA, N_HEADS * (QK_NOPE + QK_ROPE))),
    'w_ukv': (2, (DEPTH, KV_LORA, N_HEADS * (QK_NOPE + V_HEAD))),
    'w_out': (1, (DEPTH, D, D)),
    'w_up': (2, (DEPTH, D, 2 * D_FF)),
    'ffn_conv_w': (2, (DEPTH, FFN_CONV_W, 2 * D_FF)),
    'w_down': (1, (DEPTH, D_FF, D)),
}
MATMUL_WEIGHTS = ('w_in', 'w_uq', 'w_ukv', 'w_out', 'w_up', 'w_down')


def _mm(a, b):
    return jnp.dot(a.astype(MXU_DTYPE), b.astype(MXU_DTYPE), preferred_element_type=F32)


def _mm_tn(a, b):
    return lax.dot_general(a.astype(MXU_DTYPE), b.astype(MXU_DTYPE), (((0,), (0,)), ((), ())),
                           preferred_element_type=F32)


def _mm_nt(a, b):
    return lax.dot_general(a.astype(MXU_DTYPE), b.astype(MXU_DTYPE), (((1,), (1,)), ((), ())),
                           preferred_element_type=F32)


def _sigmoid(x):
    return 1.0 / (1.0 + jnp.exp(-x))


def _gelu(x):
    t = jnp.tanh(GELU_C * (x + 0.044715 * (x * x * x)))
    return 0.5 * x * (1.0 + t), t


def _gelu_grad(x, t):
    return 0.5 * (1.0 + t) + 0.5 * x * (1.0 - t * t) * (GELU_C * (1.0 + 3.0 * 0.044715 * (x * x)))


def _neg_expm1(y):
    series = -y * (1.0 + 0.5 * y * (1.0 + (y / 3.0) * (1.0 + 0.25 * y * (1.0 + 0.2 * y))))
    return jnp.where(y > -0.05, series, 1.0 - jnp.exp(y))


def _ln_stats(z):
    mu = jnp.mean(z, axis=-1, keepdims=True)
    zc = z - mu
    var = jnp.mean(zc * zc, axis=-1, keepdims=True)
    r = lax.rsqrt(var + EPS)
    return zc * r, r


def _ln_bwd(dy, z, g):
    xhat, r = _ln_stats(z)
    dxh = dy * g
    dz = r * (dxh - jnp.mean(dxh, axis=-1, keepdims=True)
              - xhat * jnp.mean(dxh * xhat, axis=-1, keepdims=True))
    return dz, jnp.sum(dy * xhat, axis=0, keepdims=True), jnp.sum(dy, axis=0, keepdims=True)


def _rms_stats(x):
    r = lax.rsqrt(jnp.mean(x * x, axis=-1, keepdims=True) + EPS)
    return x * r, r


def _rms_bwd(dy, x, g):
    xn, r = _rms_stats(x)
    dxn = dy * g
    dx = r * (dxn - xn * jnp.mean(dxn * xn, axis=-1, keepdims=True))
    return dx, jnp.sum(dy * xn, axis=0, keepdims=True)


def _shift_down(x, halo, s, axis):
    if s == 0:
        return x
    r = pltpu.roll(x, s, axis)
    hr = pltpu.roll(halo, s, axis)
    idx = lax.broadcasted_iota(jnp.int32, hr.shape, axis)
    head = lax.slice_in_dim(r, 0, SUBLANES, axis=axis)
    rest = lax.slice_in_dim(r, SUBLANES, x.shape[axis], axis=axis)
    return jnp.concatenate([jnp.where(idx < s, hr, head), rest], axis=axis)


def _shift_up(x, halo, s, axis):
    if s == 0:
        return x
    n = x.shape[axis]
    r = pltpu.roll(x, n - s, axis)
    hr = pltpu.roll(halo, SUBLANES - s, axis)
    idx = lax.broadcasted_iota(jnp.int32, hr.shape, axis)
    body = lax.slice_in_dim(r, 0, n - SUBLANES, axis=axis)
    tail = lax.slice_in_dim(r, n - SUBLANES, n, axis=axis)
    return jnp.concatenate([body, jnp.where(idx >= SUBLANES - s, hr, tail)], axis=axis)


def _const_spec(shape):
    nd = len(shape)
    return pl.BlockSpec(shape, lambda *_: (0,) * nd)


def _layer_spec(shape, l):
    nd = len(shape)
    return pl.BlockSpec((None,) + tuple(shape), lambda *_: (l,) + (0,) * nd)


def _resident(shape):
    nd = len(shape)
    return pl.BlockSpec(shape, lambda *_: (0,) * nd, pipeline_mode=pl.Buffered(1))


def _params(vmem=None):
    return pltpu.CompilerParams(vmem_limit_bytes=vmem)


def inproj_fwd(x, w_in_p, tm):
    n = x.shape[0]

    def body(x_ref, w_ref, rnn4_ref, ql_ref, kvl_ref, kr_ref):
        xb = x_ref[...].astype(MXU_DTYPE)
        for j in range(4):
            rnn4_ref[:, j * D:(j + 1) * D] = _mm(xb, w_ref[:, j * D:(j + 1) * D])
        o = 4 * D
        ql_ref[...] = _mm(xb, w_ref[:, o:o + Q_LORA])
        kvl_ref[...] = _mm(xb, w_ref[:, o + Q_LORA:o + Q_LORA + KV_LORA])
        kr_ref[...] = _mm(xb, w_ref[:, o + Q_LORA + KV_LORA:PROJ_W])

    row = lambda i: (i, 0)
    return pl.pallas_call(
        body, name="inproj_fwd", grid=(n // tm,),
        in_specs=[pl.BlockSpec((tm, D), row), _resident((D, PROJ_W))],
        out_specs=[pl.BlockSpec((tm, 4 * D), row), pl.BlockSpec((tm, Q_LORA), row),
                   pl.BlockSpec((tm, KV_LORA), row), pl.BlockSpec((tm, LANES), row)],
        out_shape=[jax.ShapeDtypeStruct((n, 4 * D), F32), jax.ShapeDtypeStruct((n, Q_LORA), F32),
                   jax.ShapeDtypeStruct((n, KV_LORA), F32), jax.ShapeDtypeStruct((n, LANES), F32)],
        compiler_params=_params(VMEM_BIG),
    )(x, w_in_p)


def inproj_bwd(dxr, dg3, dql, dkvl, dkr, dz1, w_in_p, tm):
    n = dz1.shape[0]

    def body(dxr_ref, dg3_ref, dql_ref, dkvl_ref, dkr_ref, dz_ref, w_ref, dx_ref, dp_ref):
        dp = jnp.concatenate([dxr_ref[...], dg3_ref[...], dql_ref[...], dkvl_ref[...], dkr_ref[...]],
                             axis=1).astype(MXU_DTYPE)
        dp_ref[...] = dp
        dx_ref[...] = ALPHA * dz_ref[...] + _mm_nt(dp, w_ref[...])

    row = lambda i: (i, 0)
    return pl.pallas_call(
        body, name="inproj_bwd", grid=(n // tm,),
        in_specs=[pl.BlockSpec((tm, D), row), pl.BlockSpec((tm, 3 * D), row), pl.BlockSpec((tm, Q_LORA), row),
                  pl.BlockSpec((tm, KV_LORA), row), pl.BlockSpec((tm, LANES), row), pl.BlockSpec((tm, D), row),
                  _resident((D, PROJ_W))],
        out_specs=[pl.BlockSpec((tm, D), row), pl.BlockSpec((tm, PROJ_W), row)],
        out_shape=[jax.ShapeDtypeStruct((n, D), F32), jax.ShapeDtypeStruct((n, PROJ_W), MXU_DTYPE)],
        compiler_params=_params(VMEM_BIG),
    )(dxr, dg3, dql, dkvl, dkr, dz1, w_in_p)


def matmul_dw(x, dy, tn, tmc, name):
    n, k = x.shape
    m = dy.shape[1]

    def body(x_ref, dy_ref, dw_ref):
        @pl.when(pl.program_id(1) == 0)
        def _():
            dw_ref[...] = jnp.zeros_like(dw_ref)
        dw_ref[...] += _mm_tn(x_ref[...], dy_ref[...])

    return pl.pallas_call(
        body, name=name, grid=(m // tmc, n // tn),
        in_specs=[pl.BlockSpec((tn, k), lambda j, i: (i, 0)), pl.BlockSpec((tn, tmc), lambda j, i: (i, j))],
        out_specs=pl.BlockSpec((k, tmc), lambda j, i: (0, j)),
        out_shape=jax.ShapeDtypeStruct((k, m), F32),
        compiler_params=_params(VMEM_BIG),
    )(x, dy)


def matmul_dx(dy, w, add, add_scale, tm, name):
    n, m = dy.shape
    k = w.shape[0]

    def body(dy_ref, w_ref, add_ref, dx_ref):
        dx_ref[...] = add_scale * add_ref[...] + _mm_nt(dy_ref[...], w_ref[...])

    row = lambda i: (i, 0)
    return pl.pallas_call(
        body, name=name, grid=(n // tm,),
        in_specs=[pl.BlockSpec((tm, m), row), _resident((k, m)), pl.BlockSpec((tm, k), row)],
        out_specs=pl.BlockSpec((tm, k), row),
        out_shape=jax.ShapeDtypeStruct((n, k), F32),
        compiler_params=_params(VMEM_BIG),
    )(dy, w, add)


def _group(g):
    return slice(g * MXU_GROUP, (g + 1) * MXU_GROUP)


def _rnn_gates(x, halo, g, l, cw_ref, cb_ref, wgx_ref, bgx_ref, wga_ref, bga_ref, lam_ref):
    b, ts, gw = x.shape
    sl = _group(g)
    lr = slice(l, l + 1)
    xc = cb_ref[lr, sl][None]
    for k in range(CONV_W):
        xc = xc + cw_ref[k:k + 1, sl][None] * _shift_down(x, halo, CONV_W - 1 - k, 1)
    xc2 = xc.reshape(b * ts, gw)
    xcb = xc2.astype(MXU_DTYPE)
    gx = _sigmoid(_mm(xcb, wgx_ref[g]) + bgx_ref[lr, sl])
    ga = _sigmoid(_mm(xcb, wga_ref[g]) + bga_ref[lr, sl])
    nl = -lam_ref[lr, sl]
    sp = jnp.maximum(nl, 0.0) + jnp.log1p(jnp.exp(-jnp.abs(nl)))
    log_a = (-LRU_C) * ga * sp
    a = jnp.exp(log_a)
    mult = jnp.sqrt(_neg_expm1(2.0 * log_a))
    return xc2, xcb, gx, ga, sp, a, mult


def rnn_fwd(rnn4, lw, l, b, s, ts):
    ns = s // ts

    def body(x_ref, cw_ref, cb_ref, wgx_ref, bgx_ref, wga_ref, bga_ref, lam_ref, h_ref,
             halo_sc, hstate_sc, a_sc, u_sc):
        @pl.when(pl.program_id(0) == 0)
        def _():
            halo_sc[...] = jnp.zeros_like(halo_sc)
            hstate_sc[...] = jnp.zeros_like(hstate_sc)

        for g in range(N_GATE_GROUPS):
            sl = _group(g)
            x = x_ref[:, :, sl]
            xc2, _, gx, _, _, a, mult = _rnn_gates(x, halo_sc[:, :, sl], g, l, cw_ref, cb_ref, wgx_ref, bgx_ref,
                                                   wga_ref, bga_ref, lam_ref)
            halo_sc[:, :, sl] = x[:, ts - SUBLANES:, :]
            a_sc[:, :, sl] = a.reshape(b, ts, MXU_GROUP)
            u_sc[:, :, sl] = (mult * gx * xc2).reshape(b, ts, MXU_GROUP)

        def step(t, h):
            h = a_sc[:, pl.ds(t, 1), :] * h + u_sc[:, pl.ds(t, 1), :]
            h_ref[:, pl.ds(t, 1), :] = h
            return h

        hstate_sc[...] = lax.fori_loop(0, ts, step, hstate_sc[...], unroll=8)

    tile = lambda i: (0, i, 0)
    vecs = _const_spec((DEPTH, D))
    gates = _layer_spec((N_GATE_GROUPS, MXU_GROUP, MXU_GROUP), l)
    return pl.pallas_call(
        body, name="rnn_fwd", grid=(ns,),
        in_specs=[pl.BlockSpec((b, ts, D), tile), _const_spec((CONV_W, D)), vecs, gates, vecs, gates, vecs, vecs],
        out_specs=pl.BlockSpec((b, ts, D), tile),
        out_shape=jax.ShapeDtypeStruct((b, s, D), F32),
        scratch_shapes=[pltpu.VMEM((b, SUBLANES, D), F32), pltpu.VMEM((b, 1, D), F32),
                        pltpu.VMEM((b, ts, D), F32), pltpu.VMEM((b, ts, D), F32)],
        compiler_params=_params(VMEM_BIG),
    )(rnn4, lw['conv_w'], lw['conv_b'], lw['wgx'], lw['gx_b'], lw['wga'], lw['ga_b'], lw['lru_lambda'])


def rnn_bwd(dh, rnn4, h, lw, l, b, s, ts):
    ns = s // ts
    hb = ts // SUBLANES

    def body(dh_ref, x_ref, xh_ref, h_ref, hh_ref, cw_ref, cb_ref, wgx_ref, bgx_ref, wga_ref, bga_ref, lam_ref,
             dx_ref, dcw_ref, dcb_ref, dwgx_ref, dbgx_ref, dwga_ref, dbga_ref, dlam_ref,
             carry_sc, dxc_halo_sc, a_sc, delta_sc, xc_sc, gx_sc, ga_sc, mult_sc):
        i = pl.program_id(0)

        @pl.when(i == 0)
        def _():
            carry_sc[...] = jnp.zeros_like(carry_sc)
            dxc_halo_sc[...] = jnp.zeros_like(dxc_halo_sc)
            for r in (dcw_ref, dcb_ref, dwgx_ref, dbgx_ref, dwga_ref, dbga_ref, dlam_ref):
                r[...] = jnp.zeros_like(r)

        keep = jnp.where(i == ns - 1, 0.0, 1.0)
        for g in range(N_GATE_GROUPS):
            sl = _group(g)
            xc2, _, gx, ga, _, a, mult = _rnn_gates(x_ref[:, :, sl], xh_ref[:, :, sl] * keep, g, l, cw_ref, cb_ref,
                                                    wgx_ref, bgx_ref, wga_ref, bga_ref, lam_ref)
            for sc, val in ((a_sc, a), (xc_sc, xc2), (gx_sc, gx), (ga_sc, ga), (mult_sc, mult)):
                sc[:, :, sl] = val.reshape(b, ts, MXU_GROUP)

        def step(j, c):
            t = ts - 1 - j
            d = dh_ref[:, pl.ds(t, 1), :] + c
            delta_sc[:, pl.ds(t, 1), :] = d
            return a_sc[:, pl.ds(t, 1), :] * d

        carry_sc[...] = lax.fori_loop(0, ts, step, carry_sc[...], unroll=8)

        for g in range(N_GATE_GROUPS):
            sl = _group(g)
            x = x_ref[:, :, sl]
            xhalo = xh_ref[:, :, sl] * keep
            flat = lambda sc, sl=sl: sc[:, :, sl].reshape(b * ts, MXU_GROUP)
            xc2, gx, ga, a, mult, delta = (flat(sc) for sc in (xc_sc, gx_sc, ga_sc, a_sc, mult_sc, delta_sc))
            xcb = xc2.astype(MXU_DTYPE)
            nl = -lam_ref[l:l + 1, sl]
            sp = jnp.maximum(nl, 0.0) + jnp.log1p(jnp.exp(-jnp.abs(nl)))
            hprev = _shift_down(h_ref[:, :, sl], hh_ref[:, :, sl] * keep, 1, 1).reshape(b * ts, MXU_GROUP)
            dmult = delta * gx * xc2
            dl = delta * hprev * a - dmult * (a * a) / mult
            dga = dl * ((-LRU_C) * sp)
            dlam_ref[:, sl] += (jnp.sum(dl * ((-LRU_C) * ga), axis=0, keepdims=True)
                                * (-_sigmoid(-lam_ref[l:l + 1, sl])))
            dpa = dga * ga * (1.0 - ga)
            dpx = (delta * mult * xc2) * gx * (1.0 - gx)
            dbga_ref[:, sl] += jnp.sum(dpa, axis=0, keepdims=True)
            dbgx_ref[:, sl] += jnp.sum(dpx, axis=0, keepdims=True)
            dpab = dpa.astype(MXU_DTYPE)
            dpxb = dpx.astype(MXU_DTYPE)
            dwga_ref[g] += _mm_tn(xcb, dpab)
            dwgx_ref[g] += _mm_tn(xcb, dpxb)
            dxc2 = delta * mult * gx + _mm_nt(dpab, wga_ref[g]) + _mm_nt(dpxb, wgx_ref[g])
            dcb_ref[:, sl] += jnp.sum(dxc2, axis=0, keepdims=True)
            dxc = dxc2.reshape(b, ts, MXU_GROUP)
            nhalo = dxc_halo_sc[:, :, sl]
            dx = jnp.zeros_like(dxc)
            for k in range(CONV_W):
                sft = CONV_W - 1 - k
                xs = _shift_down(x, xhalo, sft, 1)
                dcw_ref[k:k + 1, sl] += jnp.sum((dxc * xs).reshape(b * ts, MXU_GROUP), axis=0, keepdims=True)
                dx = dx + cw_ref[k:k + 1, sl][None] * _shift_up(dxc, nhalo, sft, 1)
            dx_ref[:, :, sl] = dx
            dxc_halo_sc[:, :, sl] = dxc[:, :SUBLANES, :]

    tile = lambda i: (0, ns - 1 - i, 0)
    halo = lambda i: (0, jnp.maximum((ns - 1 - i) * hb - 1, 0), 0)
    gshape = (N_GATE_GROUPS, MXU_GROUP, MXU_GROUP)
    vecs = _const_spec((DEPTH, D))
    gates = _layer_spec(gshape, l)
    vec = jax.ShapeDtypeStruct((1, D), F32)
    return pl.pallas_call(
        body, name="rnn_bwd", grid=(ns,),
        in_specs=[pl.BlockSpec((b, ts, D), tile), pl.BlockSpec((b, ts, D), tile),
                  pl.BlockSpec((b, SUBLANES, D), halo), pl.BlockSpec((b, ts, D), tile),
                  pl.BlockSpec((b, SUBLANES, D), halo),
                  _const_spec((CONV_W, D)), vecs, gates, vecs, gates, vecs, vecs],
        out_specs=[pl.BlockSpec((b, ts, D), tile), _const_spec((CONV_W, D)), _const_spec((1, D)),
                   _const_spec(gshape), _const_spec((1, D)), _const_spec(gshape), _const_spec((1, D)),
                   _const_spec((1, D))],
        out_shape=[jax.ShapeDtypeStruct((b, s, D), F32), jax.ShapeDtypeStruct((CONV_W, D), F32), vec,
                   jax.ShapeDtypeStruct(gshape, F32), vec, jax.ShapeDtypeStruct(gshape, F32), vec, vec],
        scratch_shapes=[pltpu.VMEM((b, 1, D), F32), pltpu.VMEM((b, SUBLANES, D), F32)]
        + [pltpu.VMEM((b, ts, D), F32)] * 6,
        compiler_params=_params(VMEM_BIG),
    )(dh, rnn4, rnn4, h, h, lw['conv_w'], lw['conv_b'], lw['wgx'], lw['gx_b'], lw['wga'], lw['ga_b'],
      lw['lru_lambda'])


def _rope(x, cos, sa, sb):
    return x * cos + pltpu.roll(x, HEAD_LANES - QK_ROPE // 2, 1) * sa + pltpu.roll(x, QK_ROPE // 2, 1) * sb


def _unrope(d, cos, sa, sb):
    return d * cos + pltpu.roll(d * sa, QK_ROPE // 2, 1) + pltpu.roll(d * sb, HEAD_LANES - QK_ROPE // 2, 1)


LOG2_E = 1.0 / math.log(2.0)
Q_PRESCALE = ATT_SCALE * LOG2_E


def _scores(q_blk, keys):
    return _mm_nt(q_blk, keys)


def _diag_scores(q_blk, keys):
    tq = q_blk.shape[0]
    keep = lax.broadcasted_iota(jnp.int32, (tq, tq), 0) >= lax.broadcasted_iota(jnp.int32, (tq, tq), 1)
    return jnp.where(keep, _scores(q_blk, keys), NEG_INF)


def _mla_project(ql_ref, kvl_ref, l, gq_ref, gkv_ref, wq_ref, wkv_ref):
    qn, _ = _rms_stats(ql_ref[0])
    qn = (qn * gq_ref[l:l + 1, :]).astype(MXU_DTYPE)
    kvn, _ = _rms_stats(kvl_ref[0])
    kvn = (kvn * gkv_ref[l:l + 1, :]).astype(MXU_DTYPE)
    return qn, kvn, _mm(qn, wq_ref[0]), _mm(kvn, wkv_ref[0])


def mla_fwd(ql, kvl, kr, tabs, lw, l, b, s, tq, ex=None):
    nq = s // tq
    cos_t, sa_t, sb_t = tabs
    n_in, n_out = 10, 2
    x_args, x_in, x_out, x_shapes, x_alias, x_scratch = _fuse_exchange(ex, n_in, n_out)

    def body(*refs):
        bi, p = pl.program_id(0), pl.program_id(1)
        first = (bi == 0) & (p == 0)
        last = (bi == b - 1) & (p == N_HEAD_PAIRS - 1)

        @_run_exchange(ex, refs, n_in, n_out, 0, first, last)
        def _():
            compute(*refs[:n_in], *refs[n_in + len(x_in):n_in + len(x_in) + n_out])

    def compute(ql_ref, kvl_ref, kr_ref, cos_ref, sa_ref, sb_ref, gq_ref, gkv_ref, wq_ref, wkv_ref, o_ref, lse_ref):
        _, _, qp, kvp = _mla_project(ql_ref, kvl_ref, l, gq_ref, gkv_ref, wq_ref, wkv_ref)
        cos, sa, sb = cos_ref[0], sa_ref[0], sb_ref[0]
        for hh in range(HEADS_PER_STEP):
            hs = slice(hh * HEAD_LANES, (hh + 1) * HEAD_LANES)
            q = (_rope(qp[:, hs], cos, sa, sb) * Q_PRESCALE).astype(MXU_DTYPE)
            k = _rope(kvp[:, hs] + kr_ref[0], cos, sa, sb).astype(MXU_DTYPE)
            v = kvp[:, HEADS_PER_STEP * HEAD_LANES + hh * HEAD_LANES:
                    HEADS_PER_STEP * HEAD_LANES + (hh + 1) * HEAD_LANES].astype(MXU_DTYPE)
            def block_scores(qb):
                lo, hi = qb * tq, (qb + 1) * tq
                return _diag_scores(q[lo:hi], k[lo:hi]), (_scores(q[lo:hi], k[:lo]) if qb else None)

            def block_softmax(qb, sd, sf):
                m = jnp.max(sd, axis=-1, keepdims=True)
                if qb:
                    m = jnp.maximum(m, jnp.max(sf, axis=-1, keepdims=True))
                ed = jnp.exp2(sd - m)
                den = jnp.sum(ed, axis=-1, keepdims=True)
                ef = None
                if qb:
                    ef = jnp.exp2(sf - m)
                    den = den + jnp.sum(ef, axis=-1, keepdims=True)
                return ed.astype(MXU_DTYPE), (ef.astype(MXU_DTYPE) if qb else None), den, m + jnp.log2(den)

            def block_out(qb, ed, ef, den, lse, hh=hh, v=v):
                lo, hi = qb * tq, (qb + 1) * tq
                o = _mm(ed, v[lo:hi])
                if qb:
                    o = o + _mm(ef, v[:lo])
                o = o * (1.0 / den)
                lse_ref[0, hh, lo:hi, :] = jnp.broadcast_to(lse, (tq, LANES))
                if hh == 0:
                    o_ref[0, lo:hi, :] = o
                else:
                    o_ref[0, lo:hi, :] += o

            nxt, prev = block_scores(0), None
            for qb in range(nq):
                sd, sf = nxt
                if qb + 1 < nq:
                    nxt = block_scores(qb + 1)
                if prev is not None:
                    block_out(qb - 1, *prev)
                prev = block_softmax(qb, sd, sf)
            block_out(nq - 1, *prev)

    seq = lambda bi, p: (bi, 0, 0)
    pair = lambda bi, p: (p, 0, 0)

    def per_seq(w):
        return pl.BlockSpec((1, s, w), seq, pipeline_mode=pl.Buffered(1))

    return pl.pallas_call(
        body, name="mla_fwd", grid=(b, N_HEAD_PAIRS),
        in_specs=[per_seq(Q_LORA), per_seq(KV_LORA), per_seq(LANES), per_seq(LANES), per_seq(LANES), per_seq(LANES),
                  _const_spec((DEPTH, Q_LORA)), _const_spec((DEPTH, KV_LORA)),
                  pl.BlockSpec((1, Q_LORA, HEADS_PER_STEP * HEAD_LANES), pair),
                  pl.BlockSpec((1, KV_LORA, 2 * HEADS_PER_STEP * HEAD_LANES), pair)] + x_in,
        out_specs=[pl.BlockSpec((1, s, LANES), lambda bi, p: (bi, 0, p)),
                   pl.BlockSpec((1, HEADS_PER_STEP, s, LANES), lambda bi, p: (bi, p, 0, 0))] + x_out,
        out_shape=[jax.ShapeDtypeStruct((b, s, D), F32), jax.ShapeDtypeStruct((b, N_HEADS, s, LANES), F32)] + x_shapes,
        input_output_aliases=x_alias, scratch_shapes=x_scratch,
        compiler_params=_params(VMEM_BIG),
    )(ql, kvl, kr, cos_t, sa_t, sb_t, lw['q_norm_g'], lw['kv_norm_g'], lw['wq_pairs'], lw['wkv_pairs'], *x_args)


def mla_bwd(dy, y, lse, ql, kvl, kr, tabs, lw, l, b, s, tq, ex=None):
    nq = s // tq
    cos_t, sa_t, sb_t = tabs
    qw = HEADS_PER_STEP * HEAD_LANES
    kvw = 2 * HEADS_PER_STEP * HEAD_LANES
    n_in, n_out, n_scratch = 13, 7, 2
    x_args, x_in, x_out, x_shapes, x_alias, x_scratch = _fuse_exchange(ex, n_in, n_out)

    def body(*refs):
        bi, p = pl.program_id(0), pl.program_id(1)
        first = (bi == 0) & (p == 0)
        last = (bi == b - 1) & (p == N_HEAD_PAIRS - 1)
        o0 = n_in + len(x_in)
        s0 = o0 + n_out + len(x_out)

        @_run_exchange(ex, refs, n_in, n_out, n_scratch, first, last)
        def _():
            compute(*refs[:n_in], *refs[o0:o0 + n_out], *refs[s0:s0 + n_scratch])

    def compute(dy_ref, y_ref, lse_ref, ql_ref, kvl_ref, kr_ref, cos_ref, sa_ref, sb_ref, gq_ref, gkv_ref, wq_ref,
                wkv_ref, dql_ref, dkvl_ref, dkr_ref, dwq_ref, dwkv_ref, dgq_ref, dgkv_ref, dk_sc, dv_sc):
        bi, p = pl.program_id(0), pl.program_id(1)

        @pl.when((bi == 0) & (p == 0))
        def _():
            for r in (dwq_ref, dwkv_ref, dgq_ref, dgkv_ref):
                r[...] = jnp.zeros_like(r)

        @pl.when(p == 0)
        def _():
            for r in (dql_ref, dkvl_ref, dkr_ref):
                r[...] = jnp.zeros_like(r)

        qn, kvn, qp, kvp = _mla_project(ql_ref, kvl_ref, l, gq_ref, gkv_ref, wq_ref, wkv_ref)
        cos, sa, sb = cos_ref[0], sa_ref[0], sb_ref[0]
        dof = dy_ref[0]
        do = dof.astype(MXU_DTYPE)
        prod = dof * y_ref[0]
        lane = lax.broadcasted_iota(jnp.int32, prod.shape, 1)
        dq_heads, dk_heads, dv_heads = [], [], []
        for hh in range(HEADS_PER_STEP):
            hs = slice(hh * HEAD_LANES, (hh + 1) * HEAD_LANES)
            q = (_rope(qp[:, hs], cos, sa, sb) * Q_PRESCALE).astype(MXU_DTYPE)
            k = _rope(kvp[:, hs] + kr_ref[0], cos, sa, sb).astype(MXU_DTYPE)
            v = kvp[:, qw + hh * HEAD_LANES:qw + (hh + 1) * HEAD_LANES].astype(MXU_DTYPE)
            mine = (lane >= hh * V_HEAD) & (lane < (hh + 1) * V_HEAD)
            delta = jnp.sum(jnp.where(mine, prod, 0.0), axis=-1, keepdims=True)
            dk_sc[...] = jnp.zeros_like(dk_sc)
            dv_sc[...] = jnp.zeros_like(dv_sc)
            units = []
            for qb in range(nq):
                units.append((qb, slice(qb * tq, (qb + 1) * tq), True))
                if qb:
                    units.append((qb, slice(0, qb * tq), False))

            def unit_matmuls_in(u, q=q, k=k, v=v):
                qb, ks, diag = u
                qs = slice(qb * tq, (qb + 1) * tq)
                sc = _diag_scores(q[qs], k[ks]) if diag else _scores(q[qs], k[ks])
                return sc, _mm_nt(do[qs], v[ks])

            def unit_elementwise(u, sc, dp, hh=hh, delta=delta):
                qs = slice(u[0] * tq, (u[0] + 1) * tq)
                pr = jnp.exp2(sc - lse_ref[0, hh, qs, 0:1])
                return pr.astype(MXU_DTYPE), (pr * (dp - delta[qs])).astype(MXU_DTYPE)

            dq_blocks = [None] * nq

            def unit_matmuls_out(u, prb, ds, q=q, k=k):
                qb, ks, _ = u
                qs = slice(qb * tq, (qb + 1) * tq)
                dv_sc[ks, :] += _mm_tn(prb, do[qs])
                part = _mm(ds, k[ks])
                dq_blocks[qb] = part if dq_blocks[qb] is None else dq_blocks[qb] + part
                dk_sc[ks, :] += _mm_tn(ds, q[qs])

            nxt, prev = unit_matmuls_in(units[0]), None
            for i, u in enumerate(units):
                sc, dp = nxt
                if i + 1 < len(units):
                    nxt = unit_matmuls_in(units[i + 1])
                if prev is not None:
                    unit_matmuls_out(units[i - 1], *prev)
                prev = unit_elementwise(u, sc, dp)
            unit_matmuls_out(units[-1], *prev)
            dq_heads.append(_unrope(jnp.concatenate(dq_blocks, axis=0) * ATT_SCALE, cos, sa, sb))
            dk_full = _unrope(dk_sc[...] * (1.0 / LOG2_E), cos, sa, sb)
            dkr_ref[0] += dk_full
            dk_heads.append(dk_full)
            dv_heads.append(dv_sc[...])
        dqp = jnp.concatenate(dq_heads, axis=1).astype(MXU_DTYPE)
        dkvp = jnp.concatenate(dk_heads + dv_heads, axis=1).astype(MXU_DTYPE)
        dwq_ref[p] += _mm_tn(qn, dqp)
        dwkv_ref[p] += _mm_tn(kvn, dkvp)
        dql_ref[0] += _mm_nt(dqp, wq_ref[0])
        dkvl_ref[0] += _mm_nt(dkvp, wkv_ref[0])

        @pl.when(p == N_HEAD_PAIRS - 1)
        def _():
            dx, dg = _rms_bwd(dql_ref[0], ql_ref[0], gq_ref[l:l + 1, :])
            dql_ref[0] = dx
            dgq_ref[...] += dg
            dx, dg = _rms_bwd(dkvl_ref[0], kvl_ref[0], gkv_ref[l:l + 1, :])
            dkvl_ref[0] = dx
            dgkv_ref[...] += dg

    seq = lambda bi, p: (bi, 0, 0)
    pair = lambda bi, p: (p, 0, 0)

    def per_seq(w):
        return pl.BlockSpec((1, s, w), seq, pipeline_mode=pl.Buffered(1))

    return pl.pallas_call(
        body, name="mla_bwd", grid=(b, N_HEAD_PAIRS),
        in_specs=[pl.BlockSpec((1, s, LANES), lambda bi, p: (bi, 0, p)),
                  pl.BlockSpec((1, s, LANES), lambda bi, p: (bi, 0, p)),
                  pl.BlockSpec((1, HEADS_PER_STEP, s, LANES), lambda bi, p: (bi, p, 0, 0)),
                  per_seq(Q_LORA), per_seq(KV_LORA), per_seq(LANES), per_seq(LANES), per_seq(LANES), per_seq(LANES),
                  _const_spec((DEPTH, Q_LORA)), _const_spec((DEPTH, KV_LORA)),
                  pl.BlockSpec((1, Q_LORA, qw), pair), pl.BlockSpec((1, KV_LORA, kvw), pair)] + x_in,
        out_specs=[pl.BlockSpec((1, s, Q_LORA), seq), pl.BlockSpec((1, s, KV_LORA), seq),
                   pl.BlockSpec((1, s, LANES), seq),
                   _const_spec((N_HEAD_PAIRS, Q_LORA, qw)), _const_spec((N_HEAD_PAIRS, KV_LORA, kvw)),
                   _const_spec((1, Q_LORA)), _const_spec((1, KV_LORA))] + x_out,
        out_shape=[jax.ShapeDtypeStruct((b, s, Q_LORA), F32), jax.ShapeDtypeStruct((b, s, KV_LORA), F32),
                   jax.ShapeDtypeStruct((b, s, LANES), F32),
                   jax.ShapeDtypeStruct((N_HEAD_PAIRS, Q_LORA, qw), F32),
                   jax.ShapeDtypeStruct((N_HEAD_PAIRS, KV_LORA, kvw), F32),
                   jax.ShapeDtypeStruct((1, Q_LORA), F32), jax.ShapeDtypeStruct((1, KV_LORA), F32)] + x_shapes,
        input_output_aliases=x_alias,
        scratch_shapes=[pltpu.VMEM((s, HEAD_LANES), F32), pltpu.VMEM((s, HEAD_LANES), F32)] + x_scratch,
        compiler_params=_params(VMEM_BIG),
    )(dy, y, lse, ql, kvl, kr, cos_t, sa_t, sb_t, lw['q_norm_g'], lw['kv_norm_g'], lw['wq_pairs'], lw['wkv_pairs'],
      *x_args)


COL_CHUNK = 256


def _merge(g_rnn, gate_a, gate_b, h, y_mla):
    ge, t = _gelu(g_rnn)
    sa, sb = _sigmoid(gate_a), _sigmoid(gate_b)
    y_rnn = ge * h
    return ge, t, sa, sb, y_rnn, sa * y_rnn + sb * y_mla


def mixout_fwd(x, rnn4, h, y_mla, lw, l, tm):
    n = x.shape[0]

    def body(x_ref, gr_ref, gta_ref, gtb_ref, h_ref, y_ref, w_ref, g_ref, b_ref, z_ref, o_ref):
        z = ALPHA * x_ref[...]
        for c in range(0, D, COL_CHUNK):
            cs = slice(c, c + COL_CHUNK)
            merged = _merge(gr_ref[:, cs], gta_ref[:, cs], gtb_ref[:, cs], h_ref[:, cs], y_ref[:, cs])[-1]
            z = z + _mm(merged, w_ref[cs, :])
        z_ref[...] = z
        o_ref[...] = _ln_stats(z)[0] * g_ref[l:l + 1, :] + b_ref[l:l + 1, :]

    row = lambda i: (i, 0)
    col = lambda j: (lambda i: (i, j))
    blk = pl.BlockSpec((tm, D), row)
    return pl.pallas_call(
        body, name="mixout_fwd", grid=(n // tm,),
        in_specs=[blk, pl.BlockSpec((tm, D), col(1)), pl.BlockSpec((tm, D), col(2)), pl.BlockSpec((tm, D), col(3)),
                  blk, blk, _resident((D, D)), _const_spec((DEPTH, D)), _const_spec((DEPTH, D))],
        out_specs=[blk, blk],
        out_shape=[jax.ShapeDtypeStruct((n, D), F32), jax.ShapeDtypeStruct((n, D), F32)],
        compiler_params=_params(VMEM_BIG),
    )(x, rnn4, rnn4, rnn4, h, y_mla, lw['w_out'], lw['ln1_g'], lw['ln1_b'])


def mixout_bwd(dx1, z1, rnn4, h, y_mla, lw, l, tm):
    n = dx1.shape[0]

    def body(d_ref, z_ref, gr_ref, gta_ref, gtb_ref, h_ref, y_ref, w_ref, g_ref,
             dz_ref, dh_ref, dy_ref, dg3_ref, dw_ref, dg_ref, db_ref):
        @pl.when(pl.program_id(0) == 0)
        def _():
            for r in (dw_ref, dg_ref, db_ref):
                r[...] = jnp.zeros_like(r)

        dz, dg, db = _ln_bwd(d_ref[...], z_ref[...], g_ref[l:l + 1, :])
        dz_ref[...] = dz
        dg_ref[...] += dg
        db_ref[...] += db
        dzb = dz.astype(MXU_DTYPE)
        for c in range(0, D, COL_CHUNK):
            cs = slice(c, c + COL_CHUNK)
            g_rnn, h, y_mla = gr_ref[:, cs], h_ref[:, cs], y_ref[:, cs]
            ge, t, sa, sb, y_rnn, merged = _merge(g_rnn, gta_ref[:, cs], gtb_ref[:, cs], h, y_mla)
            dw_ref[cs, :] += _mm_tn(merged, dzb)
            dm = _mm_nt(dzb, w_ref[cs, :])
            dy_rnn = dm * sa
            dy_ref[:, cs] = dm * sb
            dh_ref[:, cs] = dy_rnn * ge
            dg3_ref[:, c:c + COL_CHUNK] = dy_rnn * h * _gelu_grad(g_rnn, t)
            dg3_ref[:, D + c:D + c + COL_CHUNK] = dm * y_rnn * sa * (1.0 - sa)
            dg3_ref[:, 2 * D + c:2 * D + c + COL_CHUNK] = dm * y_mla * sb * (1.0 - sb)

    row = lambda i: (i, 0)
    col = lambda j: (lambda i: (i, j))
    blk = pl.BlockSpec((tm, D), row)
    vec = jax.ShapeDtypeStruct((1, D), F32)
    act = jax.ShapeDtypeStruct((n, D), F32)
    return pl.pallas_call(
        body, name="mixout_bwd", grid=(n // tm,),
        in_specs=[blk, blk, pl.BlockSpec((tm, D), col(1)), pl.BlockSpec((tm, D), col(2)),
                  pl.BlockSpec((tm, D), col(3)), blk, blk, _resident((D, D)), _const_spec((DEPTH, D))],
        out_specs=[blk, blk, blk, pl.BlockSpec((tm, 3 * D), row), _const_spec((D, D)), _const_spec((1, D)),
                   _const_spec((1, D))],
        out_shape=[act, act, act, jax.ShapeDtypeStruct((n, 3 * D), F32), jax.ShapeDtypeStruct((D, D), F32), vec, vec],
        compiler_params=_params(VMEM_BIG),
    )(dx1, z1, rnn4, rnn4, rnn4, h, y_mla, lw['w_out'], lw['ln1_g'])


FFN_CHUNK = 512


def _conv3(u, halo, cs, l, fcw_ref, fcb_ref):
    hc = fcb_ref[l:l + 1, cs]
    for k in range(FFN_CONV_W):
        hc = hc + fcw_ref[k:k + 1, cs] * _shift_down(u, halo, FFN_CONV_W - 1 - k, 0)
    return hc


def ffn_fwd(x1, lw, l, b, s, ts):
    ns = s // ts
    n = b * s

    def body(x_ref, wu_ref, fcw_ref, fcb_ref, wd_ref, g_ref, b_ref, up_ref, z_ref, o_ref, halo_sc):
        @pl.when(pl.program_id(1) == 0)
        def _():
            halo_sc[...] = jnp.zeros_like(halo_sc)

        x = x_ref[...]
        xb = x.astype(MXU_DTYPE)
        z = ALPHA * x

        def up_chunk(c):
            return (_mm(xb, wu_ref[:, c:c + FFN_CHUNK]), _mm(xb, wu_ref[:, D_FF + c:D_FF + c + FFN_CHUNK]))

        nxt, act_prev = up_chunk(0), None
        for c in range(0, D_FF, FFN_CHUNK):
            gs, vs = slice(c, c + FFN_CHUNK), slice(D_FF + c, D_FF + c + FFN_CHUNK)
            ug, uv = nxt
            if c + FFN_CHUNK < D_FF:
                nxt = up_chunk(c + FFN_CHUNK)
            if act_prev is not None:
                z = z + _mm(act_prev, wd_ref[c - FFN_CHUNK:c, :])
            up_ref[:, gs] = ug
            up_ref[:, vs] = uv
            hg = _conv3(ug, halo_sc[:, gs], gs, l, fcw_ref, fcb_ref)
            hv = _conv3(uv, halo_sc[:, vs], vs, l, fcw_ref, fcb_ref)
            halo_sc[:, gs] = ug[ts - SUBLANES:, :]
            halo_sc[:, vs] = uv[ts - SUBLANES:, :]
            act_prev = (_gelu(hg)[0] * hv).astype(MXU_DTYPE)
        z = z + _mm(act_prev, wd_ref[D_FF - FFN_CHUNK:D_FF, :])
        z_ref[...] = z
        o_ref[...] = _ln_stats(z)[0] * g_ref[l:l + 1, :] + b_ref[l:l + 1, :]

    row = lambda bi, i: (bi * ns + i, 0)
    blk = pl.BlockSpec((ts, D), row)
    return pl.pallas_call(
        body, name="ffn_fwd", grid=(b, ns),
        in_specs=[blk, _resident((D, 2 * D_FF)), _const_spec((FFN_CONV_W, 2 * D_FF)), _const_spec((DEPTH, 2 * D_FF)),
                  _resident((D_FF, D)), _const_spec((DEPTH, D)), _const_spec((DEPTH, D))],
        out_specs=[pl.BlockSpec((ts, 2 * D_FF), row), blk, blk],
        out_shape=[jax.ShapeDtypeStruct((n, 2 * D_FF), F32), jax.ShapeDtypeStruct((n, D), F32),
                   jax.ShapeDtypeStruct((n, D), F32)],
        scratch_shapes=[pltpu.VMEM((SUBLANES, 2 * D_FF), F32)],
        compiler_params=_params(VMEM_BIG),
    )(x1, lw['w_up'], lw['ffn_conv_w'], lw['ffn_conv_b'], lw['w_down'], lw['ln2_g'], lw['ln2_b'])


def ffn_bwd(dx2, z2, up, lw, l, b, s, ts):
    ns = s // ts
    n = b * s
    hb = ts // SUBLANES

    def body(d_ref, z_ref, up_ref, uph_ref, fcw_ref, fcb_ref, wd_ref, g_ref,
             dz_ref, dup_ref, act_ref, dfcw_ref, dfcb_ref, dg_ref, db_ref, nhalo_sc):
        bi, i = pl.program_id(0), pl.program_id(1)

        @pl.when((bi == 0) & (i == 0))
        def _():
            for r in (dfcw_ref, dfcb_ref, dg_ref, db_ref):
                r[...] = jnp.zeros_like(r)

        @pl.when(i == 0)
        def _():
            nhalo_sc[...] = jnp.zeros_like(nhalo_sc)

        dz, dg, db = _ln_bwd(d_ref[...], z_ref[...], g_ref[l:l + 1, :])
        dz_ref[...] = dz
        dg_ref[...] += dg
        db_ref[...] += db
        dzb = dz.astype(MXU_DTYPE)
        keep = jnp.where(i == ns - 1, 0.0, 1.0)
        for c in range(0, D_FF, FFN_CHUNK):
            gs, vs = slice(c, c + FFN_CHUNK), slice(D_FF + c, D_FF + c + FFN_CHUNK)
            ug, uv = up_ref[:, gs], up_ref[:, vs]
            hg_halo, hv_halo = uph_ref[:, gs] * keep, uph_ref[:, vs] * keep
            hg = _conv3(ug, hg_halo, gs, l, fcw_ref, fcb_ref)
            hv = _conv3(uv, hv_halo, vs, l, fcw_ref, fcb_ref)
            ge, t = _gelu(hg)
            act_ref[:, gs] = (ge * hv).astype(MXU_DTYPE)
            dact = _mm_nt(dzb, wd_ref[c:c + FFN_CHUNK, :])
            for cs, u, halo, dhc in ((gs, ug, hg_halo, dact * hv * _gelu_grad(hg, t)), (vs, uv, hv_halo, dact * ge)):
                dfcb_ref[:, cs] += jnp.sum(dhc, axis=0, keepdims=True)
                nhalo = nhalo_sc[:, cs]
                dup = jnp.zeros_like(dhc)
                for k in range(FFN_CONV_W):
                    sft = FFN_CONV_W - 1 - k
                    dfcw_ref[k:k + 1, cs] += jnp.sum(dhc * _shift_down(u, halo, sft, 0), axis=0, keepdims=True)
                    dup = dup + fcw_ref[k:k + 1, cs] * _shift_up(dhc, nhalo, sft, 0)
                dup_ref[:, cs] = dup.astype(MXU_DTYPE)
                nhalo_sc[:, cs] = dhc[:SUBLANES, :]

    row = lambda bi, i: (bi * ns + (ns - 1 - i), 0)
    halo = lambda bi, i: (jnp.maximum((bi * ns + (ns - 1 - i)) * hb - 1, 0), 0)
    blk = pl.BlockSpec((ts, D), row)
    wide = pl.BlockSpec((ts, 2 * D_FF), row)
    return pl.pallas_call(
        body, name="ffn_bwd", grid=(b, ns),
        in_specs=[blk, blk, wide, pl.BlockSpec((SUBLANES, 2 * D_FF), halo),
                  _const_spec((FFN_CONV_W, 2 * D_FF)), _const_spec((DEPTH, 2 * D_FF)), _resident((D_FF, D)),
                  _const_spec((DEPTH, D))],
        out_specs=[blk, wide, pl.BlockSpec((ts, D_FF), row), _const_spec((FFN_CONV_W, 2 * D_FF)),
                   _const_spec((1, 2 * D_FF)), _const_spec((1, D)), _const_spec((1, D))],
        out_shape=[jax.ShapeDtypeStruct((n, D), F32), jax.ShapeDtypeStruct((n, 2 * D_FF), MXU_DTYPE),
                   jax.ShapeDtypeStruct((n, D_FF), MXU_DTYPE), jax.ShapeDtypeStruct((FFN_CONV_W, 2 * D_FF), F32),
                   jax.ShapeDtypeStruct((1, 2 * D_FF), F32), jax.ShapeDtypeStruct((1, D), F32),
                   jax.ShapeDtypeStruct((1, D), F32)],
        scratch_shapes=[pltpu.VMEM((SUBLANES, 2 * D_FF), F32)],
        compiler_params=_params(VMEM_BIG),
    )(dx2, z2, up, up, lw['ffn_conv_w'], lw['ffn_conv_b'], lw['w_down'], lw['ln2_g'])


def loss_head(y, target, tm):
    n = y.shape[0]

    def body(y_ref, t_ref, l_ref, d_ref):
        @pl.when(pl.program_id(0) == 0)
        def _():
            l_ref[...] = jnp.zeros_like(l_ref)

        err = y_ref[...] - t_ref[...]
        d_ref[...] = err * (1.0 / D)
        part = jnp.sum(jnp.sum(err * err, axis=-1, keepdims=True), axis=0, keepdims=True)
        l_ref[...] += jnp.broadcast_to(part * (0.5 / D), l_ref.shape)

    row = lambda i: (i, 0)
    return pl.pallas_call(
        body, name="loss_head", grid=(n // tm,),
        in_specs=[pl.BlockSpec((tm, D), row), pl.BlockSpec((tm, D), row)],
        out_specs=[_const_spec((1, LANES)), pl.BlockSpec((tm, D), row)],
        out_shape=[jax.ShapeDtypeStruct((1, LANES), F32), jax.ShapeDtypeStruct((n, D), F32)],
    )(y, target)


def _adam_update(g, w, m, v):
    c1 = 1.0 - ADAM_B1 ** ADAM_STEP
    c2 = 1.0 - ADAM_B2 ** ADAM_STEP
    mn = ADAM_B1 * m + (1.0 - ADAM_B1) * g
    vn = ADAM_B2 * v + (1.0 - ADAM_B2) * (g * g)
    return -ADAM_LR * ((mn / c1) / (jnp.sqrt(vn / c2) + ADAM_EPS) + ADAM_WD * w), mn, vn


def adamw_tiled(parts, w, m, v, name):
    _, r, c = w.shape
    tr = next(t for t in (256, 128, 64, 32, 16, 8) if r % t == 0)

    def body(p_ref, w_ref, m_ref, v_ref, g_ref, d_ref, mo_ref, vo_ref):
        g = p_ref[0].astype(F32)
        for i in range(1, N_DEV):
            g = g + p_ref[i].astype(F32)
        g_ref[...] = g
        d_ref[...], mo_ref[...], vo_ref[...] = _adam_update(g, w_ref[...], m_ref[...], v_ref[...])

    blk = pl.BlockSpec((None, tr, c), lambda l, i: (l, i, 0))
    out = jax.ShapeDtypeStruct(w.shape, F32)
    return pl.pallas_call(
        body, name="adamw_" + name, grid=(DEPTH, r // tr),
        in_specs=[pl.BlockSpec((N_DEV, None, tr, c), lambda l, i: (0, l, i, 0)), blk, blk, blk],
        out_specs=[blk, blk, blk, blk],
        out_shape=[out, out, out, out],
    )(parts, w, m, v)


def adamw_small(items):
    k = len(items)

    def body(*refs):
        ins, outs = refs[:4 * k], refs[4 * k:]
        for j in range(k):
            p_ref, w_ref, m_ref, v_ref = ins[4 * j:4 * j + 4]
            g_ref, d_ref, mo_ref, vo_ref = outs[j], outs[k + j], outs[2 * k + j], outs[3 * k + j]
            if len(p_ref.shape) == len(w_ref.shape) + 1:
                g = p_ref[0]
                for i in range(1, N_DEV):
                    g = g + p_ref[i]
                g_ref[...] = g
                d_ref[...], mo_ref[...], vo_ref[...] = _adam_update(g, w_ref[...], m_ref[...], v_ref[...])
            else:
                for l in range(DEPTH):
                    lr = slice(l, l + 1)
                    g = p_ref[0, l]
                    for i in range(1, N_DEV):
                        g = g + p_ref[i, l]
                    g_ref[lr, :] = g
                    d_ref[lr, :], mo_ref[lr, :], vo_ref[lr, :] = _adam_update(g, w_ref[lr, :], m_ref[lr, :],
                                                                              v_ref[lr, :])

    flat = [a for item in items for a in item]
    outs = [jax.ShapeDtypeStruct(item[1].shape, F32) for item in items] * 4
    return pl.pallas_call(
        body, name="adamw_small",
        in_specs=[pl.BlockSpec(memory_space=pltpu.VMEM)] * len(flat),
        out_specs=[pl.BlockSpec(memory_space=pltpu.VMEM)] * len(outs),
        out_shape=outs,
        compiler_params=_params(VMEM_BIG),
    )(*flat)


class Exchange:
    def __init__(self, entries):
        self.names = [e[0] for e in entries]
        self.srcs = [e[1] for e in entries]
        self.per_peer = [e[2] for e in entries]
        self.src_layer = [e[3] for e in entries]
        self.dst_layer = [e[4] for e in entries]
        self.bufs = [e[5] for e in entries]
        self.na = len(entries)

    def updated(self, bufdict, outs):
        new = dict(bufdict)
        new.update(zip(self.names, outs))
        return new

    def scratch(self):
        return [pltpu.SemaphoreType.DMA((self.na, N_DEV - 1)), pltpu.SemaphoreType.DMA((self.na, N_DEV - 1)),
                pltpu.SemaphoreType.DMA((self.na,))]

    def out_shapes(self):
        return [jax.ShapeDtypeStruct(bf.shape, bf.dtype) for bf in self.bufs]

    def copies(self, src_refs, buf_refs, send_sems, recv_sems, local_sems):
        x, y, c = lax.axis_index("x"), lax.axis_index("y"), lax.axis_index("c")
        me = 4 * x + 2 * y + c

        def view(a, pid):
            r = src_refs[a]
            if self.src_layer[a] is not None:
                r = r.at[self.src_layer[a]]
            return r.at[pid] if self.per_peer[a] else r

        out = [pltpu.make_async_copy(view(a, me), buf_refs[a].at[me, self.dst_layer[a]], local_sems.at[a])
               for a in range(self.na)]
        for k in range(1, N_DEV):
            px = 1 - x if k & 4 else x
            py = 1 - y if k & 2 else y
            pc = 1 - c if k & 1 else c
            pid = 4 * px + 2 * py + pc
            for a in range(self.na):
                out.append(pltpu.make_async_remote_copy(
                    src_ref=view(a, pid), dst_ref=buf_refs[a].at[me, self.dst_layer[a]],
                    send_sem=send_sems.at[a, k - 1], recv_sem=recv_sems.at[a, k - 1],
                    device_id=(px, py, pc), device_id_type=pl.DeviceIdType.MESH))
        return out


ANY_SPEC = pl.BlockSpec(memory_space=pl.ANY)


def exchange_layer(ex, name):
    na = ex.na

    def body(*refs):
        src_refs, buf_refs = refs[:na], refs[2 * na:3 * na]
        cps = ex.copies(src_refs, buf_refs, *refs[3 * na:])
        for cp in cps:
            cp.start()
        for cp in cps:
            cp.wait()

    return pl.pallas_call(
        body, name=name,
        in_specs=[ANY_SPEC] * (2 * na), out_specs=[ANY_SPEC] * na,
        out_shape=ex.out_shapes(),
        input_output_aliases={na + a: a for a in range(na)},
        scratch_shapes=ex.scratch(),
    )(*ex.srcs, *ex.bufs)


def _fuse_exchange(ex, n_in, n_out):
    if ex is None:
        return [], [], [], [], {}, []
    na = ex.na
    aliases = {n_in + na + a: n_out + a for a in range(na)}
    return ex.srcs + ex.bufs, [ANY_SPEC] * (2 * na), [ANY_SPEC] * na, ex.out_shapes(), aliases, ex.scratch()


def _run_exchange(ex, refs, n_in, n_out, n_scratch, first, last):
    def deco(compute):
        if ex is None:
            compute()
            return
        na = ex.na
        n_all_in = n_in + 2 * na
        src_refs = refs[n_in:n_in + na]
        buf_refs = refs[n_all_in + n_out:n_all_in + n_out + na]
        sems = refs[n_all_in + n_out + na + n_scratch:]

        @pl.when(first)
        def _():
            for cp in ex.copies(src_refs, buf_refs, *sems):
                cp.start()

        compute()

        @pl.when(last)
        def _():
            for cp in ex.copies(src_refs, buf_refs, *sems):
                cp.wait()
    return deco


def _permute_w_in(w):
    o = [0, D, 2 * D, 2 * D + Q_LORA, 2 * D + Q_LORA + KV_LORA, 2 * D + Q_LORA + KV_LORA + QK_ROPE,
         3 * D + Q_LORA + KV_LORA + QK_ROPE, IN_WIDTH]
    xr, gr, qlat, kvl, kr, ga, gb = [w[:, o[i]:o[i + 1]] for i in range(7)]
    z = lambda c: jnp.zeros((w.shape[0], c), w.dtype)
    return jnp.concatenate([xr, gr, ga, gb, qlat, kvl, z(QK_NOPE), kr, z(HEAD_LANES - QK_NOPE - QK_ROPE)], axis=1)


def _unpermute_dw_in(dw):
    o = 4 * D
    k0 = o + Q_LORA + KV_LORA + QK_NOPE
    return jnp.concatenate([dw[:, 0:2 * D], dw[:, o:o + Q_LORA + KV_LORA], dw[:, k0:k0 + QK_ROPE],
                            dw[:, 2 * D:4 * D]], axis=1)


def _pair_wq(w):
    w = w.reshape(Q_LORA, N_HEADS, QK_NOPE + QK_ROPE)
    w = jnp.pad(w, ((0, 0), (0, 0), (0, HEAD_LANES - QK_NOPE - QK_ROPE)))
    return w.reshape(Q_LORA, N_HEAD_PAIRS, HEADS_PER_STEP * HEAD_LANES).transpose(1, 0, 2)


def _unpair_dwq(dw):
    dw = dw.transpose(1, 0, 2).reshape(Q_LORA, N_HEADS, HEAD_LANES)
    return dw[:, :, :QK_NOPE + QK_ROPE].reshape(Q_LORA, N_HEADS * (QK_NOPE + QK_ROPE))


def _pair_wkv(w):
    w = w.reshape(KV_LORA, N_HEAD_PAIRS, HEADS_PER_STEP, QK_NOPE + V_HEAD)
    kn, vv = w[..., :QK_NOPE], w[..., QK_NOPE:]
    z = jnp.zeros_like(kn[:, :, 0])
    out = jnp.concatenate([kn[:, :, 0], z, kn[:, :, 1], z, vv[:, :, 0], z, z, vv[:, :, 1]], axis=-1)
    return out.transpose(1, 0, 2)


def _unpair_dwkv(dw):
    dw = dw.transpose(1, 0, 2)
    h0 = jnp.concatenate([dw[..., 0:64], dw[..., 256:320]], axis=-1)
    h1 = jnp.concatenate([dw[..., 128:192], dw[..., 448:512]], axis=-1)
    return jnp.stack([h0, h1], axis=2).reshape(KV_LORA, N_HEADS * (QK_NOPE + V_HEAD))


def _group_gates(w):
    per = MXU_GROUP // RNN_BLOCK
    w = w.reshape(DEPTH, N_GATE_GROUPS, per, RNN_BLOCK, RNN_BLOCK)
    eye = jnp.eye(per, dtype=w.dtype)
    return jnp.einsum('lgpij,pq->lgpiqj', w, eye).reshape(DEPTH, N_GATE_GROUPS, MXU_GROUP, MXU_GROUP)


def _ungroup_dgate(dw):
    per = MXU_GROUP // RNN_BLOCK
    dw = dw.reshape(N_GATE_GROUPS, per, RNN_BLOCK, per, RNN_BLOCK)
    return jnp.stack([dw[:, p, :, p, :] for p in range(per)], axis=1).reshape(N_RNN_BLOCKS, RNN_BLOCK, RNN_BLOCK)


def _rope_tables(positions):
    inv_freq = ROPE_THETA ** (-jnp.arange(0, QK_ROPE, 2, dtype=F32) / QK_ROPE)
    ang = positions.astype(F32)[..., None] * inv_freq
    cos, sin = jnp.cos(ang), jnp.sin(ang)
    one, zero = jnp.ones_like(cos), jnp.zeros_like(cos)
    nope = lambda v: jnp.concatenate([v] * (QK_NOPE // (QK_ROPE // 2)), axis=-1)
    tail = jnp.concatenate([zero, zero], axis=-1)
    cos_t = jnp.concatenate([nope(one), cos, cos, tail], axis=-1)
    sa_t = jnp.concatenate([nope(zero), -sin, zero, tail], axis=-1)
    sb_t = jnp.concatenate([nope(zero), zero, sin, tail], axis=-1)
    return cos_t, sa_t, sb_t


def _unshard(pieces, name):
    axis = SHARDED[name][0] - 1
    return jnp.concatenate([pieces[i] for i in range(N_DEV)], axis=axis)


def _shard_pieces(g, name):
    axis = SHARDED[name][0] - 1
    return jnp.stack(jnp.split(g, N_DEV, axis=axis))


MIXER_SHARDED = ('w_in', 'conv_w', 'w_uq', 'w_ukv', 'w_out')
FFN_SHARDED = ('w_up', 'ffn_conv_w', 'w_down')
FFN_WEIGHTS = ('w_up', 'ffn_conv_w', 'ffn_conv_b', 'w_down', 'ln2_g', 'ln2_b')


def _mixer_weights(gathered, l):
    full = {name: _unshard(gathered[name][:, l], name) for name in MIXER_SHARDED}
    return {'w_in_p': _permute_w_in(full['w_in']), 'conv_w': full['conv_w'], 'wq_pairs': _pair_wq(full['w_uq']),
            'wkv_pairs': _pair_wkv(full['w_ukv']), 'w_out': full['w_out']}


def _ffn_weights(gathered, l):
    return {name: _unshard(gathered[name][:, l], name) for name in FFN_SHARDED}


TM = 256
TS_RNN = 128
TS_FFN = 256
TQ_FWD = 256
TQ_BWD = 256
TN_DW = 512


def layer_fwd(xc, tabs, shared, gbufs, send, l, b, s):
    n = b * s
    lw = dict(shared, **_mixer_weights(gbufs, l))
    rnn4, ql, kvl, kr = inproj_fwd(xc, lw['w_in_p'], TM)
    h = rnn_fwd(rnn4.reshape(b, s, 4 * D), lw, l, b, s, TS_RNN)
    lat = (ql.reshape(b, s, Q_LORA), kvl.reshape(b, s, KV_LORA), kr.reshape(b, s, LANES))
    ex = Exchange(_gather_entries(FFN_SHARDED, l, send, gbufs)
                  + (_gather_entries(MIXER_SHARDED, l + 1, send, gbufs) if l + 1 < DEPTH else []))
    y_mla, lse, *xbufs = mla_fwd(*lat, tabs, lw, l, b, s, TQ_FWD, ex)
    gbufs = ex.updated(gbufs, xbufs)
    lw.update(_ffn_weights(gbufs, l))
    z1, x1 = mixout_fwd(xc, rnn4, h.reshape(n, D), y_mla.reshape(n, D), lw, l, TM)
    up, z2, x2 = ffn_fwd(x1, lw, l, b, s, TS_FFN)
    return x2, (xc, rnn4, lat, h, y_mla, lse, z1, x1, up, z2), lw, gbufs


def _gather_entries(group, l, send, gbufs):
    return [(k, send[k], False, l, l, gbufs[k]) for k in group]


def _scatter_entries(grads, l, pbufs):
    out = []
    for k, g in grads.items():
        g = _shard_pieces(g, k) if k in SHARDED else g
        out.append((k, g.astype(pbufs[k].dtype), k in SHARDED, None, l, pbufs[k]))
    return out


def layer_bwd(dx, saved, tabs, lw, l, b, s, pending, pbufs):
    n = b * s
    x0, rnn4, lat, h, y_mla, lse, z1, x1, up, z2 = saved
    dz2, dup, act, dfcw, dfcb, dg2, db2 = ffn_bwd(dx, z2, up, lw, l, b, s, TS_FFN)
    dx1 = matmul_dx(dup, lw['w_up'], dz2, ALPHA, TM, "ffn_up_dx")
    dw_up = matmul_dw(x1, dup, TN_DW, 2 * D_FF // 3, "ffn_up_dw")
    dw_down = matmul_dw(act, dz2, TN_DW, D // 2, "ffn_down_dw")
    ffn_grads = {'w_up': dw_up, 'ffn_conv_w': dfcw, 'ffn_conv_b': dfcb, 'w_down': dw_down, 'ln2_g': dg2, 'ln2_b': db2}
    dz1, dh, dy_mla, dg3, dw_out, dg1, db1 = mixout_bwd(dx1, z1, rnn4, h.reshape(n, D), y_mla.reshape(n, D),
                                                       lw, l, TM)
    ex = Exchange(pending + _scatter_entries(ffn_grads, l, pbufs))
    dql, dkvl, dkr, dwq, dwkv, dgq, dgkv, *xbufs = mla_bwd(dy_mla.reshape(b, s, D), y_mla, lse, *lat, tabs, lw, l,
                                                           b, s, TQ_BWD, ex)
    pbufs = ex.updated(pbufs, xbufs)
    dxr, dcw, dcb, dwgx, dbgx, dwga, dbga, dlam = rnn_bwd(dh.reshape(b, s, D), rnn4.reshape(b, s, 4 * D), h,
                                                          lw, l, b, s, TS_RNN)
    dx, dproj = inproj_bwd(dxr.reshape(n, D), dg3, dql.reshape(n, Q_LORA), dkvl.reshape(n, KV_LORA),
                           dkr.reshape(n, LANES), dz1, lw['w_in_p'], TM)
    dw_in_p = matmul_dw(x0, dproj, TN_DW, PROJ_W // 2, "inproj_dw")
    mixer_grads = {
        'w_in': _unpermute_dw_in(dw_in_p), 'conv_w': dcw, 'conv_b': dcb, 'gx_w': _ungroup_dgate(dwgx), 'gx_b': dbgx,
        'ga_w': _ungroup_dgate(dwga), 'ga_b': dbga, 'lru_lambda': dlam, 'q_norm_g': dgq, 'w_uq': _unpair_dwq(dwq),
        'kv_norm_g': dgkv, 'w_ukv': _unpair_dwkv(dwkv), 'w_out': dw_out, 'ln1_g': dg1, 'ln1_b': db1,
    }
    return dx, _scatter_entries(mixer_grads, l, pbufs), pbufs


SMALL_WEIGHT_ELEMS = 1 << 16


def kernel(x, positions, w_in, conv_w, conv_b, gx_w, gx_b, ga_w, ga_b, lru_lambda, q_norm_g, w_uq, kv_norm_g, w_ukv, w_out, ln1_g, ln1_b, w_up, ffn_conv_w, ffn_conv_b, w_down, ln2_g, ln2_b, loss_target, m_w_in, m_conv_w, m_conv_b, m_gx_w, m_gx_b, m_ga_w, m_ga_b, m_lru_lambda, m_q_norm_g, m_w_uq, m_kv_norm_g, m_w_ukv, m_w_out, m_ln1_g, m_ln1_b, m_w_up, m_ffn_conv_w, m_ffn_conv_b, m_w_down, m_ln2_g, m_ln2_b, v_w_in, v_conv_w, v_conv_b, v_gx_w, v_gx_b, v_ga_w, v_ga_b, v_lru_lambda, v_q_norm_g, v_w_uq, v_kv_norm_g, v_w_ukv, v_w_out, v_ln1_g, v_ln1_b, v_w_up, v_ffn_conv_w, v_ffn_conv_b, v_w_down, v_ln2_g, v_ln2_b):
    w = dict(zip(WEIGHT_NAMES, (w_in, conv_w, conv_b, gx_w, gx_b, ga_w, ga_b, lru_lambda, q_norm_g, w_uq, kv_norm_g,
                                w_ukv, w_out, ln1_g, ln1_b, w_up, ffn_conv_w, ffn_conv_b, w_down, ln2_g, ln2_b)))
    m = dict(zip(WEIGHT_NAMES, (m_w_in, m_conv_w, m_conv_b, m_gx_w, m_gx_b, m_ga_w, m_ga_b, m_lru_lambda,
                                m_q_norm_g, m_w_uq, m_kv_norm_g, m_w_ukv, m_w_out, m_ln1_g, m_ln1_b, m_w_up,
                                m_ffn_conv_w, m_ffn_conv_b, m_w_down, m_ln2_g, m_ln2_b)))
    v = dict(zip(WEIGHT_NAMES, (v_w_in, v_conv_w, v_conv_b, v_gx_w, v_gx_b, v_ga_w, v_ga_b, v_lru_lambda,
                                v_q_norm_g, v_w_uq, v_kv_norm_g, v_w_ukv, v_w_out, v_ln1_g, v_ln1_b, v_w_up,
                                v_ffn_conv_w, v_ffn_conv_b, v_w_down, v_ln2_g, v_ln2_b)))
    b, s, _ = x.shape
    n = b * s
    tabs = _rope_tables(positions)
    shared = {name: w[name] for name in WEIGHT_NAMES if name not in SHARDED and w[name].ndim == 2}
    shared['wgx'] = _group_gates(w['gx_w']).astype(MXU_DTYPE)
    shared['wga'] = _group_gates(w['ga_w']).astype(MXU_DTYPE)

    send = {k: w[k].astype(MXU_DTYPE) if k in MATMUL_WEIGHTS else w[k] for k in SHARDED}
    gbufs = {k: lax.empty((N_DEV,) + a.shape, a.dtype) for k, a in send.items()}
    head = Exchange(_gather_entries(MIXER_SHARDED, 0, send, gbufs))
    gbufs = head.updated(gbufs, exchange_layer(head, "gather_weights"))

    lws, saved = [], []
    xc = x.reshape(n, D)
    for l in range(DEPTH):
        xc, sv, lw, gbufs = layer_fwd(xc, tabs, shared, gbufs, send, l, b, s)
        lws.append(lw)
        saved.append(sv)
    loss_part, dx = loss_head(xc, loss_target.reshape(n, D), TM)

    def part_shape(k):
        piece = w[k].shape[1:]
        return (N_DEV, DEPTH) + ((1,) + piece if len(piece) == 1 else piece)

    pbufs = {k: lax.empty(part_shape(k), MXU_DTYPE if k in MATMUL_WEIGHTS else F32) for k in WEIGHT_NAMES}
    pending = []
    for l in reversed(range(DEPTH)):
        dx, pending, pbufs = layer_bwd(dx, saved[l], tabs, lws[l], l, b, s, pending, pbufs)
    tail = Exchange(pending)
    parts = tail.updated(pbufs, exchange_layer(tail, "scatter_grads"))

    out = {}
    small = [k for k in WEIGHT_NAMES if w[k].size <= SMALL_WEIGHT_ELEMS]
    res = adamw_small([(parts[k], w[k], m[k], v[k]) for k in small])
    for j, k in enumerate(small):
        out[k] = tuple(res[i * len(small) + j] for i in range(4))
    for k in WEIGHT_NAMES:
        if k in out:
            continue
        shape = w[k].shape
        view = (DEPTH, -1, shape[-1])
        r = adamw_tiled(parts[k].reshape((N_DEV,) + view), w[k].reshape(view), m[k].reshape(view),
                        v[k].reshape(view), k)
        out[k] = tuple(a.reshape(shape) for a in r)
    loss = lax.psum(loss_part[0, 0], ("x", "y", "c"))
    return (loss, dx.reshape(b, s, D), *[out[k][0] for k in WEIGHT_NAMES], *[out[k][1] for k in WEIGHT_NAMES],
            *[out[k][2] for k in WEIGHT_NAMES], *[out[k][3] for k in WEIGHT_NAMES])
```

```python
import functools
import math

import jax
import jax.numpy as jnp
from jax import lax
from jax.experimental import pallas as pl
from jax.experimental.pallas import tpu as pltpu

F32 = jnp.float32
MXU_DTYPE = jnp.bfloat16

D = 1024
DEPTH = 4
N_RNN_BLOCKS = 16
RNN_BLOCK = 64
CONV_W = 4
LRU_C = 8.0
N_HEADS = 16
QK_NOPE = 64
QK_ROPE = 32
V_HEAD = 64
Q_LORA = 384
KV_LORA = 256
ROPE_THETA = 10000.0
D_FF = 3 * D
FFN_CONV_W = 3
IN_WIDTH = 2 * D + Q_LORA + KV_LORA + QK_ROPE + 2 * D
ALPHA = (2 * DEPTH) ** 0.25
EPS = 1e-6
NEG_INF = -1e30
ATT_SCALE = (QK_NOPE + QK_ROPE) ** -0.5
GELU_C = math.sqrt(2.0 / math.pi)

ADAM_LR = 0.001
ADAM_B1 = 0.9
ADAM_B2 = 0.999
ADAM_EPS = 1e-08
ADAM_WD = 0.01
ADAM_STEP = 10

N_DEV = 8
LANES = 128
SUBLANES = 8
MXU_GROUP = 256
N_GATE_GROUPS = D // MXU_GROUP
HEADS_PER_STEP = 2
N_HEAD_PAIRS = N_HEADS // HEADS_PER_STEP
HEAD_LANES = 128
PROJ_W = 4 * D + Q_LORA + KV_LORA + LANES
VMEM_BIG = 56 * 2 ** 20

WEIGHT_NAMES = ['w_in', 'conv_w', 'conv_b', 'gx_w', 'gx_b', 'ga_w', 'ga_b', 'lru_lambda', 'q_norm_g', 'w_uq',
                'kv_norm_g', 'w_ukv', 'w_out', 'ln1_g', 'ln1_b', 'w_up', 'ffn_conv_w', 'ffn_conv_b', 'w_down',
                'ln2_g', 'ln2_b']
SHARDED = {
    'w_in': (2, (DEPTH, D, IN_WIDTH)),
    'conv_w': (2, (DEPTH, CONV_W, D)),
    'w_uq': (2, (DEPTH, Q_LORA, N_HEADS * (QK_NOPE + QK_ROPE))),
    'w_ukv': (2, (DEPTH, KV_LORA, N_HEADS * (QK_NOPE + V_HEAD))),
    'w_out': (1, (DEPTH, D, D)),
    'w_up': (2, (DEPTH, D, 2 * D_FF)),
    'ffn_conv_w': (2, (DEPTH, FFN_CONV_W, 2 * D_FF)),
    'w_down': (1, (DEPTH, D_FF, D)),
}
MATMUL_WEIGHTS = ('w_in', 'w_uq', 'w_ukv', 'w_out', 'w_up', 'w_down')


def _mm(a, b):
    return jnp.dot(a.astype(MXU_DTYPE), b.astype(MXU_DTYPE), preferred_element_type=F32)


def _mm_tn(a, b):
    return lax.dot_general(a.astype(MXU_DTYPE), b.astype(MXU_DTYPE), (((0,), (0,)), ((), ())),
                           preferred_element_type=F32)


def _mm_nt(a, b):
    return lax.dot_general(a.astype(MXU_DTYPE), b.astype(MXU_DTYPE), (((1,), (1,)), ((), ())),
                           preferred_element_type=F32)


def _sigmoid(x):
    return 1.0 / (1.0 + jnp.exp(-x))


def _gelu(x):
    t = jnp.tanh(GELU_C * (x + 0.044715 * (x * x * x)))
    return 0.5 * x * (1.0 + t), t


def _gelu_grad(x, t):
    return 0.5 * (1.0 + t) + 0.5 * x * (1.0 - t * t) * (GELU_C * (1.0 + 3.0 * 0.044715 * (x * x)))


def _neg_expm1(y):
    series = -y * (1.0 + 0.5 * y * (1.0 + (y / 3.0) * (1.0 + 0.25 * y * (1.0 + 0.2 * y))))
    return jnp.where(y > -0.05, series, 1.0 - jnp.exp(y))


def _ln_stats(z):
    mu = jnp.mean(z, axis=-1, keepdims=True)
    zc = z - mu
    var = jnp.mean(zc * zc, axis=-1, keepdims=True)
    r = lax.rsqrt(var + EPS)
    return zc * r, r


def _ln_bwd(dy, z, g):
    xhat, r = _ln_stats(z)
    dxh = dy * g
    dz = r * (dxh - jnp.mean(dxh, axis=-1, keepdims=True)
              - xhat * jnp.mean(dxh * xhat, axis=-1, keepdims=True))
    return dz, jnp.sum(dy * xhat, axis=0, keepdims=True), jnp.sum(dy, axis=0, keepdims=True)


def _rms_stats(x):
    r = lax.rsqrt(jnp.mean(x * x, axis=-1, keepdims=True) + EPS)
    return x * r, r


def _rms_bwd(dy, x, g):
    xn, r = _rms_stats(x)
    dxn = dy * g
    dx = r * (dxn - xn * jnp.mean(dxn * xn, axis=-1, keepdims=True))
    return dx, jnp.sum(dy * xn, axis=0, keepdims=True)


def _shift_down(x, halo, s, axis):
    if s == 0:
        return x
    r = pltpu.roll(x, s, axis)
    hr = pltpu.roll(halo, s, axis)
    idx = lax.broadcasted_iota(jnp.int32, hr.shape, axis)
    head = lax.slice_in_dim(r, 0, SUBLANES, axis=axis)
    rest = lax.slice_in_dim(r, SUBLANES, x.shape[axis], axis=axis)
    return jnp.concatenate([jnp.where(idx < s, hr, head), rest], axis=axis)


def _shift_up(x, halo, s, axis):
    if s == 0:
        return x
    n = x.shape[axis]
    r = pltpu.roll(x, n - s, axis)
    hr = pltpu.roll(halo, SUBLANES - s, axis)
    idx = lax.broadcasted_iota(jnp.int32, hr.shape, axis)
    body = lax.slice_in_dim(r, 0, n - SUBLANES, axis=axis)
    tail = lax.slice_in_dim(r, n - SUBLANES, n, axis=axis)
    return jnp.concatenate([body, jnp.where(idx >= SUBLANES - s, hr, tail)], axis=axis)


def _const_spec(shape):
    nd = len(shape)
    return pl.BlockSpec(shape, lambda *_: (0,) * nd)


def _layer_spec(shape, l):
    nd = len(shape)
    return pl.BlockSpec((None,) + tuple(shape), lambda *_: (l,) + (0,) * nd)


def _resident(shape):
    nd = len(shape)
    return pl.BlockSpec(shape, lambda *_: (0,) * nd, pipeline_mode=pl.Buffered(1))


def _params(vmem=None):
    return pltpu.CompilerParams(vmem_limit_bytes=vmem)


def inproj_fwd(x, w_in_p, tm):
    n = x.shape[0]

    def body(x_ref, w_ref, rnn4_ref, ql_ref, kvl_ref, kr_ref):
        xb = x_ref[...].astype(MXU_DTYPE)
        for j in range(4):
            rnn4_ref[:, j * D:(j + 1) * D] = _mm(xb, w_ref[:, j * D:(j + 1) * D])
        o = 4 * D
        ql_ref[...] = _mm(xb, w_ref[:, o:o + Q_LORA])
        kvl_ref[...] = _mm(xb, w_ref[:, o + Q_LORA:o + Q_LORA + KV_LORA])
        kr_ref[...] = _mm(xb, w_ref[:, o + Q_LORA + KV_LORA:PROJ_W])

    row = lambda i: (i, 0)
    return pl.pallas_call(
        body, name="inproj_fwd", grid=(n // tm,),
        in_specs=[pl.BlockSpec((tm, D), row), _resident((D, PROJ_W))],
        out_specs=[pl.BlockSpec((tm, 4 * D), row), pl.BlockSpec((tm, Q_LORA), row),
                   pl.BlockSpec((tm, KV_LORA), row), pl.BlockSpec((tm, LANES), row)],
        out_shape=[jax.ShapeDtypeStruct((n, 4 * D), F32), jax.ShapeDtypeStruct((n, Q_LORA), F32),
                   jax.ShapeDtypeStruct((n, KV_LORA), F32), jax.ShapeDtypeStruct((n, LANES), F32)],
        compiler_params=_params(VMEM_BIG),
    )(x, w_in_p)


def inproj_bwd(dxr, dg3, dql, dkvl, dkr, dz1, w_in_p, tm):
    n = dz1.shape[0]

    def body(dxr_ref, dg3_ref, dql_ref, dkvl_ref, dkr_ref, dz_ref, w_ref, dx_ref, dp_ref):
        dp = jnp.concatenate([dxr_ref[...], dg3_ref[...], dql_ref[...], dkvl_ref[...], dkr_ref[...]],
                             axis=1).astype(MXU_DTYPE)
        dp_ref[...] = dp
        dx_ref[...] = ALPHA * dz_ref[...] + _mm_nt(dp, w_ref[...])

    row = lambda i: (i, 0)
    return pl.pallas_call(
        body, name="inproj_bwd", grid=(n // tm,),
        in_specs=[pl.BlockSpec((tm, D), row), pl.BlockSpec((tm, 3 * D), row), pl.BlockSpec((tm, Q_LORA), row),
                  pl.BlockSpec((tm, KV_LORA), row), pl.BlockSpec((tm, LANES), row), pl.BlockSpec((tm, D), row),
                  _resident((D, PROJ_W))],
        out_specs=[pl.BlockSpec((tm, D), row), pl.BlockSpec((tm, PROJ_W), row)],
        out_shape=[jax.ShapeDtypeStruct((n, D), F32), jax.ShapeDtypeStruct((n, PROJ_W), MXU_DTYPE)],
        compiler_params=_params(VMEM_BIG),
    )(dxr, dg3, dql, dkvl, dkr, dz1, w_in_p)


def matmul_dw(x, dy, tn, tmc, name):
    n, k = x.shape
    m = dy.shape[1]

    def body(x_ref, dy_ref, dw_ref):
        @pl.when(pl.program_id(1) == 0)
        def _():
            dw_ref[...] = jnp.zeros_like(dw_ref)
        dw_ref[...] += _mm_tn(x_ref[...], dy_ref[...])

    return pl.pallas_call(
        body, name=name, grid=(m // tmc, n // tn),
        in_specs=[pl.BlockSpec((tn, k), lambda j, i: (i, 0)), pl.BlockSpec((tn, tmc), lambda j, i: (i, j))],
        out_specs=pl.BlockSpec((k, tmc), lambda j, i: (0, j)),
        out_shape=jax.ShapeDtypeStruct((k, m), F32),
        compiler_params=_params(VMEM_BIG),
    )(x, dy)


def matmul_dx(dy, w, add, add_scale, tm, name):
    n, m = dy.shape
    k = w.shape[0]

    def body(dy_ref, w_ref, add_ref, dx_ref):
        dx_ref[...] = add_scale * add_ref[...] + _mm_nt(dy_ref[...], w_ref[...])

    row = lambda i: (i, 0)
    return pl.pallas_call(
        body, name=name, grid=(n // tm,),
        in_specs=[pl.BlockSpec((tm, m), row), _resident((k, m)), pl.BlockSpec((tm, k), row)],
        out_specs=pl.BlockSpec((tm, k), row),
        out_shape=jax.ShapeDtypeStruct((n, k), F32),
        compiler_params=_params(VMEM_BIG),
    )(dy, w, add)


def _group(g):
    return slice(g * MXU_GROUP, (g + 1) * MXU_GROUP)


def _rnn_gates(x, halo, g, l, cw_ref, cb_ref, wgx_ref, bgx_ref, wga_ref, bga_ref, lam_ref):
    b, ts, gw = x.shape
    sl = _group(g)
    lr = slice(l, l + 1)
    xc = cb_ref[lr, sl][None]
    for k in range(CONV_W):
        xc = xc + cw_ref[k:k + 1, sl][None] * _shift_down(x, halo, CONV_W - 1 - k, 1)
    xc2 = xc.reshape(b * ts, gw)
    xcb = xc2.astype(MXU_DTYPE)
    gx = _sigmoid(_mm(xcb, wgx_ref[g]) + bgx_ref[lr, sl])
    ga = _sigmoid(_mm(xcb, wga_ref[g]) + bga_ref[lr, sl])
    nl = -lam_ref[lr, sl]
    sp = jnp.maximum(nl, 0.0) + jnp.log1p(jnp.exp(-jnp.abs(nl)))
    log_a = (-LRU_C) * ga * sp
    a = jnp.exp(log_a)
    mult = jnp.sqrt(_neg_expm1(2.0 * log_a))
    return xc2, xcb, gx, ga, sp, a, mult


def rnn_fwd(rnn4, lw, l, b, s, ts):
    ns = s // ts

    def body(x_ref, cw_ref, cb_ref, wgx_ref, bgx_ref, wga_ref, bga_ref, lam_ref, h_ref,
             halo_sc, hstate_sc, a_sc, u_sc):
        @pl.when(pl.program_id(0) == 0)
        def _():
            halo_sc[...] = jnp.zeros_like(halo_sc)
            hstate_sc[...] = jnp.zeros_like(hstate_sc)

        for g in range(N_GATE_GROUPS):
            sl = _group(g)
            x = x_ref[:, :, sl]
            xc2, _, gx, _, _, a, mult = _rnn_gates(x, halo_sc[:, :, sl], g, l, cw_ref, cb_ref, wgx_ref, bgx_ref,
                                                   wga_ref, bga_ref, lam_ref)
            halo_sc[:, :, sl] = x[:, ts - SUBLANES:, :]
            a_sc[:, :, sl] = a.reshape(b, ts, MXU_GROUP)
            u_sc[:, :, sl] = (mult * gx * xc2).reshape(b, ts, MXU_GROUP)

        def step(t, h):
            h = a_sc[:, pl.ds(t, 1), :] * h + u_sc[:, pl.ds(t, 1), :]
            h_ref[:, pl.ds(t, 1), :] = h
            return h

        hstate_sc[...] = lax.fori_loop(0, ts, step, hstate_sc[...], unroll=8)

    tile = lambda i: (0, i, 0)
    vecs = _const_spec((DEPTH, D))
    gates = _layer_spec((N_GATE_GROUPS, MXU_GROUP, MXU_GROUP), l)
    return pl.pallas_call(
        body, name="rnn_fwd", grid=(ns,),
        in_specs=[pl.BlockSpec((b, ts, D), tile), _const_spec((CONV_W, D)), vecs, gates, vecs, gates, vecs, vecs],
        out_specs=pl.BlockSpec((b, ts, D), tile),
        out_shape=jax.ShapeDtypeStruct((b, s, D), F32),
        scratch_shapes=[pltpu.VMEM((b, SUBLANES, D), F32), pltpu.VMEM((b, 1, D), F32),
                        pltpu.VMEM((b, ts, D), F32), pltpu.VMEM((b, ts, D), F32)],
        compiler_params=_params(VMEM_BIG),
    )(rnn4, lw['conv_w'], lw['conv_b'], lw['wgx'], lw['gx_b'], lw['wga'], lw['ga_b'], lw['lru_lambda'])


def rnn_bwd(dh, rnn4, h, lw, l, b, s, ts):
    ns = s // ts
    hb = ts // SUBLANES

    def body(dh_ref, x_ref, xh_ref, h_ref, hh_ref, cw_ref, cb_ref, wgx_ref, bgx_ref, wga_ref, bga_ref, lam_ref,
             dx_ref, dcw_ref, dcb_ref, dwgx_ref, dbgx_ref, dwga_ref, dbga_ref, dlam_ref,
             carry_sc, dxc_halo_sc, a_sc, delta_sc, xc_sc, gx_sc, ga_sc, mult_sc):
        i = pl.program_id(0)

        @pl.when(i == 0)
        def _():
            carry_sc[...] = jnp.zeros_like(carry_sc)
            dxc_halo_sc[...] = jnp.zeros_like(dxc_halo_sc)
            for r in (dcw_ref, dcb_ref, dwgx_ref, dbgx_ref, dwga_ref, dbga_ref, dlam_ref):
                r[...] = jnp.zeros_like(r)

        keep = jnp.where(i == ns - 1, 0.0, 1.0)
        for g in range(N_GATE_GROUPS):
            sl = _group(g)
            xc2, _, gx, ga, _, a, mult = _rnn_gates(x_ref[:, :, sl], xh_ref[:, :, sl] * keep, g, l, cw_ref, cb_ref,
                                                    wgx_ref, bgx_ref, wga_ref, bga_ref, lam_ref)
            for sc, val in ((a_sc, a), (xc_sc, xc2), (gx_sc, gx), (ga_sc, ga), (mult_sc, mult)):
                sc[:, :, sl] = val.reshape(b, ts, MXU_GROUP)

        def step(j, c):
            t = ts - 1 - j
            d = dh_ref[:, pl.ds(t, 1), :] + c
            delta_sc[:, pl.ds(t, 1), :] = d
            return a_sc[:, pl.ds(t, 1), :] * d

        carry_sc[...] = lax.fori_loop(0, ts, step, carry_sc[...], unroll=8)

        for g in range(N_GATE_GROUPS):
            sl = _group(g)
            x = x_ref[:, :, sl]
            flat = lambda sc, sl=sl: sc[:, :, sl].reshape(b * ts, MXU_GROUP)
            xc2, gx, ga, a, mult, delta = (flat(sc) for sc in (xc_sc, gx_sc, ga_sc, a_sc, mult_sc, delta_sc))
            xcb = xc2.astype(MXU_DTYPE)
            nl = -lam_ref[l:l + 1, sl]
            sp = jnp.maximum(nl, 0.0) + jnp.log1p(jnp.exp(-jnp.abs(nl)))
            hprev = _shift_down(h_ref[:, :, sl], hh_ref[:, :, sl] * keep, 1, 1).reshape(b * ts, MXU_GROUP)
            dmult = delta * gx * xc2
            dl = delta * hprev * a - dmult * (a * a) / mult
            dga = dl * ((-LRU_C) * sp)
            dlam_ref[:, sl] += (jnp.sum(dl * ((-LRU_C) * ga), axis=0, keepdims=True)
                                * (-_sigmoid(-lam_ref[l:l + 1, sl])))
            dpa = dga * ga * (1.0 - ga)
            dpx = (delta * mult * xc2) * gx * (1.0 - gx)
            dbga_ref[:, sl] += jnp.sum(dpa, axis=0, keepdims=True)
            dbgx_ref[:, sl] += jnp.sum(dpx, axis=0, keepdims=True)
            dpab = dpa.astype(MXU_DTYPE)
            dpxb = dpx.astype(MXU_DTYPE)
            dwga_ref[g] += _mm_tn(xcb, dpab)
            dwgx_ref[g] += _mm_tn(xcb, dpxb)
            dxc2 = delta * mult * gx + _mm_nt(dpab, wga_ref[g]) + _mm_nt(dpxb, wgx_ref[g])
            dcb_ref[:, sl] += jnp.sum(dxc2, axis=0, keepdims=True)
            dxc = dxc2.reshape(b, ts, MXU_GROUP)
            nhalo = dxc_halo_sc[:, :, sl]
            dx = None
            for k in range(CONV_W):
                later = _shift_up(dxc, nhalo, CONV_W - 1 - k, 1)
                dcw_ref[k:k + 1, sl] += jnp.sum((later * x).reshape(b * ts, MXU_GROUP), axis=0, keepdims=True)
                term = cw_ref[k:k + 1, sl][None] * later
                dx = term if dx is None else dx + term
            dx_ref[:, :, sl] = dx
            dxc_halo_sc[:, :, sl] = dxc[:, :SUBLANES, :]

    tile = lambda i: (0, ns - 1 - i, 0)
    halo = lambda i: (0, jnp.maximum((ns - 1 - i) * hb - 1, 0), 0)
    gshape = (N_GATE_GROUPS, MXU_GROUP, MXU_GROUP)
    vecs = _const_spec((DEPTH, D))
    gates = _layer_spec(gshape, l)
    vec = jax.ShapeDtypeStruct((1, D), F32)
    return pl.pallas_call(
        body, name="rnn_bwd", grid=(ns,),
        in_specs=[pl.BlockSpec((b, ts, D), tile), pl.BlockSpec((b, ts, D), tile),
                  pl.BlockSpec((b, SUBLANES, D), halo), pl.BlockSpec((b, ts, D), tile),
                  pl.BlockSpec((b, SUBLANES, D), halo),
                  _const_spec((CONV_W, D)), vecs, gates, vecs, gates, vecs, vecs],
        out_specs=[pl.BlockSpec((b, ts, D), tile), _const_spec((CONV_W, D)), _const_spec((1, D)),
                   _const_spec(gshape), _const_spec((1, D)), _const_spec(gshape), _const_spec((1, D)),
                   _const_spec((1, D))],
        out_shape=[jax.ShapeDtypeStruct((b, s, D), F32), jax.ShapeDtypeStruct((CONV_W, D), F32), vec,
                   jax.ShapeDtypeStruct(gshape, F32), vec, jax.ShapeDtypeStruct(gshape, F32), vec, vec],
        scratch_shapes=[pltpu.VMEM((b, 1, D), F32), pltpu.VMEM((b, SUBLANES, D), F32)]
        + [pltpu.VMEM((b, ts, D), F32)] * 6,
        compiler_params=_params(VMEM_BIG),
    )(dh, rnn4, rnn4, h, h, lw['conv_w'], lw['conv_b'], lw['wgx'], lw['gx_b'], lw['wga'], lw['ga_b'],
      lw['lru_lambda'])


def _rope(x, cos, sa, sb):
    return x * cos + pltpu.roll(x, HEAD_LANES - QK_ROPE // 2, 1) * sa + pltpu.roll(x, QK_ROPE // 2, 1) * sb


def _unrope(d, cos, sa, sb):
    return d * cos + pltpu.roll(d * sa, QK_ROPE // 2, 1) + pltpu.roll(d * sb, HEAD_LANES - QK_ROPE // 2, 1)


LOG2_E = 1.0 / math.log(2.0)
Q_PRESCALE = ATT_SCALE * LOG2_E


def _scores(q_blk, keys):
    return _mm_nt(q_blk, keys)


def _diag_scores(q_blk, keys):
    tq = q_blk.shape[0]
    keep = lax.broadcasted_iota(jnp.int32, (tq, tq), 0) >= lax.broadcasted_iota(jnp.int32, (tq, tq), 1)
    return jnp.where(keep, _scores(q_blk, keys), NEG_INF)


def _mla_project(ql_ref, kvl_ref, l, gq_ref, gkv_ref, wq_ref, wkv_ref):
    qn, _ = _rms_stats(ql_ref[0])
    qn = (qn * gq_ref[l:l + 1, :]).astype(MXU_DTYPE)
    kvn, _ = _rms_stats(kvl_ref[0])
    kvn = (kvn * gkv_ref[l:l + 1, :]).astype(MXU_DTYPE)
    return qn, kvn, _mm(qn, wq_ref[0]), _mm(kvn, wkv_ref[0])


def mla_fwd(ql, kvl, kr, tabs, lw, l, b, s, tq, ex=None):
    nq = s // tq
    cos_t, sa_t, sb_t = tabs
    n_in, n_out = 10, 2
    x_args, x_in, x_out, x_shapes, x_alias, x_scratch = _fuse_exchange(ex, n_in, n_out)

    def body(*refs):
        bi, p = pl.program_id(0), pl.program_id(1)
        first = (bi == 0) & (p == 0)
        last = (bi == b - 1) & (p == N_HEAD_PAIRS - 1)

        @_run_exchange(ex, refs, n_in, n_out, 0, first, last)
        def _():
            compute(*refs[:n_in], *refs[n_in + len(x_in):n_in + len(x_in) + n_out])

    def compute(ql_ref, kvl_ref, kr_ref, cos_ref, sa_ref, sb_ref, gq_ref, gkv_ref, wq_ref, wkv_ref, o_ref, lse_ref):
        _, _, qp, kvp = _mla_project(ql_ref, kvl_ref, l, gq_ref, gkv_ref, wq_ref, wkv_ref)
        cos, sa, sb = cos_ref[0], sa_ref[0], sb_ref[0]
        for hh in range(HEADS_PER_STEP):
            hs = slice(hh * HEAD_LANES, (hh + 1) * HEAD_LANES)
            q = (_rope(qp[:, hs], cos, sa, sb) * Q_PRESCALE).astype(MXU_DTYPE)
            k = _rope(kvp[:, hs] + kr_ref[0], cos, sa, sb).astype(MXU_DTYPE)
            v = kvp[:, HEADS_PER_STEP * HEAD_LANES + hh * HEAD_LANES:
                    HEADS_PER_STEP * HEAD_LANES + (hh + 1) * HEAD_LANES].astype(MXU_DTYPE)
            def block_scores(qb):
                lo, hi = qb * tq, (qb + 1) * tq
                return _diag_scores(q[lo:hi], k[lo:hi]), (_scores(q[lo:hi], k[:lo]) if qb else None)

            def block_softmax(qb, sd, sf):
                m = jnp.max(sd, axis=-1, keepdims=True)
                if qb:
                    m = jnp.maximum(m, jnp.max(sf, axis=-1, keepdims=True))
                ed = jnp.exp2(sd - m)
                den = jnp.sum(ed, axis=-1, keepdims=True)
                ef = None
                if qb:
                    ef = jnp.exp2(sf - m)
                    den = den + jnp.sum(ef, axis=-1, keepdims=True)
                return ed.astype(MXU_DTYPE), (ef.astype(MXU_DTYPE) if qb else None), den, m + jnp.log2(den)

            def block_out(qb, ed, ef, den, lse, hh=hh, v=v):
                lo, hi = qb * tq, (qb + 1) * tq
                o = _mm(ed, v[lo:hi])
                if qb:
                    o = o + _mm(ef, v[:lo])
                o = o * (1.0 / den)
                lse_ref[0, hh, lo:hi, :] = jnp.broadcast_to(lse, (tq, LANES))
                if hh == 0:
                    o_ref[0, lo:hi, :] = o
                else:
                    o_ref[0, lo:hi, :] += o

            nxt, prev = block_scores(0), None
            for qb in range(nq):
                sd, sf = nxt
                if qb + 1 < nq:
                    nxt = block_scores(qb + 1)
                if prev is not None:
                    block_out(qb - 1, *prev)
                prev = block_softmax(qb, sd, sf)
            block_out(nq - 1, *prev)

    seq = lambda bi, p: (bi, 0, 0)
    pair = lambda bi, p: (p, 0, 0)

    def per_seq(w):
        return pl.BlockSpec((1, s, w), seq, pipeline_mode=pl.Buffered(1))

    return pl.pallas_call(
        body, name="mla_fwd", grid=(b, N_HEAD_PAIRS),
        in_specs=[per_seq(Q_LORA), per_seq(KV_LORA), per_seq(LANES), per_seq(LANES), per_seq(LANES), per_seq(LANES),
                  _const_spec((DEPTH, Q_LORA)), _const_spec((DEPTH, KV_LORA)),
                  pl.BlockSpec((1, Q_LORA, HEADS_PER_STEP * HEAD_LANES), pair),
                  pl.BlockSpec((1, KV_LORA, 2 * HEADS_PER_STEP * HEAD_LANES), pair)] + x_in,
        out_specs=[pl.BlockSpec((1, s, LANES), lambda bi, p: (bi, 0, p)),
                   pl.BlockSpec((1, HEADS_PER_STEP, s, LANES), lambda bi, p: (bi, p, 0, 0))] + x_out,
        out_shape=[jax.ShapeDtypeStruct((b, s, D), F32), jax.ShapeDtypeStruct((b, N_HEADS, s, LANES), F32)] + x_shapes,
        input_output_aliases=x_alias, scratch_shapes=x_scratch,
        compiler_params=_params(VMEM_BIG),
    )(ql, kvl, kr, cos_t, sa_t, sb_t, lw['q_norm_g'], lw['kv_norm_g'], lw['wq_pairs'], lw['wkv_pairs'], *x_args)


def mla_bwd(dy, y, lse, ql, kvl, kr, tabs, lw, l, b, s, tq, ex=None):
    nq = s // tq
    cos_t, sa_t, sb_t = tabs
    qw = HEADS_PER_STEP * HEAD_LANES
    kvw = 2 * HEADS_PER_STEP * HEAD_LANES
    n_in, n_out, n_scratch = 13, 7, 2
    x_args, x_in, x_out, x_shapes, x_alias, x_scratch = _fuse_exchange(ex, n_in, n_out)

    def body(*refs):
        bi, p = pl.program_id(0), pl.program_id(1)
        first = (bi == 0) & (p == 0)
        last = (bi == b - 1) & (p == N_HEAD_PAIRS - 1)
        o0 = n_in + len(x_in)
        s0 = o0 + n_out + len(x_out)

        @_run_exchange(ex, refs, n_in, n_out, n_scratch, first, last)
        def _():
            compute(*refs[:n_in], *refs[o0:o0 + n_out], *refs[s0:s0 + n_scratch])

    def compute(dy_ref, y_ref, lse_ref, ql_ref, kvl_ref, kr_ref, cos_ref, sa_ref, sb_ref, gq_ref, gkv_ref, wq_ref,
                wkv_ref, dql_ref, dkvl_ref, dkr_ref, dwq_ref, dwkv_ref, dgq_ref, dgkv_ref, dk_sc, dv_sc):
        bi, p = pl.program_id(0), pl.program_id(1)

        @pl.when((bi == 0) & (p == 0))
        def _():
            for r in (dwq_ref, dwkv_ref, dgq_ref, dgkv_ref):
                r[...] = jnp.zeros_like(r)

        @pl.when(p == 0)
        def _():
            for r in (dql_ref, dkvl_ref, dkr_ref):
                r[...] = jnp.zeros_like(r)

        qn, kvn, qp, kvp = _mla_project(ql_ref, kvl_ref, l, gq_ref, gkv_ref, wq_ref, wkv_ref)
        cos, sa, sb = cos_ref[0], sa_ref[0], sb_ref[0]
        dof = dy_ref[0]
        do = dof.astype(MXU_DTYPE)
        prod = dof * y_ref[0]
        lane = lax.broadcasted_iota(jnp.int32, prod.shape, 1)
        dq_heads, dk_heads, dv_heads = [], [], []
        for hh in range(HEADS_PER_STEP):
            hs = slice(hh * HEAD_LANES, (hh + 1) * HEAD_LANES)
            q = (_rope(qp[:, hs], cos, sa, sb) * Q_PRESCALE).astype(MXU_DTYPE)
            k = _rope(kvp[:, hs] + kr_ref[0], cos, sa, sb).astype(MXU_DTYPE)
            v = kvp[:, qw + hh * HEAD_LANES:qw + (hh + 1) * HEAD_LANES].astype(MXU_DTYPE)
            mine = (lane >= hh * V_HEAD) & (lane < (hh + 1) * V_HEAD)
            delta = jnp.sum(jnp.where(mine, prod, 0.0), axis=-1, keepdims=True)
            dk_sc[...] = jnp.zeros_like(dk_sc)
            dv_sc[...] = jnp.zeros_like(dv_sc)
            units = []
            for qb in range(nq):
                units.append((qb, slice(qb * tq, (qb + 1) * tq), True))
                if qb:
                    units.append((qb, slice(0, qb * tq), False))

            def unit_matmuls_in(u, q=q, k=k, v=v):
                qb, ks, diag = u
                qs = slice(qb * tq, (qb + 1) * tq)
                sc = _diag_scores(q[qs], k[ks]) if diag else _scores(q[qs], k[ks])
                return sc, _mm_nt(do[qs], v[ks])

            def unit_elementwise(u, sc, dp, hh=hh, delta=delta):
                qs = slice(u[0] * tq, (u[0] + 1) * tq)
                pr = jnp.exp2(sc - lse_ref[0, hh, qs, 0:1])
                return pr.astype(MXU_DTYPE), (pr * (dp - delta[qs])).astype(MXU_DTYPE)

            dq_blocks = [None] * nq

            def unit_matmuls_out(u, prb, ds, q=q, k=k):
                qb, ks, _ = u
                qs = slice(qb * tq, (qb + 1) * tq)
                dv_sc[ks, :] += _mm_tn(prb, do[qs])
                part = _mm(ds, k[ks])
                dq_blocks[qb] = part if dq_blocks[qb] is None else dq_blocks[qb] + part
                dk_sc[ks, :] += _mm_tn(ds, q[qs])

            nxt, prev = unit_matmuls_in(units[0]), None
            for i, u in enumerate(units):
                sc, dp = nxt
                if i + 1 < len(units):
                    nxt = unit_matmuls_in(units[i + 1])
                if prev is not None:
                    unit_matmuls_out(units[i - 1], *prev)
                prev = unit_elementwise(u, sc, dp)
            unit_matmuls_out(units[-1], *prev)
            dq_heads.append(_unrope(jnp.concatenate(dq_blocks, axis=0) * ATT_SCALE, cos, sa, sb))
            dk_full = _unrope(dk_sc[...] * (1.0 / LOG2_E), cos, sa, sb)
            dkr_ref[0] += dk_full
            dk_heads.append(dk_full)
            dv_heads.append(dv_sc[...])
        dqp = jnp.concatenate(dq_heads, axis=1).astype(MXU_DTYPE)
        dkvp = jnp.concatenate(dk_heads + dv_heads, axis=1).astype(MXU_DTYPE)
        dwq_ref[p] += _mm_tn(qn, dqp)
        dwkv_ref[p] += _mm_tn(kvn, dkvp)
        dql_ref[0] += _mm_nt(dqp, wq_ref[0])
        dkvl_ref[0] += _mm_nt(dkvp, wkv_ref[0])

        @pl.when(p == N_HEAD_PAIRS - 1)
        def _():
            dx, dg = _rms_bwd(dql_ref[0], ql_ref[0], gq_ref[l:l + 1, :])
            dql_ref[0] = dx
            dgq_ref[...] += dg
            dx, dg = _rms_bwd(dkvl_ref[0], kvl_ref[0], gkv_ref[l:l + 1, :])
            dkvl_ref[0] = dx
            dgkv_ref[...] += dg

    seq = lambda bi, p: (bi, 0, 0)
    pair = lambda bi, p: (p, 0, 0)

    def per_seq(w):
        return pl.BlockSpec((1, s, w), seq, pipeline_mode=pl.Buffered(1))

    return pl.pallas_call(
        body, name="mla_bwd", grid=(b, N_HEAD_PAIRS),
        in_specs=[pl.BlockSpec((1, s, LANES), lambda bi, p: (bi, 0, p)),
                  pl.BlockSpec((1, s, LANES), lambda bi, p: (bi, 0, p)),
                  pl.BlockSpec((1, HEADS_PER_STEP, s, LANES), lambda bi, p: (bi, p, 0, 0)),
                  per_seq(Q_LORA), per_seq(KV_LORA), per_seq(LANES), per_seq(LANES), per_seq(LANES), per_seq(LANES),
                  _const_spec((DEPTH, Q_LORA)), _const_spec((DEPTH, KV_LORA)),
                  pl.BlockSpec((1, Q_LORA, qw), pair), pl.BlockSpec((1, KV_LORA, kvw), pair)] + x_in,
        out_specs=[pl.BlockSpec((1, s, Q_LORA), seq), pl.BlockSpec((1, s, KV_LORA), seq),
                   pl.BlockSpec((1, s, LANES), seq),
                   _const_spec((N_HEAD_PAIRS, Q_LORA, qw)), _const_spec((N_HEAD_PAIRS, KV_LORA, kvw)),
                   _const_spec((1, Q_LORA)), _const_spec((1, KV_LORA))] + x_out,
        out_shape=[jax.ShapeDtypeStruct((b, s, Q_LORA), F32), jax.ShapeDtypeStruct((b, s, KV_LORA), F32),
                   jax.ShapeDtypeStruct((b, s, LANES), F32),
                   jax.ShapeDtypeStruct((N_HEAD_PAIRS, Q_LORA, qw), F32),
                   jax.ShapeDtypeStruct((N_HEAD_PAIRS, KV_LORA, kvw), F32),
                   jax.ShapeDtypeStruct((1, Q_LORA), F32), jax.ShapeDtypeStruct((1, KV_LORA), F32)] + x_shapes,
        input_output_aliases=x_alias,
        scratch_shapes=[pltpu.VMEM((s, HEAD_LANES), F32), pltpu.VMEM((s, HEAD_LANES), F32)] + x_scratch,
        compiler_params=_params(VMEM_BIG),
    )(dy, y, lse, ql, kvl, kr, cos_t, sa_t, sb_t, lw['q_norm_g'], lw['kv_norm_g'], lw['wq_pairs'], lw['wkv_pairs'],
      *x_args)


COL_CHUNK = 256


def _merge(g_rnn, gate_a, gate_b, h, y_mla):
    ge, t = _gelu(g_rnn)
    sa, sb = _sigmoid(gate_a), _sigmoid(gate_b)
    y_rnn = ge * h
    return ge, t, sa, sb, y_rnn, sa * y_rnn + sb * y_mla


def mixout_fwd(x, rnn4, h, y_mla, lw, l, tm):
    n = x.shape[0]

    def body(x_ref, gr_ref, gta_ref, gtb_ref, h_ref, y_ref, w_ref, g_ref, b_ref, z_ref, o_ref):
        z = ALPHA * x_ref[...]
        for c in range(0, D, COL_CHUNK):
            cs = slice(c, c + COL_CHUNK)
            merged = _merge(gr_ref[:, cs], gta_ref[:, cs], gtb_ref[:, cs], h_ref[:, cs], y_ref[:, cs])[-1]
            z = z + _mm(merged, w_ref[cs, :])
        z_ref[...] = z
        o_ref[...] = _ln_stats(z)[0] * g_ref[l:l + 1, :] + b_ref[l:l + 1, :]

    row = lambda i: (i, 0)
    col = lambda j: (lambda i: (i, j))
    blk = pl.BlockSpec((tm, D), row)
    return pl.pallas_call(
        body, name="mixout_fwd", grid=(n // tm,),
        in_specs=[blk, pl.BlockSpec((tm, D), col(1)), pl.BlockSpec((tm, D), col(2)), pl.BlockSpec((tm, D), col(3)),
                  blk, blk, _resident((D, D)), _const_spec((DEPTH, D)), _const_spec((DEPTH, D))],
        out_specs=[blk, blk],
        out_shape=[jax.ShapeDtypeStruct((n, D), F32), jax.ShapeDtypeStruct((n, D), F32)],
        compiler_params=_params(VMEM_BIG),
    )(x, rnn4, rnn4, rnn4, h, y_mla, lw['w_out'], lw['ln1_g'], lw['ln1_b'])


def mixout_bwd(dx1, z1, rnn4, h, y_mla, lw, l, tm):
    n = dx1.shape[0]

    def body(d_ref, z_ref, gr_ref, gta_ref, gtb_ref, h_ref, y_ref, w_ref, g_ref,
             dz_ref, dh_ref, dy_ref, dg3_ref, dw_ref, dg_ref, db_ref):
        @pl.when(pl.program_id(0) == 0)
        def _():
            for r in (dw_ref, dg_ref, db_ref):
                r[...] = jnp.zeros_like(r)

        dz, dg, db = _ln_bwd(d_ref[...], z_ref[...], g_ref[l:l + 1, :])
        dz_ref[...] = dz
        dg_ref[...] += dg
        db_ref[...] += db
        dzb = dz.astype(MXU_DTYPE)
        for c in range(0, D, COL_CHUNK):
            cs = slice(c, c + COL_CHUNK)
            g_rnn, h, y_mla = gr_ref[:, cs], h_ref[:, cs], y_ref[:, cs]
            ge, t, sa, sb, y_rnn, merged = _merge(g_rnn, gta_ref[:, cs], gtb_ref[:, cs], h, y_mla)
            dw_ref[cs, :] += _mm_tn(merged, dzb)
            dm = _mm_nt(dzb, w_ref[cs, :])
            dy_rnn = dm * sa
            dy_ref[:, cs] = dm * sb
            dh_ref[:, cs] = dy_rnn * ge
            dg3_ref[:, c:c + COL_CHUNK] = dy_rnn * h * _gelu_grad(g_rnn, t)
            dg3_ref[:, D + c:D + c + COL_CHUNK] = dm * y_rnn * sa * (1.0 - sa)
            dg3_ref[:, 2 * D + c:2 * D + c + COL_CHUNK] = dm * y_mla * sb * (1.0 - sb)

    row = lambda i: (i, 0)
    col = lambda j: (lambda i: (i, j))
    blk = pl.BlockSpec((tm, D), row)
    vec = jax.ShapeDtypeStruct((1, D), F32)
    act = jax.ShapeDtypeStruct((n, D), F32)
    return pl.pallas_call(
        body, name="mixout_bwd", grid=(n // tm,),
        in_specs=[blk, blk, pl.BlockSpec((tm, D), col(1)), pl.BlockSpec((tm, D), col(2)),
                  pl.BlockSpec((tm, D), col(3)), blk, blk, _resident((D, D)), _const_spec((DEPTH, D))],
        out_specs=[blk, blk, blk, pl.BlockSpec((tm, 3 * D), row), _const_spec((D, D)), _const_spec((1, D)),
                   _const_spec((1, D))],
        out_shape=[act, act, act, jax.ShapeDtypeStruct((n, 3 * D), F32), jax.ShapeDtypeStruct((D, D), F32), vec, vec],
        compiler_params=_params(VMEM_BIG),
    )(dx1, z1, rnn4, rnn4, rnn4, h, y_mla, lw['w_out'], lw['ln1_g'])


FFN_CHUNK = 512


def _conv3(u, halo, cs, l, fcw_ref, fcb_ref):
    hc = fcb_ref[l:l + 1, cs]
    for k in range(FFN_CONV_W):
        hc = hc + fcw_ref[k:k + 1, cs] * _shift_down(u, halo, FFN_CONV_W - 1 - k, 0)
    return hc


def _conv3_from(sc, cs, l, ts, fcw_ref, fcb_ref):
    hc = fcb_ref[l:l + 1, cs]
    for k in range(FFN_CONV_W):
        o = SUBLANES - (FFN_CONV_W - 1 - k)
        hc = hc + fcw_ref[k:k + 1, cs] * sc[o:o + ts, cs]
    return hc


def ffn_fwd(x1, lw, l, b, s, ts):
    ns = s // ts
    n = b * s

    def body(x_ref, wu_ref, fcw_ref, fcb_ref, wd_ref, g_ref, b_ref, up_ref, z_ref, o_ref, halo_sc):
        @pl.when(pl.program_id(1) == 0)
        def _():
            halo_sc[0:SUBLANES, :] = jnp.zeros((SUBLANES, 2 * D_FF), F32)

        x = x_ref[...]
        xb = x.astype(MXU_DTYPE)
        z = ALPHA * x

        def up_chunk(c):
            return (_mm(xb, wu_ref[:, c:c + FFN_CHUNK]), _mm(xb, wu_ref[:, D_FF + c:D_FF + c + FFN_CHUNK]))

        nxt, act_prev = up_chunk(0), None
        for c in range(0, D_FF, FFN_CHUNK):
            gs, vs = slice(c, c + FFN_CHUNK), slice(D_FF + c, D_FF + c + FFN_CHUNK)
            ug, uv = nxt
            if c + FFN_CHUNK < D_FF:
                nxt = up_chunk(c + FFN_CHUNK)
            if act_prev is not None:
                z = z + _mm(act_prev, wd_ref[c - FFN_CHUNK:c, :])
            up_ref[:, gs] = ug
            up_ref[:, vs] = uv
            halo_sc[SUBLANES:, gs] = ug
            halo_sc[SUBLANES:, vs] = uv
            hg, hv = (_conv3_from(halo_sc, cs, l, ts, fcw_ref, fcb_ref) for cs in (gs, vs))
            halo_sc[0:SUBLANES, gs] = ug[ts - SUBLANES:, :]
            halo_sc[0:SUBLANES, vs] = uv[ts - SUBLANES:, :]
            act_prev = (_gelu(hg)[0] * hv).astype(MXU_DTYPE)
        z = z + _mm(act_prev, wd_ref[D_FF - FFN_CHUNK:D_FF, :])
        z_ref[...] = z
        o_ref[...] = _ln_stats(z)[0] * g_ref[l:l + 1, :] + b_ref[l:l + 1, :]

    row = lambda bi, i: (bi * ns + i, 0)
    blk = pl.BlockSpec((ts, D), row)
    return pl.pallas_call(
        body, name="ffn_fwd", grid=(b, ns),
        in_specs=[blk, _resident((D, 2 * D_FF)), _const_spec((FFN_CONV_W, 2 * D_FF)), _const_spec((DEPTH, 2 * D_FF)),
                  _resident((D_FF, D)), _const_spec((DEPTH, D)), _const_spec((DEPTH, D))],
        out_specs=[pl.BlockSpec((ts, 2 * D_FF), row), blk, blk],
        out_shape=[jax.ShapeDtypeStruct((n, 2 * D_FF), F32), jax.ShapeDtypeStruct((n, D), F32),
                   jax.ShapeDtypeStruct((n, D), F32)],
        scratch_shapes=[pltpu.VMEM((SUBLANES + ts, 2 * D_FF), F32)],
        compiler_params=_params(VMEM_BIG),
    )(x1, lw['w_up'], lw['ffn_conv_w'], lw['ffn_conv_b'], lw['w_down'], lw['ln2_g'], lw['ln2_b'])


def ffn_bwd(dx2, z2, up, lw, l, b, s, ts):
    ns = s // ts
    n = b * s
    hb = ts // SUBLANES

    def body(d_ref, z_ref, up_ref, uph_ref, fcw_ref, fcb_ref, wd_ref, g_ref,
             dz_ref, dup_ref, act_ref, dfcw_ref, dfcb_ref, dg_ref, db_ref, nhalo_sc):
        bi, i = pl.program_id(0), pl.program_id(1)

        @pl.when((bi == 0) & (i == 0))
        def _():
            for r in (dfcw_ref, dfcb_ref, dg_ref, db_ref):
                r[...] = jnp.zeros_like(r)

        @pl.when(i == 0)
        def _():
            nhalo_sc[...] = jnp.zeros_like(nhalo_sc)

        dz, dg, db = _ln_bwd(d_ref[...], z_ref[...], g_ref[l:l + 1, :])
        dz_ref[...] = dz
        dg_ref[...] += dg
        db_ref[...] += db
        dzb = dz.astype(MXU_DTYPE)
        keep = jnp.where(i == ns - 1, 0.0, 1.0)
        for c in range(0, D_FF, FFN_CHUNK):
            gs, vs = slice(c, c + FFN_CHUNK), slice(D_FF + c, D_FF + c + FFN_CHUNK)
            ug, uv = up_ref[:, gs], up_ref[:, vs]
            hg_halo, hv_halo = uph_ref[:, gs] * keep, uph_ref[:, vs] * keep
            hg = _conv3(ug, hg_halo, gs, l, fcw_ref, fcb_ref)
            hv = _conv3(uv, hv_halo, vs, l, fcw_ref, fcb_ref)
            ge, t = _gelu(hg)
            act_ref[:, gs] = (ge * hv).astype(MXU_DTYPE)
            dact = _mm_nt(dzb, wd_ref[c:c + FFN_CHUNK, :])
            for cs, u, halo, dhc in ((gs, ug, hg_halo, dact * hv * _gelu_grad(hg, t)), (vs, uv, hv_halo, dact * ge)):
                dfcb_ref[:, cs] += jnp.sum(dhc, axis=0, keepdims=True)
                nhalo = nhalo_sc[:, cs]
                dup = jnp.zeros_like(dhc)
                for k in range(FFN_CONV_W):
                    sft = FFN_CONV_W - 1 - k
                    dfcw_ref[k:k + 1, cs] += jnp.sum(dhc * _shift_down(u, halo, sft, 0), axis=0, keepdims=True)
                    dup = dup + fcw_ref[k:k + 1, cs] * _shift_up(dhc, nhalo, sft, 0)
                dup_ref[:, cs] = dup.astype(MXU_DTYPE)
                nhalo_sc[:, cs] = dhc[:SUBLANES, :]

    row = lambda bi, i: (bi * ns + (ns - 1 - i), 0)
    halo = lambda bi, i: (jnp.maximum((bi * ns + (ns - 1 - i)) * hb - 1, 0), 0)
    blk = pl.BlockSpec((ts, D), row)
    wide = pl.BlockSpec((ts, 2 * D_FF), row)
    return pl.pallas_call(
        body, name="ffn_bwd", grid=(b, ns),
        in_specs=[blk, blk, wide, pl.BlockSpec((SUBLANES, 2 * D_FF), halo),
                  _const_spec((FFN_CONV_W, 2 * D_FF)), _const_spec((DEPTH, 2 * D_FF)), _resident((D_FF, D)),
                  _const_spec((DEPTH, D))],
        out_specs=[blk, wide, pl.BlockSpec((ts, D_FF), row), _const_spec((FFN_CONV_W, 2 * D_FF)),
                   _const_spec((1, 2 * D_FF)), _const_spec((1, D)), _const_spec((1, D))],
        out_shape=[jax.ShapeDtypeStruct((n, D), F32), jax.ShapeDtypeStruct((n, 2 * D_FF), MXU_DTYPE),
                   jax.ShapeDtypeStruct((n, D_FF), MXU_DTYPE), jax.ShapeDtypeStruct((FFN_CONV_W, 2 * D_FF), F32),
                   jax.ShapeDtypeStruct((1, 2 * D_FF), F32), jax.ShapeDtypeStruct((1, D), F32),
                   jax.ShapeDtypeStruct((1, D), F32)],
        scratch_shapes=[pltpu.VMEM((SUBLANES, 2 * D_FF), F32)],
        compiler_params=_params(VMEM_BIG),
    )(dx2, z2, up, up, lw['ffn_conv_w'], lw['ffn_conv_b'], lw['w_down'], lw['ln2_g'])


def loss_head(y, target, tm):
    n = y.shape[0]

    def body(y_ref, t_ref, l_ref, d_ref):
        @pl.when(pl.program_id(0) == 0)
        def _():
            l_ref[...] = jnp.zeros_like(l_ref)

        err = y_ref[...] - t_ref[...]
        d_ref[...] = err * (1.0 / D)
        part = jnp.sum(jnp.sum(err * err, axis=-1, keepdims=True), axis=0, keepdims=True)
        l_ref[...] += jnp.broadcast_to(part * (0.5 / D), l_ref.shape)

    row = lambda i: (i, 0)
    return pl.pallas_call(
        body, name="loss_head", grid=(n // tm,),
        in_specs=[pl.BlockSpec((tm, D), row), pl.BlockSpec((tm, D), row)],
        out_specs=[_const_spec((1, LANES)), pl.BlockSpec((tm, D), row)],
        out_shape=[jax.ShapeDtypeStruct((1, LANES), F32), jax.ShapeDtypeStruct((n, D), F32)],
    )(y, target)


def _adam_update(g, w, m, v):
    c1 = 1.0 - ADAM_B1 ** ADAM_STEP
    c2 = 1.0 - ADAM_B2 ** ADAM_STEP
    mn = ADAM_B1 * m + (1.0 - ADAM_B1) * g
    vn = ADAM_B2 * v + (1.0 - ADAM_B2) * (g * g)
    return -ADAM_LR * ((mn / c1) / (jnp.sqrt(vn / c2) + ADAM_EPS) + ADAM_WD * w), mn, vn


def adamw_tiled(parts, w, m, v, name):
    _, r, c = w.shape
    tr = next(t for t in (256, 128, 64, 32, 16, 8) if r % t == 0)

    def body(p_ref, w_ref, m_ref, v_ref, g_ref, d_ref, mo_ref, vo_ref):
        g = p_ref[0].astype(F32)
        for i in range(1, N_DEV):
            g = g + p_ref[i].astype(F32)
        g_ref[...] = g
        d_ref[...], mo_ref[...], vo_ref[...] = _adam_update(g, w_ref[...], m_ref[...], v_ref[...])

    blk = pl.BlockSpec((None, tr, c), lambda l, i: (l, i, 0))
    out = jax.ShapeDtypeStruct(w.shape, F32)
    return pl.pallas_call(
        body, name="adamw_" + name, grid=(DEPTH, r // tr),
        in_specs=[pl.BlockSpec((N_DEV, None, tr, c), lambda l, i: (0, l, i, 0)), blk, blk, blk],
        out_specs=[blk, blk, blk, blk],
        out_shape=[out, out, out, out],
    )(parts, w, m, v)


def adamw_small(items):
    k = len(items)

    def body(*refs):
        ins, outs = refs[:4 * k], refs[4 * k:]
        for j in range(k):
            p_ref, w_ref, m_ref, v_ref = ins[4 * j:4 * j + 4]
            g_ref, d_ref, mo_ref, vo_ref = outs[j], outs[k + j], outs[2 * k + j], outs[3 * k + j]
            if len(p_ref.shape) == len(w_ref.shape) + 1:
                g = p_ref[0]
                for i in range(1, N_DEV):
                    g = g + p_ref[i]
                g_ref[...] = g
                d_ref[...], mo_ref[...], vo_ref[...] = _adam_update(g, w_ref[...], m_ref[...], v_ref[...])
            else:
                for l in range(DEPTH):
                    lr = slice(l, l + 1)
                    g = p_ref[0, l]
                    for i in range(1, N_DEV):
                        g = g + p_ref[i, l]
                    g_ref[lr, :] = g
                    d_ref[lr, :], mo_ref[lr, :], vo_ref[lr, :] = _adam_update(g, w_ref[lr, :], m_ref[lr, :],
                                                                              v_ref[lr, :])

    flat = [a for item in items for a in item]
    outs = [jax.ShapeDtypeStruct(item[1].shape, F32) for item in items] * 4
    return pl.pallas_call(
        body, name="adamw_small",
        in_specs=[pl.BlockSpec(memory_space=pltpu.VMEM)] * len(flat),
        out_specs=[pl.BlockSpec(memory_space=pltpu.VMEM)] * len(outs),
        out_shape=outs,
        compiler_params=_params(VMEM_BIG),
    )(*flat)


class Exchange:
    def __init__(self, entries):
        self.names = [e[0] for e in entries]
        self.srcs = [e[1] for e in entries]
        self.per_peer = [e[2] for e in entries]
        self.src_layer = [e[3] for e in entries]
        self.dst_layer = [e[4] for e in entries]
        self.bufs = [e[5] for e in entries]
        self.na = len(entries)

    def updated(self, bufdict, outs):
        new = dict(bufdict)
        new.update(zip(self.names, outs))
        return new

    def scratch(self):
        return [pltpu.SemaphoreType.DMA((self.na, N_DEV - 1)), pltpu.SemaphoreType.DMA((self.na, N_DEV - 1)),
                pltpu.SemaphoreType.DMA((self.na,))]

    def out_shapes(self):
        return [jax.ShapeDtypeStruct(bf.shape, bf.dtype) for bf in self.bufs]

    def copies(self, src_refs, buf_refs, send_sems, recv_sems, local_sems):
        x, y, c = lax.axis_index("x"), lax.axis_index("y"), lax.axis_index("c")
        me = 4 * x + 2 * y + c

        def view(a, pid):
            r = src_refs[a]
            if self.src_layer[a] is not None:
                r = r.at[self.src_layer[a]]
            return r.at[pid] if self.per_peer[a] else r

        out = [pltpu.make_async_copy(view(a, me), buf_refs[a].at[me, self.dst_layer[a]], local_sems.at[a])
               for a in range(self.na)]
        for k in range(1, N_DEV):
            px = 1 - x if k & 4 else x
            py = 1 - y if k & 2 else y
            pc = 1 - c if k & 1 else c
            pid = 4 * px + 2 * py + pc
            for a in range(self.na):
                out.append(pltpu.make_async_remote_copy(
                    src_ref=view(a, pid), dst_ref=buf_refs[a].at[me, self.dst_layer[a]],
                    send_sem=send_sems.at[a, k - 1], recv_sem=recv_sems.at[a, k - 1],
                    device_id=(px, py, pc), device_id_type=pl.DeviceIdType.MESH))
        return out


ANY_SPEC = pl.BlockSpec(memory_space=pl.ANY)


def exchange_layer(ex, name):
    na = ex.na

    def body(*refs):
        src_refs, buf_refs = refs[:na], refs[2 * na:3 * na]
        cps = ex.copies(src_refs, buf_refs, *refs[3 * na:])
        for cp in cps:
            cp.start()
        for cp in cps:
            cp.wait()

    return pl.pallas_call(
        body, name=name,
        in_specs=[ANY_SPEC] * (2 * na), out_specs=[ANY_SPEC] * na,
        out_shape=ex.out_shapes(),
        input_output_aliases={na + a: a for a in range(na)},
        scratch_shapes=ex.scratch(),
    )(*ex.srcs, *ex.bufs)


def _fuse_exchange(ex, n_in, n_out):
    if ex is None:
        return [], [], [], [], {}, []
    na = ex.na
    aliases = {n_in + na + a: n_out + a for a in range(na)}
    return ex.srcs + ex.bufs, [ANY_SPEC] * (2 * na), [ANY_SPEC] * na, ex.out_shapes(), aliases, ex.scratch()


def _run_exchange(ex, refs, n_in, n_out, n_scratch, first, last):
    def deco(compute):
        if ex is None:
            compute()
            return
        na = ex.na
        n_all_in = n_in + 2 * na
        src_refs = refs[n_in:n_in + na]
        buf_refs = refs[n_all_in + n_out:n_all_in + n_out + na]
        sems = refs[n_all_in + n_out + na + n_scratch:]

        @pl.when(first)
        def _():
            for cp in ex.copies(src_refs, buf_refs, *sems):
                cp.start()

        compute()

        @pl.when(last)
        def _():
            for cp in ex.copies(src_refs, buf_refs, *sems):
                cp.wait()
    return deco


def _permute_w_in(w):
    o = [0, D, 2 * D, 2 * D + Q_LORA, 2 * D + Q_LORA + KV_LORA, 2 * D + Q_LORA + KV_LORA + QK_ROPE,
         3 * D + Q_LORA + KV_LORA + QK_ROPE, IN_WIDTH]
    xr, gr, qlat, kvl, kr, ga, gb = [w[:, o[i]:o[i + 1]] for i in range(7)]
    z = lambda c: jnp.zeros((w.shape[0], c), w.dtype)
    return jnp.concatenate([xr, gr, ga, gb, qlat, kvl, z(QK_NOPE), kr, z(HEAD_LANES - QK_NOPE - QK_ROPE)], axis=1)


def _unpermute_dw_in(dw):
    o = 4 * D
    k0 = o + Q_LORA + KV_LORA + QK_NOPE
    return jnp.concatenate([dw[:, 0:2 * D], dw[:, o:o + Q_LORA + KV_LORA], dw[:, k0:k0 + QK_ROPE],
                            dw[:, 2 * D:4 * D]], axis=1)


def _pair_wq(w):
    w = w.reshape(Q_LORA, N_HEADS, QK_NOPE + QK_ROPE)
    w = jnp.pad(w, ((0, 0), (0, 0), (0, HEAD_LANES - QK_NOPE - QK_ROPE)))
    return w.reshape(Q_LORA, N_HEAD_PAIRS, HEADS_PER_STEP * HEAD_LANES).transpose(1, 0, 2)


def _unpair_dwq(dw):
    dw = dw.transpose(1, 0, 2).reshape(Q_LORA, N_HEADS, HEAD_LANES)
    return dw[:, :, :QK_NOPE + QK_ROPE].reshape(Q_LORA, N_HEADS * (QK_NOPE + QK_ROPE))


def _pair_wkv(w):
    w = w.reshape(KV_LORA, N_HEAD_PAIRS, HEADS_PER_STEP, QK_NOPE + V_HEAD)
    kn, vv = w[..., :QK_NOPE], w[..., QK_NOPE:]
    z = jnp.zeros_like(kn[:, :, 0])
    out = jnp.concatenate([kn[:, :, 0], z, kn[:, :, 1], z, vv[:, :, 0], z, z, vv[:, :, 1]], axis=-1)
    return out.transpose(1, 0, 2)


def _unpair_dwkv(dw):
    dw = dw.transpose(1, 0, 2)
    h0 = jnp.concatenate([dw[..., 0:64], dw[..., 256:320]], axis=-1)
    h1 = jnp.concatenate([dw[..., 128:192], dw[..., 448:512]], axis=-1)
    return jnp.stack([h0, h1], axis=2).reshape(KV_LORA, N_HEADS * (QK_NOPE + V_HEAD))


def _group_gates(w):
    per = MXU_GROUP // RNN_BLOCK
    w = w.reshape(DEPTH, N_GATE_GROUPS, per, RNN_BLOCK, RNN_BLOCK)
    eye = jnp.eye(per, dtype=w.dtype)
    return jnp.einsum('lgpij,pq->lgpiqj', w, eye).reshape(DEPTH, N_GATE_GROUPS, MXU_GROUP, MXU_GROUP)


def _ungroup_dgate(dw):
    per = MXU_GROUP // RNN_BLOCK
    dw = dw.reshape(N_GATE_GROUPS, per, RNN_BLOCK, per, RNN_BLOCK)
    return jnp.stack([dw[:, p, :, p, :] for p in range(per)], axis=1).reshape(N_RNN_BLOCKS, RNN_BLOCK, RNN_BLOCK)


def _rope_tables(positions):
    inv_freq = ROPE_THETA ** (-jnp.arange(0, QK_ROPE, 2, dtype=F32) / QK_ROPE)
    ang = positions.astype(F32)[..., None] * inv_freq
    cos, sin = jnp.cos(ang), jnp.sin(ang)
    one, zero = jnp.ones_like(cos), jnp.zeros_like(cos)
    nope = lambda v: jnp.concatenate([v] * (QK_NOPE // (QK_ROPE // 2)), axis=-1)
    tail = jnp.concatenate([zero, zero], axis=-1)
    cos_t = jnp.concatenate([nope(one), cos, cos, tail], axis=-1)
    sa_t = jnp.concatenate([nope(zero), -sin, zero, tail], axis=-1)
    sb_t = jnp.concatenate([nope(zero), zero, sin, tail], axis=-1)
    return cos_t, sa_t, sb_t


def _unshard(pieces, name):
    axis = SHARDED[name][0] - 1
    return jnp.concatenate([pieces[i] for i in range(N_DEV)], axis=axis)


def _shard_pieces(g, name):
    axis = SHARDED[name][0] - 1
    return jnp.stack(jnp.split(g, N_DEV, axis=axis))


MIXER_SHARDED = ('w_in', 'conv_w', 'w_uq', 'w_ukv', 'w_out')
FFN_SHARDED = ('w_up', 'ffn_conv_w', 'w_down')
FFN_WEIGHTS = ('w_up', 'ffn_conv_w', 'ffn_conv_b', 'w_down', 'ln2_g', 'ln2_b')


def _mixer_weights(gathered, l):
    full = {name: _unshard(gathered[name][:, l], name) for name in MIXER_SHARDED}
    return {'w_in_p': _permute_w_in(full['w_in']), 'conv_w': full['conv_w'], 'wq_pairs': _pair_wq(full['w_uq']),
            'wkv_pairs': _pair_wkv(full['w_ukv']), 'w_out': full['w_out']}


def _ffn_weights(gathered, l):
    return {name: _unshard(gathered[name][:, l], name) for name in FFN_SHARDED}


TM = 256
TS_RNN = 128
TS_FFN = 256
TQ_FWD = 256
TQ_BWD = 256
TN_DW = 512


def layer_fwd(xc, tabs, shared, gbufs, send, l, b, s):
    n = b * s
    lw = dict(shared, **_mixer_weights(gbufs, l))
    rnn4, ql, kvl, kr = inproj_fwd(xc, lw['w_in_p'], TM)
    h = rnn_fwd(rnn4.reshape(b, s, 4 * D), lw, l, b, s, TS_RNN)
    lat = (ql.reshape(b, s, Q_LORA), kvl.reshape(b, s, KV_LORA), kr.reshape(b, s, LANES))
    ex = Exchange(_gather_entries(FFN_SHARDED, l, send, gbufs)
                  + (_gather_entries(MIXER_SHARDED, l + 1, send, gbufs) if l + 1 < DEPTH else []))
    y_mla, lse, *xbufs = mla_fwd(*lat, tabs, lw, l, b, s, TQ_FWD, ex)
    gbufs = ex.updated(gbufs, xbufs)
    lw.update(_ffn_weights(gbufs, l))
    z1, x1 = mixout_fwd(xc, rnn4, h.reshape(n, D), y_mla.reshape(n, D), lw, l, TM)
    up, z2, x2 = ffn_fwd(x1, lw, l, b, s, TS_FFN)
    return x2, (xc, rnn4, lat, h, y_mla, lse, z1, x1, up, z2), lw, gbufs


def _gather_entries(group, l, send, gbufs):
    return [(k, send[k], False, l, l, gbufs[k]) for k in group]


def _scatter_entries(grads, l, pbufs):
    out = []
    for k, g in grads.items():
        g = _shard_pieces(g, k) if k in SHARDED else g
        out.append((k, g.astype(pbufs[k].dtype), k in SHARDED, None, l, pbufs[k]))
    return out


def layer_bwd(dx, saved, tabs, lw, l, b, s, pending, pbufs):
    n = b * s
    x0, rnn4, lat, h, y_mla, lse, z1, x1, up, z2 = saved
    dz2, dup, act, dfcw, dfcb, dg2, db2 = ffn_bwd(dx, z2, up, lw, l, b, s, TS_FFN)
    dx1 = matmul_dx(dup, lw['w_up'], dz2, ALPHA, TM, "ffn_up_dx")
    dw_up = matmul_dw(x1, dup, TN_DW, 2 * D_FF // 3, "ffn_up_dw")
    dw_down = matmul_dw(act, dz2, TN_DW, D // 2, "ffn_down_dw")
    ffn_grads = {'w_up': dw_up, 'ffn_conv_w': dfcw, 'ffn_conv_b': dfcb, 'w_down': dw_down, 'ln2_g': dg2, 'ln2_b': db2}
    dz1, dh, dy_mla, dg3, dw_out, dg1, db1 = mixout_bwd(dx1, z1, rnn4, h.reshape(n, D), y_mla.reshape(n, D),
                                                       lw, l, TM)
    ex = Exchange(pending + _scatter_entries(ffn_grads, l, pbufs))
    dql, dkvl, dkr, dwq, dwkv, dgq, dgkv, *xbufs = mla_bwd(dy_mla.reshape(b, s, D), y_mla, lse, *lat, tabs, lw, l,
                                                           b, s, TQ_BWD, ex)
    pbufs = ex.updated(pbufs, xbufs)
    dxr, dcw, dcb, dwgx, dbgx, dwga, dbga, dlam = rnn_bwd(dh.reshape(b, s, D), rnn4.reshape(b, s, 4 * D), h,
                                                          lw, l, b, s, TS_RNN)
    dx, dproj = inproj_bwd(dxr.reshape(n, D), dg3, dql.reshape(n, Q_LORA), dkvl.reshape(n, KV_LORA),
                           dkr.reshape(n, LANES), dz1, lw['w_in_p'], TM)
    dw_in_p = matmul_dw(x0, dproj, TN_DW, PROJ_W // 2, "inproj_dw")
    mixer_grads = {
        'w_in': _unpermute_dw_in(dw_in_p), 'conv_w': dcw, 'conv_b': dcb, 'gx_w': _ungroup_dgate(dwgx), 'gx_b': dbgx,
        'ga_w': _ungroup_dgate(dwga), 'ga_b': dbga, 'lru_lambda': dlam, 'q_norm_g': dgq, 'w_uq': _unpair_dwq(dwq),
        'kv_norm_g': dgkv, 'w_ukv': _unpair_dwkv(dwkv), 'w_out': dw_out, 'ln1_g': dg1, 'ln1_b': db1,
    }
    return dx, _scatter_entries(mixer_grads, l, pbufs), pbufs


SMALL_WEIGHT_ELEMS = 1 << 16


def kernel(x, positions, w_in, conv_w, conv_b, gx_w, gx_b, ga_w, ga_b, lru_lambda, q_norm_g, w_uq, kv_norm_g, w_ukv, w_out, ln1_g, ln1_b, w_up, ffn_conv_w, ffn_conv_b, w_down, ln2_g, ln2_b, loss_target, m_w_in, m_conv_w, m_conv_b, m_gx_w, m_gx_b, m_ga_w, m_ga_b, m_lru_lambda, m_q_norm_g, m_w_uq, m_kv_norm_g, m_w_ukv, m_w_out, m_ln1_g, m_ln1_b, m_w_up, m_ffn_conv_w, m_ffn_conv_b, m_w_down, m_ln2_g, m_ln2_b, v_w_in, v_conv_w, v_conv_b, v_gx_w, v_gx_b, v_ga_w, v_ga_b, v_lru_lambda, v_q_norm_g, v_w_uq, v_kv_norm_g, v_w_ukv, v_w_out, v_ln1_g, v_ln1_b, v_w_up, v_ffn_conv_w, v_ffn_conv_b, v_w_down, v_ln2_g, v_ln2_b):
    w = dict(zip(WEIGHT_NAMES, (w_in, conv_w, conv_b, gx_w, gx_b, ga_w, ga_b, lru_lambda, q_norm_g, w_uq, kv_norm_g,
                                w_ukv, w_out, ln1_g, ln1_b, w_up, ffn_conv_w, ffn_conv_b, w_down, ln2_g, ln2_b)))
    m = dict(zip(WEIGHT_NAMES, (m_w_in, m_conv_w, m_conv_b, m_gx_w, m_gx_b, m_ga_w, m_ga_b, m_lru_lambda,
                                m_q_norm_g, m_w_uq, m_kv_norm_g, m_w_ukv, m_w_out, m_ln1_g, m_ln1_b, m_w_up,
                                m_ffn_conv_w, m_ffn_conv_b, m_w_down, m_ln2_g, m_ln2_b)))
    v = dict(zip(WEIGHT_NAMES, (v_w_in, v_conv_w, v_conv_b, v_gx_w, v_gx_b, v_ga_w, v_ga_b, v_lru_lambda,
                                v_q_norm_g, v_w_uq, v_kv_norm_g, v_w_ukv, v_w_out, v_ln1_g, v_ln1_b, v_w_up,
                                v_ffn_conv_w, v_ffn_conv_b, v_w_down, v_ln2_g, v_ln2_b)))
    b, s, _ = x.shape
    n = b * s
    tabs = _rope_tables(positions)
    shared = {name: w[name] for name in WEIGHT_NAMES if name not in SHARDED and w[name].ndim == 2}
    shared['wgx'] = _group_gates(w['gx_w']).astype(MXU_DTYPE)
    shared['wga'] = _group_gates(w['ga_w']).astype(MXU_DTYPE)

    send = {k: w[k].astype(MXU_DTYPE) if k in MATMUL_WEIGHTS else w[k] for k in SHARDED}
    gbufs = {k: lax.empty((N_DEV,) + a.shape, a.dtype) for k, a in send.items()}
    head = Exchange(_gather_entries(MIXER_SHARDED, 0, send, gbufs))
    gbufs = head.updated(gbufs, exchange_layer(head, "gather_weights"))

    lws, saved = [], []
    xc = x.reshape(n, D)
    for l in range(DEPTH):
        xc, sv, lw, gbufs = layer_fwd(xc, tabs, shared, gbufs, send, l, b, s)
        lws.append(lw)
        saved.append(sv)
    loss_part, dx = loss_head(xc, loss_target.reshape(n, D), TM)

    def part_shape(k):
        piece = w[k].shape[1:]
        return (N_DEV, DEPTH) + ((1,) + piece if len(piece) == 1 else piece)

    pbufs = {k: lax.empty(part_shape(k), MXU_DTYPE if k in MATMUL_WEIGHTS else F32) for k in WEIGHT_NAMES}
    pending = []
    for l in reversed(range(DEPTH)):
        dx, pending, pbufs = layer_bwd(dx, saved[l], tabs, lws[l], l, b, s, pending, pbufs)
    tail = Exchange(pending)
    parts = tail.updated(pbufs, exchange_layer(tail, "scatter_grads"))

    out = {}
    small = [k for k in WEIGHT_NAMES if w[k].size <= SMALL_WEIGHT_ELEMS]
    res = adamw_small([(parts[k], w[k], m[k], v[k]) for k in small])
    for j, k in enumerate(small):
        out[k] = tuple(res[i * len(small) + j] for i in range(4))
    for k in WEIGHT_NAMES:
        if k in out:
            continue
        shape = w[k].shape
        view = (DEPTH, -1, shape[-1])
        r = adamw_tiled(parts[k].reshape((N_DEV,) + view), w[k].reshape(view), m[k].reshape(view),
                        v[k].reshape(view), k)
        out[k] = tuple(a.reshape(shape) for a in r)
    loss = lax.psum(loss_part[0, 0], ("x", "y", "c"))
    return (loss, dx.reshape(b, s, D), *[out[k][0] for k in WEIGHT_NAMES], *[out[k][1] for k in WEIGHT_NAMES],
            *[out[k][2] for k in WEIGHT_NAMES], *[out[k][3] for k in WEIGHT_NAMES])
```

```python
import functools
import math

import jax
import jax.numpy as jnp
from jax import lax
from jax.experimental import pallas as pl
from jax.experimental.pallas import tpu as pltpu

F32 = jnp.float32
MXU_DTYPE = jnp.bfloat16

D = 1024
DEPTH = 4
N_RNN_BLOCKS = 16
RNN_BLOCK = 64
CONV_W = 4
LRU_C = 8.0
N_HEADS = 16
QK_NOPE = 64
QK_ROPE = 32
V_HEAD = 64
Q_LORA = 384
KV_LORA = 256
ROPE_THETA = 10000.0
D_FF = 3 * D
FFN_CONV_W = 3
IN_WIDTH = 2 * D + Q_LORA + KV_LORA + QK_ROPE + 2 * D
ALPHA = (2 * DEPTH) ** 0.25
EPS = 1e-6
NEG_INF = -1e30
ATT_SCALE = (QK_NOPE + QK_ROPE) ** -0.5
GELU_C = math.sqrt(2.0 / math.pi)

ADAM_LR = 0.001
ADAM_B1 = 0.9
ADAM_B2 = 0.999
ADAM_EPS = 1e-08
ADAM_WD = 0.01
ADAM_STEP = 10

N_DEV = 8
LANES = 128
SUBLANES = 8
MXU_GROUP = 256
N_GATE_GROUPS = D // MXU_GROUP
HEADS_PER_STEP = 2
N_HEAD_PAIRS = N_HEADS // HEADS_PER_STEP
HEAD_LANES = 128
PROJ_W = 4 * D + Q_LORA + KV_LORA + LANES
VMEM_BIG = 56 * 2 ** 20

WEIGHT_NAMES = ['w_in', 'conv_w', 'conv_b', 'gx_w', 'gx_b', 'ga_w', 'ga_b', 'lru_lambda', 'q_norm_g', 'w_uq',
                'kv_norm_g', 'w_ukv', 'w_out', 'ln1_g', 'ln1_b', 'w_up', 'ffn_conv_w', 'ffn_conv_b', 'w_down',
                'ln2_g', 'ln2_b']
SHARDED = {
    'w_in': (2, (DEPTH, D, IN_WIDTH)),
    'conv_w': (2, (DEPTH, CONV_W, D)),
    'w_uq': (2, (DEPTH, Q_LORA, N_HEADS * (QK_NOPE + QK_ROPE))),
    'w_ukv': (2, (DEPTH, KV_LORA, N_HEADS * (QK_NOPE + V_HEAD))),
    'w_out': (1, (DEPTH, D, D)),
    'w_up': (2, (DEPTH, D, 2 * D_FF)),
    'ffn_conv_w': (2, (DEPTH, FFN_CONV_W, 2 * D_FF)),
    'w_down': (1, (DEPTH, D_FF, D)),
}
MATMUL_WEIGHTS = ('w_in', 'w_uq', 'w_ukv', 'w_out', 'w_up', 'w_down')


def _mm(a, b):
    return jnp.dot(a.astype(MXU_DTYPE), b.astype(MXU_DTYPE), preferred_element_type=F32)


def _mm_tn(a, b):
    return lax.dot_general(a.astype(MXU_DTYPE), b.astype(MXU_DTYPE), (((0,), (0,)), ((), ())),
                           preferred_element_type=F32)


def _mm_nt(a, b):
    return lax.dot_general(a.astype(MXU_DTYPE), b.astype(MXU_DTYPE), (((1,), (1,)), ((), ())),
                           preferred_element_type=F32)


def _sigmoid(x):
    return 1.0 / (1.0 + jnp.exp(-x))


def _gelu(x):
    t = jnp.tanh(GELU_C * (x + 0.044715 * (x * x * x)))
    return 0.5 * x * (1.0 + t), t


def _gelu_grad(x, t):
    return 0.5 * (1.0 + t) + 0.5 * x * (1.0 - t * t) * (GELU_C * (1.0 + 3.0 * 0.044715 * (x * x)))


def _neg_expm1(y):
    series = -y * (1.0 + 0.5 * y * (1.0 + (y / 3.0) * (1.0 + 0.25 * y * (1.0 + 0.2 * y))))
    return jnp.where(y > -0.05, series, 1.0 - jnp.exp(y))


def _ln_stats(z):
    mu = jnp.mean(z, axis=-1, keepdims=True)
    zc = z - mu
    var = jnp.mean(zc * zc, axis=-1, keepdims=True)
    r = lax.rsqrt(var + EPS)
    return zc * r, r


def _ln_bwd(dy, z, g):
    xhat, r = _ln_stats(z)
    dxh = dy * g
    dz = r * (dxh - jnp.mean(dxh, axis=-1, keepdims=True)
              - xhat * jnp.mean(dxh * xhat, axis=-1, keepdims=True))
    return dz, jnp.sum(dy * xhat, axis=0, keepdims=True), jnp.sum(dy, axis=0, keepdims=True)


def _rms_stats(x):
    r = lax.rsqrt(jnp.mean(x * x, axis=-1, keepdims=True) + EPS)
    return x * r, r


def _rms_bwd(dy, x, g):
    xn, r = _rms_stats(x)
    dxn = dy * g
    dx = r * (dxn - xn * jnp.mean(dxn * xn, axis=-1, keepdims=True))
    return dx, jnp.sum(dy * xn, axis=0, keepdims=True)


def _shift_down(x, halo, s, axis):
    if s == 0:
        return x
    r = pltpu.roll(x, s, axis)
    hr = pltpu.roll(halo, s, axis)
    idx = lax.broadcasted_iota(jnp.int32, hr.shape, axis)
    head = lax.slice_in_dim(r, 0, SUBLANES, axis=axis)
    rest = lax.slice_in_dim(r, SUBLANES, x.shape[axis], axis=axis)
    return jnp.concatenate([jnp.where(idx < s, hr, head), rest], axis=axis)


def _shift_up(x, halo, s, axis):
    if s == 0:
        return x
    n = x.shape[axis]
    r = pltpu.roll(x, n - s, axis)
    hr = pltpu.roll(halo, SUBLANES - s, axis)
    idx = lax.broadcasted_iota(jnp.int32, hr.shape, axis)
    body = lax.slice_in_dim(r, 0, n - SUBLANES, axis=axis)
    tail = lax.slice_in_dim(r, n - SUBLANES, n, axis=axis)
    return jnp.concatenate([body, jnp.where(idx >= SUBLANES - s, hr, tail)], axis=axis)


def _const_spec(shape):
    nd = len(shape)
    return pl.BlockSpec(shape, lambda *_: (0,) * nd)


def _layer_spec(shape, l):
    nd = len(shape)
    return pl.BlockSpec((None,) + tuple(shape), lambda *_: (l,) + (0,) * nd)


def _resident(shape):
    nd = len(shape)
    return pl.BlockSpec(shape, lambda *_: (0,) * nd, pipeline_mode=pl.Buffered(1))


def _params(vmem=None):
    return pltpu.CompilerParams(vmem_limit_bytes=vmem)


def inproj_fwd(x, w_in_p, tm):
    n = x.shape[0]

    def body(x_ref, w_ref, rnn4_ref, ql_ref, kvl_ref, kr_ref):
        xb = x_ref[...].astype(MXU_DTYPE)
        for j in range(4):
            rnn4_ref[:, j * D:(j + 1) * D] = _mm(xb, w_ref[:, j * D:(j + 1) * D])
        o = 4 * D
        ql_ref[...] = _mm(xb, w_ref[:, o:o + Q_LORA])
        kvl_ref[...] = _mm(xb, w_ref[:, o + Q_LORA:o + Q_LORA + KV_LORA])
        kr_ref[...] = _mm(xb, w_ref[:, o + Q_LORA + KV_LORA:PROJ_W])

    row = lambda i: (i, 0)
    return pl.pallas_call(
        body, name="inproj_fwd", grid=(n // tm,),
        in_specs=[pl.BlockSpec((tm, D), row), _resident((D, PROJ_W))],
        out_specs=[pl.BlockSpec((tm, 4 * D), row), pl.BlockSpec((tm, Q_LORA), row),
                   pl.BlockSpec((tm, KV_LORA), row), pl.BlockSpec((tm, LANES), row)],
        out_shape=[jax.ShapeDtypeStruct((n, 4 * D), F32), jax.ShapeDtypeStruct((n, Q_LORA), F32),
                   jax.ShapeDtypeStruct((n, KV_LORA), F32), jax.ShapeDtypeStruct((n, LANES), F32)],
        compiler_params=_params(VMEM_BIG),
    )(x, w_in_p)


def inproj_bwd(dxr, dg3, dql, dkvl, dkr, dz1, w_in_p, tm):
    n = dz1.shape[0]

    def body(dxr_ref, dg3_ref, dql_ref, dkvl_ref, dkr_ref, dz_ref, w_ref, dx_ref, dp_ref):
        dp = jnp.concatenate([dxr_ref[...], dg3_ref[...], dql_ref[...], dkvl_ref[...], dkr_ref[...]],
                             axis=1).astype(MXU_DTYPE)
        dp_ref[...] = dp
        dx_ref[...] = ALPHA * dz_ref[...] + _mm_nt(dp, w_ref[...])

    row = lambda i: (i, 0)
    return pl.pallas_call(
        body, name="inproj_bwd", grid=(n // tm,),
        in_specs=[pl.BlockSpec((tm, D), row), pl.BlockSpec((tm, 3 * D), row), pl.BlockSpec((tm, Q_LORA), row),
                  pl.BlockSpec((tm, KV_LORA), row), pl.BlockSpec((tm, LANES), row), pl.BlockSpec((tm, D), row),
                  _resident((D, PROJ_W))],
        out_specs=[pl.BlockSpec((tm, D), row), pl.BlockSpec((tm, PROJ_W), row)],
        out_shape=[jax.ShapeDtypeStruct((n, D), F32), jax.ShapeDtypeStruct((n, PROJ_W), MXU_DTYPE)],
        compiler_params=_params(VMEM_BIG),
    )(dxr, dg3, dql, dkvl, dkr, dz1, w_in_p)


def matmul_dw(x, dy, tn, tmc, name):
    n, k = x.shape
    m = dy.shape[1]

    def body(x_ref, dy_ref, dw_ref):
        @pl.when(pl.program_id(1) == 0)
        def _():
            dw_ref[...] = jnp.zeros_like(dw_ref)
        dw_ref[...] += _mm_tn(x_ref[...], dy_ref[...])

    return pl.pallas_call(
        body, name=name, grid=(m // tmc, n // tn),
        in_specs=[pl.BlockSpec((tn, k), lambda j, i: (i, 0)), pl.BlockSpec((tn, tmc), lambda j, i: (i, j))],
        out_specs=pl.BlockSpec((k, tmc), lambda j, i: (0, j)),
        out_shape=jax.ShapeDtypeStruct((k, m), F32),
        compiler_params=_params(VMEM_BIG),
    )(x, dy)


def matmul_dx(dy, w, add, add_scale, tm, name):
    n, m = dy.shape
    k = w.shape[0]

    def body(dy_ref, w_ref, add_ref, dx_ref):
        dx_ref[...] = add_scale * add_ref[...] + _mm_nt(dy_ref[...], w_ref[...])

    row = lambda i: (i, 0)
    return pl.pallas_call(
        body, name=name, grid=(n // tm,),
        in_specs=[pl.BlockSpec((tm, m), row), _resident((k, m)), pl.BlockSpec((tm, k), row)],
        out_specs=pl.BlockSpec((tm, k), row),
        out_shape=jax.ShapeDtypeStruct((n, k), F32),
        compiler_params=_params(VMEM_BIG),
    )(dy, w, add)


def _group(g):
    return slice(g * MXU_GROUP, (g + 1) * MXU_GROUP)


def _rnn_gates(x, halo, g, l, cw_ref, cb_ref, wgx_ref, bgx_ref, wga_ref, bga_ref, lam_ref):
    b, ts, gw = x.shape
    sl = _group(g)
    lr = slice(l, l + 1)
    xc = cb_ref[lr, sl][None]
    for k in range(CONV_W):
        xc = xc + cw_ref[k:k + 1, sl][None] * _shift_down(x, halo, CONV_W - 1 - k, 1)
    xc2 = xc.reshape(b * ts, gw)
    xcb = xc2.astype(MXU_DTYPE)
    gx = _sigmoid(_mm(xcb, wgx_ref[g]) + bgx_ref[lr, sl])
    ga = _sigmoid(_mm(xcb, wga_ref[g]) + bga_ref[lr, sl])
    nl = -lam_ref[lr, sl]
    sp = jnp.maximum(nl, 0.0) + jnp.log1p(jnp.exp(-jnp.abs(nl)))
    log_a = (-LRU_C) * ga * sp
    a = jnp.exp(log_a)
    mult = jnp.sqrt(_neg_expm1(2.0 * log_a))
    return xc2, xcb, gx, ga, sp, a, mult


def rnn_fwd(rnn4, lw, l, b, s, ts):
    ns = s // ts

    def body(x_ref, cw_ref, cb_ref, wgx_ref, bgx_ref, wga_ref, bga_ref, lam_ref, h_ref,
             halo_sc, hstate_sc, a_sc, u_sc):
        @pl.when(pl.program_id(0) == 0)
        def _():
            halo_sc[...] = jnp.zeros_like(halo_sc)
            hstate_sc[...] = jnp.zeros_like(hstate_sc)

        for g in range(N_GATE_GROUPS):
            sl = _group(g)
            x = x_ref[:, :, sl]
            xc2, _, gx, _, _, a, mult = _rnn_gates(x, halo_sc[:, :, sl], g, l, cw_ref, cb_ref, wgx_ref, bgx_ref,
                                                   wga_ref, bga_ref, lam_ref)
            halo_sc[:, :, sl] = x[:, ts - SUBLANES:, :]
            a_sc[:, :, sl] = a.reshape(b, ts, MXU_GROUP)
            u_sc[:, :, sl] = (mult * gx * xc2).reshape(b, ts, MXU_GROUP)

        def step(t, h):
            h = a_sc[:, pl.ds(t, 1), :] * h + u_sc[:, pl.ds(t, 1), :]
            h_ref[:, pl.ds(t, 1), :] = h
            return h

        hstate_sc[...] = lax.fori_loop(0, ts, step, hstate_sc[...], unroll=8)

    tile = lambda i: (0, i, 0)
    vecs = _const_spec((DEPTH, D))
    gates = _layer_spec((N_GATE_GROUPS, MXU_GROUP, MXU_GROUP), l)
    return pl.pallas_call(
        body, name="rnn_fwd", grid=(ns,),
        in_specs=[pl.BlockSpec((b, ts, D), tile), _const_spec((CONV_W, D)), vecs, gates, vecs, gates, vecs, vecs],
        out_specs=pl.BlockSpec((b, ts, D), tile),
        out_shape=jax.ShapeDtypeStruct((b, s, D), F32),
        scratch_shapes=[pltpu.VMEM((b, SUBLANES, D), F32), pltpu.VMEM((b, 1, D), F32),
                        pltpu.VMEM((b, ts, D), F32), pltpu.VMEM((b, ts, D), F32)],
        compiler_params=_params(VMEM_BIG),
    )(rnn4, lw['conv_w'], lw['conv_b'], lw['wgx'], lw['gx_b'], lw['wga'], lw['ga_b'], lw['lru_lambda'])


def rnn_bwd(dh, rnn4, h, lw, l, b, s, ts):
    ns = s // ts
    hb = ts // SUBLANES

    def body(dh_ref, x_ref, xh_ref, h_ref, hh_ref, cw_ref, cb_ref, wgx_ref, bgx_ref, wga_ref, bga_ref, lam_ref,
             dx_ref, dcw_ref, dcb_ref, dwgx_ref, dbgx_ref, dwga_ref, dbga_ref, dlam_ref,
             carry_sc, dxc_halo_sc, a_sc, delta_sc, xc_sc, gx_sc, ga_sc, mult_sc):
        i = pl.program_id(0)

        @pl.when(i == 0)
        def _():
            carry_sc[...] = jnp.zeros_like(carry_sc)
            dxc_halo_sc[...] = jnp.zeros_like(dxc_halo_sc)
            for r in (dcw_ref, dcb_ref, dwgx_ref, dbgx_ref, dwga_ref, dbga_ref, dlam_ref):
                r[...] = jnp.zeros_like(r)

        keep = jnp.where(i == ns - 1, 0.0, 1.0)
        for g in range(N_GATE_GROUPS):
            sl = _group(g)
            xc2, _, gx, ga, _, a, mult = _rnn_gates(x_ref[:, :, sl], xh_ref[:, :, sl] * keep, g, l, cw_ref, cb_ref,
                                                    wgx_ref, bgx_ref, wga_ref, bga_ref, lam_ref)
            for sc, val in ((a_sc, a), (xc_sc, xc2), (gx_sc, gx), (ga_sc, ga), (mult_sc, mult)):
                sc[:, :, sl] = val.reshape(b, ts, MXU_GROUP)

        def step(j, c):
            t = ts - 1 - j
            d = dh_ref[:, pl.ds(t, 1), :] + c
            delta_sc[:, pl.ds(t, 1), :] = d
            return a_sc[:, pl.ds(t, 1), :] * d

        carry_sc[...] = lax.fori_loop(0, ts, step, carry_sc[...], unroll=8)

        for g in range(N_GATE_GROUPS):
            sl = _group(g)
            x = x_ref[:, :, sl]
            flat = lambda sc, sl=sl: sc[:, :, sl].reshape(b * ts, MXU_GROUP)
            xc2, gx, ga, a, mult, delta = (flat(sc) for sc in (xc_sc, gx_sc, ga_sc, a_sc, mult_sc, delta_sc))
            xcb = xc2.astype(MXU_DTYPE)
            nl = -lam_ref[l:l + 1, sl]
            sp = jnp.maximum(nl, 0.0) + jnp.log1p(jnp.exp(-jnp.abs(nl)))
            hprev = _shift_down(h_ref[:, :, sl], hh_ref[:, :, sl] * keep, 1, 1).reshape(b * ts, MXU_GROUP)
            dmult = delta * gx * xc2
            dl = delta * hprev * a - dmult * (a * a) / mult
            dga = dl * ((-LRU_C) * sp)
            dlam_ref[:, sl] += (jnp.sum(dl * ((-LRU_C) * ga), axis=0, keepdims=True)
                                * (-_sigmoid(-lam_ref[l:l + 1, sl])))
            dpa = dga * ga * (1.0 - ga)
            dpx = (delta * mult * xc2) * gx * (1.0 - gx)
            dbga_ref[:, sl] += jnp.sum(dpa, axis=0, keepdims=True)
            dbgx_ref[:, sl] += jnp.sum(dpx, axis=0, keepdims=True)
            dpab = dpa.astype(MXU_DTYPE)
            dpxb = dpx.astype(MXU_DTYPE)
            dwga_ref[g] += _mm_tn(xcb, dpab)
            dwgx_ref[g] += _mm_tn(xcb, dpxb)
            dxc2 = delta * mult * gx + _mm_nt(dpab, wga_ref[g]) + _mm_nt(dpxb, wgx_ref[g])
            dcb_ref[:, sl] += jnp.sum(dxc2, axis=0, keepdims=True)
            dxc = dxc2.reshape(b, ts, MXU_GROUP)
            nhalo = dxc_halo_sc[:, :, sl]
            dx = None
            for k in range(CONV_W):
                later = _shift_up(dxc, nhalo, CONV_W - 1 - k, 1)
                dcw_ref[k:k + 1, sl] += jnp.sum((later * x).reshape(b * ts, MXU_GROUP), axis=0, keepdims=True)
                term = cw_ref[k:k + 1, sl][None] * later
                dx = term if dx is None else dx + term
            dx_ref[:, :, sl] = dx
            dxc_halo_sc[:, :, sl] = dxc[:, :SUBLANES, :]

    tile = lambda i: (0, ns - 1 - i, 0)
    halo = lambda i: (0, jnp.maximum((ns - 1 - i) * hb - 1, 0), 0)
    gshape = (N_GATE_GROUPS, MXU_GROUP, MXU_GROUP)
    vecs = _const_spec((DEPTH, D))
    gates = _layer_spec(gshape, l)
    vec = jax.ShapeDtypeStruct((1, D), F32)
    return pl.pallas_call(
        body, name="rnn_bwd", grid=(ns,),
        in_specs=[pl.BlockSpec((b, ts, D), tile), pl.BlockSpec((b, ts, D), tile),
                  pl.BlockSpec((b, SUBLANES, D), halo), pl.BlockSpec((b, ts, D), tile),
                  pl.BlockSpec((b, SUBLANES, D), halo),
                  _const_spec((CONV_W, D)), vecs, gates, vecs, gates, vecs, vecs],
        out_specs=[pl.BlockSpec((b, ts, D), tile), _const_spec((CONV_W, D)), _const_spec((1, D)),
                   _const_spec(gshape), _const_spec((1, D)), _const_spec(gshape), _const_spec((1, D)),
                   _const_spec((1, D))],
        out_shape=[jax.ShapeDtypeStruct((b, s, D), F32), jax.ShapeDtypeStruct((CONV_W, D), F32), vec,
                   jax.ShapeDtypeStruct(gshape, F32), vec, jax.ShapeDtypeStruct(gshape, F32), vec, vec],
        scratch_shapes=[pltpu.VMEM((b, 1, D), F32), pltpu.VMEM((b, SUBLANES, D), F32)]
        + [pltpu.VMEM((b, ts, D), F32)] * 6,
        compiler_params=_params(VMEM_BIG),
    )(dh, rnn4, rnn4, h, h, lw['conv_w'], lw['conv_b'], lw['wgx'], lw['gx_b'], lw['wga'], lw['ga_b'],
      lw['lru_lambda'])


def _rope(x, cos, sa, sb):
    return x * cos + pltpu.roll(x, HEAD_LANES - QK_ROPE // 2, 1) * sa + pltpu.roll(x, QK_ROPE // 2, 1) * sb


def _unrope(d, cos, sa, sb):
    return d * cos + pltpu.roll(d * sa, QK_ROPE // 2, 1) + pltpu.roll(d * sb, HEAD_LANES - QK_ROPE // 2, 1)


LOG2_E = 1.0 / math.log(2.0)
Q_PRESCALE = ATT_SCALE * LOG2_E


def _scores(q_blk, keys):
    return _mm_nt(q_blk, keys)


def _diag_scores(q_blk, keys):
    tq = q_blk.shape[0]
    keep = lax.broadcasted_iota(jnp.int32, (tq, tq), 0) >= lax.broadcasted_iota(jnp.int32, (tq, tq), 1)
    return jnp.where(keep, _scores(q_blk, keys), NEG_INF)


def _mla_project(ql_ref, kvl_ref, l, gq_ref, gkv_ref, wq_ref, wkv_ref):
    qn, _ = _rms_stats(ql_ref[0])
    qn = (qn * gq_ref[l:l + 1, :]).astype(MXU_DTYPE)
    kvn, _ = _rms_stats(kvl_ref[0])
    kvn = (kvn * gkv_ref[l:l + 1, :]).astype(MXU_DTYPE)
    return qn, kvn, _mm(qn, wq_ref[0]), _mm(kvn, wkv_ref[0])


def mla_fwd(ql, kvl, kr, tabs, lw, l, b, s, tq, ex=None):
    nq = s // tq
    cos_t, sa_t, sb_t = tabs
    n_in, n_out = 10, 2
    x_args, x_in, x_out, x_shapes, x_alias, x_scratch = _fuse_exchange(ex, n_in, n_out)

    def body(*refs):
        bi, p = pl.program_id(0), pl.program_id(1)
        first = (bi == 0) & (p == 0)
        last = (bi == b - 1) & (p == N_HEAD_PAIRS - 1)

        @_run_exchange(ex, refs, n_in, n_out, 0, first, last)
        def _():
            compute(*refs[:n_in], *refs[n_in + len(x_in):n_in + len(x_in) + n_out])

    def compute(ql_ref, kvl_ref, kr_ref, cos_ref, sa_ref, sb_ref, gq_ref, gkv_ref, wq_ref, wkv_ref, o_ref, lse_ref):
        _, _, qp, kvp = _mla_project(ql_ref, kvl_ref, l, gq_ref, gkv_ref, wq_ref, wkv_ref)
        cos, sa, sb = cos_ref[0], sa_ref[0], sb_ref[0]
        for hh in range(HEADS_PER_STEP):
            hs = slice(hh * HEAD_LANES, (hh + 1) * HEAD_LANES)
            q = (_rope(qp[:, hs], cos, sa, sb) * Q_PRESCALE).astype(MXU_DTYPE)
            k = _rope(kvp[:, hs] + kr_ref[0], cos, sa, sb).astype(MXU_DTYPE)
            v = kvp[:, HEADS_PER_STEP * HEAD_LANES + hh * HEAD_LANES:
                    HEADS_PER_STEP * HEAD_LANES + (hh + 1) * HEAD_LANES].astype(MXU_DTYPE)
            def block_scores(qb):
                lo, hi = qb * tq, (qb + 1) * tq
                return _diag_scores(q[lo:hi], k[lo:hi]), (_scores(q[lo:hi], k[:lo]) if qb else None)

            def block_softmax(qb, sd, sf):
                m = jnp.max(sd, axis=-1, keepdims=True)
                if qb:
                    m = jnp.maximum(m, jnp.max(sf, axis=-1, keepdims=True))
                ed = jnp.exp2(sd - m)
                den = jnp.sum(ed, axis=-1, keepdims=True)
                ef = None
                if qb:
                    ef = jnp.exp2(sf - m)
                    den = den + jnp.sum(ef, axis=-1, keepdims=True)
                return ed.astype(MXU_DTYPE), (ef.astype(MXU_DTYPE) if qb else None), den, m + jnp.log2(den)

            def block_out(qb, ed, ef, den, lse, hh=hh, v=v):
                lo, hi = qb * tq, (qb + 1) * tq
                o = _mm(ed, v[lo:hi])
                if qb:
                    o = o + _mm(ef, v[:lo])
                o = o * (1.0 / den)
                lse_ref[0, hh, lo:hi, :] = jnp.broadcast_to(lse, (tq, LANES))
                if hh == 0:
                    o_ref[0, lo:hi, :] = o
                else:
                    o_ref[0, lo:hi, :] += o

            nxt, prev = block_scores(0), None
            for qb in range(nq):
                sd, sf = nxt
                if qb + 1 < nq:
                    nxt = block_scores(qb + 1)
                if prev is not None:
                    block_out(qb - 1, *prev)
                prev = block_softmax(qb, sd, sf)
            block_out(nq - 1, *prev)

    seq = lambda bi, p: (bi, 0, 0)
    pair = lambda bi, p: (p, 0, 0)

    def per_seq(w):
        return pl.BlockSpec((1, s, w), seq, pipeline_mode=pl.Buffered(1))

    return pl.pallas_call(
        body, name="mla_fwd", grid=(b, N_HEAD_PAIRS),
        in_specs=[per_seq(Q_LORA), per_seq(KV_LORA), per_seq(LANES), per_seq(LANES), per_seq(LANES), per_seq(LANES),
                  _const_spec((DEPTH, Q_LORA)), _const_spec((DEPTH, KV_LORA)),
                  pl.BlockSpec((1, Q_LORA, HEADS_PER_STEP * HEAD_LANES), pair),
                  pl.BlockSpec((1, KV_LORA, 2 * HEADS_PER_STEP * HEAD_LANES), pair)] + x_in,
        out_specs=[pl.BlockSpec((1, s, LANES), lambda bi, p: (bi, 0, p)),
                   pl.BlockSpec((1, HEADS_PER_STEP, s, LANES), lambda bi, p: (bi, p, 0, 0))] + x_out,
        out_shape=[jax.ShapeDtypeStruct((b, s, D), F32), jax.ShapeDtypeStruct((b, N_HEADS, s, LANES), F32)] + x_shapes,
        input_output_aliases=x_alias, scratch_shapes=x_scratch,
        compiler_params=_params(VMEM_BIG),
    )(ql, kvl, kr, cos_t, sa_t, sb_t, lw['q_norm_g'], lw['kv_norm_g'], lw['wq_pairs'], lw['wkv_pairs'], *x_args)


def mla_bwd(dy, y, lse, ql, kvl, kr, tabs, lw, l, b, s, tq, ex=None):
    nq = s // tq
    cos_t, sa_t, sb_t = tabs
    qw = HEADS_PER_STEP * HEAD_LANES
    kvw = 2 * HEADS_PER_STEP * HEAD_LANES
    n_in, n_out, n_scratch = 13, 7, 2
    x_args, x_in, x_out, x_shapes, x_alias, x_scratch = _fuse_exchange(ex, n_in, n_out)

    def body(*refs):
        bi, p = pl.program_id(0), pl.program_id(1)
        first = (bi == 0) & (p == 0)
        last = (bi == b - 1) & (p == N_HEAD_PAIRS - 1)
        o0 = n_in + len(x_in)
        s0 = o0 + n_out + len(x_out)

        @_run_exchange(ex, refs, n_in, n_out, n_scratch, first, last)
        def _():
            compute(*refs[:n_in], *refs[o0:o0 + n_out], *refs[s0:s0 + n_scratch])

    def compute(dy_ref, y_ref, lse_ref, ql_ref, kvl_ref, kr_ref, cos_ref, sa_ref, sb_ref, gq_ref, gkv_ref, wq_ref,
                wkv_ref, dql_ref, dkvl_ref, dkr_ref, dwq_ref, dwkv_ref, dgq_ref, dgkv_ref, dk_sc, dv_sc):
        bi, p = pl.program_id(0), pl.program_id(1)

        @pl.when((bi == 0) & (p == 0))
        def _():
            for r in (dwq_ref, dwkv_ref, dgq_ref, dgkv_ref):
                r[...] = jnp.zeros_like(r)

        @pl.when(p == 0)
        def _():
            for r in (dql_ref, dkvl_ref, dkr_ref):
                r[...] = jnp.zeros_like(r)

        qn, kvn, qp, kvp = _mla_project(ql_ref, kvl_ref, l, gq_ref, gkv_ref, wq_ref, wkv_ref)
        cos, sa, sb = cos_ref[0], sa_ref[0], sb_ref[0]
        dof = dy_ref[0]
        do = dof.astype(MXU_DTYPE)
        prod = dof * y_ref[0]
        lane = lax.broadcasted_iota(jnp.int32, prod.shape, 1)
        dq_heads, dk_heads, dv_heads = [], [], []
        for hh in range(HEADS_PER_STEP):
            hs = slice(hh * HEAD_LANES, (hh + 1) * HEAD_LANES)
            q = (_rope(qp[:, hs], cos, sa, sb) * Q_PRESCALE).astype(MXU_DTYPE)
            k = _rope(kvp[:, hs] + kr_ref[0], cos, sa, sb).astype(MXU_DTYPE)
            v = kvp[:, qw + hh * HEAD_LANES:qw + (hh + 1) * HEAD_LANES].astype(MXU_DTYPE)
            mine = (lane >= hh * V_HEAD) & (lane < (hh + 1) * V_HEAD)
            delta = jnp.sum(jnp.where(mine, prod, 0.0), axis=-1, keepdims=True)
            dk_sc[...] = jnp.zeros_like(dk_sc)
            dv_sc[...] = jnp.zeros_like(dv_sc)
            units = []
            for qb in range(nq):
                units.append((qb, slice(qb * tq, (qb + 1) * tq), True))
                if qb:
                    units.append((qb, slice(0, qb * tq), False))

            def unit_matmuls_in(u, q=q, k=k, v=v):
                qb, ks, diag = u
                qs = slice(qb * tq, (qb + 1) * tq)
                sc = _diag_scores(q[qs], k[ks]) if diag else _scores(q[qs], k[ks])
                return sc, _mm_nt(do[qs], v[ks])

            def unit_elementwise(u, sc, dp, hh=hh, delta=delta):
                qs = slice(u[0] * tq, (u[0] + 1) * tq)
                pr = jnp.exp2(sc - lse_ref[0, hh, qs, 0:1])
                return pr.astype(MXU_DTYPE), (pr * (dp - delta[qs])).astype(MXU_DTYPE)

            dq_blocks = [None] * nq

            def unit_matmuls_out(u, prb, ds, q=q, k=k):
                qb, ks, _ = u
                qs = slice(qb * tq, (qb + 1) * tq)
                dv_sc[ks, :] += _mm_tn(prb, do[qs])
                part = _mm(ds, k[ks])
                dq_blocks[qb] = part if dq_blocks[qb] is None else dq_blocks[qb] + part
                dk_sc[ks, :] += _mm_tn(ds, q[qs])

            nxt, prev = unit_matmuls_in(units[0]), None
            for i, u in enumerate(units):
                sc, dp = nxt
                if i + 1 < len(units):
                    nxt = unit_matmuls_in(units[i + 1])
                if prev is not None:
                    unit_matmuls_out(units[i - 1], *prev)
                prev = unit_elementwise(u, sc, dp)
            unit_matmuls_out(units[-1], *prev)
            dq_heads.append(_unrope(jnp.concatenate(dq_blocks, axis=0) * ATT_SCALE, cos, sa, sb))
            dk_full = _unrope(dk_sc[...] * (1.0 / LOG2_E), cos, sa, sb)
            dkr_ref[0] += dk_full
            dk_heads.append(dk_full)
            dv_heads.append(dv_sc[...])
        dqp = jnp.concatenate(dq_heads, axis=1).astype(MXU_DTYPE)
        dkvp = jnp.concatenate(dk_heads + dv_heads, axis=1).astype(MXU_DTYPE)
        dwq_ref[p] += _mm_tn(qn, dqp)
        dwkv_ref[p] += _mm_tn(kvn, dkvp)
        dql_ref[0] += _mm_nt(dqp, wq_ref[0])
        dkvl_ref[0] += _mm_nt(dkvp, wkv_ref[0])

        @pl.when(p == N_HEAD_PAIRS - 1)
        def _():
            dx, dg = _rms_bwd(dql_ref[0], ql_ref[0], gq_ref[l:l + 1, :])
            dql_ref[0] = dx
            dgq_ref[...] += dg
            dx, dg = _rms_bwd(dkvl_ref[0], kvl_ref[0], gkv_ref[l:l + 1, :])
            dkvl_ref[0] = dx
            dgkv_ref[...] += dg

    seq = lambda bi, p: (bi, 0, 0)
    pair = lambda bi, p: (p, 0, 0)

    def per_seq(w):
        return pl.BlockSpec((1, s, w), seq, pipeline_mode=pl.Buffered(1))

    return pl.pallas_call(
        body, name="mla_bwd", grid=(b, N_HEAD_PAIRS),
        in_specs=[pl.BlockSpec((1, s, LANES), lambda bi, p: (bi, 0, p)),
                  pl.BlockSpec((1, s, LANES), lambda bi, p: (bi, 0, p)),
                  pl.BlockSpec((1, HEADS_PER_STEP, s, LANES), lambda bi, p: (bi, p, 0, 0)),
                  per_seq(Q_LORA), per_seq(KV_LORA), per_seq(LANES), per_seq(LANES), per_seq(LANES), per_seq(LANES),
                  _const_spec((DEPTH, Q_LORA)), _const_spec((DEPTH, KV_LORA)),
                  pl.BlockSpec((1, Q_LORA, qw), pair), pl.BlockSpec((1, KV_LORA, kvw), pair)] + x_in,
        out_specs=[pl.BlockSpec((1, s, Q_LORA), seq), pl.BlockSpec((1, s, KV_LORA), seq),
                   pl.BlockSpec((1, s, LANES), seq),
                   _const_spec((N_HEAD_PAIRS, Q_LORA, qw)), _const_spec((N_HEAD_PAIRS, KV_LORA, kvw)),
                   _const_spec((1, Q_LORA)), _const_spec((1, KV_LORA))] + x_out,
        out_shape=[jax.ShapeDtypeStruct((b, s, Q_LORA), F32), jax.ShapeDtypeStruct((b, s, KV_LORA), F32),
                   jax.ShapeDtypeStruct((b, s, LANES), F32),
                   jax.ShapeDtypeStruct((N_HEAD_PAIRS, Q_LORA, qw), F32),
                   jax.ShapeDtypeStruct((N_HEAD_PAIRS, KV_LORA, kvw), F32),
                   jax.ShapeDtypeStruct((1, Q_LORA), F32), jax.ShapeDtypeStruct((1, KV_LORA), F32)] + x_shapes,
        input_output_aliases=x_alias,
        scratch_shapes=[pltpu.VMEM((s, HEAD_LANES), F32), pltpu.VMEM((s, HEAD_LANES), F32)] + x_scratch,
        compiler_params=_params(VMEM_BIG),
    )(dy, y, lse, ql, kvl, kr, cos_t, sa_t, sb_t, lw['q_norm_g'], lw['kv_norm_g'], lw['wq_pairs'], lw['wkv_pairs'],
      *x_args)


COL_CHUNK = 256


def _merge(g_rnn, gate_a, gate_b, h, y_mla):
    ge, t = _gelu(g_rnn)
    sa, sb = _sigmoid(gate_a), _sigmoid(gate_b)
    y_rnn = ge * h
    return ge, t, sa, sb, y_rnn, sa * y_rnn + sb * y_mla


def mixout_fwd(x, rnn4, h, y_mla, lw, l, tm):
    n = x.shape[0]

    def body(x_ref, gr_ref, gta_ref, gtb_ref, h_ref, y_ref, w_ref, g_ref, b_ref, z_ref, o_ref):
        z = ALPHA * x_ref[...]
        for c in range(0, D, COL_CHUNK):
            cs = slice(c, c + COL_CHUNK)
            merged = _merge(gr_ref[:, cs], gta_ref[:, cs], gtb_ref[:, cs], h_ref[:, cs], y_ref[:, cs])[-1]
            z = z + _mm(merged, w_ref[cs, :])
        z_ref[...] = z
        o_ref[...] = _ln_stats(z)[0] * g_ref[l:l + 1, :] + b_ref[l:l + 1, :]

    row = lambda i: (i, 0)
    col = lambda j: (lambda i: (i, j))
    blk = pl.BlockSpec((tm, D), row)
    return pl.pallas_call(
        body, name="mixout_fwd", grid=(n // tm,),
        in_specs=[blk, pl.BlockSpec((tm, D), col(1)), pl.BlockSpec((tm, D), col(2)), pl.BlockSpec((tm, D), col(3)),
                  blk, blk, _resident((D, D)), _const_spec((DEPTH, D)), _const_spec((DEPTH, D))],
        out_specs=[blk, blk],
        out_shape=[jax.ShapeDtypeStruct((n, D), F32), jax.ShapeDtypeStruct((n, D), F32)],
        compiler_params=_params(VMEM_BIG),
    )(x, rnn4, rnn4, rnn4, h, y_mla, lw['w_out'], lw['ln1_g'], lw['ln1_b'])


def mixout_bwd(dx1, z1, rnn4, h, y_mla, lw, l, tm):
    n = dx1.shape[0]

    def body(d_ref, z_ref, gr_ref, gta_ref, gtb_ref, h_ref, y_ref, w_ref, g_ref,
             dz_ref, dh_ref, dy_ref, dg3_ref, dw_ref, dg_ref, db_ref):
        @pl.when(pl.program_id(0) == 0)
        def _():
            for r in (dw_ref, dg_ref, db_ref):
                r[...] = jnp.zeros_like(r)

        dz, dg, db = _ln_bwd(d_ref[...], z_ref[...], g_ref[l:l + 1, :])
        dz_ref[...] = dz
        dg_ref[...] += dg
        db_ref[...] += db
        dzb = dz.astype(MXU_DTYPE)
        for c in range(0, D, COL_CHUNK):
            cs = slice(c, c + COL_CHUNK)
            g_rnn, h, y_mla = gr_ref[:, cs], h_ref[:, cs], y_ref[:, cs]
            ge, t, sa, sb, y_rnn, merged = _merge(g_rnn, gta_ref[:, cs], gtb_ref[:, cs], h, y_mla)
            dw_ref[cs, :] += _mm_tn(merged, dzb)
            dm = _mm_nt(dzb, w_ref[cs, :])
            dy_rnn = dm * sa
            dy_ref[:, cs] = dm * sb
            dh_ref[:, cs] = dy_rnn * ge
            dg3_ref[:, c:c + COL_CHUNK] = dy_rnn * h * _gelu_grad(g_rnn, t)
            dg3_ref[:, D + c:D + c + COL_CHUNK] = dm * y_rnn * sa * (1.0 - sa)
            dg3_ref[:, 2 * D + c:2 * D + c + COL_CHUNK] = dm * y_mla * sb * (1.0 - sb)

    row = lambda i: (i, 0)
    col = lambda j: (lambda i: (i, j))
    blk = pl.BlockSpec((tm, D), row)
    vec = jax.ShapeDtypeStruct((1, D), F32)
    act = jax.ShapeDtypeStruct((n, D), F32)
    return pl.pallas_call(
        body, name="mixout_bwd", grid=(n // tm,),
        in_specs=[blk, blk, pl.BlockSpec((tm, D), col(1)), pl.BlockSpec((tm, D), col(2)),
                  pl.BlockSpec((tm, D), col(3)), blk, blk, _resident((D, D)), _const_spec((DEPTH, D))],
        out_specs=[blk, blk, blk, pl.BlockSpec((tm, 3 * D), row), _const_spec((D, D)), _const_spec((1, D)),
                   _const_spec((1, D))],
        out_shape=[act, act, act, jax.ShapeDtypeStruct((n, 3 * D), F32), jax.ShapeDtypeStruct((D, D), F32), vec, vec],
        compiler_params=_params(VMEM_BIG),
    )(dx1, z1, rnn4, rnn4, rnn4, h, y_mla, lw['w_out'], lw['ln1_g'])


FFN_CHUNK = 512


def _conv3(u, halo, cs, l, fcw_ref, fcb_ref):
    hc = fcb_ref[l:l + 1, cs]
    for k in range(FFN_CONV_W):
        hc = hc + fcw_ref[k:k + 1, cs] * _shift_down(u, halo, FFN_CONV_W - 1 - k, 0)
    return hc


def _conv3_from(sc, cs, l, ts, fcw_ref, fcb_ref):
    hc = fcb_ref[l:l + 1, cs]
    for k in range(FFN_CONV_W):
        o = SUBLANES - (FFN_CONV_W - 1 - k)
        hc = hc + fcw_ref[k:k + 1, cs] * sc[o:o + ts, cs]
    return hc


def ffn_fwd(x1, lw, l, b, s, ts):
    ns = s // ts
    n = b * s

    def body(x_ref, wu_ref, fcw_ref, fcb_ref, wd_ref, g_ref, b_ref, up_ref, z_ref, o_ref, halo_sc):
        @pl.when(pl.program_id(1) == 0)
        def _():
            halo_sc[0:SUBLANES, :] = jnp.zeros((SUBLANES, 2 * D_FF), F32)

        x = x_ref[...]
        xb = x.astype(MXU_DTYPE)
        z = ALPHA * x

        def up_chunk(c):
            return (_mm(xb, wu_ref[:, c:c + FFN_CHUNK]), _mm(xb, wu_ref[:, D_FF + c:D_FF + c + FFN_CHUNK]))

        nxt, act_prev = up_chunk(0), None
        for c in range(0, D_FF, FFN_CHUNK):
            gs, vs = slice(c, c + FFN_CHUNK), slice(D_FF + c, D_FF + c + FFN_CHUNK)
            ug, uv = nxt
            if c + FFN_CHUNK < D_FF:
                nxt = up_chunk(c + FFN_CHUNK)
            if act_prev is not None:
                z = z + _mm(act_prev, wd_ref[c - FFN_CHUNK:c, :])
            up_ref[:, gs] = ug
            up_ref[:, vs] = uv
            halo_sc[SUBLANES:, gs] = ug
            halo_sc[SUBLANES:, vs] = uv
            hg, hv = (_conv3_from(halo_sc, cs, l, ts, fcw_ref, fcb_ref) for cs in (gs, vs))
            halo_sc[0:SUBLANES, gs] = ug[ts - SUBLANES:, :]
            halo_sc[0:SUBLANES, vs] = uv[ts - SUBLANES:, :]
            act_prev = (_gelu(hg)[0] * hv).astype(MXU_DTYPE)
        z = z + _mm(act_prev, wd_ref[D_FF - FFN_CHUNK:D_FF, :])
        z_ref[...] = z
        o_ref[...] = _ln_stats(z)[0] * g_ref[l:l + 1, :] + b_ref[l:l + 1, :]

    row = lambda bi, i: (bi * ns + i, 0)
    blk = pl.BlockSpec((ts, D), row)
    return pl.pallas_call(
        body, name="ffn_fwd", grid=(b, ns),
        in_specs=[blk, _resident((D, 2 * D_FF)), _const_spec((FFN_CONV_W, 2 * D_FF)), _const_spec((DEPTH, 2 * D_FF)),
                  _resident((D_FF, D)), _const_spec((DEPTH, D)), _const_spec((DEPTH, D))],
        out_specs=[pl.BlockSpec((ts, 2 * D_FF), row), blk, blk],
        out_shape=[jax.ShapeDtypeStruct((n, 2 * D_FF), F32), jax.ShapeDtypeStruct((n, D), F32),
                   jax.ShapeDtypeStruct((n, D), F32)],
        scratch_shapes=[pltpu.VMEM((SUBLANES + ts, 2 * D_FF), F32)],
        compiler_params=_params(VMEM_BIG),
    )(x1, lw['w_up'], lw['ffn_conv_w'], lw['ffn_conv_b'], lw['w_down'], lw['ln2_g'], lw['ln2_b'])


def ffn_bwd(dx2, z2, up, lw, l, b, s, ts):
    ns = s // ts
    n = b * s
    hb = ts // SUBLANES

    def body(d_ref, z_ref, up_ref, uph_ref, fcw_ref, fcb_ref, wd_ref, g_ref,
             dz_ref, dup_ref, act_ref, dfcw_ref, dfcb_ref, dg_ref, db_ref, nhalo_sc):
        bi, i = pl.program_id(0), pl.program_id(1)

        @pl.when((bi == 0) & (i == 0))
        def _():
            for r in (dfcw_ref, dfcb_ref, dg_ref, db_ref):
                r[...] = jnp.zeros_like(r)

        @pl.when(i == 0)
        def _():
            nhalo_sc[...] = jnp.zeros_like(nhalo_sc)

        dz, dg, db = _ln_bwd(d_ref[...], z_ref[...], g_ref[l:l + 1, :])
        dz_ref[...] = dz
        dg_ref[...] += dg
        db_ref[...] += db
        dzb = dz.astype(MXU_DTYPE)
        keep = jnp.where(i == ns - 1, 0.0, 1.0)
        for c in range(0, D_FF, FFN_CHUNK):
            gs, vs = slice(c, c + FFN_CHUNK), slice(D_FF + c, D_FF + c + FFN_CHUNK)
            ug, uv = up_ref[:, gs], up_ref[:, vs]
            hg_halo, hv_halo = uph_ref[:, gs] * keep, uph_ref[:, vs] * keep
            hg = _conv3(ug, hg_halo, gs, l, fcw_ref, fcb_ref)
            hv = _conv3(uv, hv_halo, vs, l, fcw_ref, fcb_ref)
            ge, t = _gelu(hg)
            act_ref[:, gs] = (ge * hv).astype(MXU_DTYPE)
            dact = _mm_nt(dzb, wd_ref[c:c + FFN_CHUNK, :])
            for cs, u, halo, dhc in ((gs, ug, hg_halo, dact * hv * _gelu_grad(hg, t)), (vs, uv, hv_halo, dact * ge)):
                dfcb_ref[:, cs] += jnp.sum(dhc, axis=0, keepdims=True)
                nhalo = nhalo_sc[:, cs]
                dup = jnp.zeros_like(dhc)
                for k in range(FFN_CONV_W):
                    sft = FFN_CONV_W - 1 - k
                    dfcw_ref[k:k + 1, cs] += jnp.sum(dhc * _shift_down(u, halo, sft, 0), axis=0, keepdims=True)
                    dup = dup + fcw_ref[k:k + 1, cs] * _shift_up(dhc, nhalo, sft, 0)
                dup_ref[:, cs] = dup.astype(MXU_DTYPE)
                nhalo_sc[:, cs] = dhc[:SUBLANES, :]

    row = lambda bi, i: (bi * ns + (ns - 1 - i), 0)
    halo = lambda bi, i: (jnp.maximum((bi * ns + (ns - 1 - i)) * hb - 1, 0), 0)
    blk = pl.BlockSpec((ts, D), row)
    wide = pl.BlockSpec((ts, 2 * D_FF), row)
    return pl.pallas_call(
        body, name="ffn_bwd", grid=(b, ns),
        in_specs=[blk, blk, wide, pl.BlockSpec((SUBLANES, 2 * D_FF), halo),
                  _const_spec((FFN_CONV_W, 2 * D_FF)), _const_spec((DEPTH, 2 * D_FF)), _resident((D_FF, D)),
                  _const_spec((DEPTH, D))],
        out_specs=[blk, wide, pl.BlockSpec((ts, D_FF), row), _const_spec((FFN_CONV_W, 2 * D_FF)),
                   _const_spec((1, 2 * D_FF)), _const_spec((1, D)), _const_spec((1, D))],
        out_shape=[jax.ShapeDtypeStruct((n, D), F32), jax.ShapeDtypeStruct((n, 2 * D_FF), MXU_DTYPE),
                   jax.ShapeDtypeStruct((n, D_FF), MXU_DTYPE), jax.ShapeDtypeStruct((FFN_CONV_W, 2 * D_FF), F32),
                   jax.ShapeDtypeStruct((1, 2 * D_FF), F32), jax.ShapeDtypeStruct((1, D), F32),
                   jax.ShapeDtypeStruct((1, D), F32)],
        scratch_shapes=[pltpu.VMEM((SUBLANES, 2 * D_FF), F32)],
        compiler_params=_params(VMEM_BIG),
    )(dx2, z2, up, up, lw['ffn_conv_w'], lw['ffn_conv_b'], lw['w_down'], lw['ln2_g'])


def loss_head(y, target, tm):
    n = y.shape[0]

    def body(y_ref, t_ref, l_ref, d_ref):
        @pl.when(pl.program_id(0) == 0)
        def _():
            l_ref[...] = jnp.zeros_like(l_ref)

        err = y_ref[...] - t_ref[...]
        d_ref[...] = err * (1.0 / D)
        part = jnp.sum(jnp.sum(err * err, axis=-1, keepdims=True), axis=0, keepdims=True)
        l_ref[...] += jnp.broadcast_to(part * (0.5 / D), l_ref.shape)

    row = lambda i: (i, 0)
    return pl.pallas_call(
        body, name="loss_head", grid=(n // tm,),
        in_specs=[pl.BlockSpec((tm, D), row), pl.BlockSpec((tm, D), row)],
        out_specs=[_const_spec((1, LANES)), pl.BlockSpec((tm, D), row)],
        out_shape=[jax.ShapeDtypeStruct((1, LANES), F32), jax.ShapeDtypeStruct((n, D), F32)],
    )(y, target)


def _adam_update(g, w, m, v):
    c1 = 1.0 - ADAM_B1 ** ADAM_STEP
    c2 = 1.0 - ADAM_B2 ** ADAM_STEP
    mn = ADAM_B1 * m + (1.0 - ADAM_B1) * g
    vn = ADAM_B2 * v + (1.0 - ADAM_B2) * (g * g)
    return -ADAM_LR * ((mn / c1) / (jnp.sqrt(vn / c2) + ADAM_EPS) + ADAM_WD * w), mn, vn


def adamw_tiled(parts, w, m, v, name):
    _, r, c = w.shape
    tr = next(t for t in (256, 128, 64, 32, 16, 8) if r % t == 0)

    def body(p_ref, w_ref, m_ref, v_ref, g_ref, d_ref, mo_ref, vo_ref):
        g = p_ref[0].astype(F32)
        for i in range(1, N_DEV):
            g = g + p_ref[i].astype(F32)
        g_ref[...] = g
        d_ref[...], mo_ref[...], vo_ref[...] = _adam_update(g, w_ref[...], m_ref[...], v_ref[...])

    blk = pl.BlockSpec((None, tr, c), lambda l, i: (l, i, 0))
    out = jax.ShapeDtypeStruct(w.shape, F32)
    return pl.pallas_call(
        body, name="adamw_" + name, grid=(DEPTH, r // tr),
        in_specs=[pl.BlockSpec((N_DEV, None, tr, c), lambda l, i: (0, l, i, 0)), blk, blk, blk],
        out_specs=[blk, blk, blk, blk],
        out_shape=[out, out, out, out],
    )(parts, w, m, v)


def adamw_small(items):
    k = len(items)

    def body(*refs):
        ins, outs = refs[:4 * k], refs[4 * k:]
        for j in range(k):
            p_ref, w_ref, m_ref, v_ref = ins[4 * j:4 * j + 4]
            g_ref, d_ref, mo_ref, vo_ref = outs[j], outs[k + j], outs[2 * k + j], outs[3 * k + j]
            if len(p_ref.shape) == len(w_ref.shape) + 1:
                g = p_ref[0]
                for i in range(1, N_DEV):
                    g = g + p_ref[i]
                g_ref[...] = g
                d_ref[...], mo_ref[...], vo_ref[...] = _adam_update(g, w_ref[...], m_ref[...], v_ref[...])
            else:
                for l in range(DEPTH):
                    lr = slice(l, l + 1)
                    g = p_ref[0, l]
                    for i in range(1, N_DEV):
                        g = g + p_ref[i, l]
                    g_ref[lr, :] = g
                    d_ref[lr, :], mo_ref[lr, :], vo_ref[lr, :] = _adam_update(g, w_ref[lr, :], m_ref[lr, :],
                                                                              v_ref[lr, :])

    flat = [a for item in items for a in item]
    outs = [jax.ShapeDtypeStruct(item[1].shape, F32) for item in items] * 4
    return pl.pallas_call(
        body, name="adamw_small",
        in_specs=[pl.BlockSpec(memory_space=pltpu.VMEM)] * len(flat),
        out_specs=[pl.BlockSpec(memory_space=pltpu.VMEM)] * len(outs),
        out_shape=outs,
        compiler_params=_params(VMEM_BIG),
    )(*flat)


class Exchange:
    def __init__(self, entries):
        self.names = [e[0] for e in entries]
        self.srcs = [e[1] for e in entries]
        self.per_peer = [e[2] for e in entries]
        self.src_layer = [e[3] for e in entries]
        self.dst_layer = [e[4] for e in entries]
        self.bufs = [e[5] for e in entries]
        self.na = len(entries)

    def updated(self, bufdict, outs):
        new = dict(bufdict)
        new.update(zip(self.names, outs))
        return new

    def scratch(self):
        return [pltpu.SemaphoreType.DMA((self.na, N_DEV - 1)), pltpu.SemaphoreType.DMA((self.na, N_DEV - 1)),
                pltpu.SemaphoreType.DMA((self.na,))]

    def out_shapes(self):
        return [jax.ShapeDtypeStruct(bf.shape, bf.dtype) for bf in self.bufs]

    def copies(self, src_refs, buf_refs, send_sems, recv_sems, local_sems):
        x, y, c = lax.axis_index("x"), lax.axis_index("y"), lax.axis_index("c")
        me = 4 * x + 2 * y + c

        def view(a, pid):
            r = src_refs[a]
            if self.src_layer[a] is not None:
                r = r.at[self.src_layer[a]]
            return r.at[pid] if self.per_peer[a] else r

        out = [pltpu.make_async_copy(view(a, me), buf_refs[a].at[me, self.dst_layer[a]], local_sems.at[a])
               for a in range(self.na)]
        for k in range(1, N_DEV):
            px = 1 - x if k & 4 else x
            py = 1 - y if k & 2 else y
            pc = 1 - c if k & 1 else c
            pid = 4 * px + 2 * py + pc
            for a in range(self.na):
                out.append(pltpu.make_async_remote_copy(
                    src_ref=view(a, pid), dst_ref=buf_refs[a].at[me, self.dst_layer[a]],
                    send_sem=send_sems.at[a, k - 1], recv_sem=recv_sems.at[a, k - 1],
                    device_id=(px, py, pc), device_id_type=pl.DeviceIdType.MESH))
        return out


ANY_SPEC = pl.BlockSpec(memory_space=pl.ANY)


def exchange_layer(ex, name):
    na = ex.na

    def body(*refs):
        src_refs, buf_refs = refs[:na], refs[2 * na:3 * na]
        cps = ex.copies(src_refs, buf_refs, *refs[3 * na:])
        for cp in cps:
            cp.start()
        for cp in cps:
            cp.wait()

    return pl.pallas_call(
        body, name=name,
        in_specs=[ANY_SPEC] * (2 * na), out_specs=[ANY_SPEC] * na,
        out_shape=ex.out_shapes(),
        input_output_aliases={na + a: a for a in range(na)},
        scratch_shapes=ex.scratch(),
    )(*ex.srcs, *ex.bufs)


def _fuse_exchange(ex, n_in, n_out):
    if ex is None:
        return [], [], [], [], {}, []
    na = ex.na
    aliases = {n_in + na + a: n_out + a for a in range(na)}
    return ex.srcs + ex.bufs, [ANY_SPEC] * (2 * na), [ANY_SPEC] * na, ex.out_shapes(), aliases, ex.scratch()


def _run_exchange(ex, refs, n_in, n_out, n_scratch, first, last):
    def deco(compute):
        if ex is None:
            compute()
            return
        na = ex.na
        n_all_in = n_in + 2 * na
        src_refs = refs[n_in:n_in + na]
        buf_refs = refs[n_all_in + n_out:n_all_in + n_out + na]
        sems = refs[n_all_in + n_out + na + n_scratch:]

        @pl.when(first)
        def _():
            for cp in ex.copies(src_refs, buf_refs, *sems):
                cp.start()

        compute()

        @pl.when(last)
        def _():
            for cp in ex.copies(src_refs, buf_refs, *sems):
                cp.wait()
    return deco


def _permute_w_in(w):
    o = [0, D, 2 * D, 2 * D + Q_LORA, 2 * D + Q_LORA + KV_LORA, 2 * D + Q_LORA + KV_LORA + QK_ROPE,
         3 * D + Q_LORA + KV_LORA + QK_ROPE, IN_WIDTH]
    xr, gr, qlat, kvl, kr, ga, gb = [w[:, o[i]:o[i + 1]] for i in range(7)]
    z = lambda c: jnp.zeros((w.shape[0], c), w.dtype)
    return jnp.concatenate([xr, gr, ga, gb, qlat, kvl, z(QK_NOPE), kr, z(HEAD_LANES - QK_NOPE - QK_ROPE)], axis=1)


def _unpermute_dw_in(dw):
    o = 4 * D
    k0 = o + Q_LORA + KV_LORA + QK_NOPE
    return jnp.concatenate([dw[:, 0:2 * D], dw[:, o:o + Q_LORA + KV_LORA], dw[:, k0:k0 + QK_ROPE],
                            dw[:, 2 * D:4 * D]], axis=1)


def _pair_wq(w):
    w = w.reshape(Q_LORA, N_HEADS, QK_NOPE + QK_ROPE)
    w = jnp.pad(w, ((0, 0), (0, 0), (0, HEAD_LANES - QK_NOPE - QK_ROPE)))
    return w.reshape(Q_LORA, N_HEAD_PAIRS, HEADS_PER_STEP * HEAD_LANES).transpose(1, 0, 2)


def _unpair_dwq(dw):
    dw = dw.transpose(1, 0, 2).reshape(Q_LORA, N_HEADS, HEAD_LANES)
    return dw[:, :, :QK_NOPE + QK_ROPE].reshape(Q_LORA, N_HEADS * (QK_NOPE + QK_ROPE))


def _pair_wkv(w):
    w = w.reshape(KV_LORA, N_HEAD_PAIRS, HEADS_PER_STEP, QK_NOPE + V_HEAD)
    kn, vv = w[..., :QK_NOPE], w[..., QK_NOPE:]
    z = jnp.zeros_like(kn[:, :, 0])
    out = jnp.concatenate([kn[:, :, 0], z, kn[:, :, 1], z, vv[:, :, 0], z, z, vv[:, :, 1]], axis=-1)
    return out.transpose(1, 0, 2)


def _unpair_dwkv(dw):
    dw = dw.transpose(1, 0, 2)
    h0 = jnp.concatenate([dw[..., 0:64], dw[..., 256:320]], axis=-1)
    h1 = jnp.concatenate([dw[..., 128:192], dw[..., 448:512]], axis=-1)
    return jnp.stack([h0, h1], axis=2).reshape(KV_LORA, N_HEADS * (QK_NOPE + V_HEAD))


def _group_gates(w):
    per = MXU_GROUP // RNN_BLOCK
    w = w.reshape(DEPTH, N_GATE_GROUPS, per, RNN_BLOCK, RNN_BLOCK)
    eye = jnp.eye(per, dtype=w.dtype)
    return jnp.einsum('lgpij,pq->lgpiqj', w, eye).reshape(DEPTH, N_GATE_GROUPS, MXU_GROUP, MXU_GROUP)


def _ungroup_dgate(dw):
    per = MXU_GROUP // RNN_BLOCK
    dw = dw.reshape(N_GATE_GROUPS, per, RNN_BLOCK, per, RNN_BLOCK)
    return jnp.stack([dw[:, p, :, p, :] for p in range(per)], axis=1).reshape(N_RNN_BLOCKS, RNN_BLOCK, RNN_BLOCK)


def _rope_tables(positions):
    inv_freq = ROPE_THETA ** (-jnp.arange(0, QK_ROPE, 2, dtype=F32) / QK_ROPE)
    ang = positions.astype(F32)[..., None] * inv_freq
    cos, sin = jnp.cos(ang), jnp.sin(ang)
    one, zero = jnp.ones_like(cos), jnp.zeros_like(cos)
    nope = lambda v: jnp.concatenate([v] * (QK_NOPE // (QK_ROPE // 2)), axis=-1)
    tail = jnp.concatenate([zero, zero], axis=-1)
    cos_t = jnp.concatenate([nope(one), cos, cos, tail], axis=-1)
    sa_t = jnp.concatenate([nope(zero), -sin, zero, tail], axis=-1)
    sb_t = jnp.concatenate([nope(zero), zero, sin, tail], axis=-1)
    return cos_t, sa_t, sb_t


def _unshard(pieces, name):
    axis = SHARDED[name][0] - 1
    return jnp.concatenate([pieces[i] for i in range(N_DEV)], axis=axis)


def _shard_pieces(g, name):
    axis = SHARDED[name][0] - 1
    return jnp.stack(jnp.split(g, N_DEV, axis=axis))


MIXER_SHARDED = ('w_in', 'conv_w', 'w_uq', 'w_ukv', 'w_out')
FFN_SHARDED = ('w_up', 'ffn_conv_w', 'w_down')
FFN_WEIGHTS = ('w_up', 'ffn_conv_w', 'ffn_conv_b', 'w_down', 'ln2_g', 'ln2_b')


def _mixer_weights(gathered, l):
    full = {name: _unshard(gathered[name][:, l], name) for name in MIXER_SHARDED}
    return {'w_in_p': _permute_w_in(full['w_in']), 'conv_w': full['conv_w'], 'wq_pairs': _pair_wq(full['w_uq']),
            'wkv_pairs': _pair_wkv(full['w_ukv']), 'w_out': full['w_out']}


def _ffn_weights(gathered, l):
    return {name: _unshard(gathered[name][:, l], name) for name in FFN_SHARDED}


TM = 256
TS_RNN = 128
TS_FFN = 256
TQ_FWD = 256
TQ_BWD = 256
TN_DW = 1024


def layer_fwd(xc, tabs, shared, gbufs, send, l, b, s):
    n = b * s
    lw = dict(shared, **_mixer_weights(gbufs, l))
    rnn4, ql, kvl, kr = inproj_fwd(xc, lw['w_in_p'], TM)
    h = rnn_fwd(rnn4.reshape(b, s, 4 * D), lw, l, b, s, TS_RNN)
    lat = (ql.reshape(b, s, Q_LORA), kvl.reshape(b, s, KV_LORA), kr.reshape(b, s, LANES))
    ex = Exchange(_gather_entries(FFN_SHARDED, l, send, gbufs)
                  + (_gather_entries(MIXER_SHARDED, l + 1, send, gbufs) if l + 1 < DEPTH else []))
    y_mla, lse, *xbufs = mla_fwd(*lat, tabs, lw, l, b, s, TQ_FWD, ex)
    gbufs = ex.updated(gbufs, xbufs)
    lw.update(_ffn_weights(gbufs, l))
    z1, x1 = mixout_fwd(xc, rnn4, h.reshape(n, D), y_mla.reshape(n, D), lw, l, TM)
    up, z2, x2 = ffn_fwd(x1, lw, l, b, s, TS_FFN)
    return x2, (xc, rnn4, lat, h, y_mla, lse, z1, x1, up, z2), lw, gbufs


def _gather_entries(group, l, send, gbufs):
    return [(k, send[k], False, l, l, gbufs[k]) for k in group]


def _scatter_entries(grads, l, pbufs):
    out = []
    for k, g in grads.items():
        g = _shard_pieces(g, k) if k in SHARDED else g
        out.append((k, g.astype(pbufs[k].dtype), k in SHARDED, None, l, pbufs[k]))
    return out


def layer_bwd(dx, saved, tabs, lw, l, b, s, pending, pbufs):
    n = b * s
    x0, rnn4, lat, h, y_mla, lse, z1, x1, up, z2 = saved
    dz2, dup, act, dfcw, dfcb, dg2, db2 = ffn_bwd(dx, z2, up, lw, l, b, s, TS_FFN)
    dx1 = matmul_dx(dup, lw['w_up'], dz2, ALPHA, TM, "ffn_up_dx")
    dw_up = matmul_dw(x1, dup, TN_DW, 2 * D_FF // 3, "ffn_up_dw")
    dw_down = matmul_dw(act, dz2, TN_DW, D // 2, "ffn_down_dw")
    ffn_grads = {'w_up': dw_up, 'ffn_conv_w': dfcw, 'ffn_conv_b': dfcb, 'w_down': dw_down, 'ln2_g': dg2, 'ln2_b': db2}
    dz1, dh, dy_mla, dg3, dw_out, dg1, db1 = mixout_bwd(dx1, z1, rnn4, h.reshape(n, D), y_mla.reshape(n, D),
                                                       lw, l, TM)
    ex = Exchange(pending + _scatter_entries(ffn_grads, l, pbufs))
    dql, dkvl, dkr, dwq, dwkv, dgq, dgkv, *xbufs = mla_bwd(dy_mla.reshape(b, s, D), y_mla, lse, *lat, tabs, lw, l,
                                                           b, s, TQ_BWD, ex)
    pbufs = ex.updated(pbufs, xbufs)
    dxr, dcw, dcb, dwgx, dbgx, dwga, dbga, dlam = rnn_bwd(dh.reshape(b, s, D), rnn4.reshape(b, s, 4 * D), h,
                                                          lw, l, b, s, TS_RNN)
    dx, dproj = inproj_bwd(dxr.reshape(n, D), dg3, dql.reshape(n, Q_LORA), dkvl.reshape(n, KV_LORA),
                           dkr.reshape(n, LANES), dz1, lw['w_in_p'], TM)
    dw_in_p = matmul_dw(x0, dproj, TN_DW, PROJ_W // 2, "inproj_dw")
    mixer_grads = {
        'w_in': _unpermute_dw_in(dw_in_p), 'conv_w': dcw, 'conv_b': dcb, 'gx_w': _ungroup_dgate(dwgx), 'gx_b': dbgx,
        'ga_w': _ungroup_dgate(dwga), 'ga_b': dbga, 'lru_lambda': dlam, 'q_norm_g': dgq, 'w_uq': _unpair_dwq(dwq),
        'kv_norm_g': dgkv, 'w_ukv': _unpair_dwkv(dwkv), 'w_out': dw_out, 'ln1_g': dg1, 'ln1_b': db1,
    }
    return dx, _scatter_entries(mixer_grads, l, pbufs), pbufs


SMALL_WEIGHT_ELEMS = 1 << 16


def kernel(x, positions, w_in, conv_w, conv_b, gx_w, gx_b, ga_w, ga_b, lru_lambda, q_norm_g, w_uq, kv_norm_g, w_ukv, w_out, ln1_g, ln1_b, w_up, ffn_conv_w, ffn_conv_b, w_down, ln2_g, ln2_b, loss_target, m_w_in, m_conv_w, m_conv_b, m_gx_w, m_gx_b, m_ga_w, m_ga_b, m_lru_lambda, m_q_norm_g, m_w_uq, m_kv_norm_g, m_w_ukv, m_w_out, m_ln1_g, m_ln1_b, m_w_up, m_ffn_conv_w, m_ffn_conv_b, m_w_down, m_ln2_g, m_ln2_b, v_w_in, v_conv_w, v_conv_b, v_gx_w, v_gx_b, v_ga_w, v_ga_b, v_lru_lambda, v_q_norm_g, v_w_uq, v_kv_norm_g, v_w_ukv, v_w_out, v_ln1_g, v_ln1_b, v_w_up, v_ffn_conv_w, v_ffn_conv_b, v_w_down, v_ln2_g, v_ln2_b):
    w = dict(zip(WEIGHT_NAMES, (w_in, conv_w, conv_b, gx_w, gx_b, ga_w, ga_b, lru_lambda, q_norm_g, w_uq, kv_norm_g,
                                w_ukv, w_out, ln1_g, ln1_b, w_up, ffn_conv_w, ffn_conv_b, w_down, ln2_g, ln2_b)))
    m = dict(zip(WEIGHT_NAMES, (m_w_in, m_conv_w, m_conv_b, m_gx_w, m_gx_b, m_ga_w, m_ga_b, m_lru_lambda,
                                m_q_norm_g, m_w_uq, m_kv_norm_g, m_w_ukv, m_w_out, m_ln1_g, m_ln1_b, m_w_up,
                                m_ffn_conv_w, m_ffn_conv_b, m_w_down, m_ln2_g, m_ln2_b)))
    v = dict(zip(WEIGHT_NAMES, (v_w_in, v_conv_w, v_conv_b, v_gx_w, v_gx_b, v_ga_w, v_ga_b, v_lru_lambda,
                                v_q_norm_g, v_w_uq, v_kv_norm_g, v_w_ukv, v_w_out, v_ln1_g, v_ln1_b, v_w_up,
                                v_ffn_conv_w, v_ffn_conv_b, v_w_down, v_ln2_g, v_ln2_b)))
    b, s, _ = x.shape
    n = b * s
    tabs = _rope_tables(positions)
    shared = {name: w[name] for name in WEIGHT_NAMES if name not in SHARDED and w[name].ndim == 2}
    shared['wgx'] = _group_gates(w['gx_w']).astype(MXU_DTYPE)
    shared['wga'] = _group_gates(w['ga_w']).astype(MXU_DTYPE)

    send = {k: w[k].astype(MXU_DTYPE) if k in MATMUL_WEIGHTS else w[k] for k in SHARDED}
    gbufs = {k: lax.empty((N_DEV,) + a.shape, a.dtype) for k, a in send.items()}
    head = Exchange(_gather_entries(MIXER_SHARDED, 0, send, gbufs))
    gbufs = head.updated(gbufs, exchange_layer(head, "gather_weights"))

    lws, saved = [], []
    xc = x.reshape(n, D)
    for l in range(DEPTH):
        xc, sv, lw, gbufs = layer_fwd(xc, tabs, shared, gbufs, send, l, b, s)
        lws.append(lw)
        saved.append(sv)
    loss_part, dx = loss_head(xc, loss_target.reshape(n, D), TM)

    def part_shape(k):
        piece = w[k].shape[1:]
        return (N_DEV, DEPTH) + ((1,) + piece if len(piece) == 1 else piece)

    pbufs = {k: lax.empty(part_shape(k), MXU_DTYPE if k in MATMUL_WEIGHTS else F32) for k in WEIGHT_NAMES}
    pending = []
    for l in reversed(range(DEPTH)):
        dx, pending, pbufs = layer_bwd(dx, saved[l], tabs, lws[l], l, b, s, pending, pbufs)
    tail = Exchange(pending)
    parts = tail.updated(pbufs, exchange_layer(tail, "scatter_grads"))

    out = {}
    small = [k for k in WEIGHT_NAMES if w[k].size <= SMALL_WEIGHT_ELEMS]
    res = adamw_small([(parts[k], w[k], m[k], v[k]) for k in small])
    for j, k in enumerate(small):
        out[k] = tuple(res[i * len(small) + j] for i in range(4))
    for k in WEIGHT_NAMES:
        if k in out:
            continue
        shape = w[k].shape
        view = (DEPTH, -1, shape[-1])
        r = adamw_tiled(parts[k].reshape((N_DEV,) + view), w[k].reshape(view), m[k].reshape(view),
                        v[k].reshape(view), k)
        out[k] = tuple(a.reshape(shape) for a in r)
    loss = lax.psum(loss_part[0, 0], ("x", "y", "c"))
    return (loss, dx.reshape(b, s, D), *[out[k][0] for k in WEIGHT_NAMES], *[out[k][1] for k in WEIGHT_NAMES],
            *[out[k][2] for k in WEIGHT_NAMES], *[out[k][3] for k in WEIGHT_NAMES])
```

```python
import functools
import math

import jax
import jax.numpy as jnp
from jax import lax
from jax.experimental import pallas as pl
from jax.experimental.pallas import tpu as pltpu

F32 = jnp.float32
MXU_DTYPE = jnp.bfloat16

D = 1024
DEPTH = 4
N_RNN_BLOCKS = 16
RNN_BLOCK = 64
CONV_W = 4
LRU_C = 8.0
N_HEADS = 16
QK_NOPE = 64
QK_ROPE = 32
V_HEAD = 64
Q_LORA = 384
KV_LORA = 256
ROPE_THETA = 10000.0
D_FF = 3 * D
FFN_CONV_W = 3
IN_WIDTH = 2 * D + Q_LORA + KV_LORA + QK_ROPE + 2 * D
ALPHA = (2 * DEPTH) ** 0.25
EPS = 1e-6
NEG_INF = -1e30
ATT_SCALE = (QK_NOPE + QK_ROPE) ** -0.5
GELU_C = math.sqrt(2.0 / math.pi)

ADAM_LR = 0.001
ADAM_B1 = 0.9
ADAM_B2 = 0.999
ADAM_EPS = 1e-08
ADAM_WD = 0.01
ADAM_STEP = 10

N_DEV = 8
LANES = 128
SUBLANES = 8
MXU_GROUP = 256
N_GATE_GROUPS = D // MXU_GROUP
HEADS_PER_STEP = 2
N_HEAD_PAIRS = N_HEADS // HEADS_PER_STEP
HEAD_LANES = 128
PROJ_W = 4 * D + Q_LORA + KV_LORA + LANES
VMEM_BIG = 56 * 2 ** 20

WEIGHT_NAMES = ['w_in', 'conv_w', 'conv_b', 'gx_w', 'gx_b', 'ga_w', 'ga_b', 'lru_lambda', 'q_norm_g', 'w_uq',
                'kv_norm_g', 'w_ukv', 'w_out', 'ln1_g', 'ln1_b', 'w_up', 'ffn_conv_w', 'ffn_conv_b', 'w_down',
                'ln2_g', 'ln2_b']
SHARDED = {
    'w_in': (2, (DEPTH, D, IN_WIDTH)),
    'conv_w': (2, (DEPTH, CONV_W, D)),
    'w_uq': (2, (DEPTH, Q_LORA, N_HEADS * (QK_NOPE + QK_ROPE))),
    'w_ukv': (2, (DEPTH, KV_LORA, N_HEADS * (QK_NOPE + V_HEAD))),
    'w_out': (1, (DEPTH, D, D)),
    'w_up': (2, (DEPTH, D, 2 * D_FF)),
    'ffn_conv_w': (2, (DEPTH, FFN_CONV_W, 2 * D_FF)),
    'w_down': (1, (DEPTH, D_FF, D)),
}
MATMUL_WEIGHTS = ('w_in', 'w_uq', 'w_ukv', 'w_out', 'w_up', 'w_down')


def _mm(a, b):
    return jnp.dot(a.astype(MXU_DTYPE), b.astype(MXU_DTYPE), preferred_element_type=F32)


def _mm_tn(a, b):
    return lax.dot_general(a.astype(MXU_DTYPE), b.astype(MXU_DTYPE), (((0,), (0,)), ((), ())),
                           preferred_element_type=F32)


def _mm_nt(a, b):
    return lax.dot_general(a.astype(MXU_DTYPE), b.astype(MXU_DTYPE), (((1,), (1,)), ((), ())),
                           preferred_element_type=F32)


def _sigmoid(x):
    return 1.0 / (1.0 + jnp.exp(-x))


def _gelu(x):
    t = jnp.tanh(GELU_C * (x + 0.044715 * (x * x * x)))
    return 0.5 * x * (1.0 + t), t


def _gelu_grad(x, t):
    return 0.5 * (1.0 + t) + 0.5 * x * (1.0 - t * t) * (GELU_C * (1.0 + 3.0 * 0.044715 * (x * x)))


def _neg_expm1(y):
    series = -y * (1.0 + 0.5 * y * (1.0 + (y / 3.0) * (1.0 + 0.25 * y * (1.0 + 0.2 * y))))
    return jnp.where(y > -0.05, series, 1.0 - jnp.exp(y))


def _ln_stats(z):
    mu = jnp.mean(z, axis=-1, keepdims=True)
    zc = z - mu
    var = jnp.mean(zc * zc, axis=-1, keepdims=True)
    r = lax.rsqrt(var + EPS)
    return zc * r, r


def _ln_bwd(dy, z, g):
    xhat, r = _ln_stats(z)
    dxh = dy * g
    dz = r * (dxh - jnp.mean(dxh, axis=-1, keepdims=True)
              - xhat * jnp.mean(dxh * xhat, axis=-1, keepdims=True))
    return dz, jnp.sum(dy * xhat, axis=0, keepdims=True), jnp.sum(dy, axis=0, keepdims=True)


def _rms_stats(x):
    r = lax.rsqrt(jnp.mean(x * x, axis=-1, keepdims=True) + EPS)
    return x * r, r


def _rms_bwd(dy, x, g):
    xn, r = _rms_stats(x)
    dxn = dy * g
    dx = r * (dxn - xn * jnp.mean(dxn * xn, axis=-1, keepdims=True))
    return dx, jnp.sum(dy * xn, axis=0, keepdims=True)


def _shift_down(x, halo, s, axis):
    if s == 0:
        return x
    r = pltpu.roll(x, s, axis)
    hr = pltpu.roll(halo, s, axis)
    idx = lax.broadcasted_iota(jnp.int32, hr.shape, axis)
    head = lax.slice_in_dim(r, 0, SUBLANES, axis=axis)
    rest = lax.slice_in_dim(r, SUBLANES, x.shape[axis], axis=axis)
    return jnp.concatenate([jnp.where(idx < s, hr, head), rest], axis=axis)


def _shift_up(x, halo, s, axis):
    if s == 0:
        return x
    n = x.shape[axis]
    r = pltpu.roll(x, n - s, axis)
    hr = pltpu.roll(halo, SUBLANES - s, axis)
    idx = lax.broadcasted_iota(jnp.int32, hr.shape, axis)
    body = lax.slice_in_dim(r, 0, n - SUBLANES, axis=axis)
    tail = lax.slice_in_dim(r, n - SUBLANES, n, axis=axis)
    return jnp.concatenate([body, jnp.where(idx >= SUBLANES - s, hr, tail)], axis=axis)


def _const_spec(shape):
    nd = len(shape)
    return pl.BlockSpec(shape, lambda *_: (0,) * nd)


def _layer_spec(shape, l):
    nd = len(shape)
    return pl.BlockSpec((None,) + tuple(shape), lambda *_: (l,) + (0,) * nd)


def _resident(shape):
    nd = len(shape)
    return pl.BlockSpec(shape, lambda *_: (0,) * nd, pipeline_mode=pl.Buffered(1))


def _params(vmem=None):
    return pltpu.CompilerParams(vmem_limit_bytes=vmem)


def inproj_fwd(x, w_in_p, tm, ex=None):
    n = x.shape[0]
    n_in, n_out = 2, 4
    x_args, x_in, x_out, x_shapes, x_alias, x_scratch = _fuse_exchange(ex, n_in, n_out)

    def body(*refs):
        i = pl.program_id(0)

        @_run_exchange(ex, refs, n_in, n_out, 0, i == 0, i == n // tm - 1)
        def _():
            compute(*refs[:n_in], *refs[n_in + len(x_in):n_in + len(x_in) + n_out])

    def compute(x_ref, w_ref, rnn4_ref, ql_ref, kvl_ref, kr_ref):
        xb = x_ref[...].astype(MXU_DTYPE)
        for j in range(4):
            rnn4_ref[:, j * D:(j + 1) * D] = _mm(xb, w_ref[:, j * D:(j + 1) * D])
        o = 4 * D
        ql_ref[...] = _mm(xb, w_ref[:, o:o + Q_LORA])
        kvl_ref[...] = _mm(xb, w_ref[:, o + Q_LORA:o + Q_LORA + KV_LORA])
        kr_ref[...] = _mm(xb, w_ref[:, o + Q_LORA + KV_LORA:PROJ_W])

    row = lambda i: (i, 0)
    return pl.pallas_call(
        body, name="inproj_fwd", grid=(n // tm,),
        in_specs=[pl.BlockSpec((tm, D), row), _resident((D, PROJ_W))] + x_in,
        out_specs=[pl.BlockSpec((tm, 4 * D), row), pl.BlockSpec((tm, Q_LORA), row),
                   pl.BlockSpec((tm, KV_LORA), row), pl.BlockSpec((tm, LANES), row)] + x_out,
        out_shape=[jax.ShapeDtypeStruct((n, 4 * D), F32), jax.ShapeDtypeStruct((n, Q_LORA), F32),
                   jax.ShapeDtypeStruct((n, KV_LORA), F32), jax.ShapeDtypeStruct((n, LANES), F32)] + x_shapes,
        input_output_aliases=x_alias, scratch_shapes=x_scratch,
        compiler_params=_params(VMEM_BIG),
    )(x, w_in_p, *x_args)


def inproj_bwd(dxr, dg3, dql, dkvl, dkr, dz1, w_in_p, tm):
    n = dz1.shape[0]

    def body(dxr_ref, dg3_ref, dql_ref, dkvl_ref, dkr_ref, dz_ref, w_ref, dx_ref, dp_ref):
        dp = jnp.concatenate([dxr_ref[...], dg3_ref[...], dql_ref[...], dkvl_ref[...], dkr_ref[...]],
                             axis=1).astype(MXU_DTYPE)
        dp_ref[...] = dp
        dx_ref[...] = ALPHA * dz_ref[...] + _mm_nt(dp, w_ref[...])

    row = lambda i: (i, 0)
    return pl.pallas_call(
        body, name="inproj_bwd", grid=(n // tm,),
        in_specs=[pl.BlockSpec((tm, D), row), pl.BlockSpec((tm, 3 * D), row), pl.BlockSpec((tm, Q_LORA), row),
                  pl.BlockSpec((tm, KV_LORA), row), pl.BlockSpec((tm, LANES), row), pl.BlockSpec((tm, D), row),
                  _resident((D, PROJ_W))],
        out_specs=[pl.BlockSpec((tm, D), row), pl.BlockSpec((tm, PROJ_W), row)],
        out_shape=[jax.ShapeDtypeStruct((n, D), F32), jax.ShapeDtypeStruct((n, PROJ_W), MXU_DTYPE)],
        compiler_params=_params(VMEM_BIG),
    )(dxr, dg3, dql, dkvl, dkr, dz1, w_in_p)


def matmul_dw(x, dy, tn, tmc, name):
    n, k = x.shape
    m = dy.shape[1]

    def body(x_ref, dy_ref, dw_ref):
        @pl.when(pl.program_id(1) == 0)
        def _():
            dw_ref[...] = jnp.zeros_like(dw_ref)
        dw_ref[...] += _mm_tn(x_ref[...], dy_ref[...])

    return pl.pallas_call(
        body, name=name, grid=(m // tmc, n // tn),
        in_specs=[pl.BlockSpec((tn, k), lambda j, i: (i, 0)), pl.BlockSpec((tn, tmc), lambda j, i: (i, j))],
        out_specs=pl.BlockSpec((k, tmc), lambda j, i: (0, j)),
        out_shape=jax.ShapeDtypeStruct((k, m), F32),
        compiler_params=_params(VMEM_BIG),
    )(x, dy)


def matmul_dx(dy, w, add, add_scale, tm, name):
    n, m = dy.shape
    k = w.shape[0]

    def body(dy_ref, w_ref, add_ref, dx_ref):
        dx_ref[...] = add_scale * add_ref[...] + _mm_nt(dy_ref[...], w_ref[...])

    row = lambda i: (i, 0)
    return pl.pallas_call(
        body, name=name, grid=(n // tm,),
        in_specs=[pl.BlockSpec((tm, m), row), _resident((k, m)), pl.BlockSpec((tm, k), row)],
        out_specs=pl.BlockSpec((tm, k), row),
        out_shape=jax.ShapeDtypeStruct((n, k), F32),
        compiler_params=_params(VMEM_BIG),
    )(dy, w, add)


def _group(g):
    return slice(g * MXU_GROUP, (g + 1) * MXU_GROUP)


def _rnn_gates(x, halo, g, l, cw_ref, cb_ref, wgx_ref, bgx_ref, wga_ref, bga_ref, lam_ref):
    b, ts, gw = x.shape
    sl = _group(g)
    lr = slice(l, l + 1)
    xc = cb_ref[lr, sl][None]
    for k in range(CONV_W):
        xc = xc + cw_ref[k:k + 1, sl][None] * _shift_down(x, halo, CONV_W - 1 - k, 1)
    xc2 = xc.reshape(b * ts, gw)
    xcb = xc2.astype(MXU_DTYPE)
    gx = _sigmoid(_mm(xcb, wgx_ref[g]) + bgx_ref[lr, sl])
    ga = _sigmoid(_mm(xcb, wga_ref[g]) + bga_ref[lr, sl])
    nl = -lam_ref[lr, sl]
    sp = jnp.maximum(nl, 0.0) + jnp.log1p(jnp.exp(-jnp.abs(nl)))
    log_a = (-LRU_C) * ga * sp
    a = jnp.exp(log_a)
    mult = jnp.sqrt(_neg_expm1(2.0 * log_a))
    return xc2, xcb, gx, ga, sp, a, mult


def rnn_fwd(rnn4, lw, l, b, s, ts):
    ns = s // ts

    def body(x_ref, cw_ref, cb_ref, wgx_ref, bgx_ref, wga_ref, bga_ref, lam_ref, h_ref,
             halo_sc, hstate_sc, a_sc, u_sc):
        @pl.when(pl.program_id(0) == 0)
        def _():
            halo_sc[...] = jnp.zeros_like(halo_sc)
            hstate_sc[...] = jnp.zeros_like(hstate_sc)

        for g in range(N_GATE_GROUPS):
            sl = _group(g)
            x = x_ref[:, :, sl]
            xc2, _, gx, _, _, a, mult = _rnn_gates(x, halo_sc[:, :, sl], g, l, cw_ref, cb_ref, wgx_ref, bgx_ref,
                                                   wga_ref, bga_ref, lam_ref)
            halo_sc[:, :, sl] = x[:, ts - SUBLANES:, :]
            a_sc[:, :, sl] = a.reshape(b, ts, MXU_GROUP)
            u_sc[:, :, sl] = (mult * gx * xc2).reshape(b, ts, MXU_GROUP)

        def step(t, h):
            h = a_sc[:, pl.ds(t, 1), :] * h + u_sc[:, pl.ds(t, 1), :]
            h_ref[:, pl.ds(t, 1), :] = h
            return h

        hstate_sc[...] = lax.fori_loop(0, ts, step, hstate_sc[...], unroll=8)

    tile = lambda i: (0, i, 0)
    vecs = _const_spec((DEPTH, D))
    gates = _layer_spec((N_GATE_GROUPS, MXU_GROUP, MXU_GROUP), l)
    return pl.pallas_call(
        body, name="rnn_fwd", grid=(ns,),
        in_specs=[pl.BlockSpec((b, ts, D), tile), _const_spec((CONV_W, D)), vecs, gates, vecs, gates, vecs, vecs],
        out_specs=pl.BlockSpec((b, ts, D), tile),
        out_shape=jax.ShapeDtypeStruct((b, s, D), F32),
        scratch_shapes=[pltpu.VMEM((b, SUBLANES, D), F32), pltpu.VMEM((b, 1, D), F32),
                        pltpu.VMEM((b, ts, D), F32), pltpu.VMEM((b, ts, D), F32)],
        compiler_params=_params(VMEM_BIG),
    )(rnn4, lw['conv_w'], lw['conv_b'], lw['wgx'], lw['gx_b'], lw['wga'], lw['ga_b'], lw['lru_lambda'])


def rnn_bwd(dh, rnn4, h, lw, l, b, s, ts):
    ns = s // ts
    hb = ts // SUBLANES

    def body(dh_ref, x_ref, xh_ref, h_ref, hh_ref, cw_ref, cb_ref, wgx_ref, bgx_ref, wga_ref, bga_ref, lam_ref,
             dx_ref, dcw_ref, dcb_ref, dwgx_ref, dbgx_ref, dwga_ref, dbga_ref, dlam_ref,
             carry_sc, dxc_halo_sc, a_sc, delta_sc, xc_sc, gx_sc, ga_sc, mult_sc):
        i = pl.program_id(0)

        @pl.when(i == 0)
        def _():
            carry_sc[...] = jnp.zeros_like(carry_sc)
            dxc_halo_sc[...] = jnp.zeros_like(dxc_halo_sc)
            for r in (dcw_ref, dcb_ref, dwgx_ref, dbgx_ref, dwga_ref, dbga_ref, dlam_ref):
                r[...] = jnp.zeros_like(r)

        keep = jnp.where(i == ns - 1, 0.0, 1.0)
        for g in range(N_GATE_GROUPS):
            sl = _group(g)
            xc2, _, gx, ga, _, a, mult = _rnn_gates(x_ref[:, :, sl], xh_ref[:, :, sl] * keep, g, l, cw_ref, cb_ref,
                                                    wgx_ref, bgx_ref, wga_ref, bga_ref, lam_ref)
            for sc, val in ((a_sc, a), (xc_sc, xc2), (gx_sc, gx), (ga_sc, ga), (mult_sc, mult)):
                sc[:, :, sl] = val.reshape(b, ts, MXU_GROUP)

        def step(j, c):
            t = ts - 1 - j
            d = dh_ref[:, pl.ds(t, 1), :] + c
            delta_sc[:, pl.ds(t, 1), :] = d
            return a_sc[:, pl.ds(t, 1), :] * d

        carry_sc[...] = lax.fori_loop(0, ts, step, carry_sc[...], unroll=8)

        for g in range(N_GATE_GROUPS):
            sl = _group(g)
            x = x_ref[:, :, sl]
            flat = lambda sc, sl=sl: sc[:, :, sl].reshape(b * ts, MXU_GROUP)
            xc2, gx, ga, a, mult, delta = (flat(sc) for sc in (xc_sc, gx_sc, ga_sc, a_sc, mult_sc, delta_sc))
            xcb = xc2.astype(MXU_DTYPE)
            nl = -lam_ref[l:l + 1, sl]
            sp = jnp.maximum(nl, 0.0) + jnp.log1p(jnp.exp(-jnp.abs(nl)))
            hprev = _shift_down(h_ref[:, :, sl], hh_ref[:, :, sl] * keep, 1, 1).reshape(b * ts, MXU_GROUP)
            dmult = delta * gx * xc2
            dl = delta * hprev * a - dmult * (a * a) / mult
            dga = dl * ((-LRU_C) * sp)
            dlam_ref[:, sl] += (jnp.sum(dl * ((-LRU_C) * ga), axis=0, keepdims=True)
                                * (-_sigmoid(-lam_ref[l:l + 1, sl])))
            dpa = dga * ga * (1.0 - ga)
            dpx = (delta * mult * xc2) * gx * (1.0 - gx)
            dbga_ref[:, sl] += jnp.sum(dpa, axis=0, keepdims=True)
            dbgx_ref[:, sl] += jnp.sum(dpx, axis=0, keepdims=True)
            dpab = dpa.astype(MXU_DTYPE)
            dpxb = dpx.astype(MXU_DTYPE)
            dwga_ref[g] += _mm_tn(xcb, dpab)
            dwgx_ref[g] += _mm_tn(xcb, dpxb)
            dxc2 = delta * mult * gx + _mm_nt(dpab, wga_ref[g]) + _mm_nt(dpxb, wgx_ref[g])
            dcb_ref[:, sl] += jnp.sum(dxc2, axis=0, keepdims=True)
            dxc = dxc2.reshape(b, ts, MXU_GROUP)
            nhalo = dxc_halo_sc[:, :, sl]
            dx = None
            for k in range(CONV_W):
                later = _shift_up(dxc, nhalo, CONV_W - 1 - k, 1)
                dcw_ref[k:k + 1, sl] += jnp.sum((later * x).reshape(b * ts, MXU_GROUP), axis=0, keepdims=True)
                term = cw_ref[k:k + 1, sl][None] * later
                dx = term if dx is None else dx + term
            dx_ref[:, :, sl] = dx
            dxc_halo_sc[:, :, sl] = dxc[:, :SUBLANES, :]

    tile = lambda i: (0, ns - 1 - i, 0)
    halo = lambda i: (0, jnp.maximum((ns - 1 - i) * hb - 1, 0), 0)
    gshape = (N_GATE_GROUPS, MXU_GROUP, MXU_GROUP)
    vecs = _const_spec((DEPTH, D))
    gates = _layer_spec(gshape, l)
    vec = jax.ShapeDtypeStruct((1, D), F32)
    return pl.pallas_call(
        body, name="rnn_bwd", grid=(ns,),
        in_specs=[pl.BlockSpec((b, ts, D), tile), pl.BlockSpec((b, ts, D), tile),
                  pl.BlockSpec((b, SUBLANES, D), halo), pl.BlockSpec((b, ts, D), tile),
                  pl.BlockSpec((b, SUBLANES, D), halo),
                  _const_spec((CONV_W, D)), vecs, gates, vecs, gates, vecs, vecs],
        out_specs=[pl.BlockSpec((b, ts, D), tile), _const_spec((CONV_W, D)), _const_spec((1, D)),
                   _const_spec(gshape), _const_spec((1, D)), _const_spec(gshape), _const_spec((1, D)),
                   _const_spec((1, D))],
        out_shape=[jax.ShapeDtypeStruct((b, s, D), F32), jax.ShapeDtypeStruct((CONV_W, D), F32), vec,
                   jax.ShapeDtypeStruct(gshape, F32), vec, jax.ShapeDtypeStruct(gshape, F32), vec, vec],
        scratch_shapes=[pltpu.VMEM((b, 1, D), F32), pltpu.VMEM((b, SUBLANES, D), F32)]
        + [pltpu.VMEM((b, ts, D), F32)] * 6,
        compiler_params=_params(VMEM_BIG),
    )(dh, rnn4, rnn4, h, h, lw['conv_w'], lw['conv_b'], lw['wgx'], lw['gx_b'], lw['wga'], lw['ga_b'],
      lw['lru_lambda'])


def _rope(x, cos, sa, sb):
    return x * cos + pltpu.roll(x, HEAD_LANES - QK_ROPE // 2, 1) * sa + pltpu.roll(x, QK_ROPE // 2, 1) * sb


def _unrope(d, cos, sa, sb):
    return d * cos + pltpu.roll(d * sa, QK_ROPE // 2, 1) + pltpu.roll(d * sb, HEAD_LANES - QK_ROPE // 2, 1)


LOG2_E = 1.0 / math.log(2.0)
Q_PRESCALE = ATT_SCALE * LOG2_E


def _scores(q_blk, keys):
    return _mm_nt(q_blk, keys)


def _diag_scores(q_blk, keys):
    tq = q_blk.shape[0]
    keep = lax.broadcasted_iota(jnp.int32, (tq, tq), 0) >= lax.broadcasted_iota(jnp.int32, (tq, tq), 1)
    return jnp.where(keep, _scores(q_blk, keys), NEG_INF)


def _mla_project(ql_ref, kvl_ref, l, gq_ref, gkv_ref, wq_ref, wkv_ref):
    qn, _ = _rms_stats(ql_ref[0])
    qn = (qn * gq_ref[l:l + 1, :]).astype(MXU_DTYPE)
    kvn, _ = _rms_stats(kvl_ref[0])
    kvn = (kvn * gkv_ref[l:l + 1, :]).astype(MXU_DTYPE)
    return qn, kvn, _mm(qn, wq_ref[0]), _mm(kvn, wkv_ref[0])


def mla_fwd(ql, kvl, kr, tabs, lw, l, b, s, tq, ex=None):
    nq = s // tq
    cos_t, sa_t, sb_t = tabs
    n_in, n_out = 10, 2
    x_args, x_in, x_out, x_shapes, x_alias, x_scratch = _fuse_exchange(ex, n_in, n_out)

    def body(*refs):
        bi, p = pl.program_id(0), pl.program_id(1)
        first = (bi == 0) & (p == 0)
        last = (bi == b - 1) & (p == N_HEAD_PAIRS - 1)

        @_run_exchange(ex, refs, n_in, n_out, 0, first, last)
        def _():
            compute(*refs[:n_in], *refs[n_in + len(x_in):n_in + len(x_in) + n_out])

    def compute(ql_ref, kvl_ref, kr_ref, cos_ref, sa_ref, sb_ref, gq_ref, gkv_ref, wq_ref, wkv_ref, o_ref, lse_ref):
        _, _, qp, kvp = _mla_project(ql_ref, kvl_ref, l, gq_ref, gkv_ref, wq_ref, wkv_ref)
        cos, sa, sb = cos_ref[0], sa_ref[0], sb_ref[0]
        for hh in range(HEADS_PER_STEP):
            hs = slice(hh * HEAD_LANES, (hh + 1) * HEAD_LANES)
            q = (_rope(qp[:, hs], cos, sa, sb) * Q_PRESCALE).astype(MXU_DTYPE)
            k = _rope(kvp[:, hs] + kr_ref[0], cos, sa, sb).astype(MXU_DTYPE)
            v = kvp[:, HEADS_PER_STEP * HEAD_LANES + hh * HEAD_LANES:
                    HEADS_PER_STEP * HEAD_LANES + (hh + 1) * HEAD_LANES].astype(MXU_DTYPE)
            def block_scores(qb):
                lo, hi = qb * tq, (qb + 1) * tq
                return _diag_scores(q[lo:hi], k[lo:hi]), (_scores(q[lo:hi], k[:lo]) if qb else None)

            def block_softmax(qb, sd, sf):
                m = jnp.max(sd, axis=-1, keepdims=True)
                if qb:
                    m = jnp.maximum(m, jnp.max(sf, axis=-1, keepdims=True))
                ed = jnp.exp2(sd - m)
                den = jnp.sum(ed, axis=-1, keepdims=True)
                ef = None
                if qb:
                    ef = jnp.exp2(sf - m)
                    den = den + jnp.sum(ef, axis=-1, keepdims=True)
                return ed.astype(MXU_DTYPE), (ef.astype(MXU_DTYPE) if qb else None), den, m + jnp.log2(den)

            def block_out(qb, ed, ef, den, lse, hh=hh, v=v):
                lo, hi = qb * tq, (qb + 1) * tq
                o = _mm(ed, v[lo:hi])
                if qb:
                    o = o + _mm(ef, v[:lo])
                o = o * (1.0 / den)
                lse_ref[0, hh, lo:hi, :] = jnp.broadcast_to(lse, (tq, LANES))
                if hh == 0:
                    o_ref[0, lo:hi, :] = o
                else:
                    o_ref[0, lo:hi, :] += o

            nxt, prev = block_scores(0), None
            for qb in range(nq):
                sd, sf = nxt
                if qb + 1 < nq:
                    nxt = block_scores(qb + 1)
                if prev is not None:
                    block_out(qb - 1, *prev)
                prev = block_softmax(qb, sd, sf)
            block_out(nq - 1, *prev)

    seq = lambda bi, p: (bi, 0, 0)
    pair = lambda bi, p: (p, 0, 0)

    def per_seq(w):
        return pl.BlockSpec((1, s, w), seq, pipeline_mode=pl.Buffered(1))

    return pl.pallas_call(
        body, name="mla_fwd", grid=(b, N_HEAD_PAIRS),
        in_specs=[per_seq(Q_LORA), per_seq(KV_LORA), per_seq(LANES), per_seq(LANES), per_seq(LANES), per_seq(LANES),
                  _const_spec((DEPTH, Q_LORA)), _const_spec((DEPTH, KV_LORA)),
                  pl.BlockSpec((1, Q_LORA, HEADS_PER_STEP * HEAD_LANES), pair),
                  pl.BlockSpec((1, KV_LORA, 2 * HEADS_PER_STEP * HEAD_LANES), pair)] + x_in,
        out_specs=[pl.BlockSpec((1, s, LANES), lambda bi, p: (bi, 0, p)),
                   pl.BlockSpec((1, HEADS_PER_STEP, s, LANES), lambda bi, p: (bi, p, 0, 0))] + x_out,
        out_shape=[jax.ShapeDtypeStruct((b, s, D), F32), jax.ShapeDtypeStruct((b, N_HEADS, s, LANES), F32)] + x_shapes,
        input_output_aliases=x_alias, scratch_shapes=x_scratch,
        compiler_params=_params(VMEM_BIG),
    )(ql, kvl, kr, cos_t, sa_t, sb_t, lw['q_norm_g'], lw['kv_norm_g'], lw['wq_pairs'], lw['wkv_pairs'], *x_args)


def mla_bwd(dy, y, lse, ql, kvl, kr, tabs, lw, l, b, s, tq, ex=None):
    nq = s // tq
    cos_t, sa_t, sb_t = tabs
    qw = HEADS_PER_STEP * HEAD_LANES
    kvw = 2 * HEADS_PER_STEP * HEAD_LANES
    n_in, n_out, n_scratch = 13, 7, 2
    x_args, x_in, x_out, x_shapes, x_alias, x_scratch = _fuse_exchange(ex, n_in, n_out)

    def body(*refs):
        bi, p = pl.program_id(0), pl.program_id(1)
        first = (bi == 0) & (p == 0)
        last = (bi == b - 1) & (p == N_HEAD_PAIRS - 1)
        o0 = n_in + len(x_in)
        s0 = o0 + n_out + len(x_out)

        @_run_exchange(ex, refs, n_in, n_out, n_scratch, first, last)
        def _():
            compute(*refs[:n_in], *refs[o0:o0 + n_out], *refs[s0:s0 + n_scratch])

    def compute(dy_ref, y_ref, lse_ref, ql_ref, kvl_ref, kr_ref, cos_ref, sa_ref, sb_ref, gq_ref, gkv_ref, wq_ref,
                wkv_ref, dql_ref, dkvl_ref, dkr_ref, dwq_ref, dwkv_ref, dgq_ref, dgkv_ref, dk_sc, dv_sc):
        bi, p = pl.program_id(0), pl.program_id(1)

        @pl.when((bi == 0) & (p == 0))
        def _():
            for r in (dwq_ref, dwkv_ref, dgq_ref, dgkv_ref):
                r[...] = jnp.zeros_like(r)

        @pl.when(p == 0)
        def _():
            for r in (dql_ref, dkvl_ref, dkr_ref):
                r[...] = jnp.zeros_like(r)

        qn, kvn, qp, kvp = _mla_project(ql_ref, kvl_ref, l, gq_ref, gkv_ref, wq_ref, wkv_ref)
        cos, sa, sb = cos_ref[0], sa_ref[0], sb_ref[0]
        dof = dy_ref[0]
        do = dof.astype(MXU_DTYPE)
        prod = dof * y_ref[0]
        lane = lax.broadcasted_iota(jnp.int32, prod.shape, 1)
        dq_heads, dk_heads, dv_heads = [], [], []
        for hh in range(HEADS_PER_STEP):
            hs = slice(hh * HEAD_LANES, (hh + 1) * HEAD_LANES)
            q = (_rope(qp[:, hs], cos, sa, sb) * Q_PRESCALE).astype(MXU_DTYPE)
            k = _rope(kvp[:, hs] + kr_ref[0], cos, sa, sb).astype(MXU_DTYPE)
            v = kvp[:, qw + hh * HEAD_LANES:qw + (hh + 1) * HEAD_LANES].astype(MXU_DTYPE)
            mine = (lane >= hh * V_HEAD) & (lane < (hh + 1) * V_HEAD)
            delta = jnp.sum(jnp.where(mine, prod, 0.0), axis=-1, keepdims=True)
            dk_sc[...] = jnp.zeros_like(dk_sc)
            dv_sc[...] = jnp.zeros_like(dv_sc)
            units = []
            for qb in range(nq):
                units.append((qb, slice(qb * tq, (qb + 1) * tq), True))
                if qb:
                    units.append((qb, slice(0, qb * tq), False))

            def unit_matmuls_in(u, q=q, k=k, v=v):
                qb, ks, diag = u
                qs = slice(qb * tq, (qb + 1) * tq)
                sc = _diag_scores(q[qs], k[ks]) if diag else _scores(q[qs], k[ks])
                return sc, _mm_nt(do[qs], v[ks])

            def unit_elementwise(u, sc, dp, hh=hh, delta=delta):
                qs = slice(u[0] * tq, (u[0] + 1) * tq)
                pr = jnp.exp2(sc - lse_ref[0, hh, qs, 0:1])
                return pr.astype(MXU_DTYPE), (pr * (dp - delta[qs])).astype(MXU_DTYPE)

            dq_blocks = [None] * nq

            def unit_matmuls_out(u, prb, ds, q=q, k=k):
                qb, ks, _ = u
                qs = slice(qb * tq, (qb + 1) * tq)
                dv_sc[ks, :] += _mm_tn(prb, do[qs])
                part = _mm(ds, k[ks])
                dq_blocks[qb] = part if dq_blocks[qb] is None else dq_blocks[qb] + part
                dk_sc[ks, :] += _mm_tn(ds, q[qs])

            nxt, prev = unit_matmuls_in(units[0]), None
            for i, u in enumerate(units):
                sc, dp = nxt
                if i + 1 < len(units):
                    nxt = unit_matmuls_in(units[i + 1])
                if prev is not None:
                    unit_matmuls_out(units[i - 1], *prev)
                prev = unit_elementwise(u, sc, dp)
            unit_matmuls_out(units[-1], *prev)
            dq_heads.append(_unrope(jnp.concatenate(dq_blocks, axis=0) * ATT_SCALE, cos, sa, sb))
            dk_full = _unrope(dk_sc[...] * (1.0 / LOG2_E), cos, sa, sb)
            dkr_ref[0] += dk_full
            dk_heads.append(dk_full)
            dv_heads.append(dv_sc[...])
        dqp = jnp.concatenate(dq_heads, axis=1).astype(MXU_DTYPE)
        dkvp = jnp.concatenate(dk_heads + dv_heads, axis=1).astype(MXU_DTYPE)
        dwq_ref[p] += _mm_tn(qn, dqp)
        dwkv_ref[p] += _mm_tn(kvn, dkvp)
        dql_ref[0] += _mm_nt(dqp, wq_ref[0])
        dkvl_ref[0] += _mm_nt(dkvp, wkv_ref[0])

        @pl.when(p == N_HEAD_PAIRS - 1)
        def _():
            dx, dg = _rms_bwd(dql_ref[0], ql_ref[0], gq_ref[l:l + 1, :])
            dql_ref[0] = dx
            dgq_ref[...] += dg
            dx, dg = _rms_bwd(dkvl_ref[0], kvl_ref[0], gkv_ref[l:l + 1, :])
            dkvl_ref[0] = dx
            dgkv_ref[...] += dg

    seq = lambda bi, p: (bi, 0, 0)
    pair = lambda bi, p: (p, 0, 0)

    def per_seq(w):
        return pl.BlockSpec((1, s, w), seq, pipeline_mode=pl.Buffered(1))

    return pl.pallas_call(
        body, name="mla_bwd", grid=(b, N_HEAD_PAIRS),
        in_specs=[pl.BlockSpec((1, s, LANES), lambda bi, p: (bi, 0, p)),
                  pl.BlockSpec((1, s, LANES), lambda bi, p: (bi, 0, p)),
                  pl.BlockSpec((1, HEADS_PER_STEP, s, LANES), lambda bi, p: (bi, p, 0, 0)),
                  per_seq(Q_LORA), per_seq(KV_LORA), per_seq(LANES), per_seq(LANES), per_seq(LANES), per_seq(LANES),
                  _const_spec((DEPTH, Q_LORA)), _const_spec((DEPTH, KV_LORA)),
                  pl.BlockSpec((1, Q_LORA, qw), pair), pl.BlockSpec((1, KV_LORA, kvw), pair)] + x_in,
        out_specs=[pl.BlockSpec((1, s, Q_LORA), seq), pl.BlockSpec((1, s, KV_LORA), seq),
                   pl.BlockSpec((1, s, LANES), seq),
                   _const_spec((N_HEAD_PAIRS, Q_LORA, qw)), _const_spec((N_HEAD_PAIRS, KV_LORA, kvw)),
                   _const_spec((1, Q_LORA)), _const_spec((1, KV_LORA))] + x_out,
        out_shape=[jax.ShapeDtypeStruct((b, s, Q_LORA), F32), jax.ShapeDtypeStruct((b, s, KV_LORA), F32),
                   jax.ShapeDtypeStruct((b, s, LANES), F32),
                   jax.ShapeDtypeStruct((N_HEAD_PAIRS, Q_LORA, qw), F32),
                   jax.ShapeDtypeStruct((N_HEAD_PAIRS, KV_LORA, kvw), F32),
                   jax.ShapeDtypeStruct((1, Q_LORA), F32), jax.ShapeDtypeStruct((1, KV_LORA), F32)] + x_shapes,
        input_output_aliases=x_alias,
        scratch_shapes=[pltpu.VMEM((s, HEAD_LANES), F32), pltpu.VMEM((s, HEAD_LANES), F32)] + x_scratch,
        compiler_params=_params(VMEM_BIG),
    )(dy, y, lse, ql, kvl, kr, cos_t, sa_t, sb_t, lw['q_norm_g'], lw['kv_norm_g'], lw['wq_pairs'], lw['wkv_pairs'],
      *x_args)


COL_CHUNK = 256


def _merge(g_rnn, gate_a, gate_b, h, y_mla):
    ge, t = _gelu(g_rnn)
    sa, sb = _sigmoid(gate_a), _sigmoid(gate_b)
    y_rnn = ge * h
    return ge, t, sa, sb, y_rnn, sa * y_rnn + sb * y_mla


def mixout_fwd(x, rnn4, h, y_mla, lw, l, tm):
    n = x.shape[0]

    def body(x_ref, gr_ref, gta_ref, gtb_ref, h_ref, y_ref, w_ref, g_ref, b_ref, z_ref, o_ref):
        z = ALPHA * x_ref[...]
        for c in range(0, D, COL_CHUNK):
            cs = slice(c, c + COL_CHUNK)
            merged = _merge(gr_ref[:, cs], gta_ref[:, cs], gtb_ref[:, cs], h_ref[:, cs], y_ref[:, cs])[-1]
            z = z + _mm(merged, w_ref[cs, :])
        z_ref[...] = z
        o_ref[...] = _ln_stats(z)[0] * g_ref[l:l + 1, :] + b_ref[l:l + 1, :]

    row = lambda i: (i, 0)
    col = lambda j: (lambda i: (i, j))
    blk = pl.BlockSpec((tm, D), row)
    return pl.pallas_call(
        body, name="mixout_fwd", grid=(n // tm,),
        in_specs=[blk, pl.BlockSpec((tm, D), col(1)), pl.BlockSpec((tm, D), col(2)), pl.BlockSpec((tm, D), col(3)),
                  blk, blk, _resident((D, D)), _const_spec((DEPTH, D)), _const_spec((DEPTH, D))],
        out_specs=[blk, blk],
        out_shape=[jax.ShapeDtypeStruct((n, D), F32), jax.ShapeDtypeStruct((n, D), F32)],
        compiler_params=_params(VMEM_BIG),
    )(x, rnn4, rnn4, rnn4, h, y_mla, lw['w_out'], lw['ln1_g'], lw['ln1_b'])


def mixout_bwd(dx1, z1, rnn4, h, y_mla, lw, l, tm):
    n = dx1.shape[0]

    def body(d_ref, z_ref, gr_ref, gta_ref, gtb_ref, h_ref, y_ref, w_ref, g_ref,
             dz_ref, dh_ref, dy_ref, dg3_ref, dw_ref, dg_ref, db_ref):
        @pl.when(pl.program_id(0) == 0)
        def _():
            for r in (dw_ref, dg_ref, db_ref):
                r[...] = jnp.zeros_like(r)

        dz, dg, db = _ln_bwd(d_ref[...], z_ref[...], g_ref[l:l + 1, :])
        dz_ref[...] = dz
        dg_ref[...] += dg
        db_ref[...] += db
        dzb = dz.astype(MXU_DTYPE)
        for c in range(0, D, COL_CHUNK):
            cs = slice(c, c + COL_CHUNK)
            g_rnn, h, y_mla = gr_ref[:, cs], h_ref[:, cs], y_ref[:, cs]
            ge, t, sa, sb, y_rnn, merged = _merge(g_rnn, gta_ref[:, cs], gtb_ref[:, cs], h, y_mla)
            dw_ref[cs, :] += _mm_tn(merged, dzb)
            dm = _mm_nt(dzb, w_ref[cs, :])
            dy_rnn = dm * sa
            dy_ref[:, cs] = dm * sb
            dh_ref[:, cs] = dy_rnn * ge
            dg3_ref[:, c:c + COL_CHUNK] = dy_rnn * h * _gelu_grad(g_rnn, t)
            dg3_ref[:, D + c:D + c + COL_CHUNK] = dm * y_rnn * sa * (1.0 - sa)
            dg3_ref[:, 2 * D + c:2 * D + c + COL_CHUNK] = dm * y_mla * sb * (1.0 - sb)

    row = lambda i: (i, 0)
    col = lambda j: (lambda i: (i, j))
    blk = pl.BlockSpec((tm, D), row)
    vec = jax.ShapeDtypeStruct((1, D), F32)
    act = jax.ShapeDtypeStruct((n, D), F32)
    return pl.pallas_call(
        body, name="mixout_bwd", grid=(n // tm,),
        in_specs=[blk, blk, pl.BlockSpec((tm, D), col(1)), pl.BlockSpec((tm, D), col(2)),
                  pl.BlockSpec((tm, D), col(3)), blk, blk, _resident((D, D)), _const_spec((DEPTH, D))],
        out_specs=[blk, blk, blk, pl.BlockSpec((tm, 3 * D), row), _const_spec((D, D)), _const_spec((1, D)),
                   _const_spec((1, D))],
        out_shape=[act, act, act, jax.ShapeDtypeStruct((n, 3 * D), F32), jax.ShapeDtypeStruct((D, D), F32), vec, vec],
        compiler_params=_params(VMEM_BIG),
    )(dx1, z1, rnn4, rnn4, rnn4, h, y_mla, lw['w_out'], lw['ln1_g'])


FFN_CHUNK = 512


def _conv3(u, halo, cs, l, fcw_ref, fcb_ref):
    hc = fcb_ref[l:l + 1, cs]
    for k in range(FFN_CONV_W):
        hc = hc + fcw_ref[k:k + 1, cs] * _shift_down(u, halo, FFN_CONV_W - 1 - k, 0)
    return hc


def _conv3_from(sc, cs, l, ts, fcw_ref, fcb_ref):
    hc = fcb_ref[l:l + 1, cs]
    for k in range(FFN_CONV_W):
        o = SUBLANES - (FFN_CONV_W - 1 - k)
        hc = hc + fcw_ref[k:k + 1, cs] * sc[o:o + ts, cs]
    return hc


def ffn_fwd(x1, lw, l, b, s, ts):
    ns = s // ts
    n = b * s

    def body(x_ref, wu_ref, fcw_ref, fcb_ref, wd_ref, g_ref, b_ref, up_ref, z_ref, o_ref, halo_sc):
        @pl.when(pl.program_id(1) == 0)
        def _():
            halo_sc[0:SUBLANES, :] = jnp.zeros((SUBLANES, 2 * D_FF), F32)

        x = x_ref[...]
        xb = x.astype(MXU_DTYPE)
        z = ALPHA * x

        def up_chunk(c):
            return (_mm(xb, wu_ref[:, c:c + FFN_CHUNK]), _mm(xb, wu_ref[:, D_FF + c:D_FF + c + FFN_CHUNK]))

        nxt, act_prev = up_chunk(0), None
        for c in range(0, D_FF, FFN_CHUNK):
            gs, vs = slice(c, c + FFN_CHUNK), slice(D_FF + c, D_FF + c + FFN_CHUNK)
            ug, uv = nxt
            if c + FFN_CHUNK < D_FF:
                nxt = up_chunk(c + FFN_CHUNK)
            if act_prev is not None:
                z = z + _mm(act_prev, wd_ref[c - FFN_CHUNK:c, :])
            up_ref[:, gs] = ug
            up_ref[:, vs] = uv
            halo_sc[SUBLANES:, gs] = ug
            halo_sc[SUBLANES:, vs] = uv
            hg, hv = (_conv3_from(halo_sc, cs, l, ts, fcw_ref, fcb_ref) for cs in (gs, vs))
            halo_sc[0:SUBLANES, gs] = ug[ts - SUBLANES:, :]
            halo_sc[0:SUBLANES, vs] = uv[ts - SUBLANES:, :]
            act_prev = (_gelu(hg)[0] * hv).astype(MXU_DTYPE)
        z = z + _mm(act_prev, wd_ref[D_FF - FFN_CHUNK:D_FF, :])
        z_ref[...] = z
        o_ref[...] = _ln_stats(z)[0] * g_ref[l:l + 1, :] + b_ref[l:l + 1, :]

    row = lambda bi, i: (bi * ns + i, 0)
    blk = pl.BlockSpec((ts, D), row)
    return pl.pallas_call(
        body, name="ffn_fwd", grid=(b, ns),
        in_specs=[blk, _resident((D, 2 * D_FF)), _const_spec((FFN_CONV_W, 2 * D_FF)), _const_spec((DEPTH, 2 * D_FF)),
                  _resident((D_FF, D)), _const_spec((DEPTH, D)), _const_spec((DEPTH, D))],
        out_specs=[pl.BlockSpec((ts, 2 * D_FF), row), blk, blk],
        out_shape=[jax.ShapeDtypeStruct((n, 2 * D_FF), F32), jax.ShapeDtypeStruct((n, D), F32),
                   jax.ShapeDtypeStruct((n, D), F32)],
        scratch_shapes=[pltpu.VMEM((SUBLANES + ts, 2 * D_FF), F32)],
        compiler_params=_params(VMEM_BIG),
    )(x1, lw['w_up'], lw['ffn_conv_w'], lw['ffn_conv_b'], lw['w_down'], lw['ln2_g'], lw['ln2_b'])


def ffn_bwd(dx2, z2, up, lw, l, b, s, ts):
    ns = s // ts
    n = b * s
    hb = ts // SUBLANES

    def body(d_ref, z_ref, up_ref, uph_ref, fcw_ref, fcb_ref, wd_ref, g_ref,
             dz_ref, dup_ref, act_ref, dfcw_ref, dfcb_ref, dg_ref, db_ref, nhalo_sc):
        bi, i = pl.program_id(0), pl.program_id(1)

        @pl.when((bi == 0) & (i == 0))
        def _():
            for r in (dfcw_ref, dfcb_ref, dg_ref, db_ref):
                r[...] = jnp.zeros_like(r)

        @pl.when(i == 0)
        def _():
            nhalo_sc[...] = jnp.zeros_like(nhalo_sc)

        dz, dg, db = _ln_bwd(d_ref[...], z_ref[...], g_ref[l:l + 1, :])
        dz_ref[...] = dz
        dg_ref[...] += dg
        db_ref[...] += db
        dzb = dz.astype(MXU_DTYPE)
        keep = jnp.where(i == ns - 1, 0.0, 1.0)
        for c in range(0, D_FF, FFN_CHUNK):
            gs, vs = slice(c, c + FFN_CHUNK), slice(D_FF + c, D_FF + c + FFN_CHUNK)
            ug, uv = up_ref[:, gs], up_ref[:, vs]
            hg_halo, hv_halo = uph_ref[:, gs] * keep, uph_ref[:, vs] * keep
            hg = _conv3(ug, hg_halo, gs, l, fcw_ref, fcb_ref)
            hv = _conv3(uv, hv_halo, vs, l, fcw_ref, fcb_ref)
            ge, t = _gelu(hg)
            act_ref[:, gs] = (ge * hv).astype(MXU_DTYPE)
            dact = _mm_nt(dzb, wd_ref[c:c + FFN_CHUNK, :])
            for cs, u, halo, dhc in ((gs, ug, hg_halo, dact * hv * _gelu_grad(hg, t)), (vs, uv, hv_halo, dact * ge)):
                dfcb_ref[:, cs] += jnp.sum(dhc, axis=0, keepdims=True)
                nhalo = nhalo_sc[:, cs]
                dup = jnp.zeros_like(dhc)
                for k in range(FFN_CONV_W):
                    sft = FFN_CONV_W - 1 - k
                    dfcw_ref[k:k + 1, cs] += jnp.sum(dhc * _shift_down(u, halo, sft, 0), axis=0, keepdims=True)
                    dup = dup + fcw_ref[k:k + 1, cs] * _shift_up(dhc, nhalo, sft, 0)
                dup_ref[:, cs] = dup.astype(MXU_DTYPE)
                nhalo_sc[:, cs] = dhc[:SUBLANES, :]

    row = lambda bi, i: (bi * ns + (ns - 1 - i), 0)
    halo = lambda bi, i: (jnp.maximum((bi * ns + (ns - 1 - i)) * hb - 1, 0), 0)
    blk = pl.BlockSpec((ts, D), row)
    wide = pl.BlockSpec((ts, 2 * D_FF), row)
    return pl.pallas_call(
        body, name="ffn_bwd", grid=(b, ns),
        in_specs=[blk, blk, wide, pl.BlockSpec((SUBLANES, 2 * D_FF), halo),
                  _const_spec((FFN_CONV_W, 2 * D_FF)), _const_spec((DEPTH, 2 * D_FF)), _resident((D_FF, D)),
                  _const_spec((DEPTH, D))],
        out_specs=[blk, wide, pl.BlockSpec((ts, D_FF), row), _const_spec((FFN_CONV_W, 2 * D_FF)),
                   _const_spec((1, 2 * D_FF)), _const_spec((1, D)), _const_spec((1, D))],
        out_shape=[jax.ShapeDtypeStruct((n, D), F32), jax.ShapeDtypeStruct((n, 2 * D_FF), MXU_DTYPE),
                   jax.ShapeDtypeStruct((n, D_FF), MXU_DTYPE), jax.ShapeDtypeStruct((FFN_CONV_W, 2 * D_FF), F32),
                   jax.ShapeDtypeStruct((1, 2 * D_FF), F32), jax.ShapeDtypeStruct((1, D), F32),
                   jax.ShapeDtypeStruct((1, D), F32)],
        scratch_shapes=[pltpu.VMEM((SUBLANES, 2 * D_FF), F32)],
        compiler_params=_params(VMEM_BIG),
    )(dx2, z2, up, up, lw['ffn_conv_w'], lw['ffn_conv_b'], lw['w_down'], lw['ln2_g'])


def loss_head(y, target, tm):
    n = y.shape[0]

    def body(y_ref, t_ref, l_ref, d_ref):
        @pl.when(pl.program_id(0) == 0)
        def _():
            l_ref[...] = jnp.zeros_like(l_ref)

        err = y_ref[...] - t_ref[...]
        d_ref[...] = err * (1.0 / D)
        part = jnp.sum(jnp.sum(err * err, axis=-1, keepdims=True), axis=0, keepdims=True)
        l_ref[...] += jnp.broadcast_to(part * (0.5 / D), l_ref.shape)

    row = lambda i: (i, 0)
    return pl.pallas_call(
        body, name="loss_head", grid=(n // tm,),
        in_specs=[pl.BlockSpec((tm, D), row), pl.BlockSpec((tm, D), row)],
        out_specs=[_const_spec((1, LANES)), pl.BlockSpec((tm, D), row)],
        out_shape=[jax.ShapeDtypeStruct((1, LANES), F32), jax.ShapeDtypeStruct((n, D), F32)],
    )(y, target)


def _adam_update(g, w, m, v):
    c1 = 1.0 - ADAM_B1 ** ADAM_STEP
    c2 = 1.0 - ADAM_B2 ** ADAM_STEP
    mn = ADAM_B1 * m + (1.0 - ADAM_B1) * g
    vn = ADAM_B2 * v + (1.0 - ADAM_B2) * (g * g)
    return -ADAM_LR * ((mn / c1) / (jnp.sqrt(vn / c2) + ADAM_EPS) + ADAM_WD * w), mn, vn


def adamw_tiled(parts, w, m, v, name):
    _, r, c = w.shape
    tr = next(t for t in (256, 128, 64, 32, 16, 8) if r % t == 0)

    def body(p_ref, w_ref, m_ref, v_ref, g_ref, d_ref, mo_ref, vo_ref):
        g = p_ref[0].astype(F32)
        for i in range(1, N_DEV):
            g = g + p_ref[i].astype(F32)
        g_ref[...] = g
        d_ref[...], mo_ref[...], vo_ref[...] = _adam_update(g, w_ref[...], m_ref[...], v_ref[...])

    blk = pl.BlockSpec((None, tr, c), lambda l, i: (l, i, 0))
    out = jax.ShapeDtypeStruct(w.shape, F32)
    return pl.pallas_call(
        body, name="adamw_" + name, grid=(DEPTH, r // tr),
        in_specs=[pl.BlockSpec((N_DEV, None, tr, c), lambda l, i: (0, l, i, 0)), blk, blk, blk],
        out_specs=[blk, blk, blk, blk],
        out_shape=[out, out, out, out],
    )(parts, w, m, v)


def adamw_small(items):
    k = len(items)

    def body(*refs):
        ins, outs = refs[:4 * k], refs[4 * k:]
        for j in range(k):
            p_ref, w_ref, m_ref, v_ref = ins[4 * j:4 * j + 4]
            g_ref, d_ref, mo_ref, vo_ref = outs[j], outs[k + j], outs[2 * k + j], outs[3 * k + j]
            if len(p_ref.shape) == len(w_ref.shape) + 1:
                g = p_ref[0]
                for i in range(1, N_DEV):
                    g = g + p_ref[i]
                g_ref[...] = g
                d_ref[...], mo_ref[...], vo_ref[...] = _adam_update(g, w_ref[...], m_ref[...], v_ref[...])
            else:
                for l in range(DEPTH):
                    lr = slice(l, l + 1)
                    g = p_ref[0, l]
                    for i in range(1, N_DEV):
                        g = g + p_ref[i, l]
                    g_ref[lr, :] = g
                    d_ref[lr, :], mo_ref[lr, :], vo_ref[lr, :] = _adam_update(g, w_ref[lr, :], m_ref[lr, :],
                                                                              v_ref[lr, :])

    flat = [a for item in items for a in item]
    outs = [jax.ShapeDtypeStruct(item[1].shape, F32) for item in items] * 4
    return pl.pallas_call(
        body, name="adamw_small",
        in_specs=[pl.BlockSpec(memory_space=pltpu.VMEM)] * len(flat),
        out_specs=[pl.BlockSpec(memory_space=pltpu.VMEM)] * len(outs),
        out_shape=outs,
        compiler_params=_params(VMEM_BIG),
    )(*flat)


class Exchange:
    def __init__(self, entries):
        self.names = [e[0] for e in entries]
        self.srcs = [e[1] for e in entries]
        self.per_peer = [e[2] for e in entries]
        self.src_layer = [e[3] for e in entries]
        self.dst_layer = [e[4] for e in entries]
        self.bufs = [e[5] for e in entries]
        self.na = len(entries)

    def updated(self, bufdict, outs):
        new = dict(bufdict)
        new.update(zip(self.names, outs))
        return new

    def scratch(self):
        return [pltpu.SemaphoreType.DMA((self.na, N_DEV - 1)), pltpu.SemaphoreType.DMA((self.na, N_DEV - 1)),
                pltpu.SemaphoreType.DMA((self.na,))]

    def out_shapes(self):
        return [jax.ShapeDtypeStruct(bf.shape, bf.dtype) for bf in self.bufs]

    def copies(self, src_refs, buf_refs, send_sems, recv_sems, local_sems):
        x, y, c = lax.axis_index("x"), lax.axis_index("y"), lax.axis_index("c")
        me = 4 * x + 2 * y + c

        def view(a, pid):
            r = src_refs[a]
            if self.src_layer[a] is not None:
                r = r.at[self.src_layer[a]]
            return r.at[pid] if self.per_peer[a] else r

        out = [pltpu.make_async_copy(view(a, me), buf_refs[a].at[me, self.dst_layer[a]], local_sems.at[a])
               for a in range(self.na)]
        for k in range(1, N_DEV):
            px = 1 - x if k & 4 else x
            py = 1 - y if k & 2 else y
            pc = 1 - c if k & 1 else c
            pid = 4 * px + 2 * py + pc
            for a in range(self.na):
                out.append(pltpu.make_async_remote_copy(
                    src_ref=view(a, pid), dst_ref=buf_refs[a].at[me, self.dst_layer[a]],
                    send_sem=send_sems.at[a, k - 1], recv_sem=recv_sems.at[a, k - 1],
                    device_id=(px, py, pc), device_id_type=pl.DeviceIdType.MESH))
        return out


ANY_SPEC = pl.BlockSpec(memory_space=pl.ANY)


def exchange_layer(ex, name):
    na = ex.na

    def body(*refs):
        src_refs, buf_refs = refs[:na], refs[2 * na:3 * na]
        cps = ex.copies(src_refs, buf_refs, *refs[3 * na:])
        for cp in cps:
            cp.start()
        for cp in cps:
            cp.wait()

    return pl.pallas_call(
        body, name=name,
        in_specs=[ANY_SPEC] * (2 * na), out_specs=[ANY_SPEC] * na,
        out_shape=ex.out_shapes(),
        input_output_aliases={na + a: a for a in range(na)},
        scratch_shapes=ex.scratch(),
    )(*ex.srcs, *ex.bufs)


def _fuse_exchange(ex, n_in, n_out):
    if ex is None:
        return [], [], [], [], {}, []
    na = ex.na
    aliases = {n_in + na + a: n_out + a for a in range(na)}
    return ex.srcs + ex.bufs, [ANY_SPEC] * (2 * na), [ANY_SPEC] * na, ex.out_shapes(), aliases, ex.scratch()


def _run_exchange(ex, refs, n_in, n_out, n_scratch, first, last):
    def deco(compute):
        if ex is None:
            compute()
            return
        na = ex.na
        n_all_in = n_in + 2 * na
        src_refs = refs[n_in:n_in + na]
        buf_refs = refs[n_all_in + n_out:n_all_in + n_out + na]
        sems = refs[n_all_in + n_out + na + n_scratch:]

        @pl.when(first)
        def _():
            for cp in ex.copies(src_refs, buf_refs, *sems):
                cp.start()

        compute()

        @pl.when(last)
        def _():
            for cp in ex.copies(src_refs, buf_refs, *sems):
                cp.wait()
    return deco


def _permute_w_in(w):
    o = [0, D, 2 * D, 2 * D + Q_LORA, 2 * D + Q_LORA + KV_LORA, 2 * D + Q_LORA + KV_LORA + QK_ROPE,
         3 * D + Q_LORA + KV_LORA + QK_ROPE, IN_WIDTH]
    xr, gr, qlat, kvl, kr, ga, gb = [w[:, o[i]:o[i + 1]] for i in range(7)]
    z = lambda c: jnp.zeros((w.shape[0], c), w.dtype)
    return jnp.concatenate([xr, gr, ga, gb, qlat, kvl, z(QK_NOPE), kr, z(HEAD_LANES - QK_NOPE - QK_ROPE)], axis=1)


def _unpermute_dw_in(dw):
    o = 4 * D
    k0 = o + Q_LORA + KV_LORA + QK_NOPE
    return jnp.concatenate([dw[:, 0:2 * D], dw[:, o:o + Q_LORA + KV_LORA], dw[:, k0:k0 + QK_ROPE],
                            dw[:, 2 * D:4 * D]], axis=1)


def _pair_wq(w):
    w = w.reshape(Q_LORA, N_HEADS, QK_NOPE + QK_ROPE)
    w = jnp.pad(w, ((0, 0), (0, 0), (0, HEAD_LANES - QK_NOPE - QK_ROPE)))
    return w.reshape(Q_LORA, N_HEAD_PAIRS, HEADS_PER_STEP * HEAD_LANES).transpose(1, 0, 2)


def _unpair_dwq(dw):
    dw = dw.transpose(1, 0, 2).reshape(Q_LORA, N_HEADS, HEAD_LANES)
    return dw[:, :, :QK_NOPE + QK_ROPE].reshape(Q_LORA, N_HEADS * (QK_NOPE + QK_ROPE))


def _pair_wkv(w):
    w = w.reshape(KV_LORA, N_HEAD_PAIRS, HEADS_PER_STEP, QK_NOPE + V_HEAD)
    kn, vv = w[..., :QK_NOPE], w[..., QK_NOPE:]
    z = jnp.zeros_like(kn[:, :, 0])
    out = jnp.concatenate([kn[:, :, 0], z, kn[:, :, 1], z, vv[:, :, 0], z, z, vv[:, :, 1]], axis=-1)
    return out.transpose(1, 0, 2)


def _unpair_dwkv(dw):
    dw = dw.transpose(1, 0, 2)
    h0 = jnp.concatenate([dw[..., 0:64], dw[..., 256:320]], axis=-1)
    h1 = jnp.concatenate([dw[..., 128:192], dw[..., 448:512]], axis=-1)
    return jnp.stack([h0, h1], axis=2).reshape(KV_LORA, N_HEADS * (QK_NOPE + V_HEAD))


def _group_gates(w):
    per = MXU_GROUP // RNN_BLOCK
    w = w.reshape(DEPTH, N_GATE_GROUPS, per, RNN_BLOCK, RNN_BLOCK)
    eye = jnp.eye(per, dtype=w.dtype)
    return jnp.einsum('lgpij,pq->lgpiqj', w, eye).reshape(DEPTH, N_GATE_GROUPS, MXU_GROUP, MXU_GROUP)


def _ungroup_dgate(dw):
    per = MXU_GROUP // RNN_BLOCK
    dw = dw.reshape(N_GATE_GROUPS, per, RNN_BLOCK, per, RNN_BLOCK)
    return jnp.stack([dw[:, p, :, p, :] for p in range(per)], axis=1).reshape(N_RNN_BLOCKS, RNN_BLOCK, RNN_BLOCK)


def _rope_tables(positions):
    inv_freq = ROPE_THETA ** (-jnp.arange(0, QK_ROPE, 2, dtype=F32) / QK_ROPE)
    ang = positions.astype(F32)[..., None] * inv_freq
    cos, sin = jnp.cos(ang), jnp.sin(ang)
    one, zero = jnp.ones_like(cos), jnp.zeros_like(cos)
    nope = lambda v: jnp.concatenate([v] * (QK_NOPE // (QK_ROPE // 2)), axis=-1)
    tail = jnp.concatenate([zero, zero], axis=-1)
    cos_t = jnp.concatenate([nope(one), cos, cos, tail], axis=-1)
    sa_t = jnp.concatenate([nope(zero), -sin, zero, tail], axis=-1)
    sb_t = jnp.concatenate([nope(zero), zero, sin, tail], axis=-1)
    return cos_t, sa_t, sb_t


def _unshard(pieces, name):
    axis = SHARDED[name][0] - 1
    return jnp.concatenate([pieces[i] for i in range(N_DEV)], axis=axis)


def _shard_pieces(g, name):
    axis = SHARDED[name][0] - 1
    return jnp.stack(jnp.split(g, N_DEV, axis=axis))


MIXER_SHARDED = ('w_in', 'conv_w', 'w_uq', 'w_ukv', 'w_out')
FFN_SHARDED = ('w_up', 'ffn_conv_w', 'w_down')
FFN_WEIGHTS = ('w_up', 'ffn_conv_w', 'ffn_conv_b', 'w_down', 'ln2_g', 'ln2_b')


MIXER_REST = ('conv_w', 'w_uq', 'w_ukv', 'w_out')


def _mixer_weights(gathered, l):
    full = {name: _unshard(gathered[name][:, l], name) for name in MIXER_REST}
    return {'conv_w': full['conv_w'], 'wq_pairs': _pair_wq(full['w_uq']), 'wkv_pairs': _pair_wkv(full['w_ukv']),
            'w_out': full['w_out']}


def _ffn_weights(gathered, l):
    return {name: _unshard(gathered[name][:, l], name) for name in FFN_SHARDED}


TM = 256
TS_RNN = 128
TS_FFN = 256
TQ_FWD = 256
TQ_BWD = 256
TN_DW = 1024


def layer_fwd(xc, tabs, shared, gbufs, send, l, b, s):
    n = b * s
    lw = dict(shared, w_in_p=_permute_w_in(_unshard(gbufs['w_in'][:, l], 'w_in')))
    ex0 = Exchange(_gather_entries(MIXER_REST, 0, send, gbufs)) if l == 0 else None
    rnn4, ql, kvl, kr, *xb = inproj_fwd(xc, lw['w_in_p'], TM, ex0)
    if ex0 is not None:
        gbufs = ex0.updated(gbufs, xb)
    lw.update(_mixer_weights(gbufs, l))
    h = rnn_fwd(rnn4.reshape(b, s, 4 * D), lw, l, b, s, TS_RNN)
    lat = (ql.reshape(b, s, Q_LORA), kvl.reshape(b, s, KV_LORA), kr.reshape(b, s, LANES))
    ex = Exchange(_gather_entries(FFN_SHARDED, l, send, gbufs)
                  + (_gather_entries(MIXER_SHARDED, l + 1, send, gbufs) if l + 1 < DEPTH else []))
    y_mla, lse, *xbufs = mla_fwd(*lat, tabs, lw, l, b, s, TQ_FWD, ex)
    gbufs = ex.updated(gbufs, xbufs)
    lw.update(_ffn_weights(gbufs, l))
    z1, x1 = mixout_fwd(xc, rnn4, h.reshape(n, D), y_mla.reshape(n, D), lw, l, TM)
    up, z2, x2 = ffn_fwd(x1, lw, l, b, s, TS_FFN)
    return x2, (xc, rnn4, lat, h, y_mla, lse, z1, x1, up, z2), lw, gbufs


def _gather_entries(group, l, send, gbufs):
    return [(k, send[k], False, l, l, gbufs[k]) for k in group]


def _scatter_entries(grads, l, pbufs):
    out = []
    for k, g in grads.items():
        g = _shard_pieces(g, k) if k in SHARDED else g
        out.append((k, g.astype(pbufs[k].dtype), k in SHARDED, None, l, pbufs[k]))
    return out


def layer_bwd(dx, saved, tabs, lw, l, b, s, pending, pbufs):
    n = b * s
    x0, rnn4, lat, h, y_mla, lse, z1, x1, up, z2 = saved
    dz2, dup, act, dfcw, dfcb, dg2, db2 = ffn_bwd(dx, z2, up, lw, l, b, s, TS_FFN)
    dx1 = matmul_dx(dup, lw['w_up'], dz2, ALPHA, TM, "ffn_up_dx")
    dw_up = matmul_dw(x1, dup, TN_DW, 2 * D_FF // 3, "ffn_up_dw")
    dw_down = matmul_dw(act, dz2, TN_DW, D // 2, "ffn_down_dw")
    ffn_grads = {'w_up': dw_up, 'ffn_conv_w': dfcw, 'ffn_conv_b': dfcb, 'w_down': dw_down, 'ln2_g': dg2, 'ln2_b': db2}
    dz1, dh, dy_mla, dg3, dw_out, dg1, db1 = mixout_bwd(dx1, z1, rnn4, h.reshape(n, D), y_mla.reshape(n, D),
                                                       lw, l, TM)
    ex = Exchange(pending + _scatter_entries(ffn_grads, l, pbufs))
    dql, dkvl, dkr, dwq, dwkv, dgq, dgkv, *xbufs = mla_bwd(dy_mla.reshape(b, s, D), y_mla, lse, *lat, tabs, lw, l,
                                                           b, s, TQ_BWD, ex)
    pbufs = ex.updated(pbufs, xbufs)
    dxr, dcw, dcb, dwgx, dbgx, dwga, dbga, dlam = rnn_bwd(dh.reshape(b, s, D), rnn4.reshape(b, s, 4 * D), h,
                                                          lw, l, b, s, TS_RNN)
    dx, dproj = inproj_bwd(dxr.reshape(n, D), dg3, dql.reshape(n, Q_LORA), dkvl.reshape(n, KV_LORA),
                           dkr.reshape(n, LANES), dz1, lw['w_in_p'], TM)
    dw_in_p = matmul_dw(x0, dproj, TN_DW, PROJ_W // 2, "inproj_dw")
    mixer_grads = {
        'w_in': _unpermute_dw_in(dw_in_p), 'conv_w': dcw, 'conv_b': dcb, 'gx_w': _ungroup_dgate(dwgx), 'gx_b': dbgx,
        'ga_w': _ungroup_dgate(dwga), 'ga_b': dbga, 'lru_lambda': dlam, 'q_norm_g': dgq, 'w_uq': _unpair_dwq(dwq),
        'kv_norm_g': dgkv, 'w_ukv': _unpair_dwkv(dwkv), 'w_out': dw_out, 'ln1_g': dg1, 'ln1_b': db1,
    }
    return dx, _scatter_entries(mixer_grads, l, pbufs), pbufs


SMALL_WEIGHT_ELEMS = 1 << 16


def kernel(x, positions, w_in, conv_w, conv_b, gx_w, gx_b, ga_w, ga_b, lru_lambda, q_norm_g, w_uq, kv_norm_g, w_ukv, w_out, ln1_g, ln1_b, w_up, ffn_conv_w, ffn_conv_b, w_down, ln2_g, ln2_b, loss_target, m_w_in, m_conv_w, m_conv_b, m_gx_w, m_gx_b, m_ga_w, m_ga_b, m_lru_lambda, m_q_norm_g, m_w_uq, m_kv_norm_g, m_w_ukv, m_w_out, m_ln1_g, m_ln1_b, m_w_up, m_ffn_conv_w, m_ffn_conv_b, m_w_down, m_ln2_g, m_ln2_b, v_w_in, v_conv_w, v_conv_b, v_gx_w, v_gx_b, v_ga_w, v_ga_b, v_lru_lambda, v_q_norm_g, v_w_uq, v_kv_norm_g, v_w_ukv, v_w_out, v_ln1_g, v_ln1_b, v_w_up, v_ffn_conv_w, v_ffn_conv_b, v_w_down, v_ln2_g, v_ln2_b):
    w = dict(zip(WEIGHT_NAMES, (w_in, conv_w, conv_b, gx_w, gx_b, ga_w, ga_b, lru_lambda, q_norm_g, w_uq, kv_norm_g,
                                w_ukv, w_out, ln1_g, ln1_b, w_up, ffn_conv_w, ffn_conv_b, w_down, ln2_g, ln2_b)))
    m = dict(zip(WEIGHT_NAMES, (m_w_in, m_conv_w, m_conv_b, m_gx_w, m_gx_b, m_ga_w, m_ga_b, m_lru_lambda,
                                m_q_norm_g, m_w_uq, m_kv_norm_g, m_w_ukv, m_w_out, m_ln1_g, m_ln1_b, m_w_up,
                                m_ffn_conv_w, m_ffn_conv_b, m_w_down, m_ln2_g, m_ln2_b)))
    v = dict(zip(WEIGHT_NAMES, (v_w_in, v_conv_w, v_conv_b, v_gx_w, v_gx_b, v_ga_w, v_ga_b, v_lru_lambda,
                                v_q_norm_g, v_w_uq, v_kv_norm_g, v_w_ukv, v_w_out, v_ln1_g, v_ln1_b, v_w_up,
                                v_ffn_conv_w, v_ffn_conv_b, v_w_down, v_ln2_g, v_ln2_b)))
    b, s, _ = x.shape
    n = b * s
    tabs = _rope_tables(positions)
    shared = {name: w[name] for name in WEIGHT_NAMES if name not in SHARDED and w[name].ndim == 2}
    shared['wgx'] = _group_gates(w['gx_w']).astype(MXU_DTYPE)
    shared['wga'] = _group_gates(w['ga_w']).astype(MXU_DTYPE)

    send = {k: w[k].astype(MXU_DTYPE) if k in MATMUL_WEIGHTS else w[k] for k in SHARDED}
    gbufs = {k: lax.empty((N_DEV,) + a.shape, a.dtype) for k, a in send.items()}
    head = Exchange(_gather_entries(('w_in',), 0, send, gbufs))
    gbufs = head.updated(gbufs, exchange_layer(head, "gather_weights"))

    lws, saved = [], []
    xc = x.reshape(n, D)
    for l in range(DEPTH):
        xc, sv, lw, gbufs = layer_fwd(xc, tabs, shared, gbufs, send, l, b, s)
        lws.append(lw)
        saved.append(sv)
    loss_part, dx = loss_head(xc, loss_target.reshape(n, D), TM)

    def part_shape(k):
        piece = w[k].shape[1:]
        return (N_DEV, DEPTH) + ((1,) + piece if len(piece) == 1 else piece)

    pbufs = {k: lax.empty(part_shape(k), MXU_DTYPE if k in MATMUL_WEIGHTS else F32) for k in WEIGHT_NAMES}
    pending = []
    for l in reversed(range(DEPTH)):
        dx, pending, pbufs = layer_bwd(dx, saved[l], tabs, lws[l], l, b, s, pending, pbufs)
    tail = Exchange(pending)
    parts = tail.updated(pbufs, exchange_layer(tail, "scatter_grads"))

    out = {}
    small = [k for k in WEIGHT_NAMES if w[k].size <= SMALL_WEIGHT_ELEMS]
    res = adamw_small([(parts[k], w[k], m[k], v[k]) for k in small])
    for j, k in enumerate(small):
        out[k] = tuple(res[i * len(small) + j] for i in range(4))
    for k in WEIGHT_NAMES:
        if k in out:
            continue
        shape = w[k].shape
        view = (DEPTH, -1, shape[-1])
        r = adamw_tiled(parts[k].reshape((N_DEV,) + view), w[k].reshape(view), m[k].reshape(view),
                        v[k].reshape(view), k)
        out[k] = tuple(a.reshape(shape) for a in r)
    loss = lax.psum(loss_part[0, 0], ("x", "y", "c"))
    return (loss, dx.reshape(b, s, D), *[out[k][0] for k in WEIGHT_NAMES], *[out[k][1] for k in WEIGHT_NAMES],
            *[out[k][2] for k in WEIGHT_NAMES], *[out[k][3] for k in WEIGHT_NAMES])
```
